```python
import math
import jax, jax.numpy as jnp
from jax import lax
import numpy as np

D_MODEL = 1024
BATCH = 8
SEQ = 4096
DEPTH = 2
DEC_BATCH = 8
DEC_SEQ = 32
PAST_LEN = 2048

CHUNK = 64
N_MIXERS = 2
N_A = (DEPTH + 1) // 2
N_B = DEPTH // 2
HEAD_DIM = 64
N_HEADS = D_MODEL // HEAD_DIM
N_KV_HEADS = 4
GROUP = N_HEADS // N_KV_HEADS
IDX_HEADS = 8
IDX_DIM = 64
TOPK_MAX = 256
Q_BLOCK = 128
N_BUCKETS = 32
MAX_DISTANCE = 128
CONV_WIDTH = 31
N_MEM = 256
MEM_HEADS = 4
MEM_HEAD_DIM = D_MODEL // MEM_HEADS
D_FF = 4 * D_MODEL
EPS = 1e-6

Q_COLS = N_HEADS * HEAD_DIM
KV_COLS = N_KV_HEADS * HEAD_DIM
IQ_COLS = IDX_HEADS * IDX_DIM
IN_A_COLS = Q_COLS + 2 * KV_COLS + IQ_COLS + IDX_DIM + IDX_HEADS
A_SPLITS = (Q_COLS, Q_COLS + KV_COLS, Q_COLS + 2 * KV_COLS,
            Q_COLS + 2 * KV_COLS + IQ_COLS, Q_COLS + 2 * KV_COLS + IQ_COLS + IDX_DIM)

kernel_name = 'dsa_conformer_streaming_step'


def rmsnorm(x, g):
    xf = x.astype(jnp.float32)
    y = xf * lax.rsqrt(jnp.mean(xf * xf, axis=-1, keepdims=True) + EPS)
    return (y * g.astype(jnp.float32)).astype(x.dtype)


def layernorm(x, g, b):
    xf = x.astype(jnp.float32)
    mu = jnp.mean(xf, axis=-1, keepdims=True)
    var = jnp.mean(jnp.square(xf - mu), axis=-1, keepdims=True)
    y = (xf - mu) * lax.rsqrt(var + EPS)
    return (y * g.astype(jnp.float32) + b.astype(jnp.float32)).astype(x.dtype)


def rel_bucket(rel):
    nb = N_BUCKETS // 2
    max_exact = nb // 2
    ret = jnp.where(rel > 0, nb, 0)
    n = jnp.abs(rel)
    nf = jnp.maximum(n, 1).astype(jnp.float32)
    large = max_exact + (jnp.log(nf / max_exact) / math.log(MAX_DISTANCE / max_exact)
                         * (nb - max_exact)).astype(jnp.int32)
    large = jnp.minimum(large, nb - 1)
    return ret + jnp.where(n < max_exact, n, large)


def attn_project(h, w_in):
    B, T, _ = h.shape
    q, k, v, qi, ki, wi = jnp.split(h @ w_in, A_SPLITS, axis=-1)
    q = q.reshape(B, T, N_HEADS, HEAD_DIM)
    k = k.reshape(B, T, N_KV_HEADS, HEAD_DIM)
    v = v.reshape(B, T, N_KV_HEADS, HEAD_DIM)
    qi = qi.reshape(B, T, IDX_HEADS, IDX_DIM)
    wi = wi * (IDX_HEADS ** -0.5)
    return q, k, v, qi, ki, wi


def sparse_attend(q, qi, wi, qpos, k_all, v_all, ki_all, rel_table, k_sel):
    B, Tq = q.shape[:2]
    L = k_all.shape[1]
    kpos = jnp.arange(L, dtype=jnp.int32)
    dots = jnp.einsum('bqhd,bsd->bqhs', qi.astype(jnp.float32), ki_all.astype(jnp.float32)) * (IDX_DIM ** -0.5)
    score = jnp.einsum('bqh,bqhs->bqs', wi.astype(jnp.float32), jax.nn.relu(dots))
    limit = (qpos // CHUNK + 1) * CHUNK
    admissible = kpos[None, :] < limit[:, None]
    score = jnp.where(admissible[None], score, -jnp.inf)
    top_val, top_idx = lax.top_k(score, k_sel)
    valid = top_val > -jnp.inf
    kg = jax.vmap(lambda a, i: a[i])(k_all, top_idx)
    vg = jax.vmap(lambda a, i: a[i])(v_all, top_idx)
    qg = q.reshape(B, Tq, N_KV_HEADS, GROUP, HEAD_DIM)
    logits = jnp.einsum('bqngd,bqknd->bqngk', qg, kg,
                        preferred_element_type=jnp.float32) * (HEAD_DIM ** -0.5)
    bias = rel_table[rel_bucket(top_idx - qpos[None, :, None])]
    bias = bias.reshape(B, Tq, k_sel, N_KV_HEADS, GROUP).transpose(0, 1, 3, 4, 2)
    logits = jnp.where(valid[:, :, None, None, :], logits + bias.astype(jnp.float32), -jnp.inf)
    p = jax.nn.softmax(logits, axis=-1).astype(v_all.dtype)
    o = jnp.einsum('bqngk,bqknd->bqngd', p, vg)
    return o.reshape(B, Tq, Q_COLS)


def mixer_a_prompt(h, w_in, w_out, rel_table):
    B, T, _ = h.shape
    q, k, v, qi, ki, wi = attn_project(h, w_in)
    k_sel = min(TOPK_MAX, T // 4)
    nblk = T // Q_BLOCK

    def to_blocks(a):
        return a.reshape(B, nblk, Q_BLOCK, *a.shape[2:]).swapaxes(0, 1)

    def blk(args):
        qb, qib, wib, pos = args
        return sparse_attend(qb, qib, wib, pos, k, v, ki, rel_table, k_sel)

    pos = jnp.arange(T, dtype=jnp.int32).reshape(nblk, Q_BLOCK)
    o = lax.map(blk, (to_blocks(q), to_blocks(qi), to_blocks(wi), pos))
    o = o.swapaxes(0, 1).reshape(B, T, Q_COLS)
    return o @ w_out, k, v, ki


def mixer_a_sample(h, ck, cv, cki, w_in, w_out, rel_table):
    B, T, _ = h.shape
    q, k, v, qi, ki, wi = attn_project(h, w_in)
    k_all = jnp.concatenate([ck, k], axis=1)
    v_all = jnp.concatenate([cv, v], axis=1)
    ki_all = jnp.concatenate([cki, ki], axis=1)
    L = k_all.shape[1]
    k_sel = min(TOPK_MAX, L // 4)
    qpos = ck.shape[1] + jnp.arange(T, dtype=jnp.int32)
    o = sparse_attend(q, qi, wi, qpos, k_all, v_all, ki_all, rel_table, k_sel)
    return o @ w_out, k, v, ki


def conv_glu(h, w_pw1, b_pw1):
    a, g = jnp.split(h @ w_pw1 + b_pw1, 2, axis=-1)
    return a * jax.nn.sigmoid(g)


def conv_rest(u_ext, w_dw, b_dw, ln_g, ln_b, w_pw2, b_pw2):
    y = lax.conv_general_dilated(u_ext, w_dw[:, None, :], window_strides=(1,), padding='VALID',
                                 dimension_numbers=('NWC', 'WIO', 'NWC'),
                                 feature_group_count=D_MODEL) + b_dw
    y = jax.nn.silu(layernorm(y, ln_g, ln_b))
    return y @ w_pw2 + b_pw2


def mem_kv(mem, g, w_kv):
    B = mem.shape[0]
    k, v = jnp.split(rmsnorm(mem, g) @ w_kv, 2, axis=-1)
    return (k.reshape(B, N_MEM, MEM_HEADS, MEM_HEAD_DIM),
            v.reshape(B, N_MEM, MEM_HEADS, MEM_HEAD_DIM))


def mem_attend(h, mk, mv, w_q, w_o):
    B, T, _ = h.shape
    q = (h @ w_q).reshape(B, T, MEM_HEADS, MEM_HEAD_DIM)
    logits = jnp.einsum('bthd,bmhd->bhtm', q, mk,
                        preferred_element_type=jnp.float32) * (MEM_HEAD_DIM ** -0.5)
    p = jax.nn.softmax(logits, axis=-1).astype(mv.dtype)
    o = jnp.einsum('bhtm,bmhd->bthd', p, mv).reshape(B, T, D_MODEL)
    return o @ w_o


def sqrelu_mlp(h, w1, w2):
    return jnp.square(jax.nn.relu(h @ w1)) @ w2


def setup_inputs(seed: int = 0) -> dict:
    key = jax.random.key(seed)
    ks = jax.random.split(key, 40)
    nrm = jax.random.normal
    f32 = jnp.float32

    def w(k, shape, fan_in):
        return nrm(k, shape, f32) * (fan_in ** -0.5)

    def gain(k, shape):
        return 1.0 + 0.01 * nrm(k, shape, f32)

    return {
        'x_prompt': nrm(ks[0], (BATCH, SEQ, D_MODEL), f32),
        'x_sample': nrm(ks[1], (DEC_BATCH, DEC_SEQ, D_MODEL), f32),
        'cache_attn_k': nrm(ks[2], (N_A, DEC_BATCH, PAST_LEN, N_KV_HEADS, HEAD_DIM), f32),
        'cache_attn_v': nrm(ks[3], (N_A, DEC_BATCH, PAST_LEN, N_KV_HEADS, HEAD_DIM), f32),
        'cache_attn_kidx': nrm(ks[4], (N_A, DEC_BATCH, PAST_LEN, IDX_DIM), f32),
        'state_conv': 0.5 * nrm(ks[5], (N_B, DEC_BATCH, CONV_WIDTH - 1, D_MODEL), f32),
        'cache_mem_k': nrm(ks[6], (DEPTH, DEC_BATCH, N_MEM, MEM_HEADS, MEM_HEAD_DIM), f32),
        'cache_mem_v': nrm(ks[7], (DEPTH, DEC_BATCH, N_MEM, MEM_HEADS, MEM_HEAD_DIM), f32),
        'mem_prompt': nrm(ks[8], (BATCH, N_MEM, D_MODEL), f32),
        'rel_bias': 0.5 * nrm(ks[9], (N_BUCKETS, N_HEADS), f32),
        'g_mix': gain(ks[10], (DEPTH, D_MODEL)),
        'w_in_attn': w(ks[11], (N_A, D_MODEL, IN_A_COLS), D_MODEL),
        'w_out_attn': w(ks[12], (N_A, Q_COLS, D_MODEL), Q_COLS),
        'w_pw1': w(ks[13], (N_B, D_MODEL, 2 * D_MODEL), D_MODEL),
        'b_pw1': 0.02 * nrm(ks[14], (N_B, 2 * D_MODEL), f32),
        'w_dw': w(ks[15], (N_B, CONV_WIDTH, D_MODEL), CONV_WIDTH),
        'b_dw': 0.02 * nrm(ks[16], (N_B, D_MODEL), f32),
        'ln_g': gain(ks[17], (N_B, D_MODEL)),
        'ln_b': 0.02 * nrm(ks[18], (N_B, D_MODEL), f32),
        'w_pw2': w(ks[19], (N_B, D_MODEL, D_MODEL), D_MODEL),
        'b_pw2': 0.02 * nrm(ks[20], (N_B, D_MODEL), f32),
        'g_mem_q': gain(ks[21], (DEPTH, D_MODEL)),
        'g_mem_src': gain(ks[22], (DEPTH, D_MODEL)),
        'w_mem_q': w(ks[23], (DEPTH, D_MODEL, D_MODEL), D_MODEL),
        'w_mem_kv': w(ks[24], (DEPTH, D_MODEL, 2 * D_MODEL), D_MODEL),
        'w_mem_o': w(ks[25], (DEPTH, D_MODEL, D_MODEL), D_MODEL),
        'g_mlp': gain(ks[26], (DEPTH, D_MODEL)),
        'w_mlp1': w(ks[27], (DEPTH, D_MODEL, D_FF), D_MODEL),
        'w_mlp2': w(ks[28], (DEPTH, D_FF, D_MODEL), D_FF),
        'g_final': gain(ks[29], (D_MODEL,)),
    }


def reference(x_prompt, x_sample, cache_attn_k, cache_attn_v, cache_attn_kidx, state_conv,
              cache_mem_k, cache_mem_v, mem_prompt, rel_bias, g_mix, w_in_attn, w_out_attn,
              w_pw1, b_pw1, w_dw, b_dw, ln_g, ln_b, w_pw2, b_pw2, g_mem_q, g_mem_src,
              w_mem_q, w_mem_kv, w_mem_o, g_mlp, w_mlp1, w_mlp2, g_final):
    xp, xs = x_prompt, x_sample
    kp_l, vp_l, kip_l, ks_l, vs_l, kis_l = [], [], [], [], [], []
    convp_l, convs_l, memk_l, memv_l = [], [], [], []
    pad = CONV_WIDTH - 1
    for i in range(DEPTH):
        j = i // N_MIXERS
        hp = rmsnorm(xp, g_mix[i])
        hs = rmsnorm(xs, g_mix[i])
        if i % N_MIXERS == 0:
            op, kp, vp, kip = mixer_a_prompt(hp, w_in_attn[j], w_out_attn[j], rel_bias)
            osm, ksm, vsm, kism = mixer_a_sample(hs, cache_attn_k[j], cache_attn_v[j],
                                                 cache_attn_kidx[j], w_in_attn[j],
                                                 w_out_attn[j], rel_bias)
            kp_l.append(kp); vp_l.append(vp); kip_l.append(kip)
            ks_l.append(ksm); vs_l.append(vsm); kis_l.append(kism)
        else:
            up = conv_glu(hp, w_pw1[j], b_pw1[j])
            up_ext = jnp.pad(up, ((0, 0), (pad, 0), (0, 0)))
            op = conv_rest(up_ext, w_dw[j], b_dw[j], ln_g[j], ln_b[j], w_pw2[j], b_pw2[j])
            us = conv_glu(hs, w_pw1[j], b_pw1[j])
            us_ext = jnp.concatenate([state_conv[j].astype(us.dtype), us], axis=1)
            osm = conv_rest(us_ext, w_dw[j], b_dw[j], ln_g[j], ln_b[j], w_pw2[j], b_pw2[j])
            convp_l.append(up[:, -pad:])
            convs_l.append(us_ext[:, -pad:])
        xp = xp + op
        xs = xs + osm
        mkp, mvp = mem_kv(mem_prompt, g_mem_src[i], w_mem_kv[i])
        memk_l.append(mkp); memv_l.append(mvp)
        xp = xp + mem_attend(rmsnorm(xp, g_mem_q[i]), mkp, mvp, w_mem_q[i], w_mem_o[i])
        xs = xs + mem_attend(rmsnorm(xs, g_mem_q[i]), cache_mem_k[i], cache_mem_v[i],
                             w_mem_q[i], w_mem_o[i])
        xp = xp + sqrelu_mlp(rmsnorm(xp, g_mlp[i]), w_mlp1[i], w_mlp2[i])
        xs = xs + sqrelu_mlp(rmsnorm(xs, g_mlp[i]), w_mlp1[i], w_mlp2[i])
    y_prompt = rmsnorm(xp, g_final)
    y_sample = rmsnorm(xs, g_final)
    new_attn_k_prompt = jnp.stack(kp_l)
    new_attn_v_prompt = jnp.stack(vp_l)
    new_attn_kidx_prompt = jnp.stack(kip_l)
    new_conv_prompt = jnp.stack(convp_l)
    new_mem_k_prompt = jnp.stack(memk_l)
    new_mem_v_prompt = jnp.stack(memv_l)
    new_attn_k_sample = jnp.stack(ks_l)
    new_attn_v_sample = jnp.stack(vs_l)
    new_attn_kidx_sample = jnp.stack(kis_l)
    new_conv_sample = jnp.stack(convs_l)
    return (y_prompt, y_sample, new_attn_k_prompt, new_attn_v_prompt, new_attn_kidx_prompt,
            new_conv_prompt, new_mem_k_prompt, new_mem_v_prompt, new_attn_k_sample,
            new_attn_v_sample, new_attn_kidx_sample, new_conv_sample)
```

```python
import functools
import math

import jax
import jax.numpy as jnp
from jax import lax
from jax.experimental import pallas as pl
from jax.experimental.pallas import tpu as pltpu

CHUNK = 64
HEAD_DIM = 64
N_KV_HEADS = 4
IDX_HEADS = 8
IDX_DIM = 64
TOPK_MAX = 256
N_BUCKETS = 32
MAX_DISTANCE = 128
CONV_WIDTH = 31
MEM_HEADS = 4
EPS = 1e-6

LANES = 128
SUBLANES = 8
KEY_BLOCK = 128
HALO = 32
VMEM_LIMIT = 56 * 1024 * 1024

NEG_BIG = -1e30
F32 = jnp.float32
BF16 = jnp.bfloat16

KEY_NEG_INF = -2139095041
KEY_POS_INF = 2139095040


def _const_spec(shape):
    nd = len(shape)
    return pl.BlockSpec(shape, lambda *_: (0,) * nd, pipeline_mode=pl.Buffered(1))


def _params(*sem):
    return pltpu.CompilerParams(dimension_semantics=sem, vmem_limit_bytes=VMEM_LIMIT)


def _rms(x, g):
    ms = jnp.mean(x * x, axis=-1, keepdims=True)
    return x * lax.rsqrt(ms + EPS) * g


def _dot(a, b):
    return jnp.dot(a, b, preferred_element_type=F32)


def _dot_nt(a, b):
    return lax.dot_general(a, b, (((1,), (1,)), ((), ())), preferred_element_type=F32)


def _row_tile(m, pref):
    t = min(m, pref)
    assert m % t == 0, (m, t)
    return t


def _bias_tiles_kernel(tab_ref, bt_ref, *, n_heads):
    nb = N_BUCKETS // 2
    max_exact = nb // 2
    r = lax.broadcasted_iota(jnp.int32, (KEY_BLOCK, KEY_BLOCK), 0)
    c = lax.broadcasted_iota(jnp.int32, (KEY_BLOCK, KEY_BLOCK), 1)
    for d in range(2):
        rel = c - r - d * KEY_BLOCK
        n = jnp.abs(rel)
        nf = jnp.maximum(n, 1).astype(F32)
        large = max_exact + (jnp.log(nf / max_exact) / math.log(MAX_DISTANCE / max_exact)
                             * (nb - max_exact)).astype(jnp.int32)
        large = jnp.minimum(large, nb - 1)
        bucket = jnp.where(rel > 0, nb, 0) + jnp.where(n < max_exact, n, large)
        for h in range(n_heads):
            acc = jnp.zeros((KEY_BLOCK, KEY_BLOCK), F32)
            for b in range(N_BUCKETS):
                acc = jnp.where(bucket == b, tab_ref[b, h], acc)
            bt_ref[d, h] = acc - tab_ref[nb - 1, h]


def _bias_tiles(rel_bias):
    n_heads = rel_bias.shape[1]
    return pl.pallas_call(
        functools.partial(_bias_tiles_kernel, n_heads=n_heads),
        out_shape=jax.ShapeDtypeStruct((2, n_heads, KEY_BLOCK, KEY_BLOCK), F32),
        in_specs=[pl.BlockSpec(memory_space=pltpu.SMEM)],
        out_specs=pl.BlockSpec(memory_space=pltpu.VMEM),
        name="bias_tiles",
    )(rel_bias)


def _attn_proj_kernel(x_ref, g_ref, w_ref, q_ref, k_ref, v_ref, ki_ref, kb_ref, vb_ref,
                      kib_ref, qi_ref, wi_ref, *, dq, dkv, dqi):
    h = _rms(x_ref[...], g_ref[...]).astype(BF16)
    o = 0
    q_ref[...] = (_dot(h, w_ref[:, o:o + dq]) * (HEAD_DIM ** -0.5)).astype(BF16)
    o += dq
    k = _dot(h, w_ref[:, o:o + dkv])
    k_ref[...] = k
    kb_ref[...] = k.astype(BF16)
    o += dkv
    v = _dot(h, w_ref[:, o:o + dkv])
    v_ref[...] = v
    vb_ref[...] = v.astype(BF16)
    o += dkv
    qi_ref[...] = (_dot(h, w_ref[:, o:o + dqi]) * (IDX_DIM ** -0.5)).astype(BF16)
    o += dqi
    ki = _dot(h, w_ref[:, o:o + LANES])[:, :IDX_DIM]
    ki_ref[...] = ki
    kib_ref[...] = ki.astype(BF16)
    o += LANES
    wi_ref[...] = _dot(h, w_ref[:, o:o + LANES]) * (IDX_HEADS ** -0.5)


def _attn_proj(x2, g, w_in):
    m, d = x2.shape
    dq = d
    dkv = N_KV_HEADS * HEAD_DIM
    dqi = IDX_HEADS * IDX_DIM
    base = dq + 2 * dkv + dqi
    w_main = w_in[:, :base]
    w_ki = jnp.pad(w_in[:, base:base + IDX_DIM], ((0, 0), (0, LANES - IDX_DIM)))
    w_wi = jnp.pad(w_in[:, base + IDX_DIM:], ((0, 0), (0, LANES - IDX_HEADS)))
    w = jnp.concatenate([w_main, w_ki, w_wi], axis=1).astype(BF16)
    tm = _row_tile(m, 512)
    row = lambda n: pl.BlockSpec((tm, n), lambda i: (i, 0))
    outs = [(dq, BF16), (dkv, F32), (dkv, F32), (IDX_DIM, F32), (dkv, BF16), (dkv, BF16),
            (IDX_DIM, BF16), (dqi, BF16), (LANES, F32)]
    return pl.pallas_call(
        functools.partial(_attn_proj_kernel, dq=dq, dkv=dkv, dqi=dqi),
        out_shape=[jax.ShapeDtypeStruct((m, n), dt) for n, dt in outs],
        grid=(m // tm,),
        in_specs=[row(d), _const_spec((1, d)), _const_spec(w.shape)],
        out_specs=[row(n) for n, _ in outs],
        compiler_params=_params("parallel"),
        name="attn_proj",
    )(x2, g.reshape(1, d), w)


def _sparse_attn_kernel(q_ref, qi_ref, wi_ref, k_ref, v_ref, ki_ref, bt_ref, o_ref,
                        s_ref, qg_ref, m_ref, l_ref, acc_ref, ce_ref,
                        *, tq, q_off, n_keys, k_sel, group):
    i = pl.program_id(1)
    qs = q_off // KEY_BLOCK + (i * tq) // KEY_BLOCK
    n_kb = qs + 1
    q_start = q_off + i * tq

    qpos = q_start + lax.broadcasted_iota(jnp.int32, (tq, 1), 0)
    lim = jnp.minimum((qpos // CHUNK + 1) * CHUNK, n_keys)
    wi = wi_ref[0]
    wcols = [jnp.broadcast_to(wi[:, h:h + 1], (tq, KEY_BLOCK)) for h in range(IDX_HEADS)]
    lane = lax.broadcasted_iota(jnp.int32, (tq, KEY_BLOCK), 1)

    def score_body(j, carry):
        k0 = pl.multiple_of(j * KEY_BLOCK, KEY_BLOCK)
        kib = ki_ref[0, pl.ds(k0, KEY_BLOCK), :]
        acc = jnp.zeros((tq, KEY_BLOCK), F32)
        for h in range(IDX_HEADS):
            d = _dot_nt(qi_ref[0, :, h * IDX_DIM:(h + 1) * IDX_DIM], kib)
            acc = acc + wcols[h] * jnp.maximum(d, 0.0)
        s_ref[j] = jnp.where(lane + k0 < lim, acc, -jnp.inf)
        return carry

    lax.fori_loop(0, n_kb, score_body, 0)

    def count(cand, strict):
        cb = jnp.broadcast_to(cand, (tq, KEY_BLOCK))

        def body(j, c):
            s = s_ref[j]
            return c + jnp.where((s > cb) if strict else (s >= cb), 1.0, 0.0)
        c = lax.fori_loop(0, n_kb, body, jnp.zeros((tq, KEY_BLOCK), F32))
        return jnp.sum(c, axis=1, keepdims=True)

    def key_to_f32(key):
        bits = key ^ ((key >> 31) & 0x7FFFFFFF)
        return lax.bitcast_convert_type(bits, F32)

    def bisect(_, carry):
        lo, hi = carry
        mid = (lo >> 1) + (hi >> 1) + (lo & hi & 1)
        ok = count(key_to_f32(mid), False) >= k_sel
        return jnp.where(ok, mid, lo), jnp.where(ok, hi, mid)

    lo, _ = lax.fori_loop(
        0, 32, bisect,
        (jnp.full((tq, 1), KEY_NEG_INF, jnp.int32), jnp.full((tq, 1), KEY_POS_INF, jnp.int32)))
    thr = key_to_f32(lo)
    n_ties = k_sel - count(thr, True)

    for n in range(N_KV_HEADS):
        for g in range(group):
            hh = n * group + g
            qg_ref[n, g * tq:(g + 1) * tq, :] = q_ref[0, :, hh * HEAD_DIM:(hh + 1) * HEAD_DIM]
    m_ref[...] = jnp.full(m_ref.shape, NEG_BIG, F32)
    l_ref[...] = jnp.zeros(l_ref.shape, F32)
    acc_ref[...] = jnp.zeros(acc_ref.shape, F32)
    ce_ref[...] = jnp.zeros(ce_ref.shape, F32)

    row_i = lax.broadcasted_iota(jnp.int32, (KEY_BLOCK, KEY_BLOCK), 0)
    col_i = lax.broadcasted_iota(jnp.int32, (KEY_BLOCK, KEY_BLOCK), 1)
    upper = jnp.where(row_i < col_i, 1.0, 0.0).astype(BF16)
    ones = jnp.ones((KEY_BLOCK, KEY_BLOCK), BF16)

    def block(j, bias_sel):
        k0 = pl.multiple_of(j * KEY_BLOCK, KEY_BLOCK)
        s = s_ref[j]
        eq = s == thr
        e = jnp.where(eq, 1.0, 0.0).astype(BF16)
        before = _dot(e, upper) + ce_ref[...]
        ce_ref[...] += _dot(e, ones)
        keep = ((s > thr) | (eq & (before < n_ties))) & (s > -jnp.inf)
        drop_g = jnp.concatenate([jnp.where(keep, 0.0, NEG_BIG)] * group, axis=0)
        for n in range(N_KV_HEADS):
            kn = k_ref[0, pl.ds(k0, KEY_BLOCK), n * HEAD_DIM:(n + 1) * HEAD_DIM]
            vn = v_ref[0, pl.ds(k0, KEY_BLOCK), n * HEAD_DIM:(n + 1) * HEAD_DIM]
            lg = _dot_nt(qg_ref[n], kn)
            if bias_sel is not None:
                lg = lg + jnp.concatenate(
                    [bt_ref[bias_sel, n * group + g, :tq, :] for g in range(group)], axis=0)
            lg = lg + drop_g
            m_old = m_ref[n]
            m_new = jnp.maximum(m_old, jnp.max(lg, axis=1, keepdims=True))
            alpha = jnp.exp(m_old - m_new)
            p = jnp.exp(lg - m_new)
            l_ref[n] = alpha * l_ref[n] + jnp.sum(p, axis=1, keepdims=True)
            acc_ref[n] = alpha * acc_ref[n] + _dot(p.astype(BF16), vn)
            m_ref[n] = m_new

    def far_body(j, carry):
        block(j, None)
        return carry

    lax.fori_loop(0, qs - 1, far_body, 0)

    @pl.when(qs >= 1)
    def _():
        block(qs - 1, 1)

    block(qs, 0)

    for n in range(N_KV_HEADS):
        on = acc_ref[n] / l_ref[n]
        for g in range(group):
            hh = n * group + g
            o_ref[0, :, hh * HEAD_DIM:(hh + 1) * HEAD_DIM] = on[g * tq:(g + 1) * tq].astype(BF16)


def _sparse_attn(q, qi, wi, kb, vb, kib, bias_tiles, *, tq, q_off, n_keys, k_sel):
    b, t, dq = q.shape
    lp = kb.shape[1]
    n_heads = dq // HEAD_DIM
    group = n_heads // N_KV_HEADS
    nq = t // tq
    assert t % tq == 0 and lp % KEY_BLOCK == 0 and q_off % KEY_BLOCK == 0
    assert tq == KEY_BLOCK or (nq == 1 and tq <= KEY_BLOCK), (tq, nq)
    assert q_off + t <= lp
    qblk = lambda n: pl.BlockSpec((1, tq, n), lambda bi, i: (bi, i, 0))
    kblk = lambda n: pl.BlockSpec((1, lp, n), lambda bi, i: (bi, 0, 0))
    return pl.pallas_call(
        functools.partial(_sparse_attn_kernel, tq=tq, q_off=q_off, n_keys=n_keys, k_sel=k_sel,
                          group=group),
        out_shape=jax.ShapeDtypeStruct((b, t, dq), BF16),
        grid=(b, nq),
        in_specs=[qblk(dq), qblk(qi.shape[2]), qblk(LANES), kblk(kb.shape[2]), kblk(vb.shape[2]),
                  kblk(kib.shape[2]), _const_spec(bias_tiles.shape)],
        out_specs=qblk(dq),
        scratch_shapes=[
            pltpu.VMEM((lp // KEY_BLOCK, tq, KEY_BLOCK), F32),
            pltpu.VMEM((N_KV_HEADS, group * tq, HEAD_DIM), BF16),
            pltpu.VMEM((N_KV_HEADS, group * tq, 1), F32),
            pltpu.VMEM((N_KV_HEADS, group * tq, 1), F32),
            pltpu.VMEM((N_KV_HEADS, group * tq, HEAD_DIM), F32),
            pltpu.VMEM((tq, KEY_BLOCK), F32),
        ],
        compiler_params=_params("parallel", "arbitrary"),
        name="sparse_attn",
    )(q, qi, wi, kb, vb, kib, bias_tiles)


def _matmul_res_kernel(x_ref, a_ref, w_ref, o_ref):
    o_ref[...] = x_ref[...] + _dot(a_ref[...], w_ref[...])


def _matmul_res(x2, a2, w):
    m, d = x2.shape
    ka = a2.shape[1]
    tm = _row_tile(m, 512)
    return pl.pallas_call(
        _matmul_res_kernel,
        out_shape=jax.ShapeDtypeStruct((m, d), F32),
        grid=(m // tm,),
        in_specs=[pl.BlockSpec((tm, d), lambda i: (i, 0)), pl.BlockSpec((tm, ka), lambda i: (i, 0)),
                  _const_spec((ka, d))],
        out_specs=pl.BlockSpec((tm, d), lambda i: (i, 0)),
        compiler_params=_params("parallel"),
        name="attn_out_proj",
    )(x2, a2, w.astype(BF16))


def _mem_kv_kernel(x_ref, g_ref, w_ref, k_ref, v_ref, *, d):
    h = _rms(x_ref[...], g_ref[...]).astype(BF16)
    k_ref[...] = _dot(h, w_ref[:, :d])
    v_ref[...] = _dot(h, w_ref[:, d:])


def _mem_kv(mem2, g, w_kv):
    m, d = mem2.shape
    tm = _row_tile(m, 512)
    row = pl.BlockSpec((tm, d), lambda i: (i, 0))
    return pl.pallas_call(
        functools.partial(_mem_kv_kernel, d=d),
        out_shape=[jax.ShapeDtypeStruct((m, d), F32)] * 2,
        grid=(m // tm,),
        in_specs=[row, _const_spec((1, d)), _const_spec((d, 2 * d))],
        out_specs=[row, row],
        compiler_params=_params("parallel"),
        name="mem_kv",
    )(mem2, g.reshape(1, d), w_kv.astype(BF16))


def _mem_attn_kernel(x_ref, g_ref, wq_ref, mk_ref, mv_ref, wo_ref, o_ref, *, hd):
    x = x_ref[...]
    h = _rms(x, g_ref[...]).astype(BF16)
    q = (_dot(h, wq_ref[...]) * (hd ** -0.5)).astype(BF16)
    heads = []
    for a in range(MEM_HEADS):
        cols = slice(a * hd, (a + 1) * hd)
        lg = _dot_nt(q[:, cols], mk_ref[0, :, cols].astype(BF16))
        p = jnp.exp(lg - jnp.max(lg, axis=1, keepdims=True))
        p = (p / jnp.sum(p, axis=1, keepdims=True)).astype(BF16)
        heads.append(_dot(p, mv_ref[0, :, cols].astype(BF16)).astype(BF16))
    o_ref[...] = x + _dot(jnp.concatenate(heads, axis=1), wo_ref[...])


def _mem_attn(x2, g, w_q, mk, mv, w_o, rows_per_batch):
    m, d = x2.shape
    n_mem = mk.shape[1]
    tm = _row_tile(rows_per_batch, 512)
    per = rows_per_batch // tm
    row = pl.BlockSpec((tm, d), lambda i: (i, 0))
    mem = pl.BlockSpec((1, n_mem, d), lambda i: (i // per, 0, 0))
    return pl.pallas_call(
        functools.partial(_mem_attn_kernel, hd=d // MEM_HEADS),
        out_shape=jax.ShapeDtypeStruct((m, d), F32),
        grid=(m // tm,),
        in_specs=[row, _const_spec((1, d)), _const_spec((d, d)), mem, mem, _const_spec((d, d))],
        out_specs=row,
        compiler_params=_params("parallel"),
        name="mem_attn",
    )(x2, g.reshape(1, d), w_q.astype(BF16), mk, mv, w_o.astype(BF16))


def _mlp_kernel(*refs, n_chunks, chunk, final):
    if final:
        x_ref, g_ref, w1_ref, w2_ref, gf_ref, o_ref = refs
    else:
        x_ref, g_ref, w1_ref, w2_ref, o_ref = refs
    x = x_ref[...]
    h = _rms(x, g_ref[...]).astype(BF16)
    acc = x
    for c in range(n_chunks):
        a = jnp.maximum(_dot(h, w1_ref[:, c * chunk:(c + 1) * chunk]), 0.0)
        acc = acc + _dot((a * a).astype(BF16), w2_ref[c * chunk:(c + 1) * chunk, :])
    if final:
        acc = _rms(acc, gf_ref[...])
    o_ref[...] = acc


def _mlp(x2, g, w1, w2, g_final=None):
    m, d = x2.shape
    dff = w1.shape[1]
    chunk = min(dff, 1024)
    tm = _row_tile(m, 512)
    row = pl.BlockSpec((tm, d), lambda i: (i, 0))
    final = g_final is not None
    in_specs = [row, _const_spec((1, d)), _const_spec((d, dff)), _const_spec((dff, d))]
    args = [x2, g.reshape(1, d), w1.astype(BF16), w2.astype(BF16)]
    if final:
        in_specs.append(_const_spec((1, d)))
        args.append(g_final.reshape(1, d))
    return pl.pallas_call(
        functools.partial(_mlp_kernel, n_chunks=dff // chunk, chunk=chunk, final=final),
        out_shape=jax.ShapeDtypeStruct((m, d), F32),
        grid=(m // tm,),
        in_specs=in_specs,
        out_specs=row,
        compiler_params=_params("parallel"),
        name="mlp",
    )(*args)


def _conv_glu_kernel(x_ref, g_ref, w_ref, b_ref, u_ref, *, d):
    h = _rms(x_ref[...], g_ref[...]).astype(BF16)
    a = _dot(h, w_ref[:, :d]) + b_ref[:, :d]
    gate = _dot(h, w_ref[:, d:]) + b_ref[:, d:]
    u_ref[...] = a * (1.0 / (1.0 + jnp.exp(-gate)))


def _conv_glu(x2, g, w_pw1, b_pw1):
    m, d = x2.shape
    tm = _row_tile(m, 512)
    row = pl.BlockSpec((tm, d), lambda i: (i, 0))
    return pl.pallas_call(
        functools.partial(_conv_glu_kernel, d=d),
        out_shape=jax.ShapeDtypeStruct((m, d), F32),
        grid=(m // tm,),
        in_specs=[row, _const_spec((1, d)), _const_spec((d, 2 * d)), _const_spec((1, 2 * d))],
        out_specs=row,
        compiler_params=_params("parallel"),
        name="conv_glu",
    )(x2, g.reshape(1, d), w_pw1.astype(BF16), b_pw1.reshape(1, 2 * d))


def _conv_rest_kernel(x_ref, u_ref, prev_ref, init_ref, wdw_ref, bdw_ref, lng_ref, lnb_ref,
                      w2_ref, b2_ref, o_ref, ext_ref, y_ref, *, tm, rc, lc):
    t = pl.program_id(1)
    d = u_ref.shape[2]
    pad = CONV_WIDTH - 1

    @pl.when(t == 0)
    def _():
        ext_ref[0:HALO, :] = init_ref[0]

    @pl.when(t > 0)
    def _():
        ext_ref[0:HALO, :] = prev_ref[0]

    ext_ref[HALO:HALO + tm, :] = u_ref[0]

    for r0 in range(0, tm, rc):
        for c0 in range(0, d, lc):
            cols = slice(c0, c0 + lc)
            y = jnp.broadcast_to(bdw_ref[:, cols], (rc, lc))
            for s in range(SUBLANES):
                taps = [w for w in range(s, CONV_WIDTH, SUBLANES)]
                span = rc + taps[-1] - s
                z = ext_ref[pl.ds(r0 + HALO - pad + s, span), cols]
                for w in taps:
                    y = y + z[w - s:w - s + rc] * wdw_ref[w:w + 1, cols]
            y_ref[r0:r0 + rc, cols] = y

    y = y_ref[...]
    mu = jnp.mean(y, axis=-1, keepdims=True)
    yc = y - mu
    var = jnp.mean(yc * yc, axis=-1, keepdims=True)
    yn = yc * lax.rsqrt(var + EPS) * lng_ref[...] + lnb_ref[...]
    act = (yn * (1.0 / (1.0 + jnp.exp(-yn)))).astype(BF16)
    o_ref[0] = x_ref[0] + _dot(act, w2_ref[...]) + b2_ref[...]


def _conv_rest(x3, u3, init, w_dw, b_dw, ln_g, ln_b, w_pw2, b_pw2):
    b, t, d = x3.shape
    tm = _row_tile(t, 256)
    rc = min(tm, 64)
    lc = min(d, 256)
    assert tm % HALO == 0 and tm % rc == 0 and d % lc == 0
    per = tm // HALO
    tile = pl.BlockSpec((1, tm, d), lambda bi, ti: (bi, ti, 0))
    prev = pl.BlockSpec((1, HALO, d), lambda bi, ti: (bi, jnp.maximum(ti * per - 1, 0), 0))
    first = pl.BlockSpec((1, HALO, d), lambda bi, ti: (bi, 0, 0))
    vec = _const_spec((1, d))
    wdw = jnp.pad(w_dw, ((0, HALO - CONV_WIDTH), (0, 0)))
    return pl.pallas_call(
        functools.partial(_conv_rest_kernel, tm=tm, rc=rc, lc=lc),
        out_shape=jax.ShapeDtypeStruct((b, t, d), F32),
        grid=(b, t // tm),
        in_specs=[tile, tile, prev, first, _const_spec((HALO, d)), vec, vec, vec,
                  _const_spec((d, d)), vec],
        out_specs=tile,
        scratch_shapes=[pltpu.VMEM((HALO + tm, d), F32), pltpu.VMEM((tm, d), F32)],
        compiler_params=_params("parallel", "arbitrary"),
        name="conv_rest",
    )(x3, u3, u3, init, wdw, b_dw.reshape(1, d), ln_g.reshape(1, d), ln_b.reshape(1, d),
      w_pw2.astype(BF16), b_pw2.reshape(1, d))


def _mixer_attn(x3, g, w_in, w_out, bias_tiles, cache=None):
    b, t, d = x3.shape
    x2 = x3.reshape(b * t, d)
    q, k, v, ki, kb, vb, kib, qi, wi = _attn_proj(x2, g, w_in)
    r3 = lambda a: a.reshape(b, t, a.shape[-1])
    kb, vb, kib = r3(kb), r3(vb), r3(kib)
    past = 0
    if cache is not None:
        ck, cv, cki = cache
        past = ck.shape[1]
        kb = jnp.concatenate([ck.reshape(b, past, -1).astype(BF16), kb], axis=1)
        vb = jnp.concatenate([cv.reshape(b, past, -1).astype(BF16), vb], axis=1)
        kib = jnp.concatenate([cki.astype(BF16), kib], axis=1)
    n_keys = past + t
    lp = -(-n_keys // KEY_BLOCK) * KEY_BLOCK
    if lp != n_keys:
        padk = lambda a: jnp.pad(a, ((0, 0), (0, lp - n_keys), (0, 0)))
        kb, vb, kib = padk(kb), padk(vb), padk(kib)
    k_sel = min(TOPK_MAX, n_keys // 4)
    o = _sparse_attn(r3(q), r3(qi), r3(wi), kb, vb, kib, bias_tiles,
                     tq=min(t, KEY_BLOCK), q_off=past, n_keys=n_keys, k_sel=k_sel)
    x2 = _matmul_res(x2, o.reshape(b * t, d), w_out)
    return (x2.reshape(b, t, d), k.reshape(b, t, N_KV_HEADS, HEAD_DIM),
            v.reshape(b, t, N_KV_HEADS, HEAD_DIM), ki.reshape(b, t, IDX_DIM))


def _mixer_conv(x3, g, w_pw1, b_pw1, w_dw, b_dw, ln_g, ln_b, w_pw2, b_pw2, state=None):
    b, t, d = x3.shape
    pad = CONV_WIDTH - 1
    u3 = _conv_glu(x3.reshape(b * t, d), g, w_pw1, b_pw1).reshape(b, t, d)
    if state is None:
        init = jnp.zeros((b, HALO, d), F32)
        tail = u3[:, -pad:] if t >= pad else jnp.pad(u3, ((0, 0), (pad - t, 0), (0, 0)))
    else:
        init = jnp.pad(state.astype(F32), ((0, 0), (HALO - pad, 0), (0, 0)))
        tail = jnp.concatenate([state.astype(F32), u3], axis=1)[:, -pad:]
    x3 = _conv_rest(x3, u3, init, w_dw, b_dw, ln_g, ln_b, w_pw2, b_pw2)
    return x3, tail


def kernel(x_prompt, x_sample, cache_attn_k, cache_attn_v, cache_attn_kidx, state_conv, cache_mem_k, cache_mem_v, mem_prompt, rel_bias, g_mix, w_in_attn, w_out_attn, w_pw1, b_pw1, w_dw, b_dw, ln_g, ln_b, w_pw2, b_pw2, g_mem_q, g_mem_src, w_mem_q, w_mem_kv, w_mem_o, g_mlp, w_mlp1, w_mlp2, g_final):
    depth = g_mix.shape[0]
    bp, tp, d = x_prompt.shape
    bs, ts, _ = x_sample.shape
    n_mem = mem_prompt.shape[1]
    mem_hd = d // MEM_HEADS
    bias_tiles = _bias_tiles(rel_bias)
    xp, xs = x_prompt, x_sample
    kp_l, vp_l, kip_l, ks_l, vs_l, kis_l = [], [], [], [], [], []
    convp_l, convs_l, memk_l, memv_l = [], [], [], []
    for i in range(depth):
        j = i // 2
        if i % 2 == 0:
            xp, kp, vp, kip = _mixer_attn(xp, g_mix[i], w_in_attn[j], w_out_attn[j], bias_tiles)
            xs, ks, vs, kis = _mixer_attn(
                xs, g_mix[i], w_in_attn[j], w_out_attn[j], bias_tiles,
                cache=(cache_attn_k[j], cache_attn_v[j], cache_attn_kidx[j]))
            kp_l.append(kp); vp_l.append(vp); kip_l.append(kip)
            ks_l.append(ks); vs_l.append(vs); kis_l.append(kis)
        else:
            cw = (w_pw1[j], b_pw1[j], w_dw[j], b_dw[j], ln_g[j], ln_b[j], w_pw2[j], b_pw2[j])
            xp, cp = _mixer_conv(xp, g_mix[i], *cw)
            xs, cs = _mixer_conv(xs, g_mix[i], *cw, state=state_conv[j])
            convp_l.append(cp); convs_l.append(cs)
        mk, mv = _mem_kv(mem_prompt.reshape(bp * n_mem, d), g_mem_src[i], w_mem_kv[i])
        mk, mv = mk.reshape(bp, n_mem, d), mv.reshape(bp, n_mem, d)
        memk_l.append(mk.reshape(bp, n_mem, MEM_HEADS, mem_hd))
        memv_l.append(mv.reshape(bp, n_mem, MEM_HEADS, mem_hd))
        xp2 = _mem_attn(xp.reshape(bp * tp, d), g_mem_q[i], w_mem_q[i], mk, mv, w_mem_o[i], tp)
        xs2 = _mem_attn(xs.reshape(bs * ts, d), g_mem_q[i], w_mem_q[i],
                        cache_mem_k[i].reshape(bs, n_mem, d), cache_mem_v[i].reshape(bs, n_mem, d),
                        w_mem_o[i], ts)
        gf = g_final if i == depth - 1 else None
        xp = _mlp(xp2, g_mlp[i], w_mlp1[i], w_mlp2[i], gf).reshape(bp, tp, d)
        xs = _mlp(xs2, g_mlp[i], w_mlp1[i], w_mlp2[i], gf).reshape(bs, ts, d)
    return (xp, xs, jnp.stack(kp_l), jnp.stack(vp_l), jnp.stack(kip_l), jnp.stack(convp_l),
            jnp.stack(memk_l), jnp.stack(memv_l), jnp.stack(ks_l), jnp.stack(vs_l),
            jnp.stack(kis_l), jnp.stack(convs_l))
```

```python
import functools
import math

import jax
import jax.numpy as jnp
from jax import lax
from jax.experimental import pallas as pl
from jax.experimental.pallas import tpu as pltpu

CHUNK = 64
HEAD_DIM = 64
N_KV_HEADS = 4
IDX_HEADS = 8
IDX_DIM = 64
TOPK_MAX = 256
N_BUCKETS = 32
MAX_DISTANCE = 128
CONV_WIDTH = 31
MEM_HEADS = 4
EPS = 1e-6

LANES = 128
SUBLANES = 8
KEY_BLOCK = 128
HEADS_PER_TILE = 2
V_AUG_ROWS = HEAD_DIM + 16
LOG2E = math.log2(math.e)
HALO = 32
VMEM_LIMIT = 56 * 1024 * 1024

NEG_BIG = -1e30
F32 = jnp.float32
BF16 = jnp.bfloat16

KEY_NEG_INF = -2139095041
KEY_POS_INF = 2139095040


def _const_spec(shape):
    nd = len(shape)
    return pl.BlockSpec(shape, lambda *_: (0,) * nd, pipeline_mode=pl.Buffered(1))


def _params(*sem):
    return pltpu.CompilerParams(dimension_semantics=sem, vmem_limit_bytes=VMEM_LIMIT)


def _rms(x, g):
    ms = jnp.mean(x * x, axis=-1, keepdims=True)
    return x * lax.rsqrt(ms + EPS) * g


def _dot(a, b):
    return jnp.dot(a, b, preferred_element_type=F32)


def _dot_nt(a, b):
    return lax.dot_general(a, b, (((1,), (1,)), ((), ())), preferred_element_type=F32)


def _row_tile(m, pref):
    t = min(m, pref)
    assert m % t == 0, (m, t)
    return t


def _bias_tiles_kernel(tab_ref, bt_ref, *, n_heads):
    nb = N_BUCKETS // 2
    max_exact = nb // 2
    c = lax.broadcasted_iota(jnp.int32, (KEY_BLOCK, KEY_BLOCK), 0)
    r = lax.broadcasted_iota(jnp.int32, (KEY_BLOCK, KEY_BLOCK), 1)
    for d in range(2):
        rel = c - r - d * KEY_BLOCK
        n = jnp.abs(rel)
        nf = jnp.maximum(n, 1).astype(F32)
        large = max_exact + (jnp.log(nf / max_exact) / math.log(MAX_DISTANCE / max_exact)
                             * (nb - max_exact)).astype(jnp.int32)
        large = jnp.minimum(large, nb - 1)
        bucket = jnp.where(rel > 0, nb, 0) + jnp.where(n < max_exact, n, large)
        for h in range(n_heads):
            acc = jnp.zeros((KEY_BLOCK, KEY_BLOCK), F32)
            for b in range(N_BUCKETS):
                acc = jnp.where(bucket == b, tab_ref[b, h], acc)
            g = h % HEADS_PER_TILE
            bt_ref[d, h // HEADS_PER_TILE, :, g * KEY_BLOCK:(g + 1) * KEY_BLOCK] = (
                (acc - tab_ref[nb - 1, h]) * LOG2E)


def _bias_tiles(rel_bias):
    n_heads = rel_bias.shape[1]
    assert n_heads % HEADS_PER_TILE == 0
    return pl.pallas_call(
        functools.partial(_bias_tiles_kernel, n_heads=n_heads),
        out_shape=jax.ShapeDtypeStruct(
            (2, n_heads // HEADS_PER_TILE, KEY_BLOCK, HEADS_PER_TILE * KEY_BLOCK), F32),
        in_specs=[pl.BlockSpec(memory_space=pltpu.SMEM)],
        out_specs=pl.BlockSpec(memory_space=pltpu.VMEM),
        name="bias_tiles",
    )(rel_bias)


def _attn_proj_kernel(x_ref, g_ref, w_ref, q_ref, k_ref, v_ref, ki_ref, kb_ref, vb_ref,
                      kib_ref, qi_ref, wi_ref, *, dq, dkv, dqi):
    h = _rms(x_ref[...], g_ref[...]).astype(BF16)
    o = 0
    q_ref[...] = (_dot(h, w_ref[:, o:o + dq]) * (HEAD_DIM ** -0.5)).astype(BF16)
    o += dq
    k = _dot(h, w_ref[:, o:o + dkv])
    k_ref[...] = k
    kb_ref[...] = k.astype(BF16)
    o += dkv
    v = _dot(h, w_ref[:, o:o + dkv])
    v_ref[...] = v
    vb_ref[...] = v.astype(BF16)
    o += dkv
    qi_ref[...] = (_dot(h, w_ref[:, o:o + dqi]) * (IDX_DIM ** -0.5)).astype(BF16)
    o += dqi
    ki = _dot(h, w_ref[:, o:o + LANES])[:, :IDX_DIM]
    ki_ref[...] = ki
    kib_ref[...] = ki.astype(BF16)
    o += LANES
    wi_ref[...] = _dot(h, w_ref[:, o:o + LANES]) * (IDX_HEADS ** -0.5)


def _attn_proj(x2, g, w_in):
    m, d = x2.shape
    dq = d
    dkv = N_KV_HEADS * HEAD_DIM
    dqi = IDX_HEADS * IDX_DIM
    base = dq + 2 * dkv + dqi
    w_main = w_in[:, :base]
    w_ki = jnp.pad(w_in[:, base:base + IDX_DIM], ((0, 0), (0, LANES - IDX_DIM)))
    w_wi = jnp.pad(w_in[:, base + IDX_DIM:], ((0, 0), (0, LANES - IDX_HEADS)))
    w = jnp.concatenate([w_main, w_ki, w_wi], axis=1).astype(BF16)
    tm = _row_tile(m, 512)
    row = lambda n: pl.BlockSpec((tm, n), lambda i: (i, 0))
    outs = [(dq, BF16), (dkv, F32), (dkv, F32), (IDX_DIM, F32), (dkv, BF16), (dkv, BF16),
            (IDX_DIM, BF16), (dqi, BF16), (LANES, F32)]
    return pl.pallas_call(
        functools.partial(_attn_proj_kernel, dq=dq, dkv=dkv, dqi=dqi),
        out_shape=[jax.ShapeDtypeStruct((m, n), dt) for n, dt in outs],
        grid=(m // tm,),
        in_specs=[row(d), _const_spec((1, d)), _const_spec(w.shape)],
        out_specs=[row(n) for n, _ in outs],
        compiler_params=_params("parallel"),
        name="attn_proj",
    )(x2, g.reshape(1, d), w)


def _sparse_attn_kernel(q_ref, qi_ref, wi_ref, k_ref, vt_ref, ki_ref, bt_ref, o_ref,
                        s_ref, qgt_ref, qit_ref, ot_ref, lg_ref, p_ref, al_ref, *state_refs,
                        q_off, n_keys, k_sel, group):
    tq = KEY_BLOCK
    hpt = HEADS_PER_TILE
    n_tiles = qgt_ref.shape[0]
    m_refs, acc_refs = state_refs[:n_tiles], state_refs[n_tiles:]
    i = pl.program_id(1)
    qs = q_off // KEY_BLOCK + i
    n_kb = qs + 1
    q_start = q_off + i * tq

    qt = q_ref[0].astype(F32).T
    for t in range(n_tiles):
        qgt_ref[t] = jnp.concatenate(
            [qt[(t * hpt + g) * HEAD_DIM:(t * hpt + g + 1) * HEAD_DIM, :] for g in range(hpt)],
            axis=1).astype(BF16)
    qit_ref[...] = qi_ref[0].astype(F32).T.astype(BF16)
    wit = wi_ref[0].T

    qpos = q_start + lax.broadcasted_iota(jnp.int32, (1, tq), 1)
    lim = jnp.minimum((qpos // CHUNK + 1) * CHUNK, n_keys)
    kidx = lax.broadcasted_iota(jnp.int32, (KEY_BLOCK, tq), 0)

    def score_body(j, carry):
        k0 = pl.multiple_of(j * KEY_BLOCK, KEY_BLOCK)
        kib = ki_ref[0, pl.ds(k0, KEY_BLOCK), :]
        acc = jnp.zeros((KEY_BLOCK, tq), F32)
        for h in range(IDX_HEADS):
            d = _dot(kib, qit_ref[h * IDX_DIM:(h + 1) * IDX_DIM, :])
            acc = acc + wit[h:h + 1, :] * jnp.maximum(d, 0.0)
        s_ref[j] = jnp.where(kidx + k0 < lim, acc, -jnp.inf)
        return carry

    lax.fori_loop(0, n_kb, score_body, 0)
    s_ref[n_kb] = jnp.full((KEY_BLOCK, tq), -jnp.inf, F32)

    def count(cand, strict):
        cb = jnp.broadcast_to(cand, (KEY_BLOCK, tq))
        hit = (lambda s: s > cb) if strict else (lambda s: s >= cb)

        def body(p, c):
            c = c + jnp.where(hit(s_ref[2 * p]), 1.0, 0.0)
            return c + jnp.where(hit(s_ref[2 * p + 1]), 1.0, 0.0)
        c = lax.fori_loop(0, (n_kb + 1) // 2, body, jnp.zeros((KEY_BLOCK, tq), F32))
        return jnp.sum(c, axis=0, keepdims=True)

    def key_to_f32(key):
        bits = key ^ ((key >> 31) & 0x7FFFFFFF)
        return lax.bitcast_convert_type(bits, F32)

    def bisect(_, carry):
        lo, hi = carry
        mid = (lo >> 1) + (hi >> 1) + (lo & hi & 1)
        ok = count(key_to_f32(mid), False) >= k_sel
        return jnp.where(ok, mid, lo), jnp.where(ok, hi, mid)

    lo, _ = lax.fori_loop(
        0, 32, bisect,
        (jnp.full((1, tq), KEY_NEG_INF, jnp.int32), jnp.full((1, tq), KEY_POS_INF, jnp.int32)))
    thr = key_to_f32(lo)

    surplus = count(thr, False) - k_sel

    @pl.when(jnp.max(surplus) > 0.0)
    def _():
        n_ties = k_sel - count(thr, True)
        row_i = lax.broadcasted_iota(jnp.int32, (KEY_BLOCK, KEY_BLOCK), 0)
        col_i = lax.broadcasted_iota(jnp.int32, (KEY_BLOCK, KEY_BLOCK), 1)
        lower = jnp.where(col_i < row_i, 1.0, 0.0).astype(BF16)
        ones = jnp.ones((KEY_BLOCK, KEY_BLOCK), BF16)

        def tie_body(j, seen):
            s = s_ref[j]
            eq = s == thr
            e = jnp.where(eq, 1.0, 0.0).astype(BF16)
            before = _dot(lower, e) + seen
            s_ref[j] = jnp.where(eq & (before >= n_ties), -jnp.inf, s)
            return seen + _dot(ones, e)

        lax.fori_loop(0, n_kb, tie_body, jnp.zeros((KEY_BLOCK, tq), F32))

    thr_sel = jnp.broadcast_to(jnp.maximum(thr, jnp.finfo(F32).min), (KEY_BLOCK, tq))

    for t in range(n_tiles):
        m_refs[t][...] = jnp.full(m_refs[t].shape, NEG_BIG, F32)
        acc_refs[t][...] = jnp.zeros(acc_refs[t].shape, F32)
    p_ref[...] = jnp.zeros(p_ref.shape, BF16)
    al_ref[...] = jnp.ones(al_ref.shape, F32)
    last_kv = k_ref.shape[1] // KEY_BLOCK - 1

    def mask_of(js):
        return jnp.concatenate([jnp.where(s_ref[js] >= thr_sel, 0.0, NEG_BIG)] * hpt, axis=1)

    def logits(jkv, mask, bias_sel, slot, t):
        n = (t * hpt) // group
        k0 = pl.multiple_of(jkv * KEY_BLOCK, KEY_BLOCK)
        kn = k_ref[0, pl.ds(k0, KEY_BLOCK), n * HEAD_DIM:(n + 1) * HEAD_DIM]
        lg = _dot(kn, qgt_ref[t]) * LOG2E + mask
        if bias_sel is not None:
            lg = lg + bt_ref[bias_sel, t]
        lg_ref[slot, t] = lg

    def softmax(slot, t):
        m_old = m_refs[t][...]
        m_new = jnp.maximum(m_old, jnp.max(lg_ref[slot, t], axis=0, keepdims=True))
        al_ref[slot, t] = jnp.exp2(m_old - m_new)
        p_ref[slot, t] = jnp.exp2(lg_ref[slot, t] - m_new).astype(BF16)
        m_refs[t][...] = m_new

    def update(jkv, slot, t):
        n = (t * hpt) // group
        acc_refs[t][...] = al_ref[slot, t] * acc_refs[t][...] + _dot(vt_ref[0, jkv, n], p_ref[slot, t])

    def second_of(j0):
        return jnp.where(j0 + 1 < qs - 1, j0 + 1, n_kb)

    mask0 = mask_of(0)
    for t in range(n_tiles):
        logits(0, mask0, None, 0, t)

    def far_body(pair, carry):
        j0 = 2 * pair
        j1 = second_of(j0)
        j2 = jnp.minimum(j0 + 2, last_kv)
        mask1, mask2 = mask_of(j1), mask_of(j2)
        for t in range(n_tiles):
            update(jnp.maximum(j0 - 2, 0), 0, t)
            logits(jnp.minimum(j1, last_kv), mask1, None, 1, t)
            softmax(0, t)
        for t in range(n_tiles):
            update(jnp.maximum(j0 - 1, 0), 1, t)
            logits(j2, mask2, None, 0, t)
            softmax(1, t)
        return carry

    n_far = qs // 2
    lax.fori_loop(0, n_far, far_body, 0)

    jl0 = jnp.maximum(2 * n_far - 2, 0)
    jl1 = jnp.minimum(second_of(jl0), last_kv)
    j_prev = jnp.where(qs >= 1, qs - 1, n_kb)
    mask_prev, mask_diag = mask_of(j_prev), mask_of(qs)
    for t in range(n_tiles):
        update(jl0, 0, t)
        logits(jnp.maximum(qs - 1, 0), mask_prev, 1, 0, t)
    for t in range(n_tiles):
        update(jl1, 1, t)
        logits(qs, mask_diag, 0, 1, t)
    for t in range(n_tiles):
        softmax(0, t)
    for t in range(n_tiles):
        softmax(1, t)
    for t in range(n_tiles):
        update(jnp.maximum(qs - 1, 0), 0, t)
        update(qs, 1, t)

    for t in range(n_tiles):
        on = acc_refs[t][0:HEAD_DIM, :] / acc_refs[t][HEAD_DIM:HEAD_DIM + 1, :]
        for g in range(hpt):
            hh = t * hpt + g
            ot_ref[hh * HEAD_DIM:(hh + 1) * HEAD_DIM, :] = on[:, g * tq:(g + 1) * tq]
    o_ref[0] = ot_ref[...].T.astype(BF16)


def _sparse_attn(q, qi, wi, kb, vb, kib, bias_tiles, *, q_off, n_keys, k_sel):
    b, t, dq = q.shape
    lp = kb.shape[1]
    dkv = kb.shape[2]
    tq = KEY_BLOCK
    n_heads = dq // HEAD_DIM
    group = n_heads // N_KV_HEADS
    hpt = HEADS_PER_TILE
    n_tiles = n_heads // hpt
    assert group % hpt == 0
    nkb = lp // KEY_BLOCK
    assert t % tq == 0 and lp % KEY_BLOCK == 0 and q_off % KEY_BLOCK == 0 and q_off + t <= lp
    vt = jnp.transpose(vb.reshape(b, nkb, KEY_BLOCK, N_KV_HEADS, HEAD_DIM), (0, 1, 3, 4, 2))
    vt = jnp.concatenate(
        [vt, jnp.ones((b, nkb, N_KV_HEADS, 1, KEY_BLOCK), BF16),
         jnp.zeros((b, nkb, N_KV_HEADS, V_AUG_ROWS - HEAD_DIM - 1, KEY_BLOCK), BF16)], axis=3)
    qblk = lambda n: pl.BlockSpec((1, tq, n), lambda bi, i: (bi, i, 0))
    kblk = lambda n: pl.BlockSpec((1, lp, n), lambda bi, i: (bi, 0, 0))
    return pl.pallas_call(
        functools.partial(_sparse_attn_kernel, q_off=q_off, n_keys=n_keys, k_sel=k_sel, group=group),
        out_shape=jax.ShapeDtypeStruct((b, t, dq), BF16),
        grid=(b, t // tq),
        in_specs=[qblk(dq), qblk(qi.shape[2]), qblk(LANES), kblk(dkv),
                  pl.BlockSpec((1, nkb, N_KV_HEADS, V_AUG_ROWS, KEY_BLOCK), lambda bi, i: (bi, 0, 0, 0, 0)),
                  kblk(kib.shape[2]), _const_spec(bias_tiles.shape)],
        out_specs=qblk(dq),
        scratch_shapes=[
            pltpu.VMEM((nkb + 1, KEY_BLOCK, tq), F32),
            pltpu.VMEM((n_tiles, HEAD_DIM, hpt * tq), BF16),
            pltpu.VMEM((IDX_HEADS * IDX_DIM, tq), BF16),
            pltpu.VMEM((dq, tq), F32),
            pltpu.VMEM((2, n_tiles, KEY_BLOCK, hpt * tq), F32),
            pltpu.VMEM((2, n_tiles, KEY_BLOCK, hpt * tq), BF16),
            pltpu.VMEM((2, n_tiles, 1, hpt * tq), F32),
        ] + [pltpu.VMEM((1, hpt * tq), F32)] * n_tiles
          + [pltpu.VMEM((V_AUG_ROWS, hpt * tq), F32)] * n_tiles,
        compiler_params=_params("parallel", "arbitrary"),
        name="sparse_attn",
    )(q, qi, wi, kb, vt, kib, bias_tiles)


def _matmul_res_kernel(x_ref, a_ref, w_ref, o_ref):
    o_ref[...] = x_ref[...] + _dot(a_ref[...], w_ref[...])


def _matmul_res(x2, a2, w):
    m, d = x2.shape
    ka = a2.shape[1]
    tm = _row_tile(m, 512)
    return pl.pallas_call(
        _matmul_res_kernel,
        out_shape=jax.ShapeDtypeStruct((m, d), F32),
        grid=(m // tm,),
        in_specs=[pl.BlockSpec((tm, d), lambda i: (i, 0)), pl.BlockSpec((tm, ka), lambda i: (i, 0)),
                  _const_spec((ka, d))],
        out_specs=pl.BlockSpec((tm, d), lambda i: (i, 0)),
        compiler_params=_params("parallel"),
        name="attn_out_proj",
    )(x2, a2, w.astype(BF16))


def _mem_kv_kernel(x_ref, g_ref, w_ref, k_ref, v_ref, *, d):
    h = _rms(x_ref[...], g_ref[...]).astype(BF16)
    k_ref[...] = _dot(h, w_ref[:, :d])
    v_ref[...] = _dot(h, w_ref[:, d:])


def _mem_kv(mem2, g, w_kv):
    m, d = mem2.shape
    tm = _row_tile(m, 512)
    row = pl.BlockSpec((tm, d), lambda i: (i, 0))
    return pl.pallas_call(
        functools.partial(_mem_kv_kernel, d=d),
        out_shape=[jax.ShapeDtypeStruct((m, d), F32)] * 2,
        grid=(m // tm,),
        in_specs=[row, _const_spec((1, d)), _const_spec((d, 2 * d))],
        out_specs=[row, row],
        compiler_params=_params("parallel"),
        name="mem_kv",
    )(mem2, g.reshape(1, d), w_kv.astype(BF16))


def _mem_attn_kernel(x_ref, g_ref, wq_ref, mk_ref, mv_ref, wo_ref, o_ref, *, hd):
    x = x_ref[...]
    h = _rms(x, g_ref[...]).astype(BF16)
    q = (_dot(h, wq_ref[...]) * (hd ** -0.5)).astype(BF16)
    heads = []
    for a in range(MEM_HEADS):
        cols = slice(a * hd, (a + 1) * hd)
        lg = _dot_nt(q[:, cols], mk_ref[0, :, cols].astype(BF16))
        p = jnp.exp(lg - jnp.max(lg, axis=1, keepdims=True))
        p = (p / jnp.sum(p, axis=1, keepdims=True)).astype(BF16)
        heads.append(_dot(p, mv_ref[0, :, cols].astype(BF16)).astype(BF16))
    o_ref[...] = x + _dot(jnp.concatenate(heads, axis=1), wo_ref[...])


def _mem_attn(x2, g, w_q, mk, mv, w_o, rows_per_batch):
    m, d = x2.shape
    n_mem = mk.shape[1]
    tm = _row_tile(rows_per_batch, 512)
    per = rows_per_batch // tm
    row = pl.BlockSpec((tm, d), lambda i: (i, 0))
    mem = pl.BlockSpec((1, n_mem, d), lambda i: (i // per, 0, 0))
    return pl.pallas_call(
        functools.partial(_mem_attn_kernel, hd=d // MEM_HEADS),
        out_shape=jax.ShapeDtypeStruct((m, d), F32),
        grid=(m // tm,),
        in_specs=[row, _const_spec((1, d)), _const_spec((d, d)), mem, mem, _const_spec((d, d))],
        out_specs=row,
        compiler_params=_params("parallel"),
        name="mem_attn",
    )(x2, g.reshape(1, d), w_q.astype(BF16), mk, mv, w_o.astype(BF16))


def _mlp_kernel(*refs, n_chunks, chunk, final):
    if final:
        x_ref, g_ref, w1_ref, w2_ref, gf_ref, o_ref = refs
    else:
        x_ref, g_ref, w1_ref, w2_ref, o_ref = refs
    x = x_ref[...]
    h = _rms(x, g_ref[...]).astype(BF16)
    acc = x
    for c in range(n_chunks):
        a = jnp.maximum(_dot(h, w1_ref[:, c * chunk:(c + 1) * chunk]), 0.0)
        acc = acc + _dot((a * a).astype(BF16), w2_ref[c * chunk:(c + 1) * chunk, :])
    if final:
        acc = _rms(acc, gf_ref[...])
    o_ref[...] = acc


def _mlp(x2, g, w1, w2, g_final=None):
    m, d = x2.shape
    dff = w1.shape[1]
    chunk = min(dff, 1024)
    tm = _row_tile(m, 512)
    row = pl.BlockSpec((tm, d), lambda i: (i, 0))
    final = g_final is not None
    in_specs = [row, _const_spec((1, d)), _const_spec((d, dff)), _const_spec((dff, d))]
    args = [x2, g.reshape(1, d), w1.astype(BF16), w2.astype(BF16)]
    if final:
        in_specs.append(_const_spec((1, d)))
        args.append(g_final.reshape(1, d))
    return pl.pallas_call(
        functools.partial(_mlp_kernel, n_chunks=dff // chunk, chunk=chunk, final=final),
        out_shape=jax.ShapeDtypeStruct((m, d), F32),
        grid=(m // tm,),
        in_specs=in_specs,
        out_specs=row,
        compiler_params=_params("parallel"),
        name="mlp",
    )(*args)


def _conv_glu_kernel(x_ref, g_ref, w_ref, b_ref, u_ref, *, d):
    h = _rms(x_ref[...], g_ref[...]).astype(BF16)
    a = _dot(h, w_ref[:, :d]) + b_ref[:, :d]
    gate = _dot(h, w_ref[:, d:]) + b_ref[:, d:]
    u_ref[...] = a * (1.0 / (1.0 + jnp.exp(-gate)))


def _conv_glu(x2, g, w_pw1, b_pw1):
    m, d = x2.shape
    tm = _row_tile(m, 512)
    row = pl.BlockSpec((tm, d), lambda i: (i, 0))
    return pl.pallas_call(
        functools.partial(_conv_glu_kernel, d=d),
        out_shape=jax.ShapeDtypeStruct((m, d), F32),
        grid=(m // tm,),
        in_specs=[row, _const_spec((1, d)), _const_spec((d, 2 * d)), _const_spec((1, 2 * d))],
        out_specs=row,
        compiler_params=_params("parallel"),
        name="conv_glu",
    )(x2, g.reshape(1, d), w_pw1.astype(BF16), b_pw1.reshape(1, 2 * d))


def _conv_rest_kernel(x_ref, u_ref, prev_ref, init_ref, wdw_ref, bdw_ref, lng_ref, lnb_ref,
                      w2_ref, b2_ref, o_ref, ext_ref, y_ref, *, tm, rc, lc):
    t = pl.program_id(1)
    d = u_ref.shape[2]
    pad = CONV_WIDTH - 1

    @pl.when(t == 0)
    def _():
        ext_ref[0:HALO, :] = init_ref[0]

    @pl.when(t > 0)
    def _():
        ext_ref[0:HALO, :] = prev_ref[0]

    ext_ref[HALO:HALO + tm, :] = u_ref[0]

    for r0 in range(0, tm, rc):
        for c0 in range(0, d, lc):
            cols = slice(c0, c0 + lc)
            y = jnp.broadcast_to(bdw_ref[:, cols], (rc, lc))
            for s in range(SUBLANES):
                taps = [w for w in range(s, CONV_WIDTH, SUBLANES)]
                span = rc + taps[-1] - s
                z = ext_ref[pl.ds(r0 + HALO - pad + s, span), cols]
                for w in taps:
                    y = y + z[w - s:w - s + rc] * wdw_ref[w:w + 1, cols]
            y_ref[r0:r0 + rc, cols] = y

    y = y_ref[...]
    mu = jnp.mean(y, axis=-1, keepdims=True)
    yc = y - mu
    var = jnp.mean(yc * yc, axis=-1, keepdims=True)
    yn = yc * lax.rsqrt(var + EPS) * lng_ref[...] + lnb_ref[...]
    act = (yn * (1.0 / (1.0 + jnp.exp(-yn)))).astype(BF16)
    o_ref[0] = x_ref[0] + _dot(act, w2_ref[...]) + b2_ref[...]


def _conv_rest(x3, u3, init, w_dw, b_dw, ln_g, ln_b, w_pw2, b_pw2):
    b, t, d = x3.shape
    tm = _row_tile(t, 256)
    rc = min(tm, 64)
    lc = min(d, 256)
    assert tm % HALO == 0 and tm % rc == 0 and d % lc == 0
    per = tm // HALO
    tile = pl.BlockSpec((1, tm, d), lambda bi, ti: (bi, ti, 0))
    prev = pl.BlockSpec((1, HALO, d), lambda bi, ti: (bi, jnp.maximum(ti * per - 1, 0), 0))
    first = pl.BlockSpec((1, HALO, d), lambda bi, ti: (bi, 0, 0))
    vec = _const_spec((1, d))
    wdw = jnp.pad(w_dw, ((0, HALO - CONV_WIDTH), (0, 0)))
    return pl.pallas_call(
        functools.partial(_conv_rest_kernel, tm=tm, rc=rc, lc=lc),
        out_shape=jax.ShapeDtypeStruct((b, t, d), F32),
        grid=(b, t // tm),
        in_specs=[tile, tile, prev, first, _const_spec((HALO, d)), vec, vec, vec,
                  _const_spec((d, d)), vec],
        out_specs=tile,
        scratch_shapes=[pltpu.VMEM((HALO + tm, d), F32), pltpu.VMEM((tm, d), F32)],
        compiler_params=_params("parallel", "arbitrary"),
        name="conv_rest",
    )(x3, u3, u3, init, wdw, b_dw.reshape(1, d), ln_g.reshape(1, d), ln_b.reshape(1, d),
      w_pw2.astype(BF16), b_pw2.reshape(1, d))


def _mixer_attn(x3, g, w_in, w_out, bias_tiles, cache=None):
    b, t, d = x3.shape
    x2 = x3.reshape(b * t, d)
    q, k, v, ki, kb, vb, kib, qi, wi = _attn_proj(x2, g, w_in)
    r3 = lambda a: a.reshape(b, t, a.shape[-1])
    kb, vb, kib = r3(kb), r3(vb), r3(kib)
    past = 0
    if cache is not None:
        ck, cv, cki = cache
        past = ck.shape[1]
        kb = jnp.concatenate([ck.reshape(b, past, -1).astype(BF16), kb], axis=1)
        vb = jnp.concatenate([cv.reshape(b, past, -1).astype(BF16), vb], axis=1)
        kib = jnp.concatenate([cki.astype(BF16), kib], axis=1)
    n_keys = past + t
    k_sel = min(TOPK_MAX, n_keys // 4)
    tpad = -(-t // KEY_BLOCK) * KEY_BLOCK
    lp = max(-(-n_keys // KEY_BLOCK) * KEY_BLOCK, past + tpad)
    padt = lambda a, n: a if a.shape[1] == n else jnp.pad(a, ((0, 0), (0, n - a.shape[1]), (0, 0)))
    o = _sparse_attn(padt(r3(q), tpad), padt(r3(qi), tpad), padt(r3(wi), tpad),
                     padt(kb, lp), padt(vb, lp), padt(kib, lp), bias_tiles,
                     q_off=past, n_keys=n_keys, k_sel=k_sel)[:, :t]
    x2 = _matmul_res(x2, o.reshape(b * t, d), w_out)
    return (x2.reshape(b, t, d), k.reshape(b, t, N_KV_HEADS, HEAD_DIM),
            v.reshape(b, t, N_KV_HEADS, HEAD_DIM), ki.reshape(b, t, IDX_DIM))


def _mixer_conv(x3, g, w_pw1, b_pw1, w_dw, b_dw, ln_g, ln_b, w_pw2, b_pw2, state=None):
    b, t, d = x3.shape
    pad = CONV_WIDTH - 1
    u3 = _conv_glu(x3.reshape(b * t, d), g, w_pw1, b_pw1).reshape(b, t, d)
    if state is None:
        init = jnp.zeros((b, HALO, d), F32)
        tail = u3[:, -pad:] if t >= pad else jnp.pad(u3, ((0, 0), (pad - t, 0), (0, 0)))
    else:
        init = jnp.pad(state.astype(F32), ((0, 0), (HALO - pad, 0), (0, 0)))
        tail = jnp.concatenate([state.astype(F32), u3], axis=1)[:, -pad:]
    x3 = _conv_rest(x3, u3, init, w_dw, b_dw, ln_g, ln_b, w_pw2, b_pw2)
    return x3, tail


def kernel(x_prompt, x_sample, cache_attn_k, cache_attn_v, cache_attn_kidx, state_conv, cache_mem_k, cache_mem_v, mem_prompt, rel_bias, g_mix, w_in_attn, w_out_attn, w_pw1, b_pw1, w_dw, b_dw, ln_g, ln_b, w_pw2, b_pw2, g_mem_q, g_mem_src, w_mem_q, w_mem_kv, w_mem_o, g_mlp, w_mlp1, w_mlp2, g_final):
    depth = g_mix.shape[0]
    bp, tp, d = x_prompt.shape
    bs, ts, _ = x_sample.shape
    n_mem = mem_prompt.shape[1]
    mem_hd = d // MEM_HEADS
    bias_tiles = _bias_tiles(rel_bias)
    xp, xs = x_prompt, x_sample
    kp_l, vp_l, kip_l, ks_l, vs_l, kis_l = [], [], [], [], [], []
    convp_l, convs_l, memk_l, memv_l = [], [], [], []
    for i in range(depth):
        j = i // 2
        if i % 2 == 0:
            xp, kp, vp, kip = _mixer_attn(xp, g_mix[i], w_in_attn[j], w_out_attn[j], bias_tiles)
            xs, ks, vs, kis = _mixer_attn(
                xs, g_mix[i], w_in_attn[j], w_out_attn[j], bias_tiles,
                cache=(cache_attn_k[j], cache_attn_v[j], cache_attn_kidx[j]))
            kp_l.append(kp); vp_l.append(vp); kip_l.append(kip)
            ks_l.append(ks); vs_l.append(vs); kis_l.append(kis)
        else:
            cw = (w_pw1[j], b_pw1[j], w_dw[j], b_dw[j], ln_g[j], ln_b[j], w_pw2[j], b_pw2[j])
            xp, cp = _mixer_conv(xp, g_mix[i], *cw)
            xs, cs = _mixer_conv(xs, g_mix[i], *cw, state=state_conv[j])
            convp_l.append(cp); convs_l.append(cs)
        mk, mv = _mem_kv(mem_prompt.reshape(bp * n_mem, d), g_mem_src[i], w_mem_kv[i])
        mk, mv = mk.reshape(bp, n_mem, d), mv.reshape(bp, n_mem, d)
        memk_l.append(mk.reshape(bp, n_mem, MEM_HEADS, mem_hd))
        memv_l.append(mv.reshape(bp, n_mem, MEM_HEADS, mem_hd))
        xp2 = _mem_attn(xp.reshape(bp * tp, d), g_mem_q[i], w_mem_q[i], mk, mv, w_mem_o[i], tp)
        xs2 = _mem_attn(xs.reshape(bs * ts, d), g_mem_q[i], w_mem_q[i],
                        cache_mem_k[i].reshape(bs, n_mem, d), cache_mem_v[i].reshape(bs, n_mem, d),
                        w_mem_o[i], ts)
        gf = g_final if i == depth - 1 else None
        xp = _mlp(xp2, g_mlp[i], w_mlp1[i], w_mlp2[i], gf).reshape(bp, tp, d)
        xs = _mlp(xs2, g_mlp[i], w_mlp1[i], w_mlp2[i], gf).reshape(bs, ts, d)
    return (xp, xs, jnp.stack(kp_l), jnp.stack(vp_l), jnp.stack(kip_l), jnp.stack(convp_l),
            jnp.stack(memk_l), jnp.stack(memv_l), jnp.stack(ks_l), jnp.stack(vs_l),
            jnp.stack(kis_l), jnp.stack(convs_l))
```

```python
import functools
import math

import jax
import jax.numpy as jnp
from jax import lax
from jax.experimental import pallas as pl
from jax.experimental.pallas import tpu as pltpu

CHUNK = 64
HEAD_DIM = 64
N_KV_HEADS = 4
IDX_HEADS = 8
IDX_DIM = 64
TOPK_MAX = 256
N_BUCKETS = 32
MAX_DISTANCE = 128
CONV_WIDTH = 31
MEM_HEADS = 4
EPS = 1e-6

LANES = 128
SUBLANES = 8
KEY_BLOCK = 128
HEADS_PER_TILE = 2
V_AUG_ROWS = HEAD_DIM + 16
LOG2E = math.log2(math.e)
HALO = 32
VMEM_LIMIT = 56 * 1024 * 1024

NEG_BIG = -1e30
F32 = jnp.float32
BF16 = jnp.bfloat16

KEY_NEG_INF = -2139095041
KEY_POS_INF = 2139095040


def _const_spec(shape):
    nd = len(shape)
    return pl.BlockSpec(shape, lambda *_: (0,) * nd, pipeline_mode=pl.Buffered(1))


def _params(*sem):
    return pltpu.CompilerParams(dimension_semantics=sem, vmem_limit_bytes=VMEM_LIMIT)


def _rms(x, g):
    ms = jnp.mean(x * x, axis=-1, keepdims=True)
    return x * lax.rsqrt(ms + EPS) * g


def _dot(a, b):
    return jnp.dot(a, b, preferred_element_type=F32)


def _dot_nt(a, b):
    return lax.dot_general(a, b, (((1,), (1,)), ((), ())), preferred_element_type=F32)


def _row_tile(m, pref):
    t = min(m, pref)
    assert m % t == 0, (m, t)
    return t


def _bias_tiles_kernel(tab_ref, bt_ref, *, n_heads):
    nb = N_BUCKETS // 2
    max_exact = nb // 2
    c = lax.broadcasted_iota(jnp.int32, (KEY_BLOCK, KEY_BLOCK), 0)
    r = lax.broadcasted_iota(jnp.int32, (KEY_BLOCK, KEY_BLOCK), 1)
    for d in range(2):
        rel = c - r - d * KEY_BLOCK
        n = jnp.abs(rel)
        nf = jnp.maximum(n, 1).astype(F32)
        large = max_exact + (jnp.log(nf / max_exact) / math.log(MAX_DISTANCE / max_exact)
                             * (nb - max_exact)).astype(jnp.int32)
        large = jnp.minimum(large, nb - 1)
        bucket = jnp.where(rel > 0, nb, 0) + jnp.where(n < max_exact, n, large)
        for h in range(n_heads):
            acc = jnp.zeros((KEY_BLOCK, KEY_BLOCK), F32)
            for b in range(N_BUCKETS):
                acc = jnp.where(bucket == b, tab_ref[b, h], acc)
            g = h % HEADS_PER_TILE
            bt_ref[d, h // HEADS_PER_TILE, :, g * KEY_BLOCK:(g + 1) * KEY_BLOCK] = (
                (acc - tab_ref[nb - 1, h]) * LOG2E)


def _bias_tiles(rel_bias):
    n_heads = rel_bias.shape[1]
    assert n_heads % HEADS_PER_TILE == 0
    return pl.pallas_call(
        functools.partial(_bias_tiles_kernel, n_heads=n_heads),
        out_shape=jax.ShapeDtypeStruct(
            (2, n_heads // HEADS_PER_TILE, KEY_BLOCK, HEADS_PER_TILE * KEY_BLOCK), F32),
        in_specs=[pl.BlockSpec(memory_space=pltpu.SMEM)],
        out_specs=pl.BlockSpec(memory_space=pltpu.VMEM),
        name="bias_tiles",
    )(rel_bias)


def _attn_proj_kernel(x_ref, g_ref, w_ref, q_ref, k_ref, v_ref, ki_ref, kb_ref, vb_ref,
                      kib_ref, qi_ref, wi_ref, *, dq, dkv, dqi):
    h = _rms(x_ref[...], g_ref[...]).astype(BF16)
    o = 0
    q_ref[...] = (_dot(h, w_ref[:, o:o + dq]) * (HEAD_DIM ** -0.5)).astype(BF16)
    o += dq
    k = _dot(h, w_ref[:, o:o + dkv])
    k_ref[...] = k
    kb_ref[...] = k.astype(BF16)
    o += dkv
    v = _dot(h, w_ref[:, o:o + dkv])
    v_ref[...] = v
    vb_ref[...] = v.astype(BF16)
    o += dkv
    qi_ref[...] = (_dot(h, w_ref[:, o:o + dqi]) * (IDX_DIM ** -0.5)).astype(BF16)
    o += dqi
    ki = _dot(h, w_ref[:, o:o + LANES])[:, :IDX_DIM]
    ki_ref[...] = ki
    kib_ref[...] = ki.astype(BF16)
    o += LANES
    wi_ref[...] = _dot(h, w_ref[:, o:o + LANES]) * (IDX_HEADS ** -0.5)


def _attn_proj(x2, g, w_in):
    m, d = x2.shape
    dq = d
    dkv = N_KV_HEADS * HEAD_DIM
    dqi = IDX_HEADS * IDX_DIM
    base = dq + 2 * dkv + dqi
    w_main = w_in[:, :base]
    w_ki = jnp.pad(w_in[:, base:base + IDX_DIM], ((0, 0), (0, LANES - IDX_DIM)))
    w_wi = jnp.pad(w_in[:, base + IDX_DIM:], ((0, 0), (0, LANES - IDX_HEADS)))
    w = jnp.concatenate([w_main, w_ki, w_wi], axis=1).astype(BF16)
    tm = _row_tile(m, 512)
    row = lambda n: pl.BlockSpec((tm, n), lambda i: (i, 0))
    outs = [(dq, BF16), (dkv, F32), (dkv, F32), (IDX_DIM, F32), (dkv, BF16), (dkv, BF16),
            (IDX_DIM, BF16), (dqi, BF16), (LANES, F32)]
    return pl.pallas_call(
        functools.partial(_attn_proj_kernel, dq=dq, dkv=dkv, dqi=dqi),
        out_shape=[jax.ShapeDtypeStruct((m, n), dt) for n, dt in outs],
        grid=(m // tm,),
        in_specs=[row(d), _const_spec((1, d)), _const_spec(w.shape)],
        out_specs=[row(n) for n, _ in outs],
        compiler_params=_params("parallel"),
        name="attn_proj",
    )(x2, g.reshape(1, d), w)


def _sparse_attn_kernel(q_ref, qi_ref, wi_ref, k_ref, vt_ref, ki_ref, bt_ref, o_ref,
                        s_ref, hi_ref, lo_ref, dots_ref, qgt_ref, qit_ref, ot_ref, lg_ref, p_ref, al_ref,
                        *state_refs,
                        q_off, n_keys, k_sel, group):
    tq = KEY_BLOCK
    hpt = HEADS_PER_TILE
    n_tiles = qgt_ref.shape[0]
    m_refs, acc_refs = state_refs[:n_tiles], state_refs[n_tiles:]
    i = pl.program_id(1)
    qs = q_off // KEY_BLOCK + i
    n_kb = qs + 1
    q_start = q_off + i * tq

    qt = q_ref[0].astype(F32).T
    for t in range(n_tiles):
        qgt_ref[t] = jnp.concatenate(
            [qt[(t * hpt + g) * HEAD_DIM:(t * hpt + g + 1) * HEAD_DIM, :] for g in range(hpt)],
            axis=1).astype(BF16)
    qit = qi_ref[0].astype(F32).T
    qit_ref[...] = jnp.concatenate([qit[h * IDX_DIM:(h + 1) * IDX_DIM, :] for h in range(IDX_HEADS)],
                                   axis=1).astype(BF16)
    last_kv = k_ref.shape[1] // KEY_BLOCK - 1
    wit = wi_ref[0].T

    qpos = q_start + lax.broadcasted_iota(jnp.int32, (1, tq), 1)
    lim = jnp.minimum((qpos // CHUNK + 1) * CHUNK, n_keys)
    kidx = lax.broadcasted_iota(jnp.int32, (KEY_BLOCK, tq), 0)

    def put_scores(j, sc):
        s_ref[j] = sc
        bits = lax.bitcast_convert_type(jnp.where(sc == 0.0, 0.0, sc), jnp.int32)
        key = bits ^ ((bits >> 31) & 0x7FFFFFFF)
        hi_ref[j] = (key >> 16).astype(jnp.int16)
        lo_ref[j] = ((key & 0xFFFF) - 32768).astype(jnp.int16)

    def score_pair(pair, carry):
        js = (2 * pair, 2 * pair + 1)
        for slot, j in enumerate(js):
            k0 = pl.multiple_of(jnp.minimum(j, last_kv) * KEY_BLOCK, KEY_BLOCK)
            dots_ref[slot] = _dot(ki_ref[0, pl.ds(k0, KEY_BLOCK), :], qit_ref[...])
        for slot, j in enumerate(js):
            acc = jnp.zeros((KEY_BLOCK, tq), F32)
            for h in range(IDX_HEADS):
                acc = acc + wit[h:h + 1, :] * jnp.maximum(dots_ref[slot, :, h * tq:(h + 1) * tq], 0.0)
            put_scores(j, jnp.where(kidx + j * KEY_BLOCK < lim, acc, -jnp.inf))
        return carry

    n_pairs = (n_kb + 1) // 2
    lax.fori_loop(0, n_pairs, score_pair, 0)
    put_scores(n_kb, jnp.full((KEY_BLOCK, tq), -jnp.inf, F32))

    def count16(ref, cand):
        cb = jnp.broadcast_to(cand, (KEY_BLOCK, tq)).astype(jnp.int16)
        one, zero = jnp.ones((), jnp.int16), jnp.zeros((), jnp.int16)

        def body(p, c):
            c = c + jnp.where(ref[2 * p] >= cb, one, zero)
            return c + jnp.where(ref[2 * p + 1] >= cb, one, zero)
        c = lax.fori_loop(0, n_pairs, body, jnp.zeros((KEY_BLOCK, tq), jnp.int16))
        return jnp.sum(c.astype(F32), axis=0, keepdims=True)

    def bisect16(ref, lo0, hi0):
        def step(_, carry):
            lo, hi = carry
            mid = (lo + hi) >> 1
            ok = count16(ref, mid) >= k_sel
            return jnp.where(ok, mid, lo), jnp.where(ok, hi, mid)
        lo, _ = lax.fori_loop(0, 16, step, (jnp.full((1, tq), lo0, jnp.int32),
                                            jnp.full((1, tq), hi0, jnp.int32)))
        return lo

    key_hi = bisect16(hi_ref, KEY_NEG_INF >> 16, (KEY_POS_INF >> 16) + 1)
    hb = jnp.broadcast_to(key_hi, (KEY_BLOCK, tq)).astype(jnp.int16)

    def narrow(j, carry):
        hi = hi_ref[j]
        lo_ref[j] = jnp.where(hi == hb, lo_ref[j],
                              jnp.where(hi > hb, jnp.int16(32767), jnp.int16(-32768)))
        return carry
    lax.fori_loop(0, 2 * n_pairs, narrow, 0)
    key_lo = bisect16(lo_ref, -32768, 32768)

    def key_to_f32(key):
        bits = key ^ ((key >> 31) & 0x7FFFFFFF)
        return lax.bitcast_convert_type(bits, F32)

    thr = key_to_f32((key_hi << 16) + (key_lo + 32768))

    def count(cand, strict):
        cb = jnp.broadcast_to(cand, (KEY_BLOCK, tq))
        hit = (lambda s: s > cb) if strict else (lambda s: s >= cb)

        def body(p, c):
            c = c + jnp.where(hit(s_ref[2 * p]), 1.0, 0.0)
            return c + jnp.where(hit(s_ref[2 * p + 1]), 1.0, 0.0)
        c = lax.fori_loop(0, n_pairs, body, jnp.zeros((KEY_BLOCK, tq), F32))
        return jnp.sum(c, axis=0, keepdims=True)

    surplus = count(thr, False) - k_sel

    @pl.when(jnp.max(surplus) > 0.0)
    def _():
        n_ties = k_sel - count(thr, True)
        row_i = lax.broadcasted_iota(jnp.int32, (KEY_BLOCK, KEY_BLOCK), 0)
        col_i = lax.broadcasted_iota(jnp.int32, (KEY_BLOCK, KEY_BLOCK), 1)
        lower = jnp.where(col_i < row_i, 1.0, 0.0).astype(BF16)
        ones = jnp.ones((KEY_BLOCK, KEY_BLOCK), BF16)

        def tie_body(j, seen):
            s = s_ref[j]
            eq = s == thr
            e = jnp.where(eq, 1.0, 0.0).astype(BF16)
            before = _dot(lower, e) + seen
            s_ref[j] = jnp.where(eq & (before >= n_ties), -jnp.inf, s)
            return seen + _dot(ones, e)

        lax.fori_loop(0, n_kb, tie_body, jnp.zeros((KEY_BLOCK, tq), F32))

    thr_sel = jnp.broadcast_to(jnp.maximum(thr, jnp.finfo(F32).min), (KEY_BLOCK, tq))

    for t in range(n_tiles):
        m_refs[t][...] = jnp.full(m_refs[t].shape, NEG_BIG, F32)
        acc_refs[t][...] = jnp.zeros(acc_refs[t].shape, F32)
    p_ref[...] = jnp.zeros(p_ref.shape, BF16)
    al_ref[...] = jnp.ones(al_ref.shape, F32)

    def mask_of(js):
        return jnp.concatenate([jnp.where(s_ref[js] >= thr_sel, 0.0, NEG_BIG)] * hpt, axis=1)

    def logits(jkv, mask, bias_sel, slot, t):
        n = (t * hpt) // group
        k0 = pl.multiple_of(jkv * KEY_BLOCK, KEY_BLOCK)
        kn = k_ref[0, pl.ds(k0, KEY_BLOCK), n * HEAD_DIM:(n + 1) * HEAD_DIM]
        lg = _dot(kn, qgt_ref[t]) * LOG2E + mask
        if bias_sel is not None:
            lg = lg + bt_ref[bias_sel, t]
        lg_ref[slot, t] = lg

    def softmax(slot, t):
        m_old = m_refs[t][...]
        m_new = jnp.maximum(m_old, jnp.max(lg_ref[slot, t], axis=0, keepdims=True))
        al_ref[slot, t] = jnp.exp2(m_old - m_new)
        p_ref[slot, t] = jnp.exp2(lg_ref[slot, t] - m_new).astype(BF16)
        m_refs[t][...] = m_new

    def update(jkv, slot, t):
        n = (t * hpt) // group
        acc_refs[t][...] = al_ref[slot, t] * acc_refs[t][...] + _dot(vt_ref[0, jkv, n], p_ref[slot, t])

    def second_of(j0):
        return jnp.where(j0 + 1 < qs - 1, j0 + 1, n_kb)

    mask0 = mask_of(0)
    for t in range(n_tiles):
        logits(0, mask0, None, 0, t)

    def far_body(pair, carry):
        j0 = 2 * pair
        j1 = second_of(j0)
        j2 = jnp.minimum(j0 + 2, last_kv)
        mask1, mask2 = mask_of(j1), mask_of(j2)
        for t in range(n_tiles):
            update(jnp.maximum(j0 - 2, 0), 0, t)
            logits(jnp.minimum(j1, last_kv), mask1, None, 1, t)
            softmax(0, t)
        for t in range(n_tiles):
            update(jnp.maximum(j0 - 1, 0), 1, t)
            logits(j2, mask2, None, 0, t)
            softmax(1, t)
        return carry

    n_far = qs // 2
    lax.fori_loop(0, n_far, far_body, 0)

    jl0 = jnp.maximum(2 * n_far - 2, 0)
    jl1 = jnp.minimum(second_of(jl0), last_kv)
    j_prev = jnp.where(qs >= 1, qs - 1, n_kb)
    mask_prev, mask_diag = mask_of(j_prev), mask_of(qs)
    for t in range(n_tiles):
        update(jl0, 0, t)
        logits(jnp.maximum(qs - 1, 0), mask_prev, 1, 0, t)
    for t in range(n_tiles):
        update(jl1, 1, t)
        logits(qs, mask_diag, 0, 1, t)
    for t in range(n_tiles):
        softmax(0, t)
    for t in range(n_tiles):
        softmax(1, t)
    for t in range(n_tiles):
        update(jnp.maximum(qs - 1, 0), 0, t)
        update(qs, 1, t)

    for t in range(n_tiles):
        on = acc_refs[t][0:HEAD_DIM, :] / acc_refs[t][HEAD_DIM:HEAD_DIM + 1, :]
        for g in range(hpt):
            hh = t * hpt + g
            ot_ref[hh * HEAD_DIM:(hh + 1) * HEAD_DIM, :] = on[:, g * tq:(g + 1) * tq]
    o_ref[0] = ot_ref[...].T.astype(BF16)


def _sparse_attn(q, qi, wi, kb, vb, kib, bias_tiles, *, q_off, n_keys, k_sel):
    b, t, dq = q.shape
    lp = kb.shape[1]
    dkv = kb.shape[2]
    tq = KEY_BLOCK
    n_heads = dq // HEAD_DIM
    group = n_heads // N_KV_HEADS
    hpt = HEADS_PER_TILE
    n_tiles = n_heads // hpt
    assert group % hpt == 0
    nkb = lp // KEY_BLOCK
    assert t % tq == 0 and lp % KEY_BLOCK == 0 and q_off % KEY_BLOCK == 0 and q_off + t <= lp
    vt = jnp.transpose(vb.reshape(b, nkb, KEY_BLOCK, N_KV_HEADS, HEAD_DIM), (0, 1, 3, 4, 2))
    vt = jnp.concatenate(
        [vt, jnp.ones((b, nkb, N_KV_HEADS, 1, KEY_BLOCK), BF16),
         jnp.zeros((b, nkb, N_KV_HEADS, V_AUG_ROWS - HEAD_DIM - 1, KEY_BLOCK), BF16)], axis=3)
    qblk = lambda n: pl.BlockSpec((1, tq, n), lambda bi, i: (bi, i, 0))
    kblk = lambda n: pl.BlockSpec((1, lp, n), lambda bi, i: (bi, 0, 0))
    return pl.pallas_call(
        functools.partial(_sparse_attn_kernel, q_off=q_off, n_keys=n_keys, k_sel=k_sel, group=group),
        out_shape=jax.ShapeDtypeStruct((b, t, dq), BF16),
        grid=(b, t // tq),
        in_specs=[qblk(dq), qblk(qi.shape[2]), qblk(LANES), kblk(dkv),
                  pl.BlockSpec((1, nkb, N_KV_HEADS, V_AUG_ROWS, KEY_BLOCK), lambda bi, i: (bi, 0, 0, 0, 0)),
                  kblk(kib.shape[2]), _const_spec(bias_tiles.shape)],
        out_specs=qblk(dq),
        scratch_shapes=[
            pltpu.VMEM((nkb + 1, KEY_BLOCK, tq), F32),
            pltpu.VMEM((nkb + 1, KEY_BLOCK, tq), jnp.int16),
            pltpu.VMEM((nkb + 1, KEY_BLOCK, tq), jnp.int16),
            pltpu.VMEM((2, KEY_BLOCK, IDX_HEADS * tq), F32),
            pltpu.VMEM((n_tiles, HEAD_DIM, hpt * tq), BF16),
            pltpu.VMEM((IDX_DIM, IDX_HEADS * tq), BF16),
            pltpu.VMEM((dq, tq), F32),
            pltpu.VMEM((2, n_tiles, KEY_BLOCK, hpt * tq), F32),
            pltpu.VMEM((2, n_tiles, KEY_BLOCK, hpt * tq), BF16),
            pltpu.VMEM((2, n_tiles, 1, hpt * tq), F32),
        ] + [pltpu.VMEM((1, hpt * tq), F32)] * n_tiles
          + [pltpu.VMEM((V_AUG_ROWS, hpt * tq), F32)] * n_tiles,
        compiler_params=_params("parallel", "arbitrary"),
        name="sparse_attn",
    )(q, qi, wi, kb, vt, kib, bias_tiles)


def _matmul_res_kernel(x_ref, a_ref, w_ref, o_ref):
    o_ref[...] = x_ref[...] + _dot(a_ref[...], w_ref[...])


def _matmul_res(x2, a2, w):
    m, d = x2.shape
    ka = a2.shape[1]
    tm = _row_tile(m, 512)
    return pl.pallas_call(
        _matmul_res_kernel,
        out_shape=jax.ShapeDtypeStruct((m, d), F32),
        grid=(m // tm,),
        in_specs=[pl.BlockSpec((tm, d), lambda i: (i, 0)), pl.BlockSpec((tm, ka), lambda i: (i, 0)),
                  _const_spec((ka, d))],
        out_specs=pl.BlockSpec((tm, d), lambda i: (i, 0)),
        compiler_params=_params("parallel"),
        name="attn_out_proj",
    )(x2, a2, w.astype(BF16))


def _mem_kv_kernel(x_ref, g_ref, w_ref, k_ref, v_ref, *, d):
    h = _rms(x_ref[...], g_ref[...]).astype(BF16)
    k_ref[...] = _dot(h, w_ref[:, :d])
    v_ref[...] = _dot(h, w_ref[:, d:])


def _mem_kv(mem2, g, w_kv):
    m, d = mem2.shape
    tm = _row_tile(m, 512)
    row = pl.BlockSpec((tm, d), lambda i: (i, 0))
    return pl.pallas_call(
        functools.partial(_mem_kv_kernel, d=d),
        out_shape=[jax.ShapeDtypeStruct((m, d), F32)] * 2,
        grid=(m // tm,),
        in_specs=[row, _const_spec((1, d)), _const_spec((d, 2 * d))],
        out_specs=[row, row],
        compiler_params=_params("parallel"),
        name="mem_kv",
    )(mem2, g.reshape(1, d), w_kv.astype(BF16))


def _mem_attn_kernel(x_ref, g_ref, wq_ref, mk_ref, mv_ref, wo_ref, o_ref, *, hd):
    x = x_ref[...]
    h = _rms(x, g_ref[...]).astype(BF16)
    q = (_dot(h, wq_ref[...]) * (hd ** -0.5)).astype(BF16)
    heads = []
    for a in range(MEM_HEADS):
        cols = slice(a * hd, (a + 1) * hd)
        lg = _dot_nt(q[:, cols], mk_ref[0, :, cols].astype(BF16))
        p = jnp.exp(lg - jnp.max(lg, axis=1, keepdims=True))
        p = (p / jnp.sum(p, axis=1, keepdims=True)).astype(BF16)
        heads.append(_dot(p, mv_ref[0, :, cols].astype(BF16)).astype(BF16))
    o_ref[...] = x + _dot(jnp.concatenate(heads, axis=1), wo_ref[...])


def _mem_attn(x2, g, w_q, mk, mv, w_o, rows_per_batch):
    m, d = x2.shape
    n_mem = mk.shape[1]
    tm = _row_tile(rows_per_batch, 512)
    per = rows_per_batch // tm
    row = pl.BlockSpec((tm, d), lambda i: (i, 0))
    mem = pl.BlockSpec((1, n_mem, d), lambda i: (i // per, 0, 0))
    return pl.pallas_call(
        functools.partial(_mem_attn_kernel, hd=d // MEM_HEADS),
        out_shape=jax.ShapeDtypeStruct((m, d), F32),
        grid=(m // tm,),
        in_specs=[row, _const_spec((1, d)), _const_spec((d, d)), mem, mem, _const_spec((d, d))],
        out_specs=row,
        compiler_params=_params("parallel"),
        name="mem_attn",
    )(x2, g.reshape(1, d), w_q.astype(BF16), mk, mv, w_o.astype(BF16))


def _mlp_kernel(*refs, n_chunks, chunk, final):
    if final:
        x_ref, g_ref, w1_ref, w2_ref, gf_ref, o_ref = refs
    else:
        x_ref, g_ref, w1_ref, w2_ref, o_ref = refs
    x = x_ref[...]
    h = _rms(x, g_ref[...]).astype(BF16)
    acc = x
    for c in range(n_chunks):
        a = jnp.maximum(_dot(h, w1_ref[:, c * chunk:(c + 1) * chunk]), 0.0)
        acc = acc + _dot((a * a).astype(BF16), w2_ref[c * chunk:(c + 1) * chunk, :])
    if final:
        acc = _rms(acc, gf_ref[...])
    o_ref[...] = acc


def _mlp(x2, g, w1, w2, g_final=None):
    m, d = x2.shape
    dff = w1.shape[1]
    chunk = min(dff, 1024)
    tm = _row_tile(m, 512)
    row = pl.BlockSpec((tm, d), lambda i: (i, 0))
    final = g_final is not None
    in_specs = [row, _const_spec((1, d)), _const_spec((d, dff)), _const_spec((dff, d))]
    args = [x2, g.reshape(1, d), w1.astype(BF16), w2.astype(BF16)]
    if final:
        in_specs.append(_const_spec((1, d)))
        args.append(g_final.reshape(1, d))
    return pl.pallas_call(
        functools.partial(_mlp_kernel, n_chunks=dff // chunk, chunk=chunk, final=final),
        out_shape=jax.ShapeDtypeStruct((m, d), F32),
        grid=(m // tm,),
        in_specs=in_specs,
        out_specs=row,
        compiler_params=_params("parallel"),
        name="mlp",
    )(*args)


def _conv_glu_kernel(x_ref, g_ref, w_ref, b_ref, u_ref, *, d):
    h = _rms(x_ref[...], g_ref[...]).astype(BF16)
    a = _dot(h, w_ref[:, :d]) + b_ref[:, :d]
    gate = _dot(h, w_ref[:, d:]) + b_ref[:, d:]
    u_ref[...] = a * (1.0 / (1.0 + jnp.exp(-gate)))


def _conv_glu(x2, g, w_pw1, b_pw1):
    m, d = x2.shape
    tm = _row_tile(m, 512)
    row = pl.BlockSpec((tm, d), lambda i: (i, 0))
    return pl.pallas_call(
        functools.partial(_conv_glu_kernel, d=d),
        out_shape=jax.ShapeDtypeStruct((m, d), F32),
        grid=(m // tm,),
        in_specs=[row, _const_spec((1, d)), _const_spec((d, 2 * d)), _const_spec((1, 2 * d))],
        out_specs=row,
        compiler_params=_params("parallel"),
        name="conv_glu",
    )(x2, g.reshape(1, d), w_pw1.astype(BF16), b_pw1.reshape(1, 2 * d))


def _conv_rest_kernel(x_ref, u_ref, prev_ref, init_ref, wdw_ref, bdw_ref, lng_ref, lnb_ref,
                      w2_ref, b2_ref, o_ref, ext_ref, sh_ref, y_ref, *, tm, rc, lc):
    t = pl.program_id(1)
    d = u_ref.shape[2]
    pad = CONV_WIDTH - 1

    @pl.when(t == 0)
    def _():
        ext_ref[0:HALO, :] = init_ref[0]

    @pl.when(t > 0)
    def _():
        ext_ref[0:HALO, :] = prev_ref[0]

    ext_ref[HALO:HALO + tm, :] = u_ref[0]

    for s in range(SUBLANES):
        rows = tm + SUBLANES * ((CONV_WIDTH - 1 - s) // SUBLANES)
        sh_ref[s, 0:rows, :] = ext_ref[pl.ds(HALO - pad + s, rows), :]
    for r0 in range(0, tm, rc):
        for c0 in range(0, d, lc):
            cols = slice(c0, c0 + lc)
            y = jnp.broadcast_to(bdw_ref[:, cols], (rc, lc))
            for w in range(CONV_WIDTH):
                a, s = divmod(w, SUBLANES)
                y = y + sh_ref[s, r0 + SUBLANES * a:r0 + SUBLANES * a + rc, cols] * wdw_ref[w:w + 1, cols]
            y_ref[r0:r0 + rc, cols] = y

    y = y_ref[...]
    mu = jnp.mean(y, axis=-1, keepdims=True)
    yc = y - mu
    var = jnp.mean(yc * yc, axis=-1, keepdims=True)
    yn = yc * lax.rsqrt(var + EPS) * lng_ref[...] + lnb_ref[...]
    act = (yn * (1.0 / (1.0 + jnp.exp(-yn)))).astype(BF16)
    o_ref[0] = x_ref[0] + _dot(act, w2_ref[...]) + b2_ref[...]


def _conv_rest(x3, u3, init, w_dw, b_dw, ln_g, ln_b, w_pw2, b_pw2):
    b, t, d = x3.shape
    tm = _row_tile(t, 256)
    rc = min(tm, 64)
    lc = min(d, 256)
    assert tm % HALO == 0 and tm % rc == 0 and d % lc == 0
    per = tm // HALO
    tile = pl.BlockSpec((1, tm, d), lambda bi, ti: (bi, ti, 0))
    prev = pl.BlockSpec((1, HALO, d), lambda bi, ti: (bi, jnp.maximum(ti * per - 1, 0), 0))
    first = pl.BlockSpec((1, HALO, d), lambda bi, ti: (bi, 0, 0))
    vec = _const_spec((1, d))
    wdw = jnp.pad(w_dw, ((0, HALO - CONV_WIDTH), (0, 0)))
    return pl.pallas_call(
        functools.partial(_conv_rest_kernel, tm=tm, rc=rc, lc=lc),
        out_shape=jax.ShapeDtypeStruct((b, t, d), F32),
        grid=(b, t // tm),
        in_specs=[tile, tile, prev, first, _const_spec((HALO, d)), vec, vec, vec,
                  _const_spec((d, d)), vec],
        out_specs=tile,
        scratch_shapes=[pltpu.VMEM((HALO + tm, d), F32),
                        pltpu.VMEM((SUBLANES, tm + HALO - SUBLANES, d), F32),
                        pltpu.VMEM((tm, d), F32)],
        compiler_params=_params("parallel", "arbitrary"),
        name="conv_rest",
    )(x3, u3, u3, init, wdw, b_dw.reshape(1, d), ln_g.reshape(1, d), ln_b.reshape(1, d),
      w_pw2.astype(BF16), b_pw2.reshape(1, d))


def _mixer_attn(x3, g, w_in, w_out, bias_tiles, cache=None):
    b, t, d = x3.shape
    x2 = x3.reshape(b * t, d)
    q, k, v, ki, kb, vb, kib, qi, wi = _attn_proj(x2, g, w_in)
    r3 = lambda a: a.reshape(b, t, a.shape[-1])
    kb, vb, kib = r3(kb), r3(vb), r3(kib)
    past = 0
    if cache is not None:
        ck, cv, cki = cache
        past = ck.shape[1]
        kb = jnp.concatenate([ck.reshape(b, past, -1).astype(BF16), kb], axis=1)
        vb = jnp.concatenate([cv.reshape(b, past, -1).astype(BF16), vb], axis=1)
        kib = jnp.concatenate([cki.astype(BF16), kib], axis=1)
    n_keys = past + t
    k_sel = min(TOPK_MAX, n_keys // 4)
    tpad = -(-t // KEY_BLOCK) * KEY_BLOCK
    lp = max(-(-n_keys // KEY_BLOCK) * KEY_BLOCK, past + tpad)
    padt = lambda a, n: a if a.shape[1] == n else jnp.pad(a, ((0, 0), (0, n - a.shape[1]), (0, 0)))
    o = _sparse_attn(padt(r3(q), tpad), padt(r3(qi), tpad), padt(r3(wi), tpad),
                     padt(kb, lp), padt(vb, lp), padt(kib, lp), bias_tiles,
                     q_off=past, n_keys=n_keys, k_sel=k_sel)[:, :t]
    x2 = _matmul_res(x2, o.reshape(b * t, d), w_out)
    return (x2.reshape(b, t, d), k.reshape(b, t, N_KV_HEADS, HEAD_DIM),
            v.reshape(b, t, N_KV_HEADS, HEAD_DIM), ki.reshape(b, t, IDX_DIM))


def _mixer_conv(x3, g, w_pw1, b_pw1, w_dw, b_dw, ln_g, ln_b, w_pw2, b_pw2, state=None):
    b, t, d = x3.shape
    pad = CONV_WIDTH - 1
    u3 = _conv_glu(x3.reshape(b * t, d), g, w_pw1, b_pw1).reshape(b, t, d)
    if state is None:
        init = jnp.zeros((b, HALO, d), F32)
        tail = u3[:, -pad:] if t >= pad else jnp.pad(u3, ((0, 0), (pad - t, 0), (0, 0)))
    else:
        init = jnp.pad(state.astype(F32), ((0, 0), (HALO - pad, 0), (0, 0)))
        tail = jnp.concatenate([state.astype(F32), u3], axis=1)[:, -pad:]
    x3 = _conv_rest(x3, u3, init, w_dw, b_dw, ln_g, ln_b, w_pw2, b_pw2)
    return x3, tail


def kernel(x_prompt, x_sample, cache_attn_k, cache_attn_v, cache_attn_kidx, state_conv, cache_mem_k, cache_mem_v, mem_prompt, rel_bias, g_mix, w_in_attn, w_out_attn, w_pw1, b_pw1, w_dw, b_dw, ln_g, ln_b, w_pw2, b_pw2, g_mem_q, g_mem_src, w_mem_q, w_mem_kv, w_mem_o, g_mlp, w_mlp1, w_mlp2, g_final):
    depth = g_mix.shape[0]
    bp, tp, d = x_prompt.shape
    bs, ts, _ = x_sample.shape
    n_mem = mem_prompt.shape[1]
    mem_hd = d // MEM_HEADS
    bias_tiles = _bias_tiles(rel_bias)
    xp, xs = x_prompt, x_sample
    kp_l, vp_l, kip_l, ks_l, vs_l, kis_l = [], [], [], [], [], []
    convp_l, convs_l, memk_l, memv_l = [], [], [], []
    for i in range(depth):
        j = i // 2
        if i % 2 == 0:
            xp, kp, vp, kip = _mixer_attn(xp, g_mix[i], w_in_attn[j], w_out_attn[j], bias_tiles)
            xs, ks, vs, kis = _mixer_attn(
                xs, g_mix[i], w_in_attn[j], w_out_attn[j], bias_tiles,
                cache=(cache_attn_k[j], cache_attn_v[j], cache_attn_kidx[j]))
            kp_l.append(kp); vp_l.append(vp); kip_l.append(kip)
            ks_l.append(ks); vs_l.append(vs); kis_l.append(kis)
        else:
            cw = (w_pw1[j], b_pw1[j], w_dw[j], b_dw[j], ln_g[j], ln_b[j], w_pw2[j], b_pw2[j])
            xp, cp = _mixer_conv(xp, g_mix[i], *cw)
            xs, cs = _mixer_conv(xs, g_mix[i], *cw, state=state_conv[j])
            convp_l.append(cp); convs_l.append(cs)
        mk, mv = _mem_kv(mem_prompt.reshape(bp * n_mem, d), g_mem_src[i], w_mem_kv[i])
        mk, mv = mk.reshape(bp, n_mem, d), mv.reshape(bp, n_mem, d)
        memk_l.append(mk.reshape(bp, n_mem, MEM_HEADS, mem_hd))
        memv_l.append(mv.reshape(bp, n_mem, MEM_HEADS, mem_hd))
        xp2 = _mem_attn(xp.reshape(bp * tp, d), g_mem_q[i], w_mem_q[i], mk, mv, w_mem_o[i], tp)
        xs2 = _mem_attn(xs.reshape(bs * ts, d), g_mem_q[i], w_mem_q[i],
                        cache_mem_k[i].reshape(bs, n_mem, d), cache_mem_v[i].reshape(bs, n_mem, d),
                        w_mem_o[i], ts)
        gf = g_final if i == depth - 1 else None
        xp = _mlp(xp2, g_mlp[i], w_mlp1[i], w_mlp2[i], gf).reshape(bp, tp, d)
        xs = _mlp(xs2, g_mlp[i], w_mlp1[i], w_mlp2[i], gf).reshape(bs, ts, d)
    return (xp, xs, jnp.stack(kp_l), jnp.stack(vp_l), jnp.stack(kip_l), jnp.stack(convp_l),
            jnp.stack(memk_l), jnp.stack(memv_l), jnp.stack(ks_l), jnp.stack(vs_l),
            jnp.stack(kis_l), jnp.stack(convs_l))
```

```python
import functools
import math

import jax
import jax.numpy as jnp
from jax import lax
from jax.experimental import pallas as pl
from jax.experimental.pallas import tpu as pltpu

CHUNK = 64
HEAD_DIM = 64
N_KV_HEADS = 4
IDX_HEADS = 8
IDX_DIM = 64
TOPK_MAX = 256
N_BUCKETS = 32
MAX_DISTANCE = 128
CONV_WIDTH = 31
MEM_HEADS = 4
EPS = 1e-6

LANES = 128
SUBLANES = 8
KEY_BLOCK = 128
HEADS_PER_TILE = 2
V_AUG_ROWS = HEAD_DIM + 16
LOG2E = math.log2(math.e)
HALO = 32
VMEM_LIMIT = 56 * 1024 * 1024

NEG_BIG = -1e30
F32 = jnp.float32
BF16 = jnp.bfloat16

KEY_NEG_INF = -2139095041
KEY_POS_INF = 2139095040


def _const_spec(shape):
    nd = len(shape)
    return pl.BlockSpec(shape, lambda *_: (0,) * nd, pipeline_mode=pl.Buffered(1))


def _params(*sem):
    return pltpu.CompilerParams(dimension_semantics=sem, vmem_limit_bytes=VMEM_LIMIT)


def _rms(x, g):
    ms = jnp.mean(x * x, axis=-1, keepdims=True)
    return x * lax.rsqrt(ms + EPS) * g


def _dot(a, b):
    return jnp.dot(a, b, preferred_element_type=F32)


def _dot_nt(a, b):
    return lax.dot_general(a, b, (((1,), (1,)), ((), ())), preferred_element_type=F32)


def _row_tile(m, pref):
    t = min(m, pref)
    assert m % t == 0, (m, t)
    return t


def _bias_tiles_kernel(tab_ref, bt_ref, *, n_heads):
    nb = N_BUCKETS // 2
    max_exact = nb // 2
    c = lax.broadcasted_iota(jnp.int32, (KEY_BLOCK, KEY_BLOCK), 0)
    r = lax.broadcasted_iota(jnp.int32, (KEY_BLOCK, KEY_BLOCK), 1)
    for d in range(2):
        rel = c - r - d * KEY_BLOCK
        n = jnp.abs(rel)
        nf = jnp.maximum(n, 1).astype(F32)
        large = max_exact + (jnp.log(nf / max_exact) / math.log(MAX_DISTANCE / max_exact)
                             * (nb - max_exact)).astype(jnp.int32)
        large = jnp.minimum(large, nb - 1)
        bucket = jnp.where(rel > 0, nb, 0) + jnp.where(n < max_exact, n, large)
        for h in range(n_heads):
            acc = jnp.zeros((KEY_BLOCK, KEY_BLOCK), F32)
            for b in range(N_BUCKETS):
                acc = jnp.where(bucket == b, tab_ref[b, h], acc)
            g = h % HEADS_PER_TILE
            rows = slice((1 - d) * KEY_BLOCK, (2 - d) * KEY_BLOCK)
            bt_ref[h // HEADS_PER_TILE, rows, g * KEY_BLOCK:(g + 1) * KEY_BLOCK] = (
                (acc - tab_ref[nb - 1, h]) * LOG2E)


def _bias_tiles(rel_bias):
    n_heads = rel_bias.shape[1]
    assert n_heads % HEADS_PER_TILE == 0
    return pl.pallas_call(
        functools.partial(_bias_tiles_kernel, n_heads=n_heads),
        out_shape=jax.ShapeDtypeStruct(
            (n_heads // HEADS_PER_TILE, 2 * KEY_BLOCK, HEADS_PER_TILE * KEY_BLOCK), F32),
        in_specs=[pl.BlockSpec(memory_space=pltpu.SMEM)],
        out_specs=pl.BlockSpec(memory_space=pltpu.VMEM),
        name="bias_tiles",
    )(rel_bias)


def _attn_proj_kernel(x_ref, g_ref, w_ref, q_ref, k_ref, v_ref, ki_ref, kb_ref, vb_ref,
                      kib_ref, qi_ref, wi_ref, *, dq, dkv, dqi):
    h = _rms(x_ref[...], g_ref[...]).astype(BF16)
    o = 0
    q_ref[...] = (_dot(h, w_ref[:, o:o + dq]) * (HEAD_DIM ** -0.5)).astype(BF16)
    o += dq
    k = _dot(h, w_ref[:, o:o + dkv])
    k_ref[...] = k
    kb_ref[...] = k.astype(BF16)
    o += dkv
    v = _dot(h, w_ref[:, o:o + dkv])
    v_ref[...] = v
    vb_ref[...] = v.astype(BF16)
    o += dkv
    qi_ref[...] = (_dot(h, w_ref[:, o:o + dqi]) * (IDX_DIM ** -0.5)).astype(BF16)
    o += dqi
    ki = _dot(h, w_ref[:, o:o + LANES])[:, :IDX_DIM]
    ki_ref[...] = ki
    kib_ref[...] = ki.astype(BF16)
    o += LANES
    wi_ref[...] = _dot(h, w_ref[:, o:o + LANES]) * (IDX_HEADS ** -0.5)


def _attn_proj(x2, g, w_in):
    m, d = x2.shape
    dq = d
    dkv = N_KV_HEADS * HEAD_DIM
    dqi = IDX_HEADS * IDX_DIM
    base = dq + 2 * dkv + dqi
    w_main = w_in[:, :base]
    w_ki = jnp.pad(w_in[:, base:base + IDX_DIM], ((0, 0), (0, LANES - IDX_DIM)))
    w_wi = jnp.pad(w_in[:, base + IDX_DIM:], ((0, 0), (0, LANES - IDX_HEADS)))
    w = jnp.concatenate([w_main, w_ki, w_wi], axis=1).astype(BF16)
    tm = _row_tile(m, 512)
    row = lambda n: pl.BlockSpec((tm, n), lambda i: (i, 0))
    outs = [(dq, BF16), (dkv, F32), (dkv, F32), (IDX_DIM, F32), (dkv, BF16), (dkv, BF16),
            (IDX_DIM, BF16), (dqi, BF16), (LANES, F32)]
    return pl.pallas_call(
        functools.partial(_attn_proj_kernel, dq=dq, dkv=dkv, dqi=dqi),
        out_shape=[jax.ShapeDtypeStruct((m, n), dt) for n, dt in outs],
        grid=(m // tm,),
        in_specs=[row(d), _const_spec((1, d)), _const_spec(w.shape)],
        out_specs=[row(n) for n, _ in outs],
        compiler_params=_params("parallel"),
        name="attn_proj",
    )(x2, g.reshape(1, d), w)


def _sparse_attn_kernel(q_ref, qi_ref, wi_ref, k_ref, vt_ref, ki_ref, bt_ref, o_ref,
                        s_ref, hi_ref, lo_ref, dots_ref, qgt_ref, qit_ref, ot_ref, lg_ref, p_ref, al_ref,
                        mb_ref, *state_refs,
                        q_off, n_keys, k_sel, group):
    tq = KEY_BLOCK
    hpt = HEADS_PER_TILE
    n_tiles = qgt_ref.shape[0]
    m_refs, acc_refs = state_refs[:n_tiles], state_refs[n_tiles:]
    i = pl.program_id(1)
    qs = q_off // KEY_BLOCK + i
    n_kb = qs + 1
    q_start = q_off + i * tq

    qt = q_ref[0].astype(F32).T * LOG2E
    for t in range(n_tiles):
        q2 = jnp.concatenate(
            [qt[(t * hpt + g) * HEAD_DIM:(t * hpt + g + 1) * HEAD_DIM, :] for g in range(hpt)], axis=1)
        q_hi = q2.astype(BF16)
        q_lo = (q2 - q_hi.astype(F32)).astype(BF16)
        qgt_ref[t] = jnp.concatenate([q_hi, q_lo], axis=0)
    qit = qi_ref[0].astype(F32).T
    qit_ref[...] = jnp.concatenate([qit[h * IDX_DIM:(h + 1) * IDX_DIM, :] for h in range(IDX_HEADS)],
                                   axis=1).astype(BF16)
    last_kv = k_ref.shape[1] // KEY_BLOCK - 1
    wit = wi_ref[0].T

    qpos = q_start + lax.broadcasted_iota(jnp.int32, (1, tq), 1)
    lim = jnp.minimum((qpos // CHUNK + 1) * CHUNK, n_keys)
    kidx = lax.broadcasted_iota(jnp.int32, (KEY_BLOCK, tq), 0)

    def put_scores(j, sc):
        s_ref[j] = sc
        bits = lax.bitcast_convert_type(jnp.where(sc == 0.0, 0.0, sc), jnp.int32)
        key = bits ^ ((bits >> 31) & 0x7FFFFFFF)
        hi_ref[j] = (key >> 16).astype(jnp.int16)
        lo_ref[j] = ((key & 0xFFFF) - 32768).astype(jnp.int16)

    def score_pair(pair, carry):
        js = (2 * pair, 2 * pair + 1)
        for slot, j in enumerate(js):
            k0 = pl.multiple_of(jnp.minimum(j, last_kv) * KEY_BLOCK, KEY_BLOCK)
            dots_ref[slot] = _dot(ki_ref[0, pl.ds(k0, KEY_BLOCK), :], qit_ref[...])
        for slot, j in enumerate(js):
            acc = jnp.zeros((KEY_BLOCK, tq), F32)
            for h in range(IDX_HEADS):
                acc = acc + wit[h:h + 1, :] * jnp.maximum(dots_ref[slot, :, h * tq:(h + 1) * tq], 0.0)
            put_scores(j, jnp.where(kidx + j * KEY_BLOCK < lim, acc, -jnp.inf))
        return carry

    n_pairs = (n_kb + 1) // 2
    lax.fori_loop(0, n_pairs, score_pair, 0)
    put_scores(n_kb, jnp.full((KEY_BLOCK, tq), -jnp.inf, F32))

    def count16(ref, cand):
        cb = jnp.broadcast_to(cand, (KEY_BLOCK, tq)).astype(jnp.int16)
        one, zero = jnp.ones((), jnp.int16), jnp.zeros((), jnp.int16)

        def body(p, c):
            c = c + jnp.where(ref[2 * p] >= cb, one, zero)
            return c + jnp.where(ref[2 * p + 1] >= cb, one, zero)
        c = lax.fori_loop(0, n_pairs, body, jnp.zeros((KEY_BLOCK, tq), jnp.int16))
        return jnp.sum(c.astype(F32), axis=0, keepdims=True)

    def bisect16(ref, lo0, hi0):
        def step(_, carry):
            lo, hi = carry
            mid = (lo + hi) >> 1
            ok = count16(ref, mid) >= k_sel
            return jnp.where(ok, mid, lo), jnp.where(ok, hi, mid)
        lo, _ = lax.fori_loop(0, 16, step, (jnp.full((1, tq), lo0, jnp.int32),
                                            jnp.full((1, tq), hi0, jnp.int32)))
        return lo

    key_hi = bisect16(hi_ref, KEY_NEG_INF >> 16, (KEY_POS_INF >> 16) + 1)
    hb = jnp.broadcast_to(key_hi, (KEY_BLOCK, tq)).astype(jnp.int16)

    def narrow(j, carry):
        hi = hi_ref[j]
        lo_ref[j] = jnp.where(hi == hb, lo_ref[j],
                              jnp.where(hi > hb, jnp.int16(32767), jnp.int16(-32768)))
        return carry
    lax.fori_loop(0, 2 * n_pairs, narrow, 0)
    key_lo = bisect16(lo_ref, -32768, 32768)

    def key_to_f32(key):
        bits = key ^ ((key >> 31) & 0x7FFFFFFF)
        return lax.bitcast_convert_type(bits, F32)

    thr = key_to_f32((key_hi << 16) + (key_lo + 32768))

    def count(cand, strict):
        cb = jnp.broadcast_to(cand, (KEY_BLOCK, tq))
        hit = (lambda s: s > cb) if strict else (lambda s: s >= cb)

        def body(p, c):
            c = c + jnp.where(hit(s_ref[2 * p]), 1.0, 0.0)
            return c + jnp.where(hit(s_ref[2 * p + 1]), 1.0, 0.0)
        c = lax.fori_loop(0, n_pairs, body, jnp.zeros((KEY_BLOCK, tq), F32))
        return jnp.sum(c, axis=0, keepdims=True)

    surplus = count(thr, False) - k_sel

    @pl.when(jnp.max(surplus) > 0.0)
    def _():
        n_ties = k_sel - count(thr, True)
        row_i = lax.broadcasted_iota(jnp.int32, (KEY_BLOCK, KEY_BLOCK), 0)
        col_i = lax.broadcasted_iota(jnp.int32, (KEY_BLOCK, KEY_BLOCK), 1)
        lower = jnp.where(col_i < row_i, 1.0, 0.0).astype(BF16)
        ones = jnp.ones((KEY_BLOCK, KEY_BLOCK), BF16)

        def tie_body(j, seen):
            s = s_ref[j]
            eq = s == thr
            e = jnp.where(eq, 1.0, 0.0).astype(BF16)
            before = _dot(lower, e) + seen
            s_ref[j] = jnp.where(eq & (before >= n_ties), -jnp.inf, s)
            return seen + _dot(ones, e)

        lax.fori_loop(0, n_kb, tie_body, jnp.zeros((KEY_BLOCK, tq), F32))

    thr_sel = jnp.broadcast_to(jnp.maximum(thr, jnp.finfo(F32).min), (KEY_BLOCK, tq))

    for t in range(n_tiles):
        m_refs[t][...] = jnp.full(m_refs[t].shape, NEG_BIG, F32)
        acc_refs[t][...] = jnp.zeros(acc_refs[t].shape, F32)
    p_ref[...] = jnp.zeros(p_ref.shape, BF16)
    al_ref[...] = jnp.ones(al_ref.shape, F32)

    def far_blocks(db):
        out = []
        for j in (2 * db, 2 * db + 1):
            js = jnp.where(j < qs - 1, j, n_kb)
            out.append((js, jnp.minimum(js, last_kv)))
        return out

    near = [(jnp.where(qs >= 1, qs - 1, n_kb), jnp.maximum(qs - 1, 0)), (qs, qs)]

    def mask_of(blocks):
        m = jnp.concatenate([jnp.where(s_ref[js] >= thr_sel, 0.0, NEG_BIG) for js, _ in blocks], axis=0)
        return jnp.concatenate([m] * hpt, axis=1)

    def logits(blocks, mask, biased, slot, t):
        n = (t * hpt) // group
        kn = jnp.concatenate(
            [k_ref[0, pl.ds(pl.multiple_of(jk * KEY_BLOCK, KEY_BLOCK), KEY_BLOCK),
                   n * 2 * HEAD_DIM:(n + 1) * 2 * HEAD_DIM] for _, jk in blocks], axis=0)
        lg = _dot(kn, qgt_ref[t]) + mask
        if biased:
            lg = lg + bt_ref[t]
        lg_ref[slot, t] = lg
        mb_ref[slot, t] = jnp.max(lg, axis=0, keepdims=True)

    def softmax(slot, t):
        m_old = m_refs[t][...]
        m_new = jnp.maximum(m_old, mb_ref[slot, t])
        al_ref[slot, t] = jnp.exp2(m_old - m_new)
        p_ref[slot, t] = jnp.exp2(lg_ref[slot, t] - m_new).astype(BF16)
        m_refs[t][...] = m_new

    def update(blocks, slot, t):
        n = (t * hpt) // group
        vtn = jnp.concatenate([vt_ref[0, jk, n] for _, jk in blocks], axis=1)
        acc_refs[t][...] = al_ref[slot, t] * acc_refs[t][...] + _dot(vtn, p_ref[slot, t])

    n_far = qs // 2
    first = far_blocks(0)
    mask_first = mask_of(first)
    for t in range(n_tiles):
        logits(first, mask_first, False, 0, t)

    def far_body(trip, carry):
        d0 = 2 * trip
        blk1, blk2 = far_blocks(d0 + 1), far_blocks(d0 + 2)
        mask1, mask2 = mask_of(blk1), mask_of(blk2)
        old0, old1 = far_blocks(jnp.maximum(d0 - 2, 0)), far_blocks(jnp.maximum(d0 - 1, 0))
        for t in range(n_tiles):
            update(old0, 0, t)
            logits(blk1, mask1, False, 1, t)
            softmax(0, t)
        for t in range(n_tiles):
            update(old1, 1, t)
            logits(blk2, mask2, False, 0, t)
            softmax(1, t)
        return carry

    n_trips = (n_far + 1) // 2
    lax.fori_loop(0, n_trips, far_body, 0)

    last0 = far_blocks(jnp.maximum(2 * n_trips - 2, 0))
    last1 = far_blocks(jnp.maximum(2 * n_trips - 1, 0))
    mask_near = mask_of(near)
    for t in range(n_tiles):
        update(last0, 0, t)
        logits(near, mask_near, True, 0, t)
    for t in range(n_tiles):
        update(last1, 1, t)
        softmax(0, t)
    for t in range(n_tiles):
        update(near, 0, t)

    for t in range(n_tiles):
        on = acc_refs[t][0:HEAD_DIM, :] / acc_refs[t][HEAD_DIM:HEAD_DIM + 1, :]
        for g in range(hpt):
            hh = t * hpt + g
            ot_ref[hh * HEAD_DIM:(hh + 1) * HEAD_DIM, :] = on[:, g * tq:(g + 1) * tq]
    o_ref[0] = ot_ref[...].T.astype(BF16)


def _sparse_attn(q, qi, wi, kb, vb, kib, bias_tiles, *, q_off, n_keys, k_sel):
    b, t, dq = q.shape
    lp = kb.shape[1]
    dkv = kb.shape[2]
    tq = KEY_BLOCK
    n_heads = dq // HEAD_DIM
    group = n_heads // N_KV_HEADS
    hpt = HEADS_PER_TILE
    n_tiles = n_heads // hpt
    assert group % hpt == 0
    nkb = lp // KEY_BLOCK
    assert t % tq == 0 and lp % KEY_BLOCK == 0 and q_off % KEY_BLOCK == 0 and q_off + t <= lp
    vt = jnp.transpose(vb.reshape(b, nkb, KEY_BLOCK, N_KV_HEADS, HEAD_DIM), (0, 1, 3, 4, 2))
    vt = jnp.concatenate(
        [vt, jnp.ones((b, nkb, N_KV_HEADS, 1, KEY_BLOCK), BF16),
         jnp.zeros((b, nkb, N_KV_HEADS, V_AUG_ROWS - HEAD_DIM - 1, KEY_BLOCK), BF16)], axis=3)
    k2 = jnp.concatenate([kb.reshape(b, lp, N_KV_HEADS, HEAD_DIM)] * 2, axis=-1).reshape(b, lp, 2 * dkv)
    qblk = lambda n: pl.BlockSpec((1, tq, n), lambda bi, i: (bi, i, 0))
    kblk = lambda n: pl.BlockSpec((1, lp, n), lambda bi, i: (bi, 0, 0))
    return pl.pallas_call(
        functools.partial(_sparse_attn_kernel, q_off=q_off, n_keys=n_keys, k_sel=k_sel, group=group),
        out_shape=jax.ShapeDtypeStruct((b, t, dq), BF16),
        grid=(b, t // tq),
        in_specs=[qblk(dq), qblk(qi.shape[2]), qblk(LANES), kblk(2 * dkv),
                  pl.BlockSpec((1, nkb, N_KV_HEADS, V_AUG_ROWS, KEY_BLOCK), lambda bi, i: (bi, 0, 0, 0, 0)),
                  kblk(kib.shape[2]), _const_spec(bias_tiles.shape)],
        out_specs=qblk(dq),
        scratch_shapes=[
            pltpu.VMEM((nkb + 1, KEY_BLOCK, tq), F32),
            pltpu.VMEM((nkb + 1, KEY_BLOCK, tq), jnp.int16),
            pltpu.VMEM((nkb + 1, KEY_BLOCK, tq), jnp.int16),
            pltpu.VMEM((2, KEY_BLOCK, IDX_HEADS * tq), F32),
            pltpu.VMEM((n_tiles, 2 * HEAD_DIM, hpt * tq), BF16),
            pltpu.VMEM((IDX_DIM, IDX_HEADS * tq), BF16),
            pltpu.VMEM((dq, tq), F32),
            pltpu.VMEM((2, n_tiles, 2 * KEY_BLOCK, hpt * tq), F32),
            pltpu.VMEM((2, n_tiles, 2 * KEY_BLOCK, hpt * tq), BF16),
            pltpu.VMEM((2, n_tiles, 1, hpt * tq), F32),
            pltpu.VMEM((2, n_tiles, 1, hpt * tq), F32),
        ] + [pltpu.VMEM((1, hpt * tq), F32)] * n_tiles
          + [pltpu.VMEM((V_AUG_ROWS, hpt * tq), F32)] * n_tiles,
        compiler_params=_params("parallel", "arbitrary"),
        name="sparse_attn",
    )(q, qi, wi, k2, vt, kib, bias_tiles)


def _matmul_res_kernel(x_ref, a_ref, w_ref, o_ref):
    o_ref[...] = x_ref[...] + _dot(a_ref[...], w_ref[...])


def _matmul_res(x2, a2, w):
    m, d = x2.shape
    ka = a2.shape[1]
    tm = _row_tile(m, 512)
    return pl.pallas_call(
        _matmul_res_kernel,
        out_shape=jax.ShapeDtypeStruct((m, d), F32),
        grid=(m // tm,),
        in_specs=[pl.BlockSpec((tm, d), lambda i: (i, 0)), pl.BlockSpec((tm, ka), lambda i: (i, 0)),
                  _const_spec((ka, d))],
        out_specs=pl.BlockSpec((tm, d), lambda i: (i, 0)),
        compiler_params=_params("parallel"),
        name="attn_out_proj",
    )(x2, a2, w.astype(BF16))


def _mem_kv_kernel(x_ref, g_ref, w_ref, k_ref, v_ref, *, d):
    h = _rms(x_ref[...], g_ref[...]).astype(BF16)
    k_ref[...] = _dot(h, w_ref[:, :d])
    v_ref[...] = _dot(h, w_ref[:, d:])


def _mem_kv(mem2, g, w_kv):
    m, d = mem2.shape
    tm = _row_tile(m, 512)
    row = pl.BlockSpec((tm, d), lambda i: (i, 0))
    return pl.pallas_call(
        functools.partial(_mem_kv_kernel, d=d),
        out_shape=[jax.ShapeDtypeStruct((m, d), F32)] * 2,
        grid=(m // tm,),
        in_specs=[row, _const_spec((1, d)), _const_spec((d, 2 * d))],
        out_specs=[row, row],
        compiler_params=_params("parallel"),
        name="mem_kv",
    )(mem2, g.reshape(1, d), w_kv.astype(BF16))


def _mem_attn_kernel(x_ref, g_ref, wq_ref, mk_ref, mv_ref, wo_ref, o_ref, *, hd):
    x = x_ref[...]
    h = _rms(x, g_ref[...]).astype(BF16)
    q = (_dot(h, wq_ref[...]) * (hd ** -0.5)).astype(BF16)
    heads = []
    for a in range(MEM_HEADS):
        cols = slice(a * hd, (a + 1) * hd)
        lg = _dot_nt(q[:, cols], mk_ref[0, :, cols].astype(BF16))
        p = jnp.exp(lg - jnp.max(lg, axis=1, keepdims=True))
        p = (p / jnp.sum(p, axis=1, keepdims=True)).astype(BF16)
        heads.append(_dot(p, mv_ref[0, :, cols].astype(BF16)).astype(BF16))
    o_ref[...] = x + _dot(jnp.concatenate(heads, axis=1), wo_ref[...])


def _mem_attn(x2, g, w_q, mk, mv, w_o, rows_per_batch):
    m, d = x2.shape
    n_mem = mk.shape[1]
    tm = _row_tile(rows_per_batch, 512)
    per = rows_per_batch // tm
    row = pl.BlockSpec((tm, d), lambda i: (i, 0))
    mem = pl.BlockSpec((1, n_mem, d), lambda i: (i // per, 0, 0))
    return pl.pallas_call(
        functools.partial(_mem_attn_kernel, hd=d // MEM_HEADS),
        out_shape=jax.ShapeDtypeStruct((m, d), F32),
        grid=(m // tm,),
        in_specs=[row, _const_spec((1, d)), _const_spec((d, d)), mem, mem, _const_spec((d, d))],
        out_specs=row,
        compiler_params=_params("parallel"),
        name="mem_attn",
    )(x2, g.reshape(1, d), w_q.astype(BF16), mk, mv, w_o.astype(BF16))


def _mlp_kernel(*refs, n_chunks, chunk, final):
    if final:
        x_ref, g_ref, w1_ref, w2_ref, gf_ref, o_ref = refs
    else:
        x_ref, g_ref, w1_ref, w2_ref, o_ref = refs
    x = x_ref[...]
    h = _rms(x, g_ref[...]).astype(BF16)
    acc = x
    for c in range(n_chunks):
        a = jnp.maximum(_dot(h, w1_ref[:, c * chunk:(c + 1) * chunk]), 0.0)
        acc = acc + _dot((a * a).astype(BF16), w2_ref[c * chunk:(c + 1) * chunk, :])
    if final:
        acc = _rms(acc, gf_ref[...])
    o_ref[...] = acc


def _mlp(x2, g, w1, w2, g_final=None):
    m, d = x2.shape
    dff = w1.shape[1]
    chunk = min(dff, 1024)
    tm = _row_tile(m, 512)
    row = pl.BlockSpec((tm, d), lambda i: (i, 0))
    final = g_final is not None
    in_specs = [row, _const_spec((1, d)), _const_spec((d, dff)), _const_spec((dff, d))]
    args = [x2, g.reshape(1, d), w1.astype(BF16), w2.astype(BF16)]
    if final:
        in_specs.append(_const_spec((1, d)))
        args.append(g_final.reshape(1, d))
    return pl.pallas_call(
        functools.partial(_mlp_kernel, n_chunks=dff // chunk, chunk=chunk, final=final),
        out_shape=jax.ShapeDtypeStruct((m, d), F32),
        grid=(m // tm,),
        in_specs=in_specs,
        out_specs=row,
        compiler_params=_params("parallel"),
        name="mlp",
    )(*args)


def _conv_glu_kernel(x_ref, g_ref, w_ref, b_ref, u_ref, *, d):
    h = _rms(x_ref[...], g_ref[...]).astype(BF16)
    a = _dot(h, w_ref[:, :d]) + b_ref[:, :d]
    gate = _dot(h, w_ref[:, d:]) + b_ref[:, d:]
    u_ref[...] = a * (1.0 / (1.0 + jnp.exp(-gate)))


def _conv_glu(x2, g, w_pw1, b_pw1):
    m, d = x2.shape
    tm = _row_tile(m, 512)
    row = pl.BlockSpec((tm, d), lambda i: (i, 0))
    return pl.pallas_call(
        functools.partial(_conv_glu_kernel, d=d),
        out_shape=jax.ShapeDtypeStruct((m, d), F32),
        grid=(m // tm,),
        in_specs=[row, _const_spec((1, d)), _const_spec((d, 2 * d)), _const_spec((1, 2 * d))],
        out_specs=row,
        compiler_params=_params("parallel"),
        name="conv_glu",
    )(x2, g.reshape(1, d), w_pw1.astype(BF16), b_pw1.reshape(1, 2 * d))


def _conv_rest_kernel(x_ref, u_ref, prev_ref, init_ref, wdw_ref, bdw_ref, lng_ref, lnb_ref,
                      w2_ref, b2_ref, o_ref, ext_ref, sh_ref, y_ref, *, tm, rc, lc):
    t = pl.program_id(1)
    d = u_ref.shape[2]
    pad = CONV_WIDTH - 1

    @pl.when(t == 0)
    def _():
        ext_ref[0:HALO, :] = init_ref[0]

    @pl.when(t > 0)
    def _():
        ext_ref[0:HALO, :] = prev_ref[0]

    ext_ref[HALO:HALO + tm, :] = u_ref[0]

    for s in range(SUBLANES):
        rows = tm + SUBLANES * ((CONV_WIDTH - 1 - s) // SUBLANES)
        sh_ref[s, 0:rows, :] = ext_ref[pl.ds(HALO - pad + s, rows), :]
    for r0 in range(0, tm, rc):
        for c0 in range(0, d, lc):
            cols = slice(c0, c0 + lc)
            y = jnp.broadcast_to(bdw_ref[:, cols], (rc, lc))
            for w in range(CONV_WIDTH):
                a, s = divmod(w, SUBLANES)
                y = y + sh_ref[s, r0 + SUBLANES * a:r0 + SUBLANES * a + rc, cols] * wdw_ref[w:w + 1, cols]
            y_ref[r0:r0 + rc, cols] = y

    y = y_ref[...]
    mu = jnp.mean(y, axis=-1, keepdims=True)
    yc = y - mu
    var = jnp.mean(yc * yc, axis=-1, keepdims=True)
    yn = yc * lax.rsqrt(var + EPS) * lng_ref[...] + lnb_ref[...]
    act = (yn * (1.0 / (1.0 + jnp.exp(-yn)))).astype(BF16)
    o_ref[0] = x_ref[0] + _dot(act, w2_ref[...]) + b2_ref[...]


def _conv_rest(x3, u3, init, w_dw, b_dw, ln_g, ln_b, w_pw2, b_pw2):
    b, t, d = x3.shape
    tm = _row_tile(t, 256)
    rc = min(tm, 64)
    lc = min(d, 256)
    assert tm % HALO == 0 and tm % rc == 0 and d % lc == 0
    per = tm // HALO
    tile = pl.BlockSpec((1, tm, d), lambda bi, ti: (bi, ti, 0))
    prev = pl.BlockSpec((1, HALO, d), lambda bi, ti: (bi, jnp.maximum(ti * per - 1, 0), 0))
    first = pl.BlockSpec((1, HALO, d), lambda bi, ti: (bi, 0, 0))
    vec = _const_spec((1, d))
    wdw = jnp.pad(w_dw, ((0, HALO - CONV_WIDTH), (0, 0)))
    return pl.pallas_call(
        functools.partial(_conv_rest_kernel, tm=tm, rc=rc, lc=lc),
        out_shape=jax.ShapeDtypeStruct((b, t, d), F32),
        grid=(b, t // tm),
        in_specs=[tile, tile, prev, first, _const_spec((HALO, d)), vec, vec, vec,
                  _const_spec((d, d)), vec],
        out_specs=tile,
        scratch_shapes=[pltpu.VMEM((HALO + tm, d), F32),
                        pltpu.VMEM((SUBLANES, tm + HALO - SUBLANES, d), F32),
                        pltpu.VMEM((tm, d), F32)],
        compiler_params=_params("parallel", "arbitrary"),
        name="conv_rest",
    )(x3, u3, u3, init, wdw, b_dw.reshape(1, d), ln_g.reshape(1, d), ln_b.reshape(1, d),
      w_pw2.astype(BF16), b_pw2.reshape(1, d))


def _mixer_attn(x3, g, w_in, w_out, bias_tiles, cache=None):
    b, t, d = x3.shape
    x2 = x3.reshape(b * t, d)
    q, k, v, ki, kb, vb, kib, qi, wi = _attn_proj(x2, g, w_in)
    r3 = lambda a: a.reshape(b, t, a.shape[-1])
    kb, vb, kib = r3(kb), r3(vb), r3(kib)
    past = 0
    if cache is not None:
        ck, cv, cki = cache
        past = ck.shape[1]
        kb = jnp.concatenate([ck.reshape(b, past, -1).astype(BF16), kb], axis=1)
        vb = jnp.concatenate([cv.reshape(b, past, -1).astype(BF16), vb], axis=1)
        kib = jnp.concatenate([cki.astype(BF16), kib], axis=1)
    n_keys = past + t
    k_sel = min(TOPK_MAX, n_keys // 4)
    tpad = -(-t // KEY_BLOCK) * KEY_BLOCK
    lp = max(-(-n_keys // KEY_BLOCK) * KEY_BLOCK, past + tpad)
    padt = lambda a, n: a if a.shape[1] == n else jnp.pad(a, ((0, 0), (0, n - a.shape[1]), (0, 0)))
    o = _sparse_attn(padt(r3(q), tpad), padt(r3(qi), tpad), padt(r3(wi), tpad),
                     padt(kb, lp), padt(vb, lp), padt(kib, lp), bias_tiles,
                     q_off=past, n_keys=n_keys, k_sel=k_sel)[:, :t]
    x2 = _matmul_res(x2, o.reshape(b * t, d), w_out)
    return (x2.reshape(b, t, d), k.reshape(b, t, N_KV_HEADS, HEAD_DIM),
            v.reshape(b, t, N_KV_HEADS, HEAD_DIM), ki.reshape(b, t, IDX_DIM))


def _mixer_conv(x3, g, w_pw1, b_pw1, w_dw, b_dw, ln_g, ln_b, w_pw2, b_pw2, state=None):
    b, t, d = x3.shape
    pad = CONV_WIDTH - 1
    u3 = _conv_glu(x3.reshape(b * t, d), g, w_pw1, b_pw1).reshape(b, t, d)
    if state is None:
        init = jnp.zeros((b, HALO, d), F32)
        tail = u3[:, -pad:] if t >= pad else jnp.pad(u3, ((0, 0), (pad - t, 0), (0, 0)))
    else:
        init = jnp.pad(state.astype(F32), ((0, 0), (HALO - pad, 0), (0, 0)))
        tail = jnp.concatenate([state.astype(F32), u3], axis=1)[:, -pad:]
    x3 = _conv_rest(x3, u3, init, w_dw, b_dw, ln_g, ln_b, w_pw2, b_pw2)
    return x3, tail


def kernel(x_prompt, x_sample, cache_attn_k, cache_attn_v, cache_attn_kidx, state_conv, cache_mem_k, cache_mem_v, mem_prompt, rel_bias, g_mix, w_in_attn, w_out_attn, w_pw1, b_pw1, w_dw, b_dw, ln_g, ln_b, w_pw2, b_pw2, g_mem_q, g_mem_src, w_mem_q, w_mem_kv, w_mem_o, g_mlp, w_mlp1, w_mlp2, g_final):
    depth = g_mix.shape[0]
    bp, tp, d = x_prompt.shape
    bs, ts, _ = x_sample.shape
    n_mem = mem_prompt.shape[1]
    mem_hd = d // MEM_HEADS
    bias_tiles = _bias_tiles(rel_bias)
    xp, xs = x_prompt, x_sample
    kp_l, vp_l, kip_l, ks_l, vs_l, kis_l = [], [], [], [], [], []
    convp_l, convs_l, memk_l, memv_l = [], [], [], []
    for i in range(depth):
        j = i // 2
        if i % 2 == 0:
            xp, kp, vp, kip = _mixer_attn(xp, g_mix[i], w_in_attn[j], w_out_attn[j], bias_tiles)
            xs, ks, vs, kis = _mixer_attn(
                xs, g_mix[i], w_in_attn[j], w_out_attn[j], bias_tiles,
                cache=(cache_attn_k[j], cache_attn_v[j], cache_attn_kidx[j]))
            kp_l.append(kp); vp_l.append(vp); kip_l.append(kip)
            ks_l.append(ks); vs_l.append(vs); kis_l.append(kis)
        else:
            cw = (w_pw1[j], b_pw1[j], w_dw[j], b_dw[j], ln_g[j], ln_b[j], w_pw2[j], b_pw2[j])
            xp, cp = _mixer_conv(xp, g_mix[i], *cw)
            xs, cs = _mixer_conv(xs, g_mix[i], *cw, state=state_conv[j])
            convp_l.append(cp); convs_l.append(cs)
        mk, mv = _mem_kv(mem_prompt.reshape(bp * n_mem, d), g_mem_src[i], w_mem_kv[i])
        mk, mv = mk.reshape(bp, n_mem, d), mv.reshape(bp, n_mem, d)
        memk_l.append(mk.reshape(bp, n_mem, MEM_HEADS, mem_hd))
        memv_l.append(mv.reshape(bp, n_mem, MEM_HEADS, mem_hd))
        xp2 = _mem_attn(xp.reshape(bp * tp, d), g_mem_q[i], w_mem_q[i], mk, mv, w_mem_o[i], tp)
        xs2 = _mem_attn(xs.reshape(bs * ts, d), g_mem_q[i], w_mem_q[i],
                        cache_mem_k[i].reshape(bs, n_mem, d), cache_mem_v[i].reshape(bs, n_mem, d),
                        w_mem_o[i], ts)
        gf = g_final if i == depth - 1 else None
        xp = _mlp(xp2, g_mlp[i], w_mlp1[i], w_mlp2[i], gf).reshape(bp, tp, d)
        xs = _mlp(xs2, g_mlp[i], w_mlp1[i], w_mlp2[i], gf).reshape(bs, ts, d)
    return (xp, xs, jnp.stack(kp_l), jnp.stack(vp_l), jnp.stack(kip_l), jnp.stack(convp_l),
            jnp.stack(memk_l), jnp.stack(memv_l), jnp.stack(ks_l), jnp.stack(vs_l),
            jnp.stack(kis_l), jnp.stack(convs_l))
```

```python
import functools
import math

import jax
import jax.numpy as jnp
from jax import lax
from jax.experimental import pallas as pl
from jax.experimental.pallas import tpu as pltpu

CHUNK = 64
HEAD_DIM = 64
N_KV_HEADS = 4
IDX_HEADS = 8
IDX_DIM = 64
TOPK_MAX = 256
N_BUCKETS = 32
MAX_DISTANCE = 128
CONV_WIDTH = 31
MEM_HEADS = 4
EPS = 1e-6

LANES = 128
SUBLANES = 8
KEY_BLOCK = 128
HEADS_PER_TILE = 2
COUNT_UNROLL = 8
V_AUG_ROWS = HEAD_DIM + 16
LOG2E = math.log2(math.e)
HALO = 32
VMEM_LIMIT = 56 * 1024 * 1024

NEG_BIG = -1e30
F32 = jnp.float32
BF16 = jnp.bfloat16

KEY_NEG_INF = -2139095041
KEY_POS_INF = 2139095040


def _const_spec(shape):
    nd = len(shape)
    return pl.BlockSpec(shape, lambda *_: (0,) * nd, pipeline_mode=pl.Buffered(1))


def _params(*sem):
    return pltpu.CompilerParams(dimension_semantics=sem, vmem_limit_bytes=VMEM_LIMIT)


def _rms(x, g):
    ms = jnp.mean(x * x, axis=-1, keepdims=True)
    return x * lax.rsqrt(ms + EPS) * g


def _dot(a, b):
    return jnp.dot(a, b, preferred_element_type=F32)


def _dot_nt(a, b):
    return lax.dot_general(a, b, (((1,), (1,)), ((), ())), preferred_element_type=F32)


def _row_tile(m, pref):
    t = min(m, pref)
    assert m % t == 0, (m, t)
    return t


def _bias_tiles_kernel(tab_ref, bt_ref, *, n_heads):
    nb = N_BUCKETS // 2
    max_exact = nb // 2
    c = lax.broadcasted_iota(jnp.int32, (KEY_BLOCK, KEY_BLOCK), 0)
    r = lax.broadcasted_iota(jnp.int32, (KEY_BLOCK, KEY_BLOCK), 1)
    for d in range(2):
        rel = c - r - d * KEY_BLOCK
        n = jnp.abs(rel)
        nf = jnp.maximum(n, 1).astype(F32)
        large = max_exact + (jnp.log(nf / max_exact) / math.log(MAX_DISTANCE / max_exact)
                             * (nb - max_exact)).astype(jnp.int32)
        large = jnp.minimum(large, nb - 1)
        bucket = jnp.where(rel > 0, nb, 0) + jnp.where(n < max_exact, n, large)
        for h in range(n_heads):
            acc = jnp.zeros((KEY_BLOCK, KEY_BLOCK), F32)
            for b in range(N_BUCKETS):
                acc = jnp.where(bucket == b, tab_ref[b, h], acc)
            g = h % HEADS_PER_TILE
            rows = slice((1 - d) * KEY_BLOCK, (2 - d) * KEY_BLOCK)
            bt_ref[h // HEADS_PER_TILE, rows, g * KEY_BLOCK:(g + 1) * KEY_BLOCK] = (
                (acc - tab_ref[nb - 1, h]) * LOG2E)


def _bias_tiles(rel_bias):
    n_heads = rel_bias.shape[1]
    assert n_heads % HEADS_PER_TILE == 0
    return pl.pallas_call(
        functools.partial(_bias_tiles_kernel, n_heads=n_heads),
        out_shape=jax.ShapeDtypeStruct(
            (n_heads // HEADS_PER_TILE, 2 * KEY_BLOCK, HEADS_PER_TILE * KEY_BLOCK), F32),
        in_specs=[pl.BlockSpec(memory_space=pltpu.SMEM)],
        out_specs=pl.BlockSpec(memory_space=pltpu.VMEM),
        name="bias_tiles",
    )(rel_bias)


def _attn_proj_kernel(x_ref, g_ref, w_ref, q_ref, k_ref, v_ref, ki_ref, kb_ref, vb_ref,
                      kib_ref, qi_ref, wi_ref, *, dq, dkv, dqi):
    h = _rms(x_ref[...], g_ref[...]).astype(BF16)
    o = 0
    q_ref[...] = (_dot(h, w_ref[:, o:o + dq]) * (HEAD_DIM ** -0.5)).astype(BF16)
    o += dq
    k = _dot(h, w_ref[:, o:o + dkv])
    k_ref[...] = k
    kb_ref[...] = k.astype(BF16)
    o += dkv
    v = _dot(h, w_ref[:, o:o + dkv])
    v_ref[...] = v
    vb_ref[...] = v.astype(BF16)
    o += dkv
    qi_ref[...] = (_dot(h, w_ref[:, o:o + dqi]) * (IDX_DIM ** -0.5)).astype(BF16)
    o += dqi
    ki = _dot(h, w_ref[:, o:o + LANES])[:, :IDX_DIM]
    ki_ref[...] = ki
    kib_ref[...] = ki.astype(BF16)
    o += LANES
    wi_ref[...] = _dot(h, w_ref[:, o:o + LANES]) * (IDX_HEADS ** -0.5)


def _attn_proj(x2, g, w_in):
    m, d = x2.shape
    dq = d
    dkv = N_KV_HEADS * HEAD_DIM
    dqi = IDX_HEADS * IDX_DIM
    base = dq + 2 * dkv + dqi
    w_main = w_in[:, :base]
    w_ki = jnp.pad(w_in[:, base:base + IDX_DIM], ((0, 0), (0, LANES - IDX_DIM)))
    w_wi = jnp.pad(w_in[:, base + IDX_DIM:], ((0, 0), (0, LANES - IDX_HEADS)))
    w = jnp.concatenate([w_main, w_ki, w_wi], axis=1).astype(BF16)
    tm = _row_tile(m, 512)
    row = lambda n: pl.BlockSpec((tm, n), lambda i: (i, 0))
    outs = [(dq, BF16), (dkv, F32), (dkv, F32), (IDX_DIM, F32), (dkv, BF16), (dkv, BF16),
            (IDX_DIM, BF16), (dqi, BF16), (LANES, F32)]
    return pl.pallas_call(
        functools.partial(_attn_proj_kernel, dq=dq, dkv=dkv, dqi=dqi),
        out_shape=[jax.ShapeDtypeStruct((m, n), dt) for n, dt in outs],
        grid=(m // tm,),
        in_specs=[row(d), _const_spec((1, d)), _const_spec(w.shape)],
        out_specs=[row(n) for n, _ in outs],
        compiler_params=_params("parallel"),
        name="attn_proj",
    )(x2, g.reshape(1, d), w)


def _sparse_attn_kernel(q_ref, qi_ref, wi_ref, k_ref, vt_ref, ki_ref, bt_ref, o_ref,
                        s_ref, hi_ref, lo_ref, dots_ref, qgt_ref, qit_ref, ot_ref, lg_ref, p_ref, al_ref,
                        mb_ref, *state_refs,
                        q_off, n_keys, k_sel, group):
    tq = KEY_BLOCK
    hpt = HEADS_PER_TILE
    n_tiles = qgt_ref.shape[0]
    m_refs, acc_refs = state_refs[:n_tiles], state_refs[n_tiles:]
    i = pl.program_id(1)
    qs = q_off // KEY_BLOCK + i
    n_kb = qs + 1
    q_start = q_off + i * tq

    qt = q_ref[0].astype(F32).T * LOG2E
    for t in range(n_tiles):
        q2 = jnp.concatenate(
            [qt[(t * hpt + g) * HEAD_DIM:(t * hpt + g + 1) * HEAD_DIM, :] for g in range(hpt)], axis=1)
        q_hi = q2.astype(BF16)
        q_lo = (q2 - q_hi.astype(F32)).astype(BF16)
        qgt_ref[t] = jnp.concatenate([q_hi, q_lo], axis=0)
    qit = qi_ref[0].astype(F32).T
    qit_ref[...] = jnp.concatenate([qit[h * IDX_DIM:(h + 1) * IDX_DIM, :] for h in range(IDX_HEADS)],
                                   axis=1).astype(BF16)
    last_kv = k_ref.shape[1] // KEY_BLOCK - 1
    wit = wi_ref[0].T

    qpos = q_start + lax.broadcasted_iota(jnp.int32, (1, tq), 1)
    lim = jnp.minimum((qpos // CHUNK + 1) * CHUNK, n_keys)
    kidx = lax.broadcasted_iota(jnp.int32, (KEY_BLOCK, tq), 0)

    def put_scores(j, sc):
        s_ref[j] = sc
        bits = lax.bitcast_convert_type(jnp.where(sc == 0.0, 0.0, sc), jnp.int32)
        key = bits ^ ((bits >> 31) & 0x7FFFFFFF)
        hi_ref[j] = (key >> 16).astype(jnp.int16)
        lo_ref[j] = ((key & 0xFFFF) - 32768).astype(jnp.int16)

    def score_pair(pair, carry):
        js = (2 * pair, 2 * pair + 1)
        for slot, j in enumerate(js):
            k0 = pl.multiple_of(jnp.minimum(j, last_kv) * KEY_BLOCK, KEY_BLOCK)
            dots_ref[slot] = _dot(ki_ref[0, pl.ds(k0, KEY_BLOCK), :], qit_ref[...])
        for slot, j in enumerate(js):
            acc = jnp.zeros((KEY_BLOCK, tq), F32)
            for h in range(IDX_HEADS):
                acc = acc + wit[h:h + 1, :] * jnp.maximum(dots_ref[slot, :, h * tq:(h + 1) * tq], 0.0)
            put_scores(j, jnp.where(kidx + j * KEY_BLOCK < lim, acc, -jnp.inf))
        return carry

    n_pairs = (n_kb + 1) // 2
    lax.fori_loop(0, n_pairs, score_pair, 0)
    for u in range(COUNT_UNROLL):
        hi_ref[2 * n_pairs + u] = jnp.full((KEY_BLOCK, tq), -32768, jnp.int16)
        lo_ref[2 * n_pairs + u] = jnp.full((KEY_BLOCK, tq), -32768, jnp.int16)
    put_scores(n_kb, jnp.full((KEY_BLOCK, tq), -jnp.inf, F32))
    n_count = (2 * n_pairs + COUNT_UNROLL - 1) // COUNT_UNROLL

    def count16(ref, cand):
        cb = jnp.broadcast_to(cand, (KEY_BLOCK, tq)).astype(jnp.int16)
        one, zero = jnp.ones((), jnp.int16), jnp.zeros((), jnp.int16)

        def body(p, c):
            for u in range(COUNT_UNROLL):
                c = c + jnp.where(ref[COUNT_UNROLL * p + u] >= cb, one, zero)
            return c
        c = lax.fori_loop(0, n_count, body, jnp.zeros((KEY_BLOCK, tq), jnp.int16))
        return jnp.sum(c.astype(F32), axis=0, keepdims=True)

    def bisect16(ref, lo0, hi0):
        def step(_, carry):
            lo, hi = carry
            mid = (lo + hi) >> 1
            ok = count16(ref, mid) >= k_sel
            return jnp.where(ok, mid, lo), jnp.where(ok, hi, mid)
        lo, _ = lax.fori_loop(0, 16, step, (jnp.full((1, tq), lo0, jnp.int32),
                                            jnp.full((1, tq), hi0, jnp.int32)))
        return lo

    key_hi = bisect16(hi_ref, KEY_NEG_INF >> 16, (KEY_POS_INF >> 16) + 1)
    hb = jnp.broadcast_to(key_hi, (KEY_BLOCK, tq)).astype(jnp.int16)

    def narrow(p, carry):
        for u in range(COUNT_UNROLL):
            j = COUNT_UNROLL * p + u
            hi = hi_ref[j]
            lo_ref[j] = jnp.where(hi == hb, lo_ref[j],
                                  jnp.where(hi > hb, jnp.int16(32767), jnp.int16(-32768)))
        return carry
    lax.fori_loop(0, n_count, narrow, 0)
    key_lo = bisect16(lo_ref, -32768, 32768)

    def key_to_f32(key):
        bits = key ^ ((key >> 31) & 0x7FFFFFFF)
        return lax.bitcast_convert_type(bits, F32)

    thr = key_to_f32((key_hi << 16) + (key_lo + 32768))

    def count(cand, strict):
        cb = jnp.broadcast_to(cand, (KEY_BLOCK, tq))
        hit = (lambda s: s > cb) if strict else (lambda s: s >= cb)

        def body(p, c):
            c = c + jnp.where(hit(s_ref[2 * p]), 1.0, 0.0)
            return c + jnp.where(hit(s_ref[2 * p + 1]), 1.0, 0.0)
        c = lax.fori_loop(0, n_pairs, body, jnp.zeros((KEY_BLOCK, tq), F32))
        return jnp.sum(c, axis=0, keepdims=True)

    surplus = count16(lo_ref, key_lo) - k_sel

    @pl.when(jnp.max(surplus) > 0.0)
    def _():
        n_ties = k_sel - count(thr, True)
        row_i = lax.broadcasted_iota(jnp.int32, (KEY_BLOCK, KEY_BLOCK), 0)
        col_i = lax.broadcasted_iota(jnp.int32, (KEY_BLOCK, KEY_BLOCK), 1)
        lower = jnp.where(col_i < row_i, 1.0, 0.0).astype(BF16)
        ones = jnp.ones((KEY_BLOCK, KEY_BLOCK), BF16)

        def tie_body(j, seen):
            s = s_ref[j]
            eq = s == thr
            e = jnp.where(eq, 1.0, 0.0).astype(BF16)
            before = _dot(lower, e) + seen
            s_ref[j] = jnp.where(eq & (before >= n_ties), -jnp.inf, s)
            return seen + _dot(ones, e)

        lax.fori_loop(0, n_kb, tie_body, jnp.zeros((KEY_BLOCK, tq), F32))

    thr_sel = jnp.broadcast_to(jnp.maximum(thr, jnp.finfo(F32).min), (KEY_BLOCK, tq))

    for t in range(n_tiles):
        m_refs[t][...] = jnp.full(m_refs[t].shape, NEG_BIG, F32)
        acc_refs[t][...] = jnp.zeros(acc_refs[t].shape, F32)
    p_ref[...] = jnp.zeros(p_ref.shape, BF16)
    al_ref[...] = jnp.ones(al_ref.shape, F32)

    def far_blocks(db):
        out = []
        for j in (2 * db, 2 * db + 1):
            js = jnp.where(j < qs - 1, j, n_kb)
            out.append((js, jnp.minimum(js, last_kv)))
        return out

    near = [(jnp.where(qs >= 1, qs - 1, n_kb), jnp.maximum(qs - 1, 0)), (qs, qs)]

    def mask_of(blocks):
        m = jnp.concatenate([jnp.where(s_ref[js] >= thr_sel, 0.0, NEG_BIG) for js, _ in blocks], axis=0)
        return jnp.concatenate([m] * hpt, axis=1)

    def logits(blocks, mask, biased, slot, t):
        n = (t * hpt) // group
        kn = jnp.concatenate(
            [k_ref[0, pl.ds(pl.multiple_of(jk * KEY_BLOCK, KEY_BLOCK), KEY_BLOCK),
                   n * 2 * HEAD_DIM:(n + 1) * 2 * HEAD_DIM] for _, jk in blocks], axis=0)
        lg = _dot(kn, qgt_ref[t]) + mask
        if biased:
            lg = lg + bt_ref[t]
        lg_ref[slot, t] = lg
        mb_ref[slot, t] = jnp.max(lg, axis=0, keepdims=True)

    def softmax(slot, t):
        m_old = m_refs[t][...]
        m_new = jnp.maximum(m_old, mb_ref[slot, t])
        al_ref[slot, t] = jnp.exp2(m_old - m_new)
        p_ref[slot, t] = jnp.exp2(lg_ref[slot, t] - m_new).astype(BF16)
        m_refs[t][...] = m_new

    def update(blocks, slot, t):
        n = (t * hpt) // group
        vtn = jnp.concatenate([vt_ref[0, jk, n] for _, jk in blocks], axis=1)
        acc_refs[t][...] = al_ref[slot, t] * acc_refs[t][...] + _dot(vtn, p_ref[slot, t])

    n_far = qs // 2
    first = far_blocks(0)
    mask_first = mask_of(first)
    for t in range(n_tiles):
        logits(first, mask_first, False, 0, t)

    def far_body(trip, carry):
        d0 = 2 * trip
        blk1, blk2 = far_blocks(d0 + 1), far_blocks(d0 + 2)
        mask1, mask2 = mask_of(blk1), mask_of(blk2)
        old0, old1 = far_blocks(jnp.maximum(d0 - 2, 0)), far_blocks(jnp.maximum(d0 - 1, 0))
        for t in range(n_tiles):
            update(old0, 0, t)
            logits(blk1, mask1, False, 1, t)
            softmax(0, t)
        for t in range(n_tiles):
            update(old1, 1, t)
            logits(blk2, mask2, False, 0, t)
            softmax(1, t)
        return carry

    n_trips = (n_far + 1) // 2
    lax.fori_loop(0, n_trips, far_body, 0)

    last0 = far_blocks(jnp.maximum(2 * n_trips - 2, 0))
    last1 = far_blocks(jnp.maximum(2 * n_trips - 1, 0))
    mask_near = mask_of(near)
    for t in range(n_tiles):
        update(last0, 0, t)
        logits(near, mask_near, True, 0, t)
    for t in range(n_tiles):
        update(last1, 1, t)
        softmax(0, t)
    for t in range(n_tiles):
        update(near, 0, t)

    for t in range(n_tiles):
        on = acc_refs[t][0:HEAD_DIM, :] / acc_refs[t][HEAD_DIM:HEAD_DIM + 1, :]
        for g in range(hpt):
            hh = t * hpt + g
            ot_ref[hh * HEAD_DIM:(hh + 1) * HEAD_DIM, :] = on[:, g * tq:(g + 1) * tq]
    o_ref[0] = ot_ref[...].T.astype(BF16)


def _sparse_attn(q, qi, wi, kb, vb, kib, bias_tiles, *, q_off, n_keys, k_sel):
    b, t, dq = q.shape
    lp = kb.shape[1]
    dkv = kb.shape[2]
    tq = KEY_BLOCK
    n_heads = dq // HEAD_DIM
    group = n_heads // N_KV_HEADS
    hpt = HEADS_PER_TILE
    n_tiles = n_heads // hpt
    assert group % hpt == 0
    nkb = lp // KEY_BLOCK
    assert t % tq == 0 and lp % KEY_BLOCK == 0 and q_off % KEY_BLOCK == 0 and q_off + t <= lp
    vt = jnp.transpose(vb.reshape(b, nkb, KEY_BLOCK, N_KV_HEADS, HEAD_DIM), (0, 1, 3, 4, 2))
    vt = jnp.concatenate(
        [vt, jnp.ones((b, nkb, N_KV_HEADS, 1, KEY_BLOCK), BF16),
         jnp.zeros((b, nkb, N_KV_HEADS, V_AUG_ROWS - HEAD_DIM - 1, KEY_BLOCK), BF16)], axis=3)
    k2 = jnp.concatenate([kb.reshape(b, lp, N_KV_HEADS, HEAD_DIM)] * 2, axis=-1).reshape(b, lp, 2 * dkv)
    qblk = lambda n: pl.BlockSpec((1, tq, n), lambda bi, i: (bi, i, 0))
    kblk = lambda n: pl.BlockSpec((1, lp, n), lambda bi, i: (bi, 0, 0))
    return pl.pallas_call(
        functools.partial(_sparse_attn_kernel, q_off=q_off, n_keys=n_keys, k_sel=k_sel, group=group),
        out_shape=jax.ShapeDtypeStruct((b, t, dq), BF16),
        grid=(b, t // tq),
        in_specs=[qblk(dq), qblk(qi.shape[2]), qblk(LANES), kblk(2 * dkv),
                  pl.BlockSpec((1, nkb, N_KV_HEADS, V_AUG_ROWS, KEY_BLOCK), lambda bi, i: (bi, 0, 0, 0, 0)),
                  kblk(kib.shape[2]), _const_spec(bias_tiles.shape)],
        out_specs=qblk(dq),
        scratch_shapes=[
            pltpu.VMEM((nkb + 1, KEY_BLOCK, tq), F32),
            pltpu.VMEM((nkb + 1 + COUNT_UNROLL, KEY_BLOCK, tq), jnp.int16),
            pltpu.VMEM((nkb + 1 + COUNT_UNROLL, KEY_BLOCK, tq), jnp.int16),
            pltpu.VMEM((2, KEY_BLOCK, IDX_HEADS * tq), F32),
            pltpu.VMEM((n_tiles, 2 * HEAD_DIM, hpt * tq), BF16),
            pltpu.VMEM((IDX_DIM, IDX_HEADS * tq), BF16),
            pltpu.VMEM((dq, tq), F32),
            pltpu.VMEM((2, n_tiles, 2 * KEY_BLOCK, hpt * tq), F32),
            pltpu.VMEM((2, n_tiles, 2 * KEY_BLOCK, hpt * tq), BF16),
            pltpu.VMEM((2, n_tiles, 1, hpt * tq), F32),
            pltpu.VMEM((2, n_tiles, 1, hpt * tq), F32),
        ] + [pltpu.VMEM((1, hpt * tq), F32)] * n_tiles
          + [pltpu.VMEM((V_AUG_ROWS, hpt * tq), F32)] * n_tiles,
        compiler_params=_params("parallel", "arbitrary"),
        name="sparse_attn",
    )(q, qi, wi, k2, vt, kib, bias_tiles)


def _matmul_res_kernel(x_ref, a_ref, w_ref, o_ref):
    o_ref[...] = x_ref[...] + _dot(a_ref[...], w_ref[...])


def _matmul_res(x2, a2, w):
    m, d = x2.shape
    ka = a2.shape[1]
    tm = _row_tile(m, 512)
    return pl.pallas_call(
        _matmul_res_kernel,
        out_shape=jax.ShapeDtypeStruct((m, d), F32),
        grid=(m // tm,),
        in_specs=[pl.BlockSpec((tm, d), lambda i: (i, 0)), pl.BlockSpec((tm, ka), lambda i: (i, 0)),
                  _const_spec((ka, d))],
        out_specs=pl.BlockSpec((tm, d), lambda i: (i, 0)),
        compiler_params=_params("parallel"),
        name="attn_out_proj",
    )(x2, a2, w.astype(BF16))


def _mem_kv_kernel(x_ref, g_ref, w_ref, k_ref, v_ref, *, d):
    h = _rms(x_ref[...], g_ref[...]).astype(BF16)
    k_ref[...] = _dot(h, w_ref[:, :d])
    v_ref[...] = _dot(h, w_ref[:, d:])


def _mem_kv(mem2, g, w_kv):
    m, d = mem2.shape
    tm = _row_tile(m, 512)
    row = pl.BlockSpec((tm, d), lambda i: (i, 0))
    return pl.pallas_call(
        functools.partial(_mem_kv_kernel, d=d),
        out_shape=[jax.ShapeDtypeStruct((m, d), F32)] * 2,
        grid=(m // tm,),
        in_specs=[row, _const_spec((1, d)), _const_spec((d, 2 * d))],
        out_specs=[row, row],
        compiler_params=_params("parallel"),
        name="mem_kv",
    )(mem2, g.reshape(1, d), w_kv.astype(BF16))


def _mem_attn_kernel(x_ref, g_ref, wq_ref, mk_ref, mv_ref, wo_ref, o_ref, *, hd):
    x = x_ref[...]
    h = _rms(x, g_ref[...]).astype(BF16)
    q = (_dot(h, wq_ref[...]) * (hd ** -0.5)).astype(BF16)
    heads = []
    for a in range(MEM_HEADS):
        cols = slice(a * hd, (a + 1) * hd)
        lg = _dot_nt(q[:, cols], mk_ref[0, :, cols].astype(BF16))
        p = jnp.exp(lg - jnp.max(lg, axis=1, keepdims=True))
        p = (p / jnp.sum(p, axis=1, keepdims=True)).astype(BF16)
        heads.append(_dot(p, mv_ref[0, :, cols].astype(BF16)).astype(BF16))
    o_ref[...] = x + _dot(jnp.concatenate(heads, axis=1), wo_ref[...])


def _mem_attn(x2, g, w_q, mk, mv, w_o, rows_per_batch):
    m, d = x2.shape
    n_mem = mk.shape[1]
    tm = _row_tile(rows_per_batch, 512)
    per = rows_per_batch // tm
    row = pl.BlockSpec((tm, d), lambda i: (i, 0))
    mem = pl.BlockSpec((1, n_mem, d), lambda i: (i // per, 0, 0))
    return pl.pallas_call(
        functools.partial(_mem_attn_kernel, hd=d // MEM_HEADS),
        out_shape=jax.ShapeDtypeStruct((m, d), F32),
        grid=(m // tm,),
        in_specs=[row, _const_spec((1, d)), _const_spec((d, d)), mem, mem, _const_spec((d, d))],
        out_specs=row,
        compiler_params=_params("parallel"),
        name="mem_attn",
    )(x2, g.reshape(1, d), w_q.astype(BF16), mk, mv, w_o.astype(BF16))


def _mlp_kernel(*refs, n_chunks, chunk, final):
    if final:
        x_ref, g_ref, w1_ref, w2_ref, gf_ref, o_ref = refs
    else:
        x_ref, g_ref, w1_ref, w2_ref, o_ref = refs
    x = x_ref[...]
    h = _rms(x, g_ref[...]).astype(BF16)
    acc = x
    for c in range(n_chunks):
        a = jnp.maximum(_dot(h, w1_ref[:, c * chunk:(c + 1) * chunk]), 0.0)
        acc = acc + _dot((a * a).astype(BF16), w2_ref[c * chunk:(c + 1) * chunk, :])
    if final:
        acc = _rms(acc, gf_ref[...])
    o_ref[...] = acc


def _mlp(x2, g, w1, w2, g_final=None):
    m, d = x2.shape
    dff = w1.shape[1]
    chunk = min(dff, 1024)
    tm = _row_tile(m, 512)
    row = pl.BlockSpec((tm, d), lambda i: (i, 0))
    final = g_final is not None
    in_specs = [row, _const_spec((1, d)), _const_spec((d, dff)), _const_spec((dff, d))]
    args = [x2, g.reshape(1, d), w1.astype(BF16), w2.astype(BF16)]
    if final:
        in_specs.append(_const_spec((1, d)))
        args.append(g_final.reshape(1, d))
    return pl.pallas_call(
        functools.partial(_mlp_kernel, n_chunks=dff // chunk, chunk=chunk, final=final),
        out_shape=jax.ShapeDtypeStruct((m, d), F32),
        grid=(m // tm,),
        in_specs=in_specs,
        out_specs=row,
        compiler_params=_params("parallel"),
        name="mlp",
    )(*args)


def _conv_glu_kernel(x_ref, g_ref, w_ref, b_ref, u_ref, *, d):
    h = _rms(x_ref[...], g_ref[...]).astype(BF16)
    a = _dot(h, w_ref[:, :d]) + b_ref[:, :d]
    gate = _dot(h, w_ref[:, d:]) + b_ref[:, d:]
    u_ref[...] = a * (1.0 / (1.0 + jnp.exp(-gate)))


def _conv_glu(x2, g, w_pw1, b_pw1):
    m, d = x2.shape
    tm = _row_tile(m, 512)
    row = pl.BlockSpec((tm, d), lambda i: (i, 0))
    return pl.pallas_call(
        functools.partial(_conv_glu_kernel, d=d),
        out_shape=jax.ShapeDtypeStruct((m, d), F32),
        grid=(m // tm,),
        in_specs=[row, _const_spec((1, d)), _const_spec((d, 2 * d)), _const_spec((1, 2 * d))],
        out_specs=row,
        compiler_params=_params("parallel"),
        name="conv_glu",
    )(x2, g.reshape(1, d), w_pw1.astype(BF16), b_pw1.reshape(1, 2 * d))


def _conv_rest_kernel(x_ref, u_ref, prev_ref, init_ref, wdw_ref, bdw_ref, lng_ref, lnb_ref,
                      w2_ref, b2_ref, o_ref, ext_ref, sh_ref, y_ref, *, tm, rc, lc):
    t = pl.program_id(1)
    d = u_ref.shape[2]
    pad = CONV_WIDTH - 1

    @pl.when(t == 0)
    def _():
        ext_ref[0:HALO, :] = init_ref[0]

    @pl.when(t > 0)
    def _():
        ext_ref[0:HALO, :] = prev_ref[0]

    ext_ref[HALO:HALO + tm, :] = u_ref[0]

    for s in range(SUBLANES):
        rows = tm + SUBLANES * ((CONV_WIDTH - 1 - s) // SUBLANES)
        sh_ref[s, 0:rows, :] = ext_ref[pl.ds(HALO - pad + s, rows), :]
    for r0 in range(0, tm, rc):
        for c0 in range(0, d, lc):
            cols = slice(c0, c0 + lc)
            y = jnp.broadcast_to(bdw_ref[:, cols], (rc, lc))
            for w in range(CONV_WIDTH):
                a, s = divmod(w, SUBLANES)
                y = y + sh_ref[s, r0 + SUBLANES * a:r0 + SUBLANES * a + rc, cols] * wdw_ref[w:w + 1, cols]
            y_ref[r0:r0 + rc, cols] = y

    y = y_ref[...]
    mu = jnp.mean(y, axis=-1, keepdims=True)
    yc = y - mu
    var = jnp.mean(yc * yc, axis=-1, keepdims=True)
    yn = yc * lax.rsqrt(var + EPS) * lng_ref[...] + lnb_ref[...]
    act = (yn * (1.0 / (1.0 + jnp.exp(-yn)))).astype(BF16)
    o_ref[0] = x_ref[0] + _dot(act, w2_ref[...]) + b2_ref[...]


def _conv_rest(x3, u3, init, w_dw, b_dw, ln_g, ln_b, w_pw2, b_pw2):
    b, t, d = x3.shape
    tm = _row_tile(t, 256)
    rc = min(tm, 64)
    lc = min(d, 256)
    assert tm % HALO == 0 and tm % rc == 0 and d % lc == 0
    per = tm // HALO
    tile = pl.BlockSpec((1, tm, d), lambda bi, ti: (bi, ti, 0))
    prev = pl.BlockSpec((1, HALO, d), lambda bi, ti: (bi, jnp.maximum(ti * per - 1, 0), 0))
    first = pl.BlockSpec((1, HALO, d), lambda bi, ti: (bi, 0, 0))
    vec = _const_spec((1, d))
    wdw = jnp.pad(w_dw, ((0, HALO - CONV_WIDTH), (0, 0)))
    return pl.pallas_call(
        functools.partial(_conv_rest_kernel, tm=tm, rc=rc, lc=lc),
        out_shape=jax.ShapeDtypeStruct((b, t, d), F32),
        grid=(b, t // tm),
        in_specs=[tile, tile, prev, first, _const_spec((HALO, d)), vec, vec, vec,
                  _const_spec((d, d)), vec],
        out_specs=tile,
        scratch_shapes=[pltpu.VMEM((HALO + tm, d), F32),
                        pltpu.VMEM((SUBLANES, tm + HALO - SUBLANES, d), F32),
                        pltpu.VMEM((tm, d), F32)],
        compiler_params=_params("parallel", "arbitrary"),
        name="conv_rest",
    )(x3, u3, u3, init, wdw, b_dw.reshape(1, d), ln_g.reshape(1, d), ln_b.reshape(1, d),
      w_pw2.astype(BF16), b_pw2.reshape(1, d))


def _mixer_attn(x3, g, w_in, w_out, bias_tiles, cache=None):
    b, t, d = x3.shape
    x2 = x3.reshape(b * t, d)
    q, k, v, ki, kb, vb, kib, qi, wi = _attn_proj(x2, g, w_in)
    r3 = lambda a: a.reshape(b, t, a.shape[-1])
    kb, vb, kib = r3(kb), r3(vb), r3(kib)
    past = 0
    if cache is not None:
        ck, cv, cki = cache
        past = ck.shape[1]
        kb = jnp.concatenate([ck.reshape(b, past, -1).astype(BF16), kb], axis=1)
        vb = jnp.concatenate([cv.reshape(b, past, -1).astype(BF16), vb], axis=1)
        kib = jnp.concatenate([cki.astype(BF16), kib], axis=1)
    n_keys = past + t
    k_sel = min(TOPK_MAX, n_keys // 4)
    tpad = -(-t // KEY_BLOCK) * KEY_BLOCK
    lp = max(-(-n_keys // KEY_BLOCK) * KEY_BLOCK, past + tpad)
    padt = lambda a, n: a if a.shape[1] == n else jnp.pad(a, ((0, 0), (0, n - a.shape[1]), (0, 0)))
    o = _sparse_attn(padt(r3(q), tpad), padt(r3(qi), tpad), padt(r3(wi), tpad),
                     padt(kb, lp), padt(vb, lp), padt(kib, lp), bias_tiles,
                     q_off=past, n_keys=n_keys, k_sel=k_sel)[:, :t]
    x2 = _matmul_res(x2, o.reshape(b * t, d), w_out)
    return (x2.reshape(b, t, d), k.reshape(b, t, N_KV_HEADS, HEAD_DIM),
            v.reshape(b, t, N_KV_HEADS, HEAD_DIM), ki.reshape(b, t, IDX_DIM))


def _mixer_conv(x3, g, w_pw1, b_pw1, w_dw, b_dw, ln_g, ln_b, w_pw2, b_pw2, state=None):
    b, t, d = x3.shape
    pad = CONV_WIDTH - 1
    u3 = _conv_glu(x3.reshape(b * t, d), g, w_pw1, b_pw1).reshape(b, t, d)
    if state is None:
        init = jnp.zeros((b, HALO, d), F32)
        tail = u3[:, -pad:] if t >= pad else jnp.pad(u3, ((0, 0), (pad - t, 0), (0, 0)))
    else:
        init = jnp.pad(state.astype(F32), ((0, 0), (HALO - pad, 0), (0, 0)))
        tail = jnp.concatenate([state.astype(F32), u3], axis=1)[:, -pad:]
    x3 = _conv_rest(x3, u3, init, w_dw, b_dw, ln_g, ln_b, w_pw2, b_pw2)
    return x3, tail


def kernel(x_prompt, x_sample, cache_attn_k, cache_attn_v, cache_attn_kidx, state_conv, cache_mem_k, cache_mem_v, mem_prompt, rel_bias, g_mix, w_in_attn, w_out_attn, w_pw1, b_pw1, w_dw, b_dw, ln_g, ln_b, w_pw2, b_pw2, g_mem_q, g_mem_src, w_mem_q, w_mem_kv, w_mem_o, g_mlp, w_mlp1, w_mlp2, g_final):
    depth = g_mix.shape[0]
    bp, tp, d = x_prompt.shape
    bs, ts, _ = x_sample.shape
    n_mem = mem_prompt.shape[1]
    mem_hd = d // MEM_HEADS
    bias_tiles = _bias_tiles(rel_bias)
    xp, xs = x_prompt, x_sample
    kp_l, vp_l, kip_l, ks_l, vs_l, kis_l = [], [], [], [], [], []
    convp_l, convs_l, memk_l, memv_l = [], [], [], []
    for i in range(depth):
        j = i // 2
        if i % 2 == 0:
            xp, kp, vp, kip = _mixer_attn(xp, g_mix[i], w_in_attn[j], w_out_attn[j], bias_tiles)
            xs, ks, vs, kis = _mixer_attn(
                xs, g_mix[i], w_in_attn[j], w_out_attn[j], bias_tiles,
                cache=(cache_attn_k[j], cache_attn_v[j], cache_attn_kidx[j]))
            kp_l.append(kp); vp_l.append(vp); kip_l.append(kip)
            ks_l.append(ks); vs_l.append(vs); kis_l.append(kis)
        else:
            cw = (w_pw1[j], b_pw1[j], w_dw[j], b_dw[j], ln_g[j], ln_b[j], w_pw2[j], b_pw2[j])
            xp, cp = _mixer_conv(xp, g_mix[i], *cw)
            xs, cs = _mixer_conv(xs, g_mix[i], *cw, state=state_conv[j])
            convp_l.append(cp); convs_l.append(cs)
        mk, mv = _mem_kv(mem_prompt.reshape(bp * n_mem, d), g_mem_src[i], w_mem_kv[i])
        mk, mv = mk.reshape(bp, n_mem, d), mv.reshape(bp, n_mem, d)
        memk_l.append(mk.reshape(bp, n_mem, MEM_HEADS, mem_hd))
        memv_l.append(mv.reshape(bp, n_mem, MEM_HEADS, mem_hd))
        xp2 = _mem_attn(xp.reshape(bp * tp, d), g_mem_q[i], w_mem_q[i], mk, mv, w_mem_o[i], tp)
        xs2 = _mem_attn(xs.reshape(bs * ts, d), g_mem_q[i], w_mem_q[i],
                        cache_mem_k[i].reshape(bs, n_mem, d), cache_mem_v[i].reshape(bs, n_mem, d),
                        w_mem_o[i], ts)
        gf = g_final if i == depth - 1 else None
        xp = _mlp(xp2, g_mlp[i], w_mlp1[i], w_mlp2[i], gf).reshape(bp, tp, d)
        xs = _mlp(xs2, g_mlp[i], w_mlp1[i], w_mlp2[i], gf).reshape(bs, ts, d)
    return (xp, xs, jnp.stack(kp_l), jnp.stack(vp_l), jnp.stack(kip_l), jnp.stack(convp_l),
            jnp.stack(memk_l), jnp.stack(memv_l), jnp.stack(ks_l), jnp.stack(vs_l),
            jnp.stack(kis_l), jnp.stack(convs_l))
```

```python
import functools
import math

import jax
import jax.numpy as jnp
from jax import lax
from jax.experimental import pallas as pl
from jax.experimental.pallas import tpu as pltpu

CHUNK = 64
HEAD_DIM = 64
N_KV_HEADS = 4
IDX_HEADS = 8
IDX_DIM = 64
TOPK_MAX = 256
N_BUCKETS = 32
MAX_DISTANCE = 128
CONV_WIDTH = 31
MEM_HEADS = 4
EPS = 1e-6

LANES = 128
SUBLANES = 8
KEY_BLOCK = 128
HEADS_PER_TILE = 2
COUNT_UNROLL = 8
V_AUG_ROWS = HEAD_DIM + 16
LOG2E = math.log2(math.e)
HALO = 32
VMEM_LIMIT = 56 * 1024 * 1024

NEG_BIG = -1e30
F32 = jnp.float32
BF16 = jnp.bfloat16

KEY_NEG_INF = -2139095041
KEY_POS_INF = 2139095040


def _const_spec(shape):
    nd = len(shape)
    return pl.BlockSpec(shape, lambda *_: (0,) * nd, pipeline_mode=pl.Buffered(1))


def _params(*sem):
    return pltpu.CompilerParams(dimension_semantics=sem, vmem_limit_bytes=VMEM_LIMIT)


def _rms(x, g):
    ms = jnp.mean(x * x, axis=-1, keepdims=True)
    return x * lax.rsqrt(ms + EPS) * g


def _dot(a, b):
    return jnp.dot(a, b, preferred_element_type=F32)


def _dot_nt(a, b):
    return lax.dot_general(a, b, (((1,), (1,)), ((), ())), preferred_element_type=F32)


def _row_tile(m, pref):
    t = min(m, pref)
    assert m % t == 0, (m, t)
    return t


def _bias_tiles_kernel(tab_ref, bt_ref, *, n_heads):
    nb = N_BUCKETS // 2
    max_exact = nb // 2
    c = lax.broadcasted_iota(jnp.int32, (KEY_BLOCK, KEY_BLOCK), 0)
    r = lax.broadcasted_iota(jnp.int32, (KEY_BLOCK, KEY_BLOCK), 1)
    for d in range(2):
        rel = c - r - d * KEY_BLOCK
        n = jnp.abs(rel)
        nf = jnp.maximum(n, 1).astype(F32)
        large = max_exact + (jnp.log(nf / max_exact) / math.log(MAX_DISTANCE / max_exact)
                             * (nb - max_exact)).astype(jnp.int32)
        large = jnp.minimum(large, nb - 1)
        bucket = jnp.where(rel > 0, nb, 0) + jnp.where(n < max_exact, n, large)
        for h in range(n_heads):
            acc = jnp.zeros((KEY_BLOCK, KEY_BLOCK), F32)
            for b in range(N_BUCKETS):
                acc = jnp.where(bucket == b, tab_ref[b, h], acc)
            g = h % HEADS_PER_TILE
            rows = slice((1 - d) * KEY_BLOCK, (2 - d) * KEY_BLOCK)
            bt_ref[h // HEADS_PER_TILE, rows, g * KEY_BLOCK:(g + 1) * KEY_BLOCK] = (
                (acc - tab_ref[nb - 1, h]) * LOG2E)


def _bias_tiles(rel_bias):
    n_heads = rel_bias.shape[1]
    assert n_heads % HEADS_PER_TILE == 0
    return pl.pallas_call(
        functools.partial(_bias_tiles_kernel, n_heads=n_heads),
        out_shape=jax.ShapeDtypeStruct(
            (n_heads // HEADS_PER_TILE, 2 * KEY_BLOCK, HEADS_PER_TILE * KEY_BLOCK), F32),
        in_specs=[pl.BlockSpec(memory_space=pltpu.SMEM)],
        out_specs=pl.BlockSpec(memory_space=pltpu.VMEM),
        name="bias_tiles",
    )(rel_bias)


def _attn_proj_kernel(x_ref, g_ref, w_ref, q_ref, k_ref, v_ref, ki_ref, kb_ref, vb_ref,
                      kib_ref, qi_ref, wi_ref, *, dq, dkv, dqi):
    h = _rms(x_ref[...], g_ref[...]).astype(BF16)
    o = 0
    q_ref[...] = (_dot(h, w_ref[:, o:o + dq]) * (HEAD_DIM ** -0.5)).astype(BF16)
    o += dq
    k = _dot(h, w_ref[:, o:o + dkv])
    k_ref[...] = k
    kb_ref[...] = k.astype(BF16)
    o += dkv
    v = _dot(h, w_ref[:, o:o + dkv])
    v_ref[...] = v
    vb_ref[...] = v.astype(BF16)
    o += dkv
    qi_ref[...] = (_dot(h, w_ref[:, o:o + dqi]) * (IDX_DIM ** -0.5)).astype(BF16)
    o += dqi
    ki = _dot(h, w_ref[:, o:o + LANES])[:, :IDX_DIM]
    ki_ref[...] = ki
    kib_ref[...] = ki.astype(BF16)
    o += LANES
    wi_ref[...] = _dot(h, w_ref[:, o:o + LANES]) * (IDX_HEADS ** -0.5)


def _attn_proj(x2, g, w_in):
    m, d = x2.shape
    dq = d
    dkv = N_KV_HEADS * HEAD_DIM
    dqi = IDX_HEADS * IDX_DIM
    base = dq + 2 * dkv + dqi
    w_main = w_in[:, :base]
    w_ki = jnp.pad(w_in[:, base:base + IDX_DIM], ((0, 0), (0, LANES - IDX_DIM)))
    w_wi = jnp.pad(w_in[:, base + IDX_DIM:], ((0, 0), (0, LANES - IDX_HEADS)))
    w = jnp.concatenate([w_main, w_ki, w_wi], axis=1).astype(BF16)
    tm = _row_tile(m, 512)
    row = lambda n: pl.BlockSpec((tm, n), lambda i: (i, 0))
    outs = [(dq, BF16), (dkv, F32), (dkv, F32), (IDX_DIM, F32), (dkv, BF16), (dkv, BF16),
            (IDX_DIM, BF16), (dqi, BF16), (LANES, F32)]
    return pl.pallas_call(
        functools.partial(_attn_proj_kernel, dq=dq, dkv=dkv, dqi=dqi),
        out_shape=[jax.ShapeDtypeStruct((m, n), dt) for n, dt in outs],
        grid=(m // tm,),
        in_specs=[row(d), _const_spec((1, d)), _const_spec(w.shape)],
        out_specs=[row(n) for n, _ in outs],
        compiler_params=_params("parallel"),
        name="attn_proj",
    )(x2, g.reshape(1, d), w)


def _sparse_attn_kernel(q_ref, qi_ref, wi_ref, k_ref, vt_ref, ki_ref, bt_ref, o_ref,
                        s_ref, sb_ref, dots_ref, qgt_ref, qit_ref, ot_ref, lg_ref, p_ref, al_ref,
                        mb_ref, *state_refs,
                        q_off, n_keys, k_sel, group):
    tq = KEY_BLOCK
    hpt = HEADS_PER_TILE
    n_tiles = qgt_ref.shape[0]
    m_refs, acc_refs = state_refs[:n_tiles], state_refs[n_tiles:]
    i = pl.program_id(1)
    qs = q_off // KEY_BLOCK + i
    n_kb = qs + 1
    q_start = q_off + i * tq

    qt = q_ref[0].astype(F32).T * LOG2E
    for t in range(n_tiles):
        q2 = jnp.concatenate(
            [qt[(t * hpt + g) * HEAD_DIM:(t * hpt + g + 1) * HEAD_DIM, :] for g in range(hpt)], axis=1)
        q_hi = q2.astype(BF16)
        q_lo = (q2 - q_hi.astype(F32)).astype(BF16)
        qgt_ref[t] = jnp.concatenate([q_hi, q_lo], axis=0)
    qit = qi_ref[0].astype(F32).T
    qit_ref[...] = jnp.concatenate([qit[h * IDX_DIM:(h + 1) * IDX_DIM, :] for h in range(IDX_HEADS)],
                                   axis=1).astype(BF16)
    last_kv = k_ref.shape[1] // KEY_BLOCK - 1
    wit = wi_ref[0].T

    qpos = q_start + lax.broadcasted_iota(jnp.int32, (1, tq), 1)
    lim = jnp.minimum((qpos // CHUNK + 1) * CHUNK, n_keys)
    kidx = lax.broadcasted_iota(jnp.int32, (KEY_BLOCK, tq), 0)

    def key_to_f32(key):
        bits = key ^ ((key >> 31) & 0x7FFFFFFF)
        return lax.bitcast_convert_type(bits, F32)

    def put_scores(j, sc):
        s_ref[j] = sc
        bits = lax.bitcast_convert_type(jnp.where(sc == 0.0, 0.0, sc), jnp.int32)
        key = bits ^ ((bits >> 31) & 0x7FFFFFFF)
        down = jnp.maximum((key >> 16) << 16, KEY_NEG_INF)
        sb_ref[j] = key_to_f32(down).astype(BF16)

    def score_pair(pair, carry):
        js = (2 * pair, 2 * pair + 1)
        for slot, j in enumerate(js):
            k0 = pl.multiple_of(jnp.minimum(j, last_kv) * KEY_BLOCK, KEY_BLOCK)
            dots_ref[slot] = _dot(ki_ref[0, pl.ds(k0, KEY_BLOCK), :], qit_ref[...])
        for slot, j in enumerate(js):
            acc = jnp.zeros((KEY_BLOCK, tq), F32)
            for h in range(IDX_HEADS):
                acc = acc + wit[h:h + 1, :] * jnp.maximum(dots_ref[slot, :, h * tq:(h + 1) * tq], 0.0)
            put_scores(j, jnp.where(kidx + j * KEY_BLOCK < lim, acc, -jnp.inf))
        return carry

    n_pairs = (n_kb + 1) // 2
    lax.fori_loop(0, n_pairs, score_pair, 0)
    for u in range(COUNT_UNROLL):
        put_scores(2 * n_pairs + u, jnp.full((KEY_BLOCK, tq), -jnp.inf, F32))
    n_count = (2 * n_pairs + COUNT_UNROLL - 1) // COUNT_UNROLL

    def count_ge(ref, cand):
        cb = jnp.broadcast_to(cand, (KEY_BLOCK, tq)).astype(ref.dtype)
        one, zero = jnp.ones((), ref.dtype), jnp.zeros((), ref.dtype)

        def body(p, c):
            for u in range(COUNT_UNROLL):
                c = c + jnp.where(ref[COUNT_UNROLL * p + u] >= cb, one, zero)
            return c
        c = lax.fori_loop(0, n_count, body, jnp.zeros((KEY_BLOCK, tq), ref.dtype))
        return jnp.sum(c.astype(F32), axis=0, keepdims=True)

    def bisect16(count_at, lo0, hi0):
        def step(_, carry):
            lo, hi = carry
            mid = (lo + hi) >> 1
            ok = count_at(mid) >= k_sel
            return jnp.where(ok, mid, lo), jnp.where(ok, hi, mid)
        lo, _ = lax.fori_loop(0, 16, step, (lo0, hi0))
        return lo

    full = lambda v: jnp.full((1, tq), v, jnp.int32)
    key_hi = bisect16(lambda v: count_ge(sb_ref, key_to_f32(v << 16)),
                      full((KEY_NEG_INF + 1) >> 16), full((KEY_POS_INF >> 16) + 1))
    key_lo = bisect16(lambda v: count_ge(s_ref, key_to_f32((key_hi << 16) + v)), full(0), full(1 << 16))
    thr = key_to_f32((key_hi << 16) + key_lo)

    def count(cand, strict):
        cb = jnp.broadcast_to(cand, (KEY_BLOCK, tq))
        hit = (lambda s: s > cb) if strict else (lambda s: s >= cb)

        def body(p, c):
            c = c + jnp.where(hit(s_ref[2 * p]), 1.0, 0.0)
            return c + jnp.where(hit(s_ref[2 * p + 1]), 1.0, 0.0)
        c = lax.fori_loop(0, n_pairs, body, jnp.zeros((KEY_BLOCK, tq), F32))
        return jnp.sum(c, axis=0, keepdims=True)

    surplus = count_ge(s_ref, thr) - k_sel

    @pl.when(jnp.max(surplus) > 0.0)
    def _():
        n_ties = k_sel - count(thr, True)
        row_i = lax.broadcasted_iota(jnp.int32, (KEY_BLOCK, KEY_BLOCK), 0)
        col_i = lax.broadcasted_iota(jnp.int32, (KEY_BLOCK, KEY_BLOCK), 1)
        lower = jnp.where(col_i < row_i, 1.0, 0.0).astype(BF16)
        ones = jnp.ones((KEY_BLOCK, KEY_BLOCK), BF16)

        def tie_body(j, seen):
            s = s_ref[j]
            eq = s == thr
            e = jnp.where(eq, 1.0, 0.0).astype(BF16)
            before = _dot(lower, e) + seen
            s_ref[j] = jnp.where(eq & (before >= n_ties), -jnp.inf, s)
            return seen + _dot(ones, e)

        lax.fori_loop(0, n_kb, tie_body, jnp.zeros((KEY_BLOCK, tq), F32))

    thr_sel = jnp.broadcast_to(jnp.maximum(thr, jnp.finfo(F32).min), (KEY_BLOCK, tq))

    for t in range(n_tiles):
        m_refs[t][...] = jnp.full(m_refs[t].shape, NEG_BIG, F32)
        acc_refs[t][...] = jnp.zeros(acc_refs[t].shape, F32)
    p_ref[...] = jnp.zeros(p_ref.shape, BF16)
    al_ref[...] = jnp.ones(al_ref.shape, F32)

    def far_blocks(db):
        out = []
        for j in (2 * db, 2 * db + 1):
            js = jnp.where(j < qs - 1, j, n_kb)
            out.append((js, jnp.minimum(js, last_kv)))
        return out

    near = [(jnp.where(qs >= 1, qs - 1, n_kb), jnp.maximum(qs - 1, 0)), (qs, qs)]

    def mask_of(blocks):
        m = jnp.concatenate([jnp.where(s_ref[js] >= thr_sel, 0.0, NEG_BIG) for js, _ in blocks], axis=0)
        return jnp.concatenate([m] * hpt, axis=1)

    def logits(blocks, mask, biased, slot, t):
        n = (t * hpt) // group
        kn = jnp.concatenate(
            [k_ref[0, pl.ds(pl.multiple_of(jk * KEY_BLOCK, KEY_BLOCK), KEY_BLOCK),
                   n * 2 * HEAD_DIM:(n + 1) * 2 * HEAD_DIM] for _, jk in blocks], axis=0)
        lg = _dot(kn, qgt_ref[t]) + mask
        if biased:
            lg = lg + bt_ref[t]
        lg_ref[slot, t] = lg
        mb_ref[slot, t] = jnp.max(lg, axis=0, keepdims=True)

    def softmax(slot, t):
        m_old = m_refs[t][...]
        m_new = jnp.maximum(m_old, mb_ref[slot, t])
        al_ref[slot, t] = jnp.exp2(m_old - m_new)
        p_ref[slot, t] = jnp.exp2(lg_ref[slot, t] - m_new).astype(BF16)
        m_refs[t][...] = m_new

    def update(blocks, slot, t):
        n = (t * hpt) // group
        vtn = jnp.concatenate([vt_ref[0, jk, n] for _, jk in blocks], axis=1)
        acc_refs[t][...] = al_ref[slot, t] * acc_refs[t][...] + _dot(vtn, p_ref[slot, t])

    n_far = qs // 2
    first = far_blocks(0)
    mask_first = mask_of(first)
    for t in range(n_tiles):
        logits(first, mask_first, False, 0, t)

    def far_body(trip, carry):
        d0 = 2 * trip
        blk1, blk2 = far_blocks(d0 + 1), far_blocks(d0 + 2)
        mask1, mask2 = mask_of(blk1), mask_of(blk2)
        old0, old1 = far_blocks(jnp.maximum(d0 - 2, 0)), far_blocks(jnp.maximum(d0 - 1, 0))
        for t in range(n_tiles):
            update(old0, 0, t)
            logits(blk1, mask1, False, 1, t)
            softmax(0, t)
        for t in range(n_tiles):
            update(old1, 1, t)
            logits(blk2, mask2, False, 0, t)
            softmax(1, t)
        return carry

    n_trips = (n_far + 1) // 2
    lax.fori_loop(0, n_trips, far_body, 0)

    last0 = far_blocks(jnp.maximum(2 * n_trips - 2, 0))
    last1 = far_blocks(jnp.maximum(2 * n_trips - 1, 0))
    mask_near = mask_of(near)
    for t in range(n_tiles):
        update(last0, 0, t)
        logits(near, mask_near, True, 0, t)
    for t in range(n_tiles):
        update(last1, 1, t)
        softmax(0, t)
    for t in range(n_tiles):
        update(near, 0, t)

    for t in range(n_tiles):
        on = acc_refs[t][0:HEAD_DIM, :] / acc_refs[t][HEAD_DIM:HEAD_DIM + 1, :]
        for g in range(hpt):
            hh = t * hpt + g
            ot_ref[hh * HEAD_DIM:(hh + 1) * HEAD_DIM, :] = on[:, g * tq:(g + 1) * tq]
    o_ref[0] = ot_ref[...].T.astype(BF16)


def _sparse_attn(q, qi, wi, kb, vb, kib, bias_tiles, *, q_off, n_keys, k_sel):
    b, t, dq = q.shape
    lp = kb.shape[1]
    dkv = kb.shape[2]
    tq = KEY_BLOCK
    n_heads = dq // HEAD_DIM
    group = n_heads // N_KV_HEADS
    hpt = HEADS_PER_TILE
    n_tiles = n_heads // hpt
    assert group % hpt == 0
    nkb = lp // KEY_BLOCK
    assert t % tq == 0 and lp % KEY_BLOCK == 0 and q_off % KEY_BLOCK == 0 and q_off + t <= lp
    vt = jnp.transpose(vb.reshape(b, nkb, KEY_BLOCK, N_KV_HEADS, HEAD_DIM), (0, 1, 3, 4, 2))
    vt = jnp.concatenate(
        [vt, jnp.ones((b, nkb, N_KV_HEADS, 1, KEY_BLOCK), BF16),
         jnp.zeros((b, nkb, N_KV_HEADS, V_AUG_ROWS - HEAD_DIM - 1, KEY_BLOCK), BF16)], axis=3)
    k2 = jnp.concatenate([kb.reshape(b, lp, N_KV_HEADS, HEAD_DIM)] * 2, axis=-1).reshape(b, lp, 2 * dkv)
    qblk = lambda n: pl.BlockSpec((1, tq, n), lambda bi, i: (bi, i, 0))
    kblk = lambda n: pl.BlockSpec((1, lp, n), lambda bi, i: (bi, 0, 0))
    return pl.pallas_call(
        functools.partial(_sparse_attn_kernel, q_off=q_off, n_keys=n_keys, k_sel=k_sel, group=group),
        out_shape=jax.ShapeDtypeStruct((b, t, dq), BF16),
        grid=(b, t // tq),
        in_specs=[qblk(dq), qblk(qi.shape[2]), qblk(LANES), kblk(2 * dkv),
                  pl.BlockSpec((1, nkb, N_KV_HEADS, V_AUG_ROWS, KEY_BLOCK), lambda bi, i: (bi, 0, 0, 0, 0)),
                  kblk(kib.shape[2]), _const_spec(bias_tiles.shape)],
        out_specs=qblk(dq),
        scratch_shapes=[
            pltpu.VMEM((nkb + 1 + COUNT_UNROLL, KEY_BLOCK, tq), F32),
            pltpu.VMEM((nkb + 1 + COUNT_UNROLL, KEY_BLOCK, tq), BF16),
            pltpu.VMEM((2, KEY_BLOCK, IDX_HEADS * tq), F32),
            pltpu.VMEM((n_tiles, 2 * HEAD_DIM, hpt * tq), BF16),
            pltpu.VMEM((IDX_DIM, IDX_HEADS * tq), BF16),
            pltpu.VMEM((dq, tq), F32),
            pltpu.VMEM((2, n_tiles, 2 * KEY_BLOCK, hpt * tq), F32),
            pltpu.VMEM((2, n_tiles, 2 * KEY_BLOCK, hpt * tq), BF16),
            pltpu.VMEM((2, n_tiles, 1, hpt * tq), F32),
            pltpu.VMEM((2, n_tiles, 1, hpt * tq), F32),
        ] + [pltpu.VMEM((1, hpt * tq), F32)] * n_tiles
          + [pltpu.VMEM((V_AUG_ROWS, hpt * tq), F32)] * n_tiles,
        compiler_params=_params("parallel", "arbitrary"),
        name="sparse_attn",
    )(q, qi, wi, k2, vt, kib, bias_tiles)


def _matmul_res_kernel(x_ref, a_ref, w_ref, o_ref):
    o_ref[...] = x_ref[...] + _dot(a_ref[...], w_ref[...])


def _matmul_res(x2, a2, w):
    m, d = x2.shape
    ka = a2.shape[1]
    tm = _row_tile(m, 512)
    return pl.pallas_call(
        _matmul_res_kernel,
        out_shape=jax.ShapeDtypeStruct((m, d), F32),
        grid=(m // tm,),
        in_specs=[pl.BlockSpec((tm, d), lambda i: (i, 0)), pl.BlockSpec((tm, ka), lambda i: (i, 0)),
                  _const_spec((ka, d))],
        out_specs=pl.BlockSpec((tm, d), lambda i: (i, 0)),
        compiler_params=_params("parallel"),
        name="attn_out_proj",
    )(x2, a2, w.astype(BF16))


def _mem_kv_kernel(x_ref, g_ref, w_ref, k_ref, v_ref, *, d):
    h = _rms(x_ref[...], g_ref[...]).astype(BF16)
    k_ref[...] = _dot(h, w_ref[:, :d])
    v_ref[...] = _dot(h, w_ref[:, d:])


def _mem_kv(mem2, g, w_kv):
    m, d = mem2.shape
    tm = _row_tile(m, 512)
    row = pl.BlockSpec((tm, d), lambda i: (i, 0))
    return pl.pallas_call(
        functools.partial(_mem_kv_kernel, d=d),
        out_shape=[jax.ShapeDtypeStruct((m, d), F32)] * 2,
        grid=(m // tm,),
        in_specs=[row, _const_spec((1, d)), _const_spec((d, 2 * d))],
        out_specs=[row, row],
        compiler_params=_params("parallel"),
        name="mem_kv",
    )(mem2, g.reshape(1, d), w_kv.astype(BF16))


def _mem_attn_kernel(x_ref, g_ref, wq_ref, mk_ref, mv_ref, wo_ref, o_ref, *, hd):
    x = x_ref[...]
    h = _rms(x, g_ref[...]).astype(BF16)
    q = (_dot(h, wq_ref[...]) * (hd ** -0.5)).astype(BF16)
    heads = []
    for a in range(MEM_HEADS):
        cols = slice(a * hd, (a + 1) * hd)
        lg = _dot_nt(q[:, cols], mk_ref[0, :, cols].astype(BF16))
        p = jnp.exp(lg - jnp.max(lg, axis=1, keepdims=True))
        p = (p / jnp.sum(p, axis=1, keepdims=True)).astype(BF16)
        heads.append(_dot(p, mv_ref[0, :, cols].astype(BF16)).astype(BF16))
    o_ref[...] = x + _dot(jnp.concatenate(heads, axis=1), wo_ref[...])


def _mem_attn(x2, g, w_q, mk, mv, w_o, rows_per_batch):
    m, d = x2.shape
    n_mem = mk.shape[1]
    tm = _row_tile(rows_per_batch, 512)
    per = rows_per_batch // tm
    row = pl.BlockSpec((tm, d), lambda i: (i, 0))
    mem = pl.BlockSpec((1, n_mem, d), lambda i: (i // per, 0, 0))
    return pl.pallas_call(
        functools.partial(_mem_attn_kernel, hd=d // MEM_HEADS),
        out_shape=jax.ShapeDtypeStruct((m, d), F32),
        grid=(m // tm,),
        in_specs=[row, _const_spec((1, d)), _const_spec((d, d)), mem, mem, _const_spec((d, d))],
        out_specs=row,
        compiler_params=_params("parallel"),
        name="mem_attn",
    )(x2, g.reshape(1, d), w_q.astype(BF16), mk, mv, w_o.astype(BF16))


def _mlp_kernel(*refs, n_chunks, chunk, final):
    if final:
        x_ref, g_ref, w1_ref, w2_ref, gf_ref, o_ref = refs
    else:
        x_ref, g_ref, w1_ref, w2_ref, o_ref = refs
    x = x_ref[...]
    h = _rms(x, g_ref[...]).astype(BF16)
    acc = x
    for c in range(n_chunks):
        a = jnp.maximum(_dot(h, w1_ref[:, c * chunk:(c + 1) * chunk]), 0.0)
        acc = acc + _dot((a * a).astype(BF16), w2_ref[c * chunk:(c + 1) * chunk, :])
    if final:
        acc = _rms(acc, gf_ref[...])
    o_ref[...] = acc


def _mlp(x2, g, w1, w2, g_final=None):
    m, d = x2.shape
    dff = w1.shape[1]
    chunk = min(dff, 1024)
    tm = _row_tile(m, 512)
    row = pl.BlockSpec((tm, d), lambda i: (i, 0))
    final = g_final is not None
    in_specs = [row, _const_spec((1, d)), _const_spec((d, dff)), _const_spec((dff, d))]
    args = [x2, g.reshape(1, d), w1.astype(BF16), w2.astype(BF16)]
    if final:
        in_specs.append(_const_spec((1, d)))
        args.append(g_final.reshape(1, d))
    return pl.pallas_call(
        functools.partial(_mlp_kernel, n_chunks=dff // chunk, chunk=chunk, final=final),
        out_shape=jax.ShapeDtypeStruct((m, d), F32),
        grid=(m // tm,),
        in_specs=in_specs,
        out_specs=row,
        compiler_params=_params("parallel"),
        name="mlp",
    )(*args)


def _conv_glu_kernel(x_ref, g_ref, w_ref, b_ref, u_ref, *, d):
    h = _rms(x_ref[...], g_ref[...]).astype(BF16)
    a = _dot(h, w_ref[:, :d]) + b_ref[:, :d]
    gate = _dot(h, w_ref[:, d:]) + b_ref[:, d:]
    u_ref[...] = a * (1.0 / (1.0 + jnp.exp(-gate)))


def _conv_glu(x2, g, w_pw1, b_pw1):
    m, d = x2.shape
    tm = _row_tile(m, 512)
    row = pl.BlockSpec((tm, d), lambda i: (i, 0))
    return pl.pallas_call(
        functools.partial(_conv_glu_kernel, d=d),
        out_shape=jax.ShapeDtypeStruct((m, d), F32),
        grid=(m // tm,),
        in_specs=[row, _const_spec((1, d)), _const_spec((d, 2 * d)), _const_spec((1, 2 * d))],
        out_specs=row,
        compiler_params=_params("parallel"),
        name="conv_glu",
    )(x2, g.reshape(1, d), w_pw1.astype(BF16), b_pw1.reshape(1, 2 * d))


def _conv_rest_kernel(x_ref, u_ref, prev_ref, init_ref, wdw_ref, bdw_ref, lng_ref, lnb_ref,
                      w2_ref, b2_ref, o_ref, ext_ref, sh_ref, y_ref, *, tm, rc, lc):
    t = pl.program_id(1)
    d = u_ref.shape[2]
    pad = CONV_WIDTH - 1

    @pl.when(t == 0)
    def _():
        ext_ref[0:HALO, :] = init_ref[0]

    @pl.when(t > 0)
    def _():
        ext_ref[0:HALO, :] = prev_ref[0]

    ext_ref[HALO:HALO + tm, :] = u_ref[0]

    for s in range(SUBLANES):
        rows = tm + SUBLANES * ((CONV_WIDTH - 1 - s) // SUBLANES)
        sh_ref[s, 0:rows, :] = ext_ref[pl.ds(HALO - pad + s, rows), :]
    for r0 in range(0, tm, rc):
        for c0 in range(0, d, lc):
            cols = slice(c0, c0 + lc)
            y = jnp.broadcast_to(bdw_ref[:, cols], (rc, lc))
            for w in range(CONV_WIDTH):
                a, s = divmod(w, SUBLANES)
                y = y + sh_ref[s, r0 + SUBLANES * a:r0 + SUBLANES * a + rc, cols] * wdw_ref[w:w + 1, cols]
            y_ref[r0:r0 + rc, cols] = y

    y = y_ref[...]
    mu = jnp.mean(y, axis=-1, keepdims=True)
    yc = y - mu
    var = jnp.mean(yc * yc, axis=-1, keepdims=True)
    yn = yc * lax.rsqrt(var + EPS) * lng_ref[...] + lnb_ref[...]
    act = (yn * (1.0 / (1.0 + jnp.exp(-yn)))).astype(BF16)
    o_ref[0] = x_ref[0] + _dot(act, w2_ref[...]) + b2_ref[...]


def _conv_rest(x3, u3, init, w_dw, b_dw, ln_g, ln_b, w_pw2, b_pw2):
    b, t, d = x3.shape
    tm = _row_tile(t, 256)
    rc = min(tm, 64)
    lc = min(d, 256)
    assert tm % HALO == 0 and tm % rc == 0 and d % lc == 0
    per = tm // HALO
    tile = pl.BlockSpec((1, tm, d), lambda bi, ti: (bi, ti, 0))
    prev = pl.BlockSpec((1, HALO, d), lambda bi, ti: (bi, jnp.maximum(ti * per - 1, 0), 0))
    first = pl.BlockSpec((1, HALO, d), lambda bi, ti: (bi, 0, 0))
    vec = _const_spec((1, d))
    wdw = jnp.pad(w_dw, ((0, HALO - CONV_WIDTH), (0, 0)))
    return pl.pallas_call(
        functools.partial(_conv_rest_kernel, tm=tm, rc=rc, lc=lc),
        out_shape=jax.ShapeDtypeStruct((b, t, d), F32),
        grid=(b, t // tm),
        in_specs=[tile, tile, prev, first, _const_spec((HALO, d)), vec, vec, vec,
                  _const_spec((d, d)), vec],
        out_specs=tile,
        scratch_shapes=[pltpu.VMEM((HALO + tm, d), F32),
                        pltpu.VMEM((SUBLANES, tm + HALO - SUBLANES, d), F32),
                        pltpu.VMEM((tm, d), F32)],
        compiler_params=_params("parallel", "arbitrary"),
        name="conv_rest",
    )(x3, u3, u3, init, wdw, b_dw.reshape(1, d), ln_g.reshape(1, d), ln_b.reshape(1, d),
      w_pw2.astype(BF16), b_pw2.reshape(1, d))


def _mixer_attn(x3, g, w_in, w_out, bias_tiles, cache=None):
    b, t, d = x3.shape
    x2 = x3.reshape(b * t, d)
    q, k, v, ki, kb, vb, kib, qi, wi = _attn_proj(x2, g, w_in)
    r3 = lambda a: a.reshape(b, t, a.shape[-1])
    kb, vb, kib = r3(kb), r3(vb), r3(kib)
    past = 0
    if cache is not None:
        ck, cv, cki = cache
        past = ck.shape[1]
        kb = jnp.concatenate([ck.reshape(b, past, -1).astype(BF16), kb], axis=1)
        vb = jnp.concatenate([cv.reshape(b, past, -1).astype(BF16), vb], axis=1)
        kib = jnp.concatenate([cki.astype(BF16), kib], axis=1)
    n_keys = past + t
    k_sel = min(TOPK_MAX, n_keys // 4)
    tpad = -(-t // KEY_BLOCK) * KEY_BLOCK
    lp = max(-(-n_keys // KEY_BLOCK) * KEY_BLOCK, past + tpad)
    padt = lambda a, n: a if a.shape[1] == n else jnp.pad(a, ((0, 0), (0, n - a.shape[1]), (0, 0)))
    o = _sparse_attn(padt(r3(q), tpad), padt(r3(qi), tpad), padt(r3(wi), tpad),
                     padt(kb, lp), padt(vb, lp), padt(kib, lp), bias_tiles,
                     q_off=past, n_keys=n_keys, k_sel=k_sel)[:, :t]
    x2 = _matmul_res(x2, o.reshape(b * t, d), w_out)
    return (x2.reshape(b, t, d), k.reshape(b, t, N_KV_HEADS, HEAD_DIM),
            v.reshape(b, t, N_KV_HEADS, HEAD_DIM), ki.reshape(b, t, IDX_DIM))


def _mixer_conv(x3, g, w_pw1, b_pw1, w_dw, b_dw, ln_g, ln_b, w_pw2, b_pw2, state=None):
    b, t, d = x3.shape
    pad = CONV_WIDTH - 1
    u3 = _conv_glu(x3.reshape(b * t, d), g, w_pw1, b_pw1).reshape(b, t, d)
    if state is None:
        init = jnp.zeros((b, HALO, d), F32)
        tail = u3[:, -pad:] if t >= pad else jnp.pad(u3, ((0, 0), (pad - t, 0), (0, 0)))
    else:
        init = jnp.pad(state.astype(F32), ((0, 0), (HALO - pad, 0), (0, 0)))
        tail = jnp.concatenate([state.astype(F32), u3], axis=1)[:, -pad:]
    x3 = _conv_rest(x3, u3, init, w_dw, b_dw, ln_g, ln_b, w_pw2, b_pw2)
    return x3, tail


def kernel(x_prompt, x_sample, cache_attn_k, cache_attn_v, cache_attn_kidx, state_conv, cache_mem_k, cache_mem_v, mem_prompt, rel_bias, g_mix, w_in_attn, w_out_attn, w_pw1, b_pw1, w_dw, b_dw, ln_g, ln_b, w_pw2, b_pw2, g_mem_q, g_mem_src, w_mem_q, w_mem_kv, w_mem_o, g_mlp, w_mlp1, w_mlp2, g_final):
    depth = g_mix.shape[0]
    bp, tp, d = x_prompt.shape
    bs, ts, _ = x_sample.shape
    n_mem = mem_prompt.shape[1]
    mem_hd = d // MEM_HEADS
    bias_tiles = _bias_tiles(rel_bias)
    xp, xs = x_prompt, x_sample
    kp_l, vp_l, kip_l, ks_l, vs_l, kis_l = [], [], [], [], [], []
    convp_l, convs_l, memk_l, memv_l = [], [], [], []
    for i in range(depth):
        j = i // 2
        if i % 2 == 0:
            xp, kp, vp, kip = _mixer_attn(xp, g_mix[i], w_in_attn[j], w_out_attn[j], bias_tiles)
            xs, ks, vs, kis = _mixer_attn(
                xs, g_mix[i], w_in_attn[j], w_out_attn[j], bias_tiles,
                cache=(cache_attn_k[j], cache_attn_v[j], cache_attn_kidx[j]))
            kp_l.append(kp); vp_l.append(vp); kip_l.append(kip)
            ks_l.append(ks); vs_l.append(vs); kis_l.append(kis)
        else:
            cw = (w_pw1[j], b_pw1[j], w_dw[j], b_dw[j], ln_g[j], ln_b[j], w_pw2[j], b_pw2[j])
            xp, cp = _mixer_conv(xp, g_mix[i], *cw)
            xs, cs = _mixer_conv(xs, g_mix[i], *cw, state=state_conv[j])
            convp_l.append(cp); convs_l.append(cs)
        mk, mv = _mem_kv(mem_prompt.reshape(bp * n_mem, d), g_mem_src[i], w_mem_kv[i])
        mk, mv = mk.reshape(bp, n_mem, d), mv.reshape(bp, n_mem, d)
        memk_l.append(mk.reshape(bp, n_mem, MEM_HEADS, mem_hd))
        memv_l.append(mv.reshape(bp, n_mem, MEM_HEADS, mem_hd))
        xp2 = _mem_attn(xp.reshape(bp * tp, d), g_mem_q[i], w_mem_q[i], mk, mv, w_mem_o[i], tp)
        xs2 = _mem_attn(xs.reshape(bs * ts, d), g_mem_q[i], w_mem_q[i],
                        cache_mem_k[i].reshape(bs, n_mem, d), cache_mem_v[i].reshape(bs, n_mem, d),
                        w_mem_o[i], ts)
        gf = g_final if i == depth - 1 else None
        xp = _mlp(xp2, g_mlp[i], w_mlp1[i], w_mlp2[i], gf).reshape(bp, tp, d)
        xs = _mlp(xs2, g_mlp[i], w_mlp1[i], w_mlp2[i], gf).reshape(bs, ts, d)
    return (xp, xs, jnp.stack(kp_l), jnp.stack(vp_l), jnp.stack(kip_l), jnp.stack(convp_l),
            jnp.stack(memk_l), jnp.stack(memv_l), jnp.stack(ks_l), jnp.stack(vs_l),
            jnp.stack(kis_l), jnp.stack(convs_l))
```

```python
import functools
import math

import jax
import jax.numpy as jnp
from jax import lax
from jax.experimental import pallas as pl
from jax.experimental.pallas import tpu as pltpu

CHUNK = 64
HEAD_DIM = 64
N_KV_HEADS = 4
IDX_HEADS = 8
IDX_DIM = 64
TOPK_MAX = 256
N_BUCKETS = 32
MAX_DISTANCE = 128
CONV_WIDTH = 31
MEM_HEADS = 4
EPS = 1e-6

LANES = 128
SUBLANES = 8
KEY_BLOCK = 128
HEADS_PER_TILE = 2
COUNT_UNROLL = 8
V_AUG_ROWS = HEAD_DIM + 16
LOG2E = math.log2(math.e)
HALO = 32
VMEM_LIMIT = 56 * 1024 * 1024

NEG_BIG = -1e30
F32 = jnp.float32
BF16 = jnp.bfloat16

KEY_NEG_INF = -2139095041
KEY_POS_INF = 2139095040


def _const_spec(shape):
    nd = len(shape)
    return pl.BlockSpec(shape, lambda *_: (0,) * nd, pipeline_mode=pl.Buffered(1))


def _params(*sem):
    return pltpu.CompilerParams(dimension_semantics=sem, vmem_limit_bytes=VMEM_LIMIT)


def _rms(x, g):
    ms = jnp.mean(x * x, axis=-1, keepdims=True)
    return x * lax.rsqrt(ms + EPS) * g


def _dot(a, b):
    return jnp.dot(a, b, preferred_element_type=F32)


def _dot_nt(a, b):
    return lax.dot_general(a, b, (((1,), (1,)), ((), ())), preferred_element_type=F32)


def _row_tile(m, pref):
    t = min(m, pref)
    assert m % t == 0, (m, t)
    return t


def _bias_tiles_kernel(tab_ref, bt_ref, *, n_heads):
    nb = N_BUCKETS // 2
    max_exact = nb // 2
    c = lax.broadcasted_iota(jnp.int32, (KEY_BLOCK, KEY_BLOCK), 0)
    r = lax.broadcasted_iota(jnp.int32, (KEY_BLOCK, KEY_BLOCK), 1)
    for d in range(2):
        rel = c - r - d * KEY_BLOCK
        n = jnp.abs(rel)
        nf = jnp.maximum(n, 1).astype(F32)
        large = max_exact + (jnp.log(nf / max_exact) / math.log(MAX_DISTANCE / max_exact)
                             * (nb - max_exact)).astype(jnp.int32)
        large = jnp.minimum(large, nb - 1)
        bucket = jnp.where(rel > 0, nb, 0) + jnp.where(n < max_exact, n, large)
        for h in range(n_heads):
            acc = jnp.zeros((KEY_BLOCK, KEY_BLOCK), F32)
            for b in range(N_BUCKETS):
                acc = jnp.where(bucket == b, tab_ref[b, h], acc)
            g = h % HEADS_PER_TILE
            rows = slice((1 - d) * KEY_BLOCK, (2 - d) * KEY_BLOCK)
            bt_ref[h // HEADS_PER_TILE, rows, g * KEY_BLOCK:(g + 1) * KEY_BLOCK] = (
                (acc - tab_ref[nb - 1, h]) * LOG2E)


def _bias_tiles(rel_bias):
    n_heads = rel_bias.shape[1]
    assert n_heads % HEADS_PER_TILE == 0
    return pl.pallas_call(
        functools.partial(_bias_tiles_kernel, n_heads=n_heads),
        out_shape=jax.ShapeDtypeStruct(
            (n_heads // HEADS_PER_TILE, 2 * KEY_BLOCK, HEADS_PER_TILE * KEY_BLOCK), F32),
        in_specs=[pl.BlockSpec(memory_space=pltpu.SMEM)],
        out_specs=pl.BlockSpec(memory_space=pltpu.VMEM),
        name="bias_tiles",
    )(rel_bias)


def _attn_proj_kernel(x_ref, g_ref, w_ref, q_ref, k_ref, v_ref, ki_ref, kb_ref, vb_ref,
                      kib_ref, qi_ref, wi_ref, *, dq, dkv, dqi):
    h = _rms(x_ref[...], g_ref[...]).astype(BF16)
    o = 0
    q_ref[...] = (_dot(h, w_ref[:, o:o + dq]) * (HEAD_DIM ** -0.5)).astype(BF16)
    o += dq
    k = _dot(h, w_ref[:, o:o + dkv])
    k_ref[...] = k
    kb_ref[...] = k.astype(BF16)
    o += dkv
    v = _dot(h, w_ref[:, o:o + dkv])
    v_ref[...] = v
    vb_ref[...] = v.astype(BF16)
    o += dkv
    qi_ref[...] = (_dot(h, w_ref[:, o:o + dqi]) * (IDX_DIM ** -0.5)).astype(BF16)
    o += dqi
    ki = _dot(h, w_ref[:, o:o + LANES])[:, :IDX_DIM]
    ki_ref[...] = ki
    kib_ref[...] = ki.astype(BF16)
    o += LANES
    wi_ref[...] = _dot(h, w_ref[:, o:o + LANES]) * (IDX_HEADS ** -0.5)


def _attn_proj(x2, g, w_in):
    m, d = x2.shape
    dq = d
    dkv = N_KV_HEADS * HEAD_DIM
    dqi = IDX_HEADS * IDX_DIM
    base = dq + 2 * dkv + dqi
    w_main = w_in[:, :base]
    w_ki = jnp.pad(w_in[:, base:base + IDX_DIM], ((0, 0), (0, LANES - IDX_DIM)))
    w_wi = jnp.pad(w_in[:, base + IDX_DIM:], ((0, 0), (0, LANES - IDX_HEADS)))
    w = jnp.concatenate([w_main, w_ki, w_wi], axis=1).astype(BF16)
    tm = _row_tile(m, 512)
    row = lambda n: pl.BlockSpec((tm, n), lambda i: (i, 0))
    outs = [(dq, BF16), (dkv, F32), (dkv, F32), (IDX_DIM, F32), (dkv, BF16), (dkv, BF16),
            (IDX_DIM, BF16), (dqi, BF16), (LANES, F32)]
    return pl.pallas_call(
        functools.partial(_attn_proj_kernel, dq=dq, dkv=dkv, dqi=dqi),
        out_shape=[jax.ShapeDtypeStruct((m, n), dt) for n, dt in outs],
        grid=(m // tm,),
        in_specs=[row(d), _const_spec((1, d)), _const_spec(w.shape)],
        out_specs=[row(n) for n, _ in outs],
        compiler_params=_params("parallel"),
        name="attn_proj",
    )(x2, g.reshape(1, d), w)


def _sparse_attn_kernel(q_ref, qi_ref, wi_ref, k_ref, vt_ref, ki_ref, bt_ref, o_ref,
                        s_ref, sb_ref, dots_ref, qgt_ref, qit_ref, ot_ref, lg_ref, p_ref, al_ref,
                        mb_ref, *state_refs,
                        q_off, n_keys, k_sel, group):
    tq = KEY_BLOCK
    hpt = HEADS_PER_TILE
    n_tiles = qgt_ref.shape[0]
    m_refs, acc_refs = state_refs[:n_tiles], state_refs[n_tiles:]
    i = pl.program_id(1)
    qs = q_off // KEY_BLOCK + i
    n_kb = qs + 1
    q_start = q_off + i * tq

    qt = q_ref[0].astype(F32).T * LOG2E
    for t in range(n_tiles):
        q2 = jnp.concatenate(
            [qt[(t * hpt + g) * HEAD_DIM:(t * hpt + g + 1) * HEAD_DIM, :] for g in range(hpt)], axis=1)
        q_hi = q2.astype(BF16)
        q_lo = (q2 - q_hi.astype(F32)).astype(BF16)
        qgt_ref[t] = jnp.concatenate([q_hi, q_lo], axis=0)
    qit = qi_ref[0].astype(F32).T
    qit_ref[...] = jnp.concatenate([qit[h * IDX_DIM:(h + 1) * IDX_DIM, :] for h in range(IDX_HEADS)],
                                   axis=1).astype(BF16)
    last_kv = k_ref.shape[1] // KEY_BLOCK - 1
    wit = wi_ref[0].T

    qpos = q_start + lax.broadcasted_iota(jnp.int32, (1, tq), 1)
    lim = jnp.minimum((qpos // CHUNK + 1) * CHUNK, n_keys)
    kidx = lax.broadcasted_iota(jnp.int32, (KEY_BLOCK, tq), 0)

    def key_to_f32(key):
        bits = key ^ ((key >> 31) & 0x7FFFFFFF)
        return lax.bitcast_convert_type(bits, F32)

    def put_scores(j, sc):
        s_ref[j] = sc
        bits = lax.bitcast_convert_type(jnp.where(sc == 0.0, 0.0, sc), jnp.int32)
        key = bits ^ ((bits >> 31) & 0x7FFFFFFF)
        down = jnp.maximum((key >> 16) << 16, KEY_NEG_INF)
        sb_ref[j] = key_to_f32(down).astype(BF16)

    def score_pair(pair, carry):
        js = (2 * pair, 2 * pair + 1)
        for slot, j in enumerate(js):
            k0 = pl.multiple_of(jnp.minimum(j, last_kv) * KEY_BLOCK, KEY_BLOCK)
            dots_ref[slot] = _dot(ki_ref[0, pl.ds(k0, KEY_BLOCK), :], qit_ref[...])
        for slot, j in enumerate(js):
            acc = jnp.zeros((KEY_BLOCK, tq), F32)
            for h in range(IDX_HEADS):
                acc = acc + wit[h:h + 1, :] * jnp.maximum(dots_ref[slot, :, h * tq:(h + 1) * tq], 0.0)
            put_scores(j, jnp.where(kidx + j * KEY_BLOCK < lim, acc, -jnp.inf))
        return carry

    n_pairs = (n_kb + 1) // 2
    lax.fori_loop(0, n_pairs, score_pair, 0)
    for u in range(COUNT_UNROLL):
        put_scores(2 * n_pairs + u, jnp.full((KEY_BLOCK, tq), -jnp.inf, F32))
    n_count = (2 * n_pairs + COUNT_UNROLL - 1) // COUNT_UNROLL

    def count_ge(ref, cand):
        cb = jnp.broadcast_to(cand, (KEY_BLOCK, tq)).astype(ref.dtype)
        one, zero = jnp.ones((), ref.dtype), jnp.zeros((), ref.dtype)

        def body(p, c):
            for u in range(COUNT_UNROLL):
                c = c + jnp.where(ref[COUNT_UNROLL * p + u] >= cb, one, zero)
            return c
        c = lax.fori_loop(0, n_count, body, jnp.zeros((KEY_BLOCK, tq), ref.dtype))
        return jnp.sum(c.astype(F32), axis=0, keepdims=True)

    def bisect16(count_at, lo0, hi0):
        def step(_, carry):
            lo, hi = carry
            mid = (lo + hi) >> 1
            ok = count_at(mid) >= k_sel
            return jnp.where(ok, mid, lo), jnp.where(ok, hi, mid)
        lo, _ = lax.fori_loop(0, 16, step, (lo0, hi0))
        return lo

    full = lambda v: jnp.full((1, tq), v, jnp.int32)
    key_hi = bisect16(lambda v: count_ge(sb_ref, key_to_f32(v << 16)),
                      full((KEY_NEG_INF + 1) >> 16), full((KEY_POS_INF >> 16) + 1))
    key_lo = bisect16(lambda v: count_ge(s_ref, key_to_f32((key_hi << 16) + v)), full(0), full(1 << 16))
    thr = key_to_f32((key_hi << 16) + key_lo)

    def count(cand, strict):
        cb = jnp.broadcast_to(cand, (KEY_BLOCK, tq))
        hit = (lambda s: s > cb) if strict else (lambda s: s >= cb)

        def body(p, c):
            c = c + jnp.where(hit(s_ref[2 * p]), 1.0, 0.0)
            return c + jnp.where(hit(s_ref[2 * p + 1]), 1.0, 0.0)
        c = lax.fori_loop(0, n_pairs, body, jnp.zeros((KEY_BLOCK, tq), F32))
        return jnp.sum(c, axis=0, keepdims=True)

    surplus = count_ge(s_ref, thr) - k_sel

    @pl.when(jnp.max(surplus) > 0.0)
    def _():
        n_ties = k_sel - count(thr, True)
        row_i = lax.broadcasted_iota(jnp.int32, (KEY_BLOCK, KEY_BLOCK), 0)
        col_i = lax.broadcasted_iota(jnp.int32, (KEY_BLOCK, KEY_BLOCK), 1)
        lower = jnp.where(col_i < row_i, 1.0, 0.0).astype(BF16)
        ones = jnp.ones((KEY_BLOCK, KEY_BLOCK), BF16)

        def tie_body(j, seen):
            s = s_ref[j]
            eq = s == thr
            e = jnp.where(eq, 1.0, 0.0).astype(BF16)
            before = _dot(lower, e) + seen
            s_ref[j] = jnp.where(eq & (before >= n_ties), -jnp.inf, s)
            return seen + _dot(ones, e)

        lax.fori_loop(0, n_kb, tie_body, jnp.zeros((KEY_BLOCK, tq), F32))

    thr_sel = jnp.broadcast_to(jnp.maximum(thr, jnp.finfo(F32).min), (KEY_BLOCK, tq))

    for t in range(n_tiles):
        m_refs[t][...] = jnp.full(m_refs[t].shape, NEG_BIG, F32)
        acc_refs[t][...] = jnp.zeros(acc_refs[t].shape, F32)
    p_ref[...] = jnp.zeros(p_ref.shape, BF16)
    al_ref[...] = jnp.ones(al_ref.shape, F32)

    def far_blocks(db):
        out = []
        for j in (2 * db, 2 * db + 1):
            js = jnp.where(j < qs - 1, j, n_kb)
            out.append((js, jnp.minimum(js, last_kv)))
        return out

    near = [(jnp.where(qs >= 1, qs - 1, n_kb), jnp.maximum(qs - 1, 0)), (qs, qs)]

    def mask_of(blocks):
        m = jnp.concatenate([jnp.where(s_ref[js] >= thr_sel, 0.0, NEG_BIG) for js, _ in blocks], axis=0)
        return jnp.concatenate([m] * hpt, axis=1)

    def logits(blocks, mask, biased, slot, t):
        n = (t * hpt) // group
        kn = jnp.concatenate(
            [k_ref[0, pl.ds(pl.multiple_of(jk * KEY_BLOCK, KEY_BLOCK), KEY_BLOCK),
                   n * 2 * HEAD_DIM:(n + 1) * 2 * HEAD_DIM] for _, jk in blocks], axis=0)
        lg = _dot(kn, qgt_ref[t]) + mask
        if biased:
            lg = lg + bt_ref[t]
        lg_ref[slot, t] = lg
        mb_ref[slot, t] = jnp.max(lg, axis=0, keepdims=True)

    def softmax(slot, t):
        m_old = m_refs[t][...]
        m_new = jnp.maximum(m_old, mb_ref[slot, t])
        al_ref[slot, t] = jnp.exp2(m_old - m_new)
        p_ref[slot, t] = jnp.exp2(lg_ref[slot, t] - m_new).astype(BF16)
        m_refs[t][...] = m_new

    def update(blocks, slot, t):
        n = (t * hpt) // group
        vtn = jnp.concatenate([vt_ref[0, jk, n] for _, jk in blocks], axis=1)
        acc_refs[t][...] = al_ref[slot, t] * acc_refs[t][...] + _dot(vtn, p_ref[slot, t])

    def dblock(j):
        far = far_blocks(j - 1)
        return [tuple(jnp.where(j == 0, a, b) for a, b in zip(near[h], far[h])) for h in range(2)]

    n_far = qs // 2
    mask_near = mask_of(near)
    for t in range(n_tiles):
        logits(near, mask_near, True, 0, t)

    def trip_body(trip, carry):
        d0 = 2 * trip
        blk1, blk2 = far_blocks(d0), far_blocks(d0 + 1)
        mask1, mask2 = mask_of(blk1), mask_of(blk2)
        old0, old1 = dblock(jnp.maximum(d0 - 2, 0)), dblock(jnp.maximum(d0 - 1, 0))
        for t in range(n_tiles):
            update(old0, 0, t)
            logits(blk1, mask1, False, 1, t)
            softmax(0, t)
        for t in range(n_tiles):
            update(old1, 1, t)
            logits(blk2, mask2, False, 0, t)
            softmax(1, t)
        return carry

    n_trips = (n_far + 2) // 2
    lax.fori_loop(0, n_trips, trip_body, 0)
    last0, last1 = dblock(2 * n_trips - 2), dblock(2 * n_trips - 1)
    for t in range(n_tiles):
        update(last0, 0, t)
        update(last1, 1, t)

    for t in range(n_tiles):
        on = acc_refs[t][0:HEAD_DIM, :] / acc_refs[t][HEAD_DIM:HEAD_DIM + 1, :]
        for g in range(hpt):
            hh = t * hpt + g
            ot_ref[hh * HEAD_DIM:(hh + 1) * HEAD_DIM, :] = on[:, g * tq:(g + 1) * tq]
    o_ref[0] = ot_ref[...].T.astype(BF16)


def _sparse_attn(q, qi, wi, kb, vb, kib, bias_tiles, *, q_off, n_keys, k_sel):
    b, t, dq = q.shape
    lp = kb.shape[1]
    dkv = kb.shape[2]
    tq = KEY_BLOCK
    n_heads = dq // HEAD_DIM
    group = n_heads // N_KV_HEADS
    hpt = HEADS_PER_TILE
    n_tiles = n_heads // hpt
    assert group % hpt == 0
    nkb = lp // KEY_BLOCK
    assert t % tq == 0 and lp % KEY_BLOCK == 0 and q_off % KEY_BLOCK == 0 and q_off + t <= lp
    vt = jnp.transpose(vb.reshape(b, nkb, KEY_BLOCK, N_KV_HEADS, HEAD_DIM), (0, 1, 3, 4, 2))
    vt = jnp.concatenate(
        [vt, jnp.ones((b, nkb, N_KV_HEADS, 1, KEY_BLOCK), BF16),
         jnp.zeros((b, nkb, N_KV_HEADS, V_AUG_ROWS - HEAD_DIM - 1, KEY_BLOCK), BF16)], axis=3)
    k2 = jnp.concatenate([kb.reshape(b, lp, N_KV_HEADS, HEAD_DIM)] * 2, axis=-1).reshape(b, lp, 2 * dkv)
    qblk = lambda n: pl.BlockSpec((1, tq, n), lambda bi, i: (bi, i, 0))
    kblk = lambda n: pl.BlockSpec((1, lp, n), lambda bi, i: (bi, 0, 0))
    return pl.pallas_call(
        functools.partial(_sparse_attn_kernel, q_off=q_off, n_keys=n_keys, k_sel=k_sel, group=group),
        out_shape=jax.ShapeDtypeStruct((b, t, dq), BF16),
        grid=(b, t // tq),
        in_specs=[qblk(dq), qblk(qi.shape[2]), qblk(LANES), kblk(2 * dkv),
                  pl.BlockSpec((1, nkb, N_KV_HEADS, V_AUG_ROWS, KEY_BLOCK), lambda bi, i: (bi, 0, 0, 0, 0)),
                  kblk(kib.shape[2]), _const_spec(bias_tiles.shape)],
        out_specs=qblk(dq),
        scratch_shapes=[
            pltpu.VMEM((nkb + 1 + COUNT_UNROLL, KEY_BLOCK, tq), F32),
            pltpu.VMEM((nkb + 1 + COUNT_UNROLL, KEY_BLOCK, tq), BF16),
            pltpu.VMEM((2, KEY_BLOCK, IDX_HEADS * tq), F32),
            pltpu.VMEM((n_tiles, 2 * HEAD_DIM, hpt * tq), BF16),
            pltpu.VMEM((IDX_DIM, IDX_HEADS * tq), BF16),
            pltpu.VMEM((dq, tq), F32),
            pltpu.VMEM((2, n_tiles, 2 * KEY_BLOCK, hpt * tq), F32),
            pltpu.VMEM((2, n_tiles, 2 * KEY_BLOCK, hpt * tq), BF16),
            pltpu.VMEM((2, n_tiles, 1, hpt * tq), F32),
            pltpu.VMEM((2, n_tiles, 1, hpt * tq), F32),
        ] + [pltpu.VMEM((1, hpt * tq), F32)] * n_tiles
          + [pltpu.VMEM((V_AUG_ROWS, hpt * tq), F32)] * n_tiles,
        compiler_params=_params("parallel", "arbitrary"),
        name="sparse_attn",
    )(q, qi, wi, k2, vt, kib, bias_tiles)


def _matmul_res_kernel(x_ref, a_ref, w_ref, o_ref):
    o_ref[...] = x_ref[...] + _dot(a_ref[...], w_ref[...])


def _matmul_res(x2, a2, w):
    m, d = x2.shape
    ka = a2.shape[1]
    tm = _row_tile(m, 512)
    return pl.pallas_call(
        _matmul_res_kernel,
        out_shape=jax.ShapeDtypeStruct((m, d), F32),
        grid=(m // tm,),
        in_specs=[pl.BlockSpec((tm, d), lambda i: (i, 0)), pl.BlockSpec((tm, ka), lambda i: (i, 0)),
                  _const_spec((ka, d))],
        out_specs=pl.BlockSpec((tm, d), lambda i: (i, 0)),
        compiler_params=_params("parallel"),
        name="attn_out_proj",
    )(x2, a2, w.astype(BF16))


def _mem_kv_kernel(x_ref, g_ref, w_ref, k_ref, v_ref, *, d):
    h = _rms(x_ref[...], g_ref[...]).astype(BF16)
    k_ref[...] = _dot(h, w_ref[:, :d])
    v_ref[...] = _dot(h, w_ref[:, d:])


def _mem_kv(mem2, g, w_kv):
    m, d = mem2.shape
    tm = _row_tile(m, 512)
    row = pl.BlockSpec((tm, d), lambda i: (i, 0))
    return pl.pallas_call(
        functools.partial(_mem_kv_kernel, d=d),
        out_shape=[jax.ShapeDtypeStruct((m, d), F32)] * 2,
        grid=(m // tm,),
        in_specs=[row, _const_spec((1, d)), _const_spec((d, 2 * d))],
        out_specs=[row, row],
        compiler_params=_params("parallel"),
        name="mem_kv",
    )(mem2, g.reshape(1, d), w_kv.astype(BF16))


def _mem_attn_kernel(x_ref, g_ref, wq_ref, mk_ref, mv_ref, wo_ref, o_ref, *, hd):
    x = x_ref[...]
    h = _rms(x, g_ref[...]).astype(BF16)
    q = (_dot(h, wq_ref[...]) * (hd ** -0.5)).astype(BF16)
    heads = []
    for a in range(MEM_HEADS):
        cols = slice(a * hd, (a + 1) * hd)
        lg = _dot_nt(q[:, cols], mk_ref[0, :, cols].astype(BF16))
        p = jnp.exp(lg - jnp.max(lg, axis=1, keepdims=True))
        p = (p / jnp.sum(p, axis=1, keepdims=True)).astype(BF16)
        heads.append(_dot(p, mv_ref[0, :, cols].astype(BF16)).astype(BF16))
    o_ref[...] = x + _dot(jnp.concatenate(heads, axis=1), wo_ref[...])


def _mem_attn(x2, g, w_q, mk, mv, w_o, rows_per_batch):
    m, d = x2.shape
    n_mem = mk.shape[1]
    tm = _row_tile(rows_per_batch, 512)
    per = rows_per_batch // tm
    row = pl.BlockSpec((tm, d), lambda i: (i, 0))
    mem = pl.BlockSpec((1, n_mem, d), lambda i: (i // per, 0, 0))
    return pl.pallas_call(
        functools.partial(_mem_attn_kernel, hd=d // MEM_HEADS),
        out_shape=jax.ShapeDtypeStruct((m, d), F32),
        grid=(m // tm,),
        in_specs=[row, _const_spec((1, d)), _const_spec((d, d)), mem, mem, _const_spec((d, d))],
        out_specs=row,
        compiler_params=_params("parallel"),
        name="mem_attn",
    )(x2, g.reshape(1, d), w_q.astype(BF16), mk, mv, w_o.astype(BF16))


def _mlp_kernel(*refs, n_chunks, chunk, final):
    if final:
        x_ref, g_ref, w1_ref, w2_ref, gf_ref, o_ref = refs
    else:
        x_ref, g_ref, w1_ref, w2_ref, o_ref = refs
    x = x_ref[...]
    h = _rms(x, g_ref[...]).astype(BF16)
    acc = x
    for c in range(n_chunks):
        a = jnp.maximum(_dot(h, w1_ref[:, c * chunk:(c + 1) * chunk]), 0.0)
        acc = acc + _dot((a * a).astype(BF16), w2_ref[c * chunk:(c + 1) * chunk, :])
    if final:
        acc = _rms(acc, gf_ref[...])
    o_ref[...] = acc


def _mlp(x2, g, w1, w2, g_final=None):
    m, d = x2.shape
    dff = w1.shape[1]
    chunk = min(dff, 1024)
    tm = _row_tile(m, 512)
    row = pl.BlockSpec((tm, d), lambda i: (i, 0))
    final = g_final is not None
    in_specs = [row, _const_spec((1, d)), _const_spec((d, dff)), _const_spec((dff, d))]
    args = [x2, g.reshape(1, d), w1.astype(BF16), w2.astype(BF16)]
    if final:
        in_specs.append(_const_spec((1, d)))
        args.append(g_final.reshape(1, d))
    return pl.pallas_call(
        functools.partial(_mlp_kernel, n_chunks=dff // chunk, chunk=chunk, final=final),
        out_shape=jax.ShapeDtypeStruct((m, d), F32),
        grid=(m // tm,),
        in_specs=in_specs,
        out_specs=row,
        compiler_params=_params("parallel"),
        name="mlp",
    )(*args)


def _conv_glu_kernel(x_ref, g_ref, w_ref, b_ref, u_ref, *, d):
    h = _rms(x_ref[...], g_ref[...]).astype(BF16)
    a = _dot(h, w_ref[:, :d]) + b_ref[:, :d]
    gate = _dot(h, w_ref[:, d:]) + b_ref[:, d:]
    u_ref[...] = a * (1.0 / (1.0 + jnp.exp(-gate)))


def _conv_glu(x2, g, w_pw1, b_pw1):
    m, d = x2.shape
    tm = _row_tile(m, 512)
    row = pl.BlockSpec((tm, d), lambda i: (i, 0))
    return pl.pallas_call(
        functools.partial(_conv_glu_kernel, d=d),
        out_shape=jax.ShapeDtypeStruct((m, d), F32),
        grid=(m // tm,),
        in_specs=[row, _const_spec((1, d)), _const_spec((d, 2 * d)), _const_spec((1, 2 * d))],
        out_specs=row,
        compiler_params=_params("parallel"),
        name="conv_glu",
    )(x2, g.reshape(1, d), w_pw1.astype(BF16), b_pw1.reshape(1, 2 * d))


def _conv_rest_kernel(x_ref, u_ref, prev_ref, init_ref, wdw_ref, bdw_ref, lng_ref, lnb_ref,
                      w2_ref, b2_ref, o_ref, ext_ref, sh_ref, y_ref, *, tm, rc, lc):
    t = pl.program_id(1)
    d = u_ref.shape[2]
    pad = CONV_WIDTH - 1

    @pl.when(t == 0)
    def _():
        ext_ref[0:HALO, :] = init_ref[0]

    @pl.when(t > 0)
    def _():
        ext_ref[0:HALO, :] = prev_ref[0]

    ext_ref[HALO:HALO + tm, :] = u_ref[0]

    for s in range(SUBLANES):
        rows = tm + SUBLANES * ((CONV_WIDTH - 1 - s) // SUBLANES)
        sh_ref[s, 0:rows, :] = ext_ref[pl.ds(HALO - pad + s, rows), :]
    for r0 in range(0, tm, rc):
        for c0 in range(0, d, lc):
            cols = slice(c0, c0 + lc)
            y = jnp.broadcast_to(bdw_ref[:, cols], (rc, lc))
            for w in range(CONV_WIDTH):
                a, s = divmod(w, SUBLANES)
                y = y + sh_ref[s, r0 + SUBLANES * a:r0 + SUBLANES * a + rc, cols] * wdw_ref[w:w + 1, cols]
            y_ref[r0:r0 + rc, cols] = y

    y = y_ref[...]
    mu = jnp.mean(y, axis=-1, keepdims=True)
    yc = y - mu
    var = jnp.mean(yc * yc, axis=-1, keepdims=True)
    yn = yc * lax.rsqrt(var + EPS) * lng_ref[...] + lnb_ref[...]
    act = (yn * (1.0 / (1.0 + jnp.exp(-yn)))).astype(BF16)
    o_ref[0] = x_ref[0] + _dot(act, w2_ref[...]) + b2_ref[...]


def _conv_rest(x3, u3, init, w_dw, b_dw, ln_g, ln_b, w_pw2, b_pw2):
    b, t, d = x3.shape
    tm = _row_tile(t, 256)
    rc = min(tm, 64)
    lc = min(d, 256)
    assert tm % HALO == 0 and tm % rc == 0 and d % lc == 0
    per = tm // HALO
    tile = pl.BlockSpec((1, tm, d), lambda bi, ti: (bi, ti, 0))
    prev = pl.BlockSpec((1, HALO, d), lambda bi, ti: (bi, jnp.maximum(ti * per - 1, 0), 0))
    first = pl.BlockSpec((1, HALO, d), lambda bi, ti: (bi, 0, 0))
    vec = _const_spec((1, d))
    wdw = jnp.pad(w_dw, ((0, HALO - CONV_WIDTH), (0, 0)))
    return pl.pallas_call(
        functools.partial(_conv_rest_kernel, tm=tm, rc=rc, lc=lc),
        out_shape=jax.ShapeDtypeStruct((b, t, d), F32),
        grid=(b, t // tm),
        in_specs=[tile, tile, prev, first, _const_spec((HALO, d)), vec, vec, vec,
                  _const_spec((d, d)), vec],
        out_specs=tile,
        scratch_shapes=[pltpu.VMEM((HALO + tm, d), F32),
                        pltpu.VMEM((SUBLANES, tm + HALO - SUBLANES, d), F32),
                        pltpu.VMEM((tm, d), F32)],
        compiler_params=_params("parallel", "arbitrary"),
        name="conv_rest",
    )(x3, u3, u3, init, wdw, b_dw.reshape(1, d), ln_g.reshape(1, d), ln_b.reshape(1, d),
      w_pw2.astype(BF16), b_pw2.reshape(1, d))


def _mixer_attn(x3, g, w_in, w_out, bias_tiles, cache=None):
    b, t, d = x3.shape
    x2 = x3.reshape(b * t, d)
    q, k, v, ki, kb, vb, kib, qi, wi = _attn_proj(x2, g, w_in)
    r3 = lambda a: a.reshape(b, t, a.shape[-1])
    kb, vb, kib = r3(kb), r3(vb), r3(kib)
    past = 0
    if cache is not None:
        ck, cv, cki = cache
        past = ck.shape[1]
        kb = jnp.concatenate([ck.reshape(b, past, -1).astype(BF16), kb], axis=1)
        vb = jnp.concatenate([cv.reshape(b, past, -1).astype(BF16), vb], axis=1)
        kib = jnp.concatenate([cki.astype(BF16), kib], axis=1)
    n_keys = past + t
    k_sel = min(TOPK_MAX, n_keys // 4)
    tpad = -(-t // KEY_BLOCK) * KEY_BLOCK
    lp = max(-(-n_keys // KEY_BLOCK) * KEY_BLOCK, past + tpad)
    padt = lambda a, n: a if a.shape[1] == n else jnp.pad(a, ((0, 0), (0, n - a.shape[1]), (0, 0)))
    o = _sparse_attn(padt(r3(q), tpad), padt(r3(qi), tpad), padt(r3(wi), tpad),
                     padt(kb, lp), padt(vb, lp), padt(kib, lp), bias_tiles,
                     q_off=past, n_keys=n_keys, k_sel=k_sel)[:, :t]
    x2 = _matmul_res(x2, o.reshape(b * t, d), w_out)
    return (x2.reshape(b, t, d), k.reshape(b, t, N_KV_HEADS, HEAD_DIM),
            v.reshape(b, t, N_KV_HEADS, HEAD_DIM), ki.reshape(b, t, IDX_DIM))


def _mixer_conv(x3, g, w_pw1, b_pw1, w_dw, b_dw, ln_g, ln_b, w_pw2, b_pw2, state=None):
    b, t, d = x3.shape
    pad = CONV_WIDTH - 1
    u3 = _conv_glu(x3.reshape(b * t, d), g, w_pw1, b_pw1).reshape(b, t, d)
    if state is None:
        init = jnp.zeros((b, HALO, d), F32)
        tail = u3[:, -pad:] if t >= pad else jnp.pad(u3, ((0, 0), (pad - t, 0), (0, 0)))
    else:
        init = jnp.pad(state.astype(F32), ((0, 0), (HALO - pad, 0), (0, 0)))
        tail = jnp.concatenate([state.astype(F32), u3], axis=1)[:, -pad:]
    x3 = _conv_rest(x3, u3, init, w_dw, b_dw, ln_g, ln_b, w_pw2, b_pw2)
    return x3, tail


def kernel(x_prompt, x_sample, cache_attn_k, cache_attn_v, cache_attn_kidx, state_conv, cache_mem_k, cache_mem_v, mem_prompt, rel_bias, g_mix, w_in_attn, w_out_attn, w_pw1, b_pw1, w_dw, b_dw, ln_g, ln_b, w_pw2, b_pw2, g_mem_q, g_mem_src, w_mem_q, w_mem_kv, w_mem_o, g_mlp, w_mlp1, w_mlp2, g_final):
    depth = g_mix.shape[0]
    bp, tp, d = x_prompt.shape
    bs, ts, _ = x_sample.shape
    n_mem = mem_prompt.shape[1]
    mem_hd = d // MEM_HEADS
    bias_tiles = _bias_tiles(rel_bias)
    xp, xs = x_prompt, x_sample
    kp_l, vp_l, kip_l, ks_l, vs_l, kis_l = [], [], [], [], [], []
    convp_l, convs_l, memk_l, memv_l = [], [], [], []
    for i in range(depth):
        j = i // 2
        if i % 2 == 0:
            xp, kp, vp, kip = _mixer_attn(xp, g_mix[i], w_in_attn[j], w_out_attn[j], bias_tiles)
            xs, ks, vs, kis = _mixer_attn(
                xs, g_mix[i], w_in_attn[j], w_out_attn[j], bias_tiles,
                cache=(cache_attn_k[j], cache_attn_v[j], cache_attn_kidx[j]))
            kp_l.append(kp); vp_l.append(vp); kip_l.append(kip)
            ks_l.append(ks); vs_l.append(vs); kis_l.append(kis)
        else:
            cw = (w_pw1[j], b_pw1[j], w_dw[j], b_dw[j], ln_g[j], ln_b[j], w_pw2[j], b_pw2[j])
            xp, cp = _mixer_conv(xp, g_mix[i], *cw)
            xs, cs = _mixer_conv(xs, g_mix[i], *cw, state=state_conv[j])
            convp_l.append(cp); convs_l.append(cs)
        mk, mv = _mem_kv(mem_prompt.reshape(bp * n_mem, d), g_mem_src[i], w_mem_kv[i])
        mk, mv = mk.reshape(bp, n_mem, d), mv.reshape(bp, n_mem, d)
        memk_l.append(mk.reshape(bp, n_mem, MEM_HEADS, mem_hd))
        memv_l.append(mv.reshape(bp, n_mem, MEM_HEADS, mem_hd))
        xp2 = _mem_attn(xp.reshape(bp * tp, d), g_mem_q[i], w_mem_q[i], mk, mv, w_mem_o[i], tp)
        xs2 = _mem_attn(xs.reshape(bs * ts, d), g_mem_q[i], w_mem_q[i],
                        cache_mem_k[i].reshape(bs, n_mem, d), cache_mem_v[i].reshape(bs, n_mem, d),
                        w_mem_o[i], ts)
        gf = g_final if i == depth - 1 else None
        xp = _mlp(xp2, g_mlp[i], w_mlp1[i], w_mlp2[i], gf).reshape(bp, tp, d)
        xs = _mlp(xs2, g_mlp[i], w_mlp1[i], w_mlp2[i], gf).reshape(bs, ts, d)
    return (xp, xs, jnp.stack(kp_l), jnp.stack(vp_l), jnp.stack(kip_l), jnp.stack(convp_l),
            jnp.stack(memk_l), jnp.stack(memv_l), jnp.stack(ks_l), jnp.stack(vs_l),
            jnp.stack(kis_l), jnp.stack(convs_l))
```

```python
import functools
import math

import jax
import jax.numpy as jnp
from jax import lax
from jax.experimental import pallas as pl
from jax.experimental.pallas import tpu as pltpu

CHUNK = 64
HEAD_DIM = 64
N_KV_HEADS = 4
IDX_HEADS = 8
IDX_DIM = 64
TOPK_MAX = 256
N_BUCKETS = 32
MAX_DISTANCE = 128
CONV_WIDTH = 31
MEM_HEADS = 4
EPS = 1e-6

LANES = 128
SUBLANES = 8
KEY_BLOCK = 128
HEADS_PER_TILE = 2
COUNT_UNROLL = 8
V_AUG_ROWS = HEAD_DIM + 16
LOG2E = math.log2(math.e)
HALO = 32
VMEM_LIMIT = 56 * 1024 * 1024

NEG_BIG = -1e30
F32 = jnp.float32
BF16 = jnp.bfloat16

KEY_NEG_INF = -2139095041
KEY_POS_INF = 2139095040


def _const_spec(shape):
    nd = len(shape)
    return pl.BlockSpec(shape, lambda *_: (0,) * nd, pipeline_mode=pl.Buffered(1))


def _params(*sem):
    return pltpu.CompilerParams(dimension_semantics=sem, vmem_limit_bytes=VMEM_LIMIT)


def _rms(x, g):
    ms = jnp.mean(x * x, axis=-1, keepdims=True)
    return x * lax.rsqrt(ms + EPS) * g


def _dot(a, b):
    return jnp.dot(a, b, preferred_element_type=F32)


def _dot_nt(a, b):
    return lax.dot_general(a, b, (((1,), (1,)), ((), ())), preferred_element_type=F32)


def _row_tile(m, pref):
    t = min(m, pref)
    assert m % t == 0, (m, t)
    return t


def _bias_tiles_kernel(tab_ref, bt_ref, *, n_heads):
    nb = N_BUCKETS // 2
    max_exact = nb // 2
    c = lax.broadcasted_iota(jnp.int32, (KEY_BLOCK, KEY_BLOCK), 0)
    r = lax.broadcasted_iota(jnp.int32, (KEY_BLOCK, KEY_BLOCK), 1)
    for d in range(2):
        rel = c - r - d * KEY_BLOCK
        n = jnp.abs(rel)
        nf = jnp.maximum(n, 1).astype(F32)
        large = max_exact + (jnp.log(nf / max_exact) / math.log(MAX_DISTANCE / max_exact)
                             * (nb - max_exact)).astype(jnp.int32)
        large = jnp.minimum(large, nb - 1)
        bucket = jnp.where(rel > 0, nb, 0) + jnp.where(n < max_exact, n, large)
        for h in range(n_heads):
            acc = jnp.zeros((KEY_BLOCK, KEY_BLOCK), F32)
            for b in range(N_BUCKETS):
                acc = jnp.where(bucket == b, tab_ref[b, h], acc)
            g = h % HEADS_PER_TILE
            rows = slice((1 - d) * KEY_BLOCK, (2 - d) * KEY_BLOCK)
            bt_ref[h // HEADS_PER_TILE, rows, g * KEY_BLOCK:(g + 1) * KEY_BLOCK] = (
                (acc - tab_ref[nb - 1, h]) * LOG2E)


def _bias_tiles(rel_bias):
    n_heads = rel_bias.shape[1]
    assert n_heads % HEADS_PER_TILE == 0
    return pl.pallas_call(
        functools.partial(_bias_tiles_kernel, n_heads=n_heads),
        out_shape=jax.ShapeDtypeStruct(
            (n_heads // HEADS_PER_TILE, 2 * KEY_BLOCK, HEADS_PER_TILE * KEY_BLOCK), F32),
        in_specs=[pl.BlockSpec(memory_space=pltpu.SMEM)],
        out_specs=pl.BlockSpec(memory_space=pltpu.VMEM),
        name="bias_tiles",
    )(rel_bias)


def _attn_proj_kernel(x_ref, g_ref, w_ref, q_ref, k_ref, v_ref, ki_ref, kb_ref, vb_ref,
                      kib_ref, qi_ref, wi_ref, *, dq, dkv, dqi):
    h = _rms(x_ref[...], g_ref[...]).astype(BF16)
    o = 0
    q_ref[...] = (_dot(h, w_ref[:, o:o + dq]) * (HEAD_DIM ** -0.5)).astype(BF16)
    o += dq
    k = _dot(h, w_ref[:, o:o + dkv])
    k_ref[...] = k
    kb_ref[...] = k.astype(BF16)
    o += dkv
    v = _dot(h, w_ref[:, o:o + dkv])
    v_ref[...] = v
    vb_ref[...] = v.astype(BF16)
    o += dkv
    qi_ref[...] = (_dot(h, w_ref[:, o:o + dqi]) * (IDX_DIM ** -0.5)).astype(BF16)
    o += dqi
    ki = _dot(h, w_ref[:, o:o + LANES])[:, :IDX_DIM]
    ki_ref[...] = ki
    kib_ref[...] = ki.astype(BF16)
    o += LANES
    wi_ref[...] = _dot(h, w_ref[:, o:o + LANES]) * (IDX_HEADS ** -0.5)


def _attn_proj(x2, g, w_in):
    m, d = x2.shape
    dq = d
    dkv = N_KV_HEADS * HEAD_DIM
    dqi = IDX_HEADS * IDX_DIM
    base = dq + 2 * dkv + dqi
    w_main = w_in[:, :base]
    w_ki = jnp.pad(w_in[:, base:base + IDX_DIM], ((0, 0), (0, LANES - IDX_DIM)))
    w_wi = jnp.pad(w_in[:, base + IDX_DIM:], ((0, 0), (0, LANES - IDX_HEADS)))
    w = jnp.concatenate([w_main, w_ki, w_wi], axis=1).astype(BF16)
    tm = _row_tile(m, 512)
    row = lambda n: pl.BlockSpec((tm, n), lambda i: (i, 0))
    outs = [(dq, BF16), (dkv, F32), (dkv, F32), (IDX_DIM, F32), (dkv, BF16), (dkv, BF16),
            (IDX_DIM, BF16), (dqi, BF16), (LANES, F32)]
    return pl.pallas_call(
        functools.partial(_attn_proj_kernel, dq=dq, dkv=dkv, dqi=dqi),
        out_shape=[jax.ShapeDtypeStruct((m, n), dt) for n, dt in outs],
        grid=(m // tm,),
        in_specs=[row(d), _const_spec((1, d)), _const_spec(w.shape)],
        out_specs=[row(n) for n, _ in outs],
        compiler_params=_params("parallel"),
        name="attn_proj",
    )(x2, g.reshape(1, d), w)


def _sparse_attn_kernel(q_ref, qi_ref, wi_ref, k_ref, vt_ref, ki_ref, bt_ref, o_ref,
                        s_ref, sb_ref, dots_ref, qgt_ref, qit_ref, ot_ref, lg_ref, mb_ref, *state_refs,
                        q_off, n_keys, k_sel, group):
    tq = KEY_BLOCK
    hpt = HEADS_PER_TILE
    n_tiles = qgt_ref.shape[0]
    m_refs, acc_refs = state_refs[:n_tiles], state_refs[n_tiles:]
    i = pl.program_id(1)
    qs = q_off // KEY_BLOCK + i
    n_kb = qs + 1
    q_start = q_off + i * tq

    qt = q_ref[0].astype(F32).T * LOG2E
    for t in range(n_tiles):
        q2 = jnp.concatenate(
            [qt[(t * hpt + g) * HEAD_DIM:(t * hpt + g + 1) * HEAD_DIM, :] for g in range(hpt)], axis=1)
        q_hi = q2.astype(BF16)
        q_lo = (q2 - q_hi.astype(F32)).astype(BF16)
        qgt_ref[t] = jnp.concatenate([q_hi, q_lo], axis=0)
    qit = qi_ref[0].astype(F32).T
    qit_ref[...] = jnp.concatenate([qit[h * IDX_DIM:(h + 1) * IDX_DIM, :] for h in range(IDX_HEADS)],
                                   axis=1).astype(BF16)
    last_kv = k_ref.shape[1] // KEY_BLOCK - 1
    wit = wi_ref[0].T

    qpos = q_start + lax.broadcasted_iota(jnp.int32, (1, tq), 1)
    lim = jnp.minimum((qpos // CHUNK + 1) * CHUNK, n_keys)
    kidx = lax.broadcasted_iota(jnp.int32, (KEY_BLOCK, tq), 0)

    def key_to_f32(key):
        bits = key ^ ((key >> 31) & 0x7FFFFFFF)
        return lax.bitcast_convert_type(bits, F32)

    def put_scores(j, sc):
        s_ref[j] = sc
        bits = lax.bitcast_convert_type(jnp.where(sc == 0.0, 0.0, sc), jnp.int32)
        key = bits ^ ((bits >> 31) & 0x7FFFFFFF)
        down = jnp.maximum((key >> 16) << 16, KEY_NEG_INF)
        sb_ref[j] = key_to_f32(down).astype(BF16)

    def score_pair(pair, carry):
        js = (2 * pair, 2 * pair + 1)
        for slot, j in enumerate(js):
            k0 = pl.multiple_of(jnp.minimum(j, last_kv) * KEY_BLOCK, KEY_BLOCK)
            dots_ref[slot] = _dot(ki_ref[0, pl.ds(k0, KEY_BLOCK), :], qit_ref[...])
        for slot, j in enumerate(js):
            acc = jnp.zeros((KEY_BLOCK, tq), F32)
            for h in range(IDX_HEADS):
                acc = acc + wit[h:h + 1, :] * jnp.maximum(dots_ref[slot, :, h * tq:(h + 1) * tq], 0.0)
            put_scores(j, jnp.where(kidx + j * KEY_BLOCK < lim, acc, -jnp.inf))
        return carry

    n_pairs = (n_kb + 1) // 2
    lax.fori_loop(0, n_pairs, score_pair, 0)
    for u in range(COUNT_UNROLL):
        put_scores(2 * n_pairs + u, jnp.full((KEY_BLOCK, tq), -jnp.inf, F32))
    n_count = (2 * n_pairs + COUNT_UNROLL - 1) // COUNT_UNROLL

    def count_ge(ref, cand):
        cb = jnp.broadcast_to(cand, (KEY_BLOCK, tq)).astype(ref.dtype)
        one, zero = jnp.ones((), ref.dtype), jnp.zeros((), ref.dtype)

        def body(p, c):
            for u in range(COUNT_UNROLL):
                c = c + jnp.where(ref[COUNT_UNROLL * p + u] >= cb, one, zero)
            return c
        c = lax.fori_loop(0, n_count, body, jnp.zeros((KEY_BLOCK, tq), ref.dtype))
        return jnp.sum(c.astype(F32), axis=0, keepdims=True)

    def bisect16(count_at, lo0, hi0):
        def step(_, carry):
            lo, hi = carry
            mid = (lo + hi) >> 1
            ok = count_at(mid) >= k_sel
            return jnp.where(ok, mid, lo), jnp.where(ok, hi, mid)
        lo, _ = lax.fori_loop(0, 16, step, (lo0, hi0))
        return lo

    full = lambda v: jnp.full((1, tq), v, jnp.int32)
    key_hi = bisect16(lambda v: count_ge(sb_ref, key_to_f32(v << 16)),
                      full((KEY_NEG_INF + 1) >> 16), full((KEY_POS_INF >> 16) + 1))
    key_lo = bisect16(lambda v: count_ge(s_ref, key_to_f32((key_hi << 16) + v)), full(0), full(1 << 16))
    thr = key_to_f32((key_hi << 16) + key_lo)

    def count(cand, strict):
        cb = jnp.broadcast_to(cand, (KEY_BLOCK, tq))
        hit = (lambda s: s > cb) if strict else (lambda s: s >= cb)

        def body(p, c):
            c = c + jnp.where(hit(s_ref[2 * p]), 1.0, 0.0)
            return c + jnp.where(hit(s_ref[2 * p + 1]), 1.0, 0.0)
        c = lax.fori_loop(0, n_pairs, body, jnp.zeros((KEY_BLOCK, tq), F32))
        return jnp.sum(c, axis=0, keepdims=True)

    surplus = count_ge(s_ref, thr) - k_sel

    @pl.when(jnp.max(surplus) > 0.0)
    def _():
        n_ties = k_sel - count(thr, True)
        row_i = lax.broadcasted_iota(jnp.int32, (KEY_BLOCK, KEY_BLOCK), 0)
        col_i = lax.broadcasted_iota(jnp.int32, (KEY_BLOCK, KEY_BLOCK), 1)
        lower = jnp.where(col_i < row_i, 1.0, 0.0).astype(BF16)
        ones = jnp.ones((KEY_BLOCK, KEY_BLOCK), BF16)

        def tie_body(j, seen):
            s = s_ref[j]
            eq = s == thr
            e = jnp.where(eq, 1.0, 0.0).astype(BF16)
            before = _dot(lower, e) + seen
            s_ref[j] = jnp.where(eq & (before >= n_ties), -jnp.inf, s)
            return seen + _dot(ones, e)

        lax.fori_loop(0, n_kb, tie_body, jnp.zeros((KEY_BLOCK, tq), F32))

    thr_sel = jnp.broadcast_to(jnp.maximum(thr, jnp.finfo(F32).min), (KEY_BLOCK, tq))

    for t in range(n_tiles):
        m_refs[t][...] = jnp.full(m_refs[t].shape, NEG_BIG, F32)
        acc_refs[t][...] = jnp.zeros(acc_refs[t].shape, F32)

    def far_blocks(db):
        out = []
        for j in (2 * db, 2 * db + 1):
            js = jnp.where(j < qs - 1, j, n_kb)
            out.append((js, jnp.minimum(js, last_kv)))
        return out

    near = [(jnp.where(qs >= 1, qs - 1, n_kb), jnp.maximum(qs - 1, 0)), (qs, qs)]

    def mask_of(blocks):
        m = jnp.concatenate([jnp.where(s_ref[js] >= thr_sel, 0.0, NEG_BIG) for js, _ in blocks], axis=0)
        return jnp.concatenate([m] * hpt, axis=1)

    def logits(blocks, mask, biased, slot, t):
        n = (t * hpt) // group
        kn = jnp.concatenate(
            [k_ref[0, pl.ds(pl.multiple_of(jk * KEY_BLOCK, KEY_BLOCK), KEY_BLOCK),
                   n * 2 * HEAD_DIM:(n + 1) * 2 * HEAD_DIM] for _, jk in blocks], axis=0)
        lg = _dot(kn, qgt_ref[t]) + mask
        if biased:
            lg = lg + bt_ref[t]
        lg_ref[slot, t] = lg
        mb_ref[slot, t] = jnp.max(lg, axis=0, keepdims=True)

    def softmax_update(blocks, slot, t):
        n = (t * hpt) // group
        m_old = m_refs[t][...]
        m_new = jnp.maximum(m_old, mb_ref[slot, t])
        alpha = jnp.exp2(m_old - m_new)
        p = jnp.exp2(lg_ref[slot, t] - m_new).astype(BF16)
        m_refs[t][...] = m_new
        vtn = jnp.concatenate([vt_ref[0, jk, n] for _, jk in blocks], axis=1)
        acc_refs[t][...] = alpha * acc_refs[t][...] + _dot(vtn, p)

    n_far = qs // 2
    mask_near = mask_of(near)
    for t in range(n_tiles):
        logits(near, mask_near, True, 0, t)

    def trip_body(trip, carry):
        d0 = 2 * trip
        blk1, blk2 = far_blocks(d0), far_blocks(d0 + 1)
        mask1, mask2 = mask_of(blk1), mask_of(blk2)
        far0 = far_blocks(d0 - 1)
        blk0 = [tuple(jnp.where(d0 == 0, a, b) for a, b in zip(near[h], far0[h])) for h in range(2)]
        for t in range(n_tiles):
            logits(blk1, mask1, False, 1, t)
            softmax_update(blk0, 0, t)
        for t in range(n_tiles):
            logits(blk2, mask2, False, 0, t)
            softmax_update(blk1, 1, t)
        return carry

    lax.fori_loop(0, (n_far + 2) // 2, trip_body, 0)

    for t in range(n_tiles):
        on = acc_refs[t][0:HEAD_DIM, :] / acc_refs[t][HEAD_DIM:HEAD_DIM + 1, :]
        for g in range(hpt):
            hh = t * hpt + g
            ot_ref[hh * HEAD_DIM:(hh + 1) * HEAD_DIM, :] = on[:, g * tq:(g + 1) * tq]
    o_ref[0] = ot_ref[...].T.astype(BF16)


def _sparse_attn(q, qi, wi, kb, vb, kib, bias_tiles, *, q_off, n_keys, k_sel):
    b, t, dq = q.shape
    lp = kb.shape[1]
    dkv = kb.shape[2]
    tq = KEY_BLOCK
    n_heads = dq // HEAD_DIM
    group = n_heads // N_KV_HEADS
    hpt = HEADS_PER_TILE
    n_tiles = n_heads // hpt
    assert group % hpt == 0
    nkb = lp // KEY_BLOCK
    assert t % tq == 0 and lp % KEY_BLOCK == 0 and q_off % KEY_BLOCK == 0 and q_off + t <= lp
    vt = jnp.transpose(vb.reshape(b, nkb, KEY_BLOCK, N_KV_HEADS, HEAD_DIM), (0, 1, 3, 4, 2))
    vt = jnp.concatenate(
        [vt, jnp.ones((b, nkb, N_KV_HEADS, 1, KEY_BLOCK), BF16),
         jnp.zeros((b, nkb, N_KV_HEADS, V_AUG_ROWS - HEAD_DIM - 1, KEY_BLOCK), BF16)], axis=3)
    k2 = jnp.concatenate([kb.reshape(b, lp, N_KV_HEADS, HEAD_DIM)] * 2, axis=-1).reshape(b, lp, 2 * dkv)
    qblk = lambda n: pl.BlockSpec((1, tq, n), lambda bi, i: (bi, i, 0))
    kblk = lambda n: pl.BlockSpec((1, lp, n), lambda bi, i: (bi, 0, 0))
    return pl.pallas_call(
        functools.partial(_sparse_attn_kernel, q_off=q_off, n_keys=n_keys, k_sel=k_sel, group=group),
        out_shape=jax.ShapeDtypeStruct((b, t, dq), BF16),
        grid=(b, t // tq),
        in_specs=[qblk(dq), qblk(qi.shape[2]), qblk(LANES), kblk(2 * dkv),
                  pl.BlockSpec((1, nkb, N_KV_HEADS, V_AUG_ROWS, KEY_BLOCK), lambda bi, i: (bi, 0, 0, 0, 0)),
                  kblk(kib.shape[2]), _const_spec(bias_tiles.shape)],
        out_specs=qblk(dq),
        scratch_shapes=[
            pltpu.VMEM((nkb + 1 + COUNT_UNROLL, KEY_BLOCK, tq), F32),
            pltpu.VMEM((nkb + 1 + COUNT_UNROLL, KEY_BLOCK, tq), BF16),
            pltpu.VMEM((2, KEY_BLOCK, IDX_HEADS * tq), F32),
            pltpu.VMEM((n_tiles, 2 * HEAD_DIM, hpt * tq), BF16),
            pltpu.VMEM((IDX_DIM, IDX_HEADS * tq), BF16),
            pltpu.VMEM((dq, tq), F32),
            pltpu.VMEM((2, n_tiles, 2 * KEY_BLOCK, hpt * tq), F32),
            pltpu.VMEM((2, n_tiles, 1, hpt * tq), F32),
        ] + [pltpu.VMEM((1, hpt * tq), F32)] * n_tiles
          + [pltpu.VMEM((V_AUG_ROWS, hpt * tq), F32)] * n_tiles,
        compiler_params=_params("parallel", "arbitrary"),
        name="sparse_attn",
    )(q, qi, wi, k2, vt, kib, bias_tiles)


def _matmul_res_kernel(x_ref, a_ref, w_ref, o_ref):
    o_ref[...] = x_ref[...] + _dot(a_ref[...], w_ref[...])


def _matmul_res(x2, a2, w):
    m, d = x2.shape
    ka = a2.shape[1]
    tm = _row_tile(m, 512)
    return pl.pallas_call(
        _matmul_res_kernel,
        out_shape=jax.ShapeDtypeStruct((m, d), F32),
        grid=(m // tm,),
        in_specs=[pl.BlockSpec((tm, d), lambda i: (i, 0)), pl.BlockSpec((tm, ka), lambda i: (i, 0)),
                  _const_spec((ka, d))],
        out_specs=pl.BlockSpec((tm, d), lambda i: (i, 0)),
        compiler_params=_params("parallel"),
        name="attn_out_proj",
    )(x2, a2, w.astype(BF16))


def _mem_kv_kernel(x_ref, g_ref, w_ref, k_ref, v_ref, *, d):
    h = _rms(x_ref[...], g_ref[...]).astype(BF16)
    k_ref[...] = _dot(h, w_ref[:, :d])
    v_ref[...] = _dot(h, w_ref[:, d:])


def _mem_kv(mem2, g, w_kv):
    m, d = mem2.shape
    tm = _row_tile(m, 512)
    row = pl.BlockSpec((tm, d), lambda i: (i, 0))
    return pl.pallas_call(
        functools.partial(_mem_kv_kernel, d=d),
        out_shape=[jax.ShapeDtypeStruct((m, d), F32)] * 2,
        grid=(m // tm,),
        in_specs=[row, _const_spec((1, d)), _const_spec((d, 2 * d))],
        out_specs=[row, row],
        compiler_params=_params("parallel"),
        name="mem_kv",
    )(mem2, g.reshape(1, d), w_kv.astype(BF16))


def _mem_attn_kernel(x_ref, g_ref, wq_ref, mk_ref, mv_ref, wo_ref, o_ref, *, hd):
    x = x_ref[...]
    h = _rms(x, g_ref[...]).astype(BF16)
    q = (_dot(h, wq_ref[...]) * (hd ** -0.5)).astype(BF16)
    heads = []
    for a in range(MEM_HEADS):
        cols = slice(a * hd, (a + 1) * hd)
        lg = _dot_nt(q[:, cols], mk_ref[0, :, cols].astype(BF16))
        p = jnp.exp(lg - jnp.max(lg, axis=1, keepdims=True))
        p = (p / jnp.sum(p, axis=1, keepdims=True)).astype(BF16)
        heads.append(_dot(p, mv_ref[0, :, cols].astype(BF16)).astype(BF16))
    o_ref[...] = x + _dot(jnp.concatenate(heads, axis=1), wo_ref[...])


def _mem_attn(x2, g, w_q, mk, mv, w_o, rows_per_batch):
    m, d = x2.shape
    n_mem = mk.shape[1]
    tm = _row_tile(rows_per_batch, 512)
    per = rows_per_batch // tm
    row = pl.BlockSpec((tm, d), lambda i: (i, 0))
    mem = pl.BlockSpec((1, n_mem, d), lambda i: (i // per, 0, 0))
    return pl.pallas_call(
        functools.partial(_mem_attn_kernel, hd=d // MEM_HEADS),
        out_shape=jax.ShapeDtypeStruct((m, d), F32),
        grid=(m // tm,),
        in_specs=[row, _const_spec((1, d)), _const_spec((d, d)), mem, mem, _const_spec((d, d))],
        out_specs=row,
        compiler_params=_params("parallel"),
        name="mem_attn",
    )(x2, g.reshape(1, d), w_q.astype(BF16), mk, mv, w_o.astype(BF16))


def _mlp_kernel(*refs, n_chunks, chunk, final):
    if final:
        x_ref, g_ref, w1_ref, w2_ref, gf_ref, o_ref = refs
    else:
        x_ref, g_ref, w1_ref, w2_ref, o_ref = refs
    x = x_ref[...]
    h = _rms(x, g_ref[...]).astype(BF16)
    acc = x
    for c in range(n_chunks):
        a = jnp.maximum(_dot(h, w1_ref[:, c * chunk:(c + 1) * chunk]), 0.0)
        acc = acc + _dot((a * a).astype(BF16), w2_ref[c * chunk:(c + 1) * chunk, :])
    if final:
        acc = _rms(acc, gf_ref[...])
    o_ref[...] = acc


def _mlp(x2, g, w1, w2, g_final=None):
    m, d = x2.shape
    dff = w1.shape[1]
    chunk = min(dff, 1024)
    tm = _row_tile(m, 512)
    row = pl.BlockSpec((tm, d), lambda i: (i, 0))
    final = g_final is not None
    in_specs = [row, _const_spec((1, d)), _const_spec((d, dff)), _const_spec((dff, d))]
    args = [x2, g.reshape(1, d), w1.astype(BF16), w2.astype(BF16)]
    if final:
        in_specs.append(_const_spec((1, d)))
        args.append(g_final.reshape(1, d))
    return pl.pallas_call(
        functools.partial(_mlp_kernel, n_chunks=dff // chunk, chunk=chunk, final=final),
        out_shape=jax.ShapeDtypeStruct((m, d), F32),
        grid=(m // tm,),
        in_specs=in_specs,
        out_specs=row,
        compiler_params=_params("parallel"),
        name="mlp",
    )(*args)


def _conv_glu_kernel(x_ref, g_ref, w_ref, b_ref, u_ref, *, d):
    h = _rms(x_ref[...], g_ref[...]).astype(BF16)
    a = _dot(h, w_ref[:, :d]) + b_ref[:, :d]
    gate = _dot(h, w_ref[:, d:]) + b_ref[:, d:]
    u_ref[...] = a * (1.0 / (1.0 + jnp.exp(-gate)))


def _conv_glu(x2, g, w_pw1, b_pw1):
    m, d = x2.shape
    tm = _row_tile(m, 512)
    row = pl.BlockSpec((tm, d), lambda i: (i, 0))
    return pl.pallas_call(
        functools.partial(_conv_glu_kernel, d=d),
        out_shape=jax.ShapeDtypeStruct((m, d), F32),
        grid=(m // tm,),
        in_specs=[row, _const_spec((1, d)), _const_spec((d, 2 * d)), _const_spec((1, 2 * d))],
        out_specs=row,
        compiler_params=_params("parallel"),
        name="conv_glu",
    )(x2, g.reshape(1, d), w_pw1.astype(BF16), b_pw1.reshape(1, 2 * d))


def _conv_rest_kernel(x_ref, u_ref, prev_ref, init_ref, wdw_ref, bdw_ref, lng_ref, lnb_ref,
                      w2_ref, b2_ref, o_ref, ext_ref, sh_ref, y_ref, *, tm, rc, lc):
    t = pl.program_id(1)
    d = u_ref.shape[2]
    pad = CONV_WIDTH - 1

    @pl.when(t == 0)
    def _():
        ext_ref[0:HALO, :] = init_ref[0]

    @pl.when(t > 0)
    def _():
        ext_ref[0:HALO, :] = prev_ref[0]

    ext_ref[HALO:HALO + tm, :] = u_ref[0]

    for s in range(SUBLANES):
        rows = tm + SUBLANES * ((CONV_WIDTH - 1 - s) // SUBLANES)
        sh_ref[s, 0:rows, :] = ext_ref[pl.ds(HALO - pad + s, rows), :]
    for r0 in range(0, tm, rc):
        for c0 in range(0, d, lc):
            cols = slice(c0, c0 + lc)
            y = jnp.broadcast_to(bdw_ref[:, cols], (rc, lc))
            for w in range(CONV_WIDTH):
                a, s = divmod(w, SUBLANES)
                y = y + sh_ref[s, r0 + SUBLANES * a:r0 + SUBLANES * a + rc, cols] * wdw_ref[w:w + 1, cols]
            y_ref[r0:r0 + rc, cols] = y

    y = y_ref[...]
    mu = jnp.mean(y, axis=-1, keepdims=True)
    yc = y - mu
    var = jnp.mean(yc * yc, axis=-1, keepdims=True)
    yn = yc * lax.rsqrt(var + EPS) * lng_ref[...] + lnb_ref[...]
    act = (yn * (1.0 / (1.0 + jnp.exp(-yn)))).astype(BF16)
    o_ref[0] = x_ref[0] + _dot(act, w2_ref[...]) + b2_ref[...]


def _conv_rest(x3, u3, init, w_dw, b_dw, ln_g, ln_b, w_pw2, b_pw2):
    b, t, d = x3.shape
    tm = _row_tile(t, 256)
    rc = min(tm, 64)
    lc = min(d, 256)
    assert tm % HALO == 0 and tm % rc == 0 and d % lc == 0
    per = tm // HALO
    tile = pl.BlockSpec((1, tm, d), lambda bi, ti: (bi, ti, 0))
    prev = pl.BlockSpec((1, HALO, d), lambda bi, ti: (bi, jnp.maximum(ti * per - 1, 0), 0))
    first = pl.BlockSpec((1, HALO, d), lambda bi, ti: (bi, 0, 0))
    vec = _const_spec((1, d))
    wdw = jnp.pad(w_dw, ((0, HALO - CONV_WIDTH), (0, 0)))
    return pl.pallas_call(
        functools.partial(_conv_rest_kernel, tm=tm, rc=rc, lc=lc),
        out_shape=jax.ShapeDtypeStruct((b, t, d), F32),
        grid=(b, t // tm),
        in_specs=[tile, tile, prev, first, _const_spec((HALO, d)), vec, vec, vec,
                  _const_spec((d, d)), vec],
        out_specs=tile,
        scratch_shapes=[pltpu.VMEM((HALO + tm, d), F32),
                        pltpu.VMEM((SUBLANES, tm + HALO - SUBLANES, d), F32),
                        pltpu.VMEM((tm, d), F32)],
        compiler_params=_params("parallel", "arbitrary"),
        name="conv_rest",
    )(x3, u3, u3, init, wdw, b_dw.reshape(1, d), ln_g.reshape(1, d), ln_b.reshape(1, d),
      w_pw2.astype(BF16), b_pw2.reshape(1, d))


def _mixer_attn(x3, g, w_in, w_out, bias_tiles, cache=None):
    b, t, d = x3.shape
    x2 = x3.reshape(b * t, d)
    q, k, v, ki, kb, vb, kib, qi, wi = _attn_proj(x2, g, w_in)
    r3 = lambda a: a.reshape(b, t, a.shape[-1])
    kb, vb, kib = r3(kb), r3(vb), r3(kib)
    past = 0
    if cache is not None:
        ck, cv, cki = cache
        past = ck.shape[1]
        kb = jnp.concatenate([ck.reshape(b, past, -1).astype(BF16), kb], axis=1)
        vb = jnp.concatenate([cv.reshape(b, past, -1).astype(BF16), vb], axis=1)
        kib = jnp.concatenate([cki.astype(BF16), kib], axis=1)
    n_keys = past + t
    k_sel = min(TOPK_MAX, n_keys // 4)
    tpad = -(-t // KEY_BLOCK) * KEY_BLOCK
    lp = max(-(-n_keys // KEY_BLOCK) * KEY_BLOCK, past + tpad)
    padt = lambda a, n: a if a.shape[1] == n else jnp.pad(a, ((0, 0), (0, n - a.shape[1]), (0, 0)))
    o = _sparse_attn(padt(r3(q), tpad), padt(r3(qi), tpad), padt(r3(wi), tpad),
                     padt(kb, lp), padt(vb, lp), padt(kib, lp), bias_tiles,
                     q_off=past, n_keys=n_keys, k_sel=k_sel)[:, :t]
    x2 = _matmul_res(x2, o.reshape(b * t, d), w_out)
    return (x2.reshape(b, t, d), k.reshape(b, t, N_KV_HEADS, HEAD_DIM),
            v.reshape(b, t, N_KV_HEADS, HEAD_DIM), ki.reshape(b, t, IDX_DIM))


def _mixer_conv(x3, g, w_pw1, b_pw1, w_dw, b_dw, ln_g, ln_b, w_pw2, b_pw2, state=None):
    b, t, d = x3.shape
    pad = CONV_WIDTH - 1
    u3 = _conv_glu(x3.reshape(b * t, d), g, w_pw1, b_pw1).reshape(b, t, d)
    if state is None:
        init = jnp.zeros((b, HALO, d), F32)
        tail = u3[:, -pad:] if t >= pad else jnp.pad(u3, ((0, 0), (pad - t, 0), (0, 0)))
    else:
        init = jnp.pad(state.astype(F32), ((0, 0), (HALO - pad, 0), (0, 0)))
        tail = jnp.concatenate([state.astype(F32), u3], axis=1)[:, -pad:]
    x3 = _conv_rest(x3, u3, init, w_dw, b_dw, ln_g, ln_b, w_pw2, b_pw2)
    return x3, tail


def kernel(x_prompt, x_sample, cache_attn_k, cache_attn_v, cache_attn_kidx, state_conv, cache_mem_k, cache_mem_v, mem_prompt, rel_bias, g_mix, w_in_attn, w_out_attn, w_pw1, b_pw1, w_dw, b_dw, ln_g, ln_b, w_pw2, b_pw2, g_mem_q, g_mem_src, w_mem_q, w_mem_kv, w_mem_o, g_mlp, w_mlp1, w_mlp2, g_final):
    depth = g_mix.shape[0]
    bp, tp, d = x_prompt.shape
    bs, ts, _ = x_sample.shape
    n_mem = mem_prompt.shape[1]
    mem_hd = d // MEM_HEADS
    bias_tiles = _bias_tiles(rel_bias)
    xp, xs = x_prompt, x_sample
    kp_l, vp_l, kip_l, ks_l, vs_l, kis_l = [], [], [], [], [], []
    convp_l, convs_l, memk_l, memv_l = [], [], [], []
    for i in range(depth):
        j = i // 2
        if i % 2 == 0:
            xp, kp, vp, kip = _mixer_attn(xp, g_mix[i], w_in_attn[j], w_out_attn[j], bias_tiles)
            xs, ks, vs, kis = _mixer_attn(
                xs, g_mix[i], w_in_attn[j], w_out_attn[j], bias_tiles,
                cache=(cache_attn_k[j], cache_attn_v[j], cache_attn_kidx[j]))
            kp_l.append(kp); vp_l.append(vp); kip_l.append(kip)
            ks_l.append(ks); vs_l.append(vs); kis_l.append(kis)
        else:
            cw = (w_pw1[j], b_pw1[j], w_dw[j], b_dw[j], ln_g[j], ln_b[j], w_pw2[j], b_pw2[j])
            xp, cp = _mixer_conv(xp, g_mix[i], *cw)
            xs, cs = _mixer_conv(xs, g_mix[i], *cw, state=state_conv[j])
            convp_l.append(cp); convs_l.append(cs)
        mk, mv = _mem_kv(mem_prompt.reshape(bp * n_mem, d), g_mem_src[i], w_mem_kv[i])
        mk, mv = mk.reshape(bp, n_mem, d), mv.reshape(bp, n_mem, d)
        memk_l.append(mk.reshape(bp, n_mem, MEM_HEADS, mem_hd))
        memv_l.append(mv.reshape(bp, n_mem, MEM_HEADS, mem_hd))
        xp2 = _mem_attn(xp.reshape(bp * tp, d), g_mem_q[i], w_mem_q[i], mk, mv, w_mem_o[i], tp)
        xs2 = _mem_attn(xs.reshape(bs * ts, d), g_mem_q[i], w_mem_q[i],
                        cache_mem_k[i].reshape(bs, n_mem, d), cache_mem_v[i].reshape(bs, n_mem, d),
                        w_mem_o[i], ts)
        gf = g_final if i == depth - 1 else None
        xp = _mlp(xp2, g_mlp[i], w_mlp1[i], w_mlp2[i], gf).reshape(bp, tp, d)
        xs = _mlp(xs2, g_mlp[i], w_mlp1[i], w_mlp2[i], gf).reshape(bs, ts, d)
    return (xp, xs, jnp.stack(kp_l), jnp.stack(vp_l), jnp.stack(kip_l), jnp.stack(convp_l),
            jnp.stack(memk_l), jnp.stack(memv_l), jnp.stack(ks_l), jnp.stack(vs_l),
            jnp.stack(kis_l), jnp.stack(convs_l))
```

```python
import functools
import math

import jax
import jax.numpy as jnp
from jax import lax
from jax.experimental import pallas as pl
from jax.experimental.pallas import tpu as pltpu

CHUNK = 64
HEAD_DIM = 64
N_KV_HEADS = 4
IDX_HEADS = 8
IDX_DIM = 64
TOPK_MAX = 256
N_BUCKETS = 32
MAX_DISTANCE = 128
CONV_WIDTH = 31
MEM_HEADS = 4
EPS = 1e-6

LANES = 128
SUBLANES = 8
KEY_BLOCK = 128
HEADS_PER_TILE = 2
COUNT_UNROLL = 8
LOW_CHECK_EVERY = 4
V_AUG_ROWS = HEAD_DIM + 16
LOG2E = math.log2(math.e)
HALO = 32
VMEM_LIMIT = 56 * 1024 * 1024

NEG_BIG = -1e30
F32 = jnp.float32
BF16 = jnp.bfloat16

KEY_NEG_INF = -2139095041
KEY_POS_INF = 2139095040


def _const_spec(shape):
    nd = len(shape)
    return pl.BlockSpec(shape, lambda *_: (0,) * nd, pipeline_mode=pl.Buffered(1))


def _params(*sem):
    return pltpu.CompilerParams(dimension_semantics=sem, vmem_limit_bytes=VMEM_LIMIT)


def _rms(x, g):
    ms = jnp.mean(x * x, axis=-1, keepdims=True)
    return x * lax.rsqrt(ms + EPS) * g


def _dot(a, b):
    return jnp.dot(a, b, preferred_element_type=F32)


def _dot_nt(a, b):
    return lax.dot_general(a, b, (((1,), (1,)), ((), ())), preferred_element_type=F32)


def _row_tile(m, pref):
    t = min(m, pref)
    assert m % t == 0, (m, t)
    return t


def _bias_tiles_kernel(tab_ref, bt_ref, *, n_heads):
    nb = N_BUCKETS // 2
    max_exact = nb // 2
    c = lax.broadcasted_iota(jnp.int32, (KEY_BLOCK, KEY_BLOCK), 0)
    r = lax.broadcasted_iota(jnp.int32, (KEY_BLOCK, KEY_BLOCK), 1)
    for d in range(2):
        rel = c - r - d * KEY_BLOCK
        n = jnp.abs(rel)
        nf = jnp.maximum(n, 1).astype(F32)
        large = max_exact + (jnp.log(nf / max_exact) / math.log(MAX_DISTANCE / max_exact)
                             * (nb - max_exact)).astype(jnp.int32)
        large = jnp.minimum(large, nb - 1)
        bucket = jnp.where(rel > 0, nb, 0) + jnp.where(n < max_exact, n, large)
        for h in range(n_heads):
            acc = jnp.zeros((KEY_BLOCK, KEY_BLOCK), F32)
            for b in range(N_BUCKETS):
                acc = jnp.where(bucket == b, tab_ref[b, h], acc)
            g = h % HEADS_PER_TILE
            rows = slice((1 - d) * KEY_BLOCK, (2 - d) * KEY_BLOCK)
            bt_ref[h // HEADS_PER_TILE, rows, g * KEY_BLOCK:(g + 1) * KEY_BLOCK] = (
                (acc - tab_ref[nb - 1, h]) * LOG2E)


def _bias_tiles(rel_bias):
    n_heads = rel_bias.shape[1]
    assert n_heads % HEADS_PER_TILE == 0
    return pl.pallas_call(
        functools.partial(_bias_tiles_kernel, n_heads=n_heads),
        out_shape=jax.ShapeDtypeStruct(
            (n_heads // HEADS_PER_TILE, 2 * KEY_BLOCK, HEADS_PER_TILE * KEY_BLOCK), F32),
        in_specs=[pl.BlockSpec(memory_space=pltpu.SMEM)],
        out_specs=pl.BlockSpec(memory_space=pltpu.VMEM),
        name="bias_tiles",
    )(rel_bias)


def _attn_proj_kernel(x_ref, g_ref, w_ref, q_ref, k_ref, v_ref, ki_ref, kb_ref, vb_ref,
                      kib_ref, qi_ref, wi_ref, *, dq, dkv, dqi):
    h = _rms(x_ref[...], g_ref[...]).astype(BF16)
    o = 0
    q_ref[...] = (_dot(h, w_ref[:, o:o + dq]) * (HEAD_DIM ** -0.5)).astype(BF16)
    o += dq
    k = _dot(h, w_ref[:, o:o + dkv])
    k_ref[...] = k
    kb_ref[...] = k.astype(BF16)
    o += dkv
    v = _dot(h, w_ref[:, o:o + dkv])
    v_ref[...] = v
    vb_ref[...] = v.astype(BF16)
    o += dkv
    qi_ref[...] = (_dot(h, w_ref[:, o:o + dqi]) * (IDX_DIM ** -0.5)).astype(BF16)
    o += dqi
    ki = _dot(h, w_ref[:, o:o + LANES])[:, :IDX_DIM]
    ki_ref[...] = ki
    kib_ref[...] = ki.astype(BF16)
    o += LANES
    wi_ref[...] = _dot(h, w_ref[:, o:o + LANES]) * (IDX_HEADS ** -0.5)


def _attn_proj(x2, g, w_in):
    m, d = x2.shape
    dq = d
    dkv = N_KV_HEADS * HEAD_DIM
    dqi = IDX_HEADS * IDX_DIM
    base = dq + 2 * dkv + dqi
    w_main = w_in[:, :base]
    w_ki = jnp.pad(w_in[:, base:base + IDX_DIM], ((0, 0), (0, LANES - IDX_DIM)))
    w_wi = jnp.pad(w_in[:, base + IDX_DIM:], ((0, 0), (0, LANES - IDX_HEADS)))
    w = jnp.concatenate([w_main, w_ki, w_wi], axis=1).astype(BF16)
    tm = _row_tile(m, 512)
    row = lambda n: pl.BlockSpec((tm, n), lambda i: (i, 0))
    outs = [(dq, BF16), (dkv, F32), (dkv, F32), (IDX_DIM, F32), (dkv, BF16), (dkv, BF16),
            (IDX_DIM, BF16), (dqi, BF16), (LANES, F32)]
    return pl.pallas_call(
        functools.partial(_attn_proj_kernel, dq=dq, dkv=dkv, dqi=dqi),
        out_shape=[jax.ShapeDtypeStruct((m, n), dt) for n, dt in outs],
        grid=(m // tm,),
        in_specs=[row(d), _const_spec((1, d)), _const_spec(w.shape)],
        out_specs=[row(n) for n, _ in outs],
        compiler_params=_params("parallel"),
        name="attn_proj",
    )(x2, g.reshape(1, d), w)


def _sparse_attn_kernel(q_ref, qi_ref, wi_ref, k_ref, vt_ref, ki_ref, bt_ref, o_ref,
                        s_ref, sb_ref, dots_ref, qgt_ref, qit_ref, ot_ref, lg_ref, mb_ref, *state_refs,
                        q_off, n_keys, k_sel, group):
    tq = KEY_BLOCK
    hpt = HEADS_PER_TILE
    n_tiles = qgt_ref.shape[0]
    m_refs, acc_refs = state_refs[:n_tiles], state_refs[n_tiles:]
    i = pl.program_id(1)
    qs = q_off // KEY_BLOCK + i
    n_kb = qs + 1
    q_start = q_off + i * tq

    qt = q_ref[0].astype(F32).T * LOG2E
    for t in range(n_tiles):
        q2 = jnp.concatenate(
            [qt[(t * hpt + g) * HEAD_DIM:(t * hpt + g + 1) * HEAD_DIM, :] for g in range(hpt)], axis=1)
        q_hi = q2.astype(BF16)
        q_lo = (q2 - q_hi.astype(F32)).astype(BF16)
        qgt_ref[t] = jnp.concatenate([q_hi, q_lo], axis=0)
    qit = qi_ref[0].astype(F32).T
    qit_ref[...] = jnp.concatenate([qit[h * IDX_DIM:(h + 1) * IDX_DIM, :] for h in range(IDX_HEADS)],
                                   axis=1).astype(BF16)
    last_kv = k_ref.shape[1] // KEY_BLOCK - 1
    wit = wi_ref[0].T

    qpos = q_start + lax.broadcasted_iota(jnp.int32, (1, tq), 1)
    lim = jnp.minimum((qpos // CHUNK + 1) * CHUNK, n_keys)
    kidx = lax.broadcasted_iota(jnp.int32, (KEY_BLOCK, tq), 0)

    def key_to_f32(key):
        bits = key ^ ((key >> 31) & 0x7FFFFFFF)
        return lax.bitcast_convert_type(bits, F32)

    def put_scores(j, sc):
        s_ref[j] = sc
        bits = lax.bitcast_convert_type(jnp.where(sc == 0.0, 0.0, sc), jnp.int32)
        key = bits ^ ((bits >> 31) & 0x7FFFFFFF)
        down = jnp.maximum((key >> 16) << 16, KEY_NEG_INF)
        sb_ref[j] = key_to_f32(down).astype(BF16)

    def dots(j, slot):
        k0 = pl.multiple_of(jnp.minimum(j, last_kv) * KEY_BLOCK, KEY_BLOCK)
        dots_ref[slot] = _dot(ki_ref[0, pl.ds(k0, KEY_BLOCK), :], qit_ref[...])

    def finish(j, slot):
        acc = jnp.zeros((KEY_BLOCK, tq), F32)
        for h in range(IDX_HEADS):
            acc = acc + wit[h:h + 1, :] * jnp.maximum(dots_ref[slot, :, h * tq:(h + 1) * tq], 0.0)
        put_scores(j, jnp.where(kidx + j * KEY_BLOCK < lim, acc, -jnp.inf))

    def score_quad(quad, carry):
        j = 4 * quad
        dots(j + 2, 2)
        dots(j + 3, 3)
        finish(j, 0)
        finish(j + 1, 1)
        dots(j + 4, 0)
        dots(j + 5, 1)
        finish(j + 2, 2)
        finish(j + 3, 3)
        return carry

    n_pairs = (n_kb + 1) // 2
    dots(0, 0)
    dots(1, 1)
    lax.fori_loop(0, (n_kb + 3) // 4, score_quad, 0)
    for u in range(COUNT_UNROLL):
        put_scores(2 * n_pairs + u, jnp.full((KEY_BLOCK, tq), -jnp.inf, F32))
    n_count = (2 * n_pairs + COUNT_UNROLL - 1) // COUNT_UNROLL

    def count_ge(ref, cand):
        cb = jnp.broadcast_to(cand, (KEY_BLOCK, tq)).astype(ref.dtype)
        one, zero = jnp.ones((), ref.dtype), jnp.zeros((), ref.dtype)

        def body(p, c):
            for u in range(COUNT_UNROLL):
                c = c + jnp.where(ref[COUNT_UNROLL * p + u] >= cb, one, zero)
            return c
        c = lax.fori_loop(0, n_count, body, jnp.zeros((KEY_BLOCK, tq), ref.dtype))
        return jnp.sum(c.astype(F32), axis=0, keepdims=True)

    def bisect_step(count_at, carry):
        lo, hi, n_lo = carry
        mid = (lo + hi) >> 1
        n_mid = count_at(mid)
        ok = n_mid >= k_sel
        return jnp.where(ok, mid, lo), jnp.where(ok, hi, mid), jnp.where(ok, n_mid, n_lo)

    full = lambda v: jnp.full((1, tq), v, jnp.int32)
    count_hi = lambda v: count_ge(sb_ref, key_to_f32(v << 16))
    key_hi, _, n_sel = lax.fori_loop(
        0, 16, lambda _, c: bisect_step(count_hi, c),
        (full((KEY_NEG_INF + 1) >> 16), full((KEY_POS_INF >> 16) + 1), jnp.full((1, tq), jnp.inf, F32)))

    count_lo = lambda v: count_ge(s_ref, key_to_f32((key_hi << 16) + v))

    def low_steps(carry):
        step, c = carry[0], carry[1:]
        for _ in range(LOW_CHECK_EVERY):
            c = bisect_step(count_lo, c)
        return (step + LOW_CHECK_EVERY,) + c

    def low_unfinished(carry):
        step, n_lo = carry[0], carry[3]
        return (step < 16) & (jnp.max(jnp.abs(n_lo - k_sel)) > 0.0)

    _, key_lo, _, _ = lax.while_loop(low_unfinished, low_steps, (0, full(0), full(1 << 16), n_sel))
    thr = key_to_f32((key_hi << 16) + key_lo)

    def count(cand, strict):
        cb = jnp.broadcast_to(cand, (KEY_BLOCK, tq))
        hit = (lambda s: s > cb) if strict else (lambda s: s >= cb)

        def body(p, c):
            c = c + jnp.where(hit(s_ref[2 * p]), 1.0, 0.0)
            return c + jnp.where(hit(s_ref[2 * p + 1]), 1.0, 0.0)
        c = lax.fori_loop(0, n_pairs, body, jnp.zeros((KEY_BLOCK, tq), F32))
        return jnp.sum(c, axis=0, keepdims=True)

    surplus = count_ge(s_ref, thr) - k_sel

    @pl.when(jnp.max(surplus) > 0.0)
    def _():
        n_ties = k_sel - count(thr, True)
        row_i = lax.broadcasted_iota(jnp.int32, (KEY_BLOCK, KEY_BLOCK), 0)
        col_i = lax.broadcasted_iota(jnp.int32, (KEY_BLOCK, KEY_BLOCK), 1)
        lower = jnp.where(col_i < row_i, 1.0, 0.0).astype(BF16)
        ones = jnp.ones((KEY_BLOCK, KEY_BLOCK), BF16)

        def tie_body(j, seen):
            s = s_ref[j]
            eq = s == thr
            e = jnp.where(eq, 1.0, 0.0).astype(BF16)
            before = _dot(lower, e) + seen
            s_ref[j] = jnp.where(eq & (before >= n_ties), -jnp.inf, s)
            return seen + _dot(ones, e)

        lax.fori_loop(0, n_kb, tie_body, jnp.zeros((KEY_BLOCK, tq), F32))

    thr_sel = jnp.broadcast_to(jnp.maximum(thr, jnp.finfo(F32).min), (KEY_BLOCK, tq))

    for t in range(n_tiles):
        m_refs[t][...] = jnp.full(m_refs[t].shape, NEG_BIG, F32)
        acc_refs[t][...] = jnp.zeros(acc_refs[t].shape, F32)

    def far_blocks(db):
        out = []
        for j in (2 * db, 2 * db + 1):
            js = jnp.where(j < qs - 1, j, n_kb)
            out.append((js, jnp.minimum(js, last_kv)))
        return out

    near = [(jnp.where(qs >= 1, qs - 1, n_kb), jnp.maximum(qs - 1, 0)), (qs, qs)]

    def mask_of(blocks):
        m = jnp.concatenate([jnp.where(s_ref[js] >= thr_sel, 0.0, NEG_BIG) for js, _ in blocks], axis=0)
        return jnp.concatenate([m] * hpt, axis=1)

    def logits(blocks, mask, biased, slot, t):
        n = (t * hpt) // group
        kn = jnp.concatenate(
            [k_ref[0, pl.ds(pl.multiple_of(jk * KEY_BLOCK, KEY_BLOCK), KEY_BLOCK),
                   n * 2 * HEAD_DIM:(n + 1) * 2 * HEAD_DIM] for _, jk in blocks], axis=0)
        lg = _dot(kn, qgt_ref[t]) + mask
        if biased:
            lg = lg + bt_ref[t]
        lg_ref[slot, t] = lg
        mb_ref[slot, t] = jnp.max(lg, axis=0, keepdims=True)

    def softmax_update(blocks, slot, t):
        n = (t * hpt) // group
        m_old = m_refs[t][...]
        m_new = jnp.maximum(m_old, mb_ref[slot, t])
        alpha = jnp.exp2(m_old - m_new)
        p = jnp.exp2(lg_ref[slot, t] - m_new).astype(BF16)
        m_refs[t][...] = m_new
        vtn = jnp.concatenate([vt_ref[0, jk, n] for _, jk in blocks], axis=1)
        acc_refs[t][...] = alpha * acc_refs[t][...] + _dot(vtn, p)

    n_far = qs // 2
    mask_near = mask_of(near)
    for t in range(n_tiles):
        logits(near, mask_near, True, 0, t)

    def trip_body(trip, carry):
        d0 = 2 * trip
        blk1, blk2 = far_blocks(d0), far_blocks(d0 + 1)
        mask1, mask2 = mask_of(blk1), mask_of(blk2)
        far0 = far_blocks(d0 - 1)
        blk0 = [tuple(jnp.where(d0 == 0, a, b) for a, b in zip(near[h], far0[h])) for h in range(2)]
        for t in range(n_tiles):
            logits(blk1, mask1, False, 1, t)
            softmax_update(blk0, 0, t)
        for t in range(n_tiles):
            logits(blk2, mask2, False, 0, t)
            softmax_update(blk1, 1, t)
        return carry

    lax.fori_loop(0, (n_far + 2) // 2, trip_body, 0)

    for t in range(n_tiles):
        on = acc_refs[t][0:HEAD_DIM, :] / acc_refs[t][HEAD_DIM:HEAD_DIM + 1, :]
        for g in range(hpt):
            hh = t * hpt + g
            ot_ref[hh * HEAD_DIM:(hh + 1) * HEAD_DIM, :] = on[:, g * tq:(g + 1) * tq]
    o_ref[0] = ot_ref[...].T.astype(BF16)


def _sparse_attn(q, qi, wi, kb, vb, kib, bias_tiles, *, q_off, n_keys, k_sel):
    b, t, dq = q.shape
    lp = kb.shape[1]
    dkv = kb.shape[2]
    tq = KEY_BLOCK
    n_heads = dq // HEAD_DIM
    group = n_heads // N_KV_HEADS
    hpt = HEADS_PER_TILE
    n_tiles = n_heads // hpt
    assert group % hpt == 0
    nkb = lp // KEY_BLOCK
    assert t % tq == 0 and lp % KEY_BLOCK == 0 and q_off % KEY_BLOCK == 0 and q_off + t <= lp
    vt = jnp.transpose(vb.reshape(b, nkb, KEY_BLOCK, N_KV_HEADS, HEAD_DIM), (0, 1, 3, 4, 2))
    vt = jnp.concatenate(
        [vt, jnp.ones((b, nkb, N_KV_HEADS, 1, KEY_BLOCK), BF16),
         jnp.zeros((b, nkb, N_KV_HEADS, V_AUG_ROWS - HEAD_DIM - 1, KEY_BLOCK), BF16)], axis=3)
    k2 = jnp.concatenate([kb.reshape(b, lp, N_KV_HEADS, HEAD_DIM)] * 2, axis=-1).reshape(b, lp, 2 * dkv)
    qblk = lambda n: pl.BlockSpec((1, tq, n), lambda bi, i: (bi, i, 0))
    kblk = lambda n: pl.BlockSpec((1, lp, n), lambda bi, i: (bi, 0, 0))
    return pl.pallas_call(
        functools.partial(_sparse_attn_kernel, q_off=q_off, n_keys=n_keys, k_sel=k_sel, group=group),
        out_shape=jax.ShapeDtypeStruct((b, t, dq), BF16),
        grid=(b, t // tq),
        in_specs=[qblk(dq), qblk(qi.shape[2]), qblk(LANES), kblk(2 * dkv),
                  pl.BlockSpec((1, nkb, N_KV_HEADS, V_AUG_ROWS, KEY_BLOCK), lambda bi, i: (bi, 0, 0, 0, 0)),
                  kblk(kib.shape[2]), _const_spec(bias_tiles.shape)],
        out_specs=qblk(dq),
        scratch_shapes=[
            pltpu.VMEM((nkb + 1 + COUNT_UNROLL, KEY_BLOCK, tq), F32),
            pltpu.VMEM((nkb + 1 + COUNT_UNROLL, KEY_BLOCK, tq), BF16),
            pltpu.VMEM((4, KEY_BLOCK, IDX_HEADS * tq), F32),
            pltpu.VMEM((n_tiles, 2 * HEAD_DIM, hpt * tq), BF16),
            pltpu.VMEM((IDX_DIM, IDX_HEADS * tq), BF16),
            pltpu.VMEM((dq, tq), F32),
            pltpu.VMEM((2, n_tiles, 2 * KEY_BLOCK, hpt * tq), F32),
            pltpu.VMEM((2, n_tiles, 1, hpt * tq), F32),
        ] + [pltpu.VMEM((1, hpt * tq), F32)] * n_tiles
          + [pltpu.VMEM((V_AUG_ROWS, hpt * tq), F32)] * n_tiles,
        compiler_params=_params("parallel", "arbitrary"),
        name="sparse_attn",
    )(q, qi, wi, k2, vt, kib, bias_tiles)


def _matmul_res_kernel(x_ref, a_ref, w_ref, o_ref):
    o_ref[...] = x_ref[...] + _dot(a_ref[...], w_ref[...])


def _matmul_res(x2, a2, w):
    m, d = x2.shape
    ka = a2.shape[1]
    tm = _row_tile(m, 512)
    return pl.pallas_call(
        _matmul_res_kernel,
        out_shape=jax.ShapeDtypeStruct((m, d), F32),
        grid=(m // tm,),
        in_specs=[pl.BlockSpec((tm, d), lambda i: (i, 0)), pl.BlockSpec((tm, ka), lambda i: (i, 0)),
                  _const_spec((ka, d))],
        out_specs=pl.BlockSpec((tm, d), lambda i: (i, 0)),
        compiler_params=_params("parallel"),
        name="attn_out_proj",
    )(x2, a2, w.astype(BF16))


def _mem_kv_kernel(x_ref, g_ref, w_ref, k_ref, v_ref, *, d):
    h = _rms(x_ref[...], g_ref[...]).astype(BF16)
    k_ref[...] = _dot(h, w_ref[:, :d])
    v_ref[...] = _dot(h, w_ref[:, d:])


def _mem_kv(mem2, g, w_kv):
    m, d = mem2.shape
    tm = _row_tile(m, 512)
    row = pl.BlockSpec((tm, d), lambda i: (i, 0))
    return pl.pallas_call(
        functools.partial(_mem_kv_kernel, d=d),
        out_shape=[jax.ShapeDtypeStruct((m, d), F32)] * 2,
        grid=(m // tm,),
        in_specs=[row, _const_spec((1, d)), _const_spec((d, 2 * d))],
        out_specs=[row, row],
        compiler_params=_params("parallel"),
        name="mem_kv",
    )(mem2, g.reshape(1, d), w_kv.astype(BF16))


def _mem_attn_kernel(x_ref, g_ref, wq_ref, mk_ref, mv_ref, wo_ref, o_ref, *, hd):
    x = x_ref[...]
    h = _rms(x, g_ref[...]).astype(BF16)
    q = (_dot(h, wq_ref[...]) * (hd ** -0.5)).astype(BF16)
    heads = []
    for a in range(MEM_HEADS):
        cols = slice(a * hd, (a + 1) * hd)
        lg = _dot_nt(q[:, cols], mk_ref[0, :, cols].astype(BF16))
        p = jnp.exp(lg - jnp.max(lg, axis=1, keepdims=True))
        p = (p / jnp.sum(p, axis=1, keepdims=True)).astype(BF16)
        heads.append(_dot(p, mv_ref[0, :, cols].astype(BF16)).astype(BF16))
    o_ref[...] = x + _dot(jnp.concatenate(heads, axis=1), wo_ref[...])


def _mem_attn(x2, g, w_q, mk, mv, w_o, rows_per_batch):
    m, d = x2.shape
    n_mem = mk.shape[1]
    tm = _row_tile(rows_per_batch, 512)
    per = rows_per_batch // tm
    row = pl.BlockSpec((tm, d), lambda i: (i, 0))
    mem = pl.BlockSpec((1, n_mem, d), lambda i: (i // per, 0, 0))
    return pl.pallas_call(
        functools.partial(_mem_attn_kernel, hd=d // MEM_HEADS),
        out_shape=jax.ShapeDtypeStruct((m, d), F32),
        grid=(m // tm,),
        in_specs=[row, _const_spec((1, d)), _const_spec((d, d)), mem, mem, _const_spec((d, d))],
        out_specs=row,
        compiler_params=_params("parallel"),
        name="mem_attn",
    )(x2, g.reshape(1, d), w_q.astype(BF16), mk, mv, w_o.astype(BF16))


def _mlp_kernel(*refs, n_chunks, chunk, final):
    if final:
        x_ref, g_ref, w1_ref, w2_ref, gf_ref, o_ref = refs
    else:
        x_ref, g_ref, w1_ref, w2_ref, o_ref = refs
    x = x_ref[...]
    h = _rms(x, g_ref[...]).astype(BF16)
    acc = x
    for c in range(n_chunks):
        a = jnp.maximum(_dot(h, w1_ref[:, c * chunk:(c + 1) * chunk]), 0.0)
        acc = acc + _dot((a * a).astype(BF16), w2_ref[c * chunk:(c + 1) * chunk, :])
    if final:
        acc = _rms(acc, gf_ref[...])
    o_ref[...] = acc


def _mlp(x2, g, w1, w2, g_final=None):
    m, d = x2.shape
    dff = w1.shape[1]
    chunk = min(dff, 1024)
    tm = _row_tile(m, 512)
    row = pl.BlockSpec((tm, d), lambda i: (i, 0))
    final = g_final is not None
    in_specs = [row, _const_spec((1, d)), _const_spec((d, dff)), _const_spec((dff, d))]
    args = [x2, g.reshape(1, d), w1.astype(BF16), w2.astype(BF16)]
    if final:
        in_specs.append(_const_spec((1, d)))
        args.append(g_final.reshape(1, d))
    return pl.pallas_call(
        functools.partial(_mlp_kernel, n_chunks=dff // chunk, chunk=chunk, final=final),
        out_shape=jax.ShapeDtypeStruct((m, d), F32),
        grid=(m // tm,),
        in_specs=in_specs,
        out_specs=row,
        compiler_params=_params("parallel"),
        name="mlp",
    )(*args)


def _conv_glu_kernel(x_ref, g_ref, w_ref, b_ref, u_ref, *, d):
    h = _rms(x_ref[...], g_ref[...]).astype(BF16)
    a = _dot(h, w_ref[:, :d]) + b_ref[:, :d]
    gate = _dot(h, w_ref[:, d:]) + b_ref[:, d:]
    u_ref[...] = a * (1.0 / (1.0 + jnp.exp(-gate)))


def _conv_glu(x2, g, w_pw1, b_pw1):
    m, d = x2.shape
    tm = _row_tile(m, 512)
    row = pl.BlockSpec((tm, d), lambda i: (i, 0))
    return pl.pallas_call(
        functools.partial(_conv_glu_kernel, d=d),
        out_shape=jax.ShapeDtypeStruct((m, d), F32),
        grid=(m // tm,),
        in_specs=[row, _const_spec((1, d)), _const_spec((d, 2 * d)), _const_spec((1, 2 * d))],
        out_specs=row,
        compiler_params=_params("parallel"),
        name="conv_glu",
    )(x2, g.reshape(1, d), w_pw1.astype(BF16), b_pw1.reshape(1, 2 * d))


def _conv_rest_kernel(x_ref, u_ref, prev_ref, init_ref, wdw_ref, bdw_ref, lng_ref, lnb_ref,
                      w2_ref, b2_ref, o_ref, ext_ref, sh_ref, y_ref, *, tm, rc, lc):
    t = pl.program_id(1)
    d = u_ref.shape[2]
    pad = CONV_WIDTH - 1

    @pl.when(t == 0)
    def _():
        ext_ref[0:HALO, :] = init_ref[0]

    @pl.when(t > 0)
    def _():
        ext_ref[0:HALO, :] = prev_ref[0]

    ext_ref[HALO:HALO + tm, :] = u_ref[0]

    for s in range(SUBLANES):
        rows = tm + SUBLANES * ((CONV_WIDTH - 1 - s) // SUBLANES)
        sh_ref[s, 0:rows, :] = ext_ref[pl.ds(HALO - pad + s, rows), :]
    for r0 in range(0, tm, rc):
        for c0 in range(0, d, lc):
            cols = slice(c0, c0 + lc)
            y = jnp.broadcast_to(bdw_ref[:, cols], (rc, lc))
            for w in range(CONV_WIDTH):
                a, s = divmod(w, SUBLANES)
                y = y + sh_ref[s, r0 + SUBLANES * a:r0 + SUBLANES * a + rc, cols] * wdw_ref[w:w + 1, cols]
            y_ref[r0:r0 + rc, cols] = y

    y = y_ref[...]
    mu = jnp.mean(y, axis=-1, keepdims=True)
    yc = y - mu
    var = jnp.mean(yc * yc, axis=-1, keepdims=True)
    yn = yc * lax.rsqrt(var + EPS) * lng_ref[...] + lnb_ref[...]
    act = (yn * (1.0 / (1.0 + jnp.exp(-yn)))).astype(BF16)
    o_ref[0] = x_ref[0] + _dot(act, w2_ref[...]) + b2_ref[...]


def _conv_rest(x3, u3, init, w_dw, b_dw, ln_g, ln_b, w_pw2, b_pw2):
    b, t, d = x3.shape
    tm = _row_tile(t, 256)
    rc = min(tm, 64)
    lc = min(d, 256)
    assert tm % HALO == 0 and tm % rc == 0 and d % lc == 0
    per = tm // HALO
    tile = pl.BlockSpec((1, tm, d), lambda bi, ti: (bi, ti, 0))
    prev = pl.BlockSpec((1, HALO, d), lambda bi, ti: (bi, jnp.maximum(ti * per - 1, 0), 0))
    first = pl.BlockSpec((1, HALO, d), lambda bi, ti: (bi, 0, 0))
    vec = _const_spec((1, d))
    wdw = jnp.pad(w_dw, ((0, HALO - CONV_WIDTH), (0, 0)))
    return pl.pallas_call(
        functools.partial(_conv_rest_kernel, tm=tm, rc=rc, lc=lc),
        out_shape=jax.ShapeDtypeStruct((b, t, d), F32),
        grid=(b, t // tm),
        in_specs=[tile, tile, prev, first, _const_spec((HALO, d)), vec, vec, vec,
                  _const_spec((d, d)), vec],
        out_specs=tile,
        scratch_shapes=[pltpu.VMEM((HALO + tm, d), F32),
                        pltpu.VMEM((SUBLANES, tm + HALO - SUBLANES, d), F32),
                        pltpu.VMEM((tm, d), F32)],
        compiler_params=_params("parallel", "arbitrary"),
        name="conv_rest",
    )(x3, u3, u3, init, wdw, b_dw.reshape(1, d), ln_g.reshape(1, d), ln_b.reshape(1, d),
      w_pw2.astype(BF16), b_pw2.reshape(1, d))


def _mixer_attn(x3, g, w_in, w_out, bias_tiles, cache=None):
    b, t, d = x3.shape
    x2 = x3.reshape(b * t, d)
    q, k, v, ki, kb, vb, kib, qi, wi = _attn_proj(x2, g, w_in)
    r3 = lambda a: a.reshape(b, t, a.shape[-1])
    kb, vb, kib = r3(kb), r3(vb), r3(kib)
    past = 0
    if cache is not None:
        ck, cv, cki = cache
        past = ck.shape[1]
        kb = jnp.concatenate([ck.reshape(b, past, -1).astype(BF16), kb], axis=1)
        vb = jnp.concatenate([cv.reshape(b, past, -1).astype(BF16), vb], axis=1)
        kib = jnp.concatenate([cki.astype(BF16), kib], axis=1)
    n_keys = past + t
    k_sel = min(TOPK_MAX, n_keys // 4)
    tpad = -(-t // KEY_BLOCK) * KEY_BLOCK
    lp = max(-(-n_keys // KEY_BLOCK) * KEY_BLOCK, past + tpad)
    padt = lambda a, n: a if a.shape[1] == n else jnp.pad(a, ((0, 0), (0, n - a.shape[1]), (0, 0)))
    o = _sparse_attn(padt(r3(q), tpad), padt(r3(qi), tpad), padt(r3(wi), tpad),
                     padt(kb, lp), padt(vb, lp), padt(kib, lp), bias_tiles,
                     q_off=past, n_keys=n_keys, k_sel=k_sel)[:, :t]
    x2 = _matmul_res(x2, o.reshape(b * t, d), w_out)
    return (x2.reshape(b, t, d), k.reshape(b, t, N_KV_HEADS, HEAD_DIM),
            v.reshape(b, t, N_KV_HEADS, HEAD_DIM), ki.reshape(b, t, IDX_DIM))


def _mixer_conv(x3, g, w_pw1, b_pw1, w_dw, b_dw, ln_g, ln_b, w_pw2, b_pw2, state=None):
    b, t, d = x3.shape
    pad = CONV_WIDTH - 1
    u3 = _conv_glu(x3.reshape(b * t, d), g, w_pw1, b_pw1).reshape(b, t, d)
    if state is None:
        init = jnp.zeros((b, HALO, d), F32)
        tail = u3[:, -pad:] if t >= pad else jnp.pad(u3, ((0, 0), (pad - t, 0), (0, 0)))
    else:
        init = jnp.pad(state.astype(F32), ((0, 0), (HALO - pad, 0), (0, 0)))
        tail = jnp.concatenate([state.astype(F32), u3], axis=1)[:, -pad:]
    x3 = _conv_rest(x3, u3, init, w_dw, b_dw, ln_g, ln_b, w_pw2, b_pw2)
    return x3, tail


def kernel(x_prompt, x_sample, cache_attn_k, cache_attn_v, cache_attn_kidx, state_conv, cache_mem_k, cache_mem_v, mem_prompt, rel_bias, g_mix, w_in_attn, w_out_attn, w_pw1, b_pw1, w_dw, b_dw, ln_g, ln_b, w_pw2, b_pw2, g_mem_q, g_mem_src, w_mem_q, w_mem_kv, w_mem_o, g_mlp, w_mlp1, w_mlp2, g_final):
    depth = g_mix.shape[0]
    bp, tp, d = x_prompt.shape
    bs, ts, _ = x_sample.shape
    n_mem = mem_prompt.shape[1]
    mem_hd = d // MEM_HEADS
    bias_tiles = _bias_tiles(rel_bias)
    xp, xs = x_prompt, x_sample
    kp_l, vp_l, kip_l, ks_l, vs_l, kis_l = [], [], [], [], [], []
    convp_l, convs_l, memk_l, memv_l = [], [], [], []
    for i in range(depth):
        j = i // 2
        if i % 2 == 0:
            xp, kp, vp, kip = _mixer_attn(xp, g_mix[i], w_in_attn[j], w_out_attn[j], bias_tiles)
            xs, ks, vs, kis = _mixer_attn(
                xs, g_mix[i], w_in_attn[j], w_out_attn[j], bias_tiles,
                cache=(cache_attn_k[j], cache_attn_v[j], cache_attn_kidx[j]))
            kp_l.append(kp); vp_l.append(vp); kip_l.append(kip)
            ks_l.append(ks); vs_l.append(vs); kis_l.append(kis)
        else:
            cw = (w_pw1[j], b_pw1[j], w_dw[j], b_dw[j], ln_g[j], ln_b[j], w_pw2[j], b_pw2[j])
            xp, cp = _mixer_conv(xp, g_mix[i], *cw)
            xs, cs = _mixer_conv(xs, g_mix[i], *cw, state=state_conv[j])
            convp_l.append(cp); convs_l.append(cs)
        mk, mv = _mem_kv(mem_prompt.reshape(bp * n_mem, d), g_mem_src[i], w_mem_kv[i])
        mk, mv = mk.reshape(bp, n_mem, d), mv.reshape(bp, n_mem, d)
        memk_l.append(mk.reshape(bp, n_mem, MEM_HEADS, mem_hd))
        memv_l.append(mv.reshape(bp, n_mem, MEM_HEADS, mem_hd))
        xp2 = _mem_attn(xp.reshape(bp * tp, d), g_mem_q[i], w_mem_q[i], mk, mv, w_mem_o[i], tp)
        xs2 = _mem_attn(xs.reshape(bs * ts, d), g_mem_q[i], w_mem_q[i],
                        cache_mem_k[i].reshape(bs, n_mem, d), cache_mem_v[i].reshape(bs, n_mem, d),
                        w_mem_o[i], ts)
        gf = g_final if i == depth - 1 else None
        xp = _mlp(xp2, g_mlp[i], w_mlp1[i], w_mlp2[i], gf).reshape(bp, tp, d)
        xs = _mlp(xs2, g_mlp[i], w_mlp1[i], w_mlp2[i], gf).reshape(bs, ts, d)
    return (xp, xs, jnp.stack(kp_l), jnp.stack(vp_l), jnp.stack(kip_l), jnp.stack(convp_l),
            jnp.stack(memk_l), jnp.stack(memv_l), jnp.stack(ks_l), jnp.stack(vs_l),
            jnp.stack(kis_l), jnp.stack(convs_l))
```

```python
import functools
import math

import jax
import jax.numpy as jnp
from jax import lax
from jax.experimental import pallas as pl
from jax.experimental.pallas import tpu as pltpu

CHUNK = 64
HEAD_DIM = 64
N_KV_HEADS = 4
IDX_HEADS = 8
IDX_DIM = 64
TOPK_MAX = 256
N_BUCKETS = 32
MAX_DISTANCE = 128
CONV_WIDTH = 31
MEM_HEADS = 4
EPS = 1e-6

LANES = 128
SUBLANES = 8
KEY_BLOCK = 128
HEADS_PER_TILE = 2
COUNT_UNROLL = 8
LOW_CHECK_EVERY = 4
V_AUG_ROWS = HEAD_DIM + 16
LOG2E = math.log2(math.e)
HALO = 32
VMEM_LIMIT = 56 * 1024 * 1024

NEG_BIG = -1e30
F32 = jnp.float32
BF16 = jnp.bfloat16

KEY_NEG_INF = -2139095041
KEY_POS_INF = 2139095040


def _const_spec(shape):
    nd = len(shape)
    return pl.BlockSpec(shape, lambda *_: (0,) * nd, pipeline_mode=pl.Buffered(1))


def _layer_spec(stacked, layer):
    nd = stacked.ndim
    return pl.BlockSpec((None,) + stacked.shape[1:], lambda *_: (layer,) + (0,) * (nd - 1),
                        pipeline_mode=pl.Buffered(1))


def _params(*sem):
    return pltpu.CompilerParams(dimension_semantics=sem, vmem_limit_bytes=VMEM_LIMIT)


def _rms(x, g):
    ms = jnp.mean(x * x, axis=-1, keepdims=True)
    return x * lax.rsqrt(ms + EPS) * g


def _dot(a, b):
    return jnp.dot(a, b, preferred_element_type=F32)


def _dot_nt(a, b):
    return lax.dot_general(a, b, (((1,), (1,)), ((), ())), preferred_element_type=F32)


def _row_tile(m, pref):
    t = min(m, pref)
    assert m % t == 0, (m, t)
    return t


def _bias_tiles_kernel(tab_ref, bt_ref, *, n_heads):
    nb = N_BUCKETS // 2
    max_exact = nb // 2
    c = lax.broadcasted_iota(jnp.int32, (KEY_BLOCK, KEY_BLOCK), 0)
    r = lax.broadcasted_iota(jnp.int32, (KEY_BLOCK, KEY_BLOCK), 1)
    for d in range(2):
        rel = c - r - d * KEY_BLOCK
        n = jnp.abs(rel)
        nf = jnp.maximum(n, 1).astype(F32)
        large = max_exact + (jnp.log(nf / max_exact) / math.log(MAX_DISTANCE / max_exact)
                             * (nb - max_exact)).astype(jnp.int32)
        large = jnp.minimum(large, nb - 1)
        bucket = jnp.where(rel > 0, nb, 0) + jnp.where(n < max_exact, n, large)
        for h in range(n_heads):
            acc = jnp.zeros((KEY_BLOCK, KEY_BLOCK), F32)
            for b in range(N_BUCKETS):
                acc = jnp.where(bucket == b, tab_ref[b, h], acc)
            g = h % HEADS_PER_TILE
            rows = slice((1 - d) * KEY_BLOCK, (2 - d) * KEY_BLOCK)
            bt_ref[h // HEADS_PER_TILE, rows, g * KEY_BLOCK:(g + 1) * KEY_BLOCK] = (
                (acc - tab_ref[nb - 1, h]) * LOG2E)


def _bias_tiles(rel_bias):
    n_heads = rel_bias.shape[1]
    assert n_heads % HEADS_PER_TILE == 0
    return pl.pallas_call(
        functools.partial(_bias_tiles_kernel, n_heads=n_heads),
        out_shape=jax.ShapeDtypeStruct(
            (n_heads // HEADS_PER_TILE, 2 * KEY_BLOCK, HEADS_PER_TILE * KEY_BLOCK), F32),
        in_specs=[pl.BlockSpec(memory_space=pltpu.SMEM)],
        out_specs=pl.BlockSpec(memory_space=pltpu.VMEM),
        name="bias_tiles",
    )(rel_bias)


def _attn_proj_kernel(x_ref, g_ref, w_ref, q_ref, k_ref, v_ref, ki_ref, kb_ref, vb_ref,
                      kib_ref, qi_ref, wi_ref, *, dq, dkv, dqi):
    h = _rms(x_ref[...], g_ref[...]).astype(BF16)
    o = 0
    q_ref[...] = (_dot(h, w_ref[:, o:o + dq]) * (HEAD_DIM ** -0.5)).astype(BF16)
    o += dq
    k = _dot(h, w_ref[:, o:o + dkv])
    k_ref[...] = k
    kb_ref[...] = k.astype(BF16)
    o += dkv
    v = _dot(h, w_ref[:, o:o + dkv])
    v_ref[...] = v
    vb_ref[...] = v.astype(BF16)
    o += dkv
    qi_ref[...] = (_dot(h, w_ref[:, o:o + dqi]) * (IDX_DIM ** -0.5)).astype(BF16)
    o += dqi
    ki = _dot(h, w_ref[:, o:o + LANES])[:, :IDX_DIM]
    ki_ref[...] = ki
    kib_ref[...] = ki.astype(BF16)
    o += LANES
    wi_ref[...] = _dot(h, w_ref[:, o:o + LANES]) * (IDX_HEADS ** -0.5)


def _attn_proj_weights(w_in):
    d = w_in.shape[1]
    base = d + 2 * N_KV_HEADS * HEAD_DIM + IDX_HEADS * IDX_DIM
    pad = lambda a: jnp.pad(a, ((0, 0), (0, 0), (0, LANES - a.shape[2])))
    return jnp.concatenate([w_in[:, :, :base], pad(w_in[:, :, base:base + IDX_DIM]),
                            pad(w_in[:, :, base + IDX_DIM:])], axis=2).astype(BF16)


def _attn_proj(x2, g, w, layer):
    m, d = x2.shape
    dq = d
    dkv = N_KV_HEADS * HEAD_DIM
    dqi = IDX_HEADS * IDX_DIM
    tm = _row_tile(m, 512)
    row = lambda n: pl.BlockSpec((tm, n), lambda i: (i, 0))
    outs = [(dq, BF16), (dkv, F32), (dkv, F32), (IDX_DIM, F32), (dkv, BF16), (dkv, BF16),
            (IDX_DIM, BF16), (dqi, BF16), (LANES, F32)]
    return pl.pallas_call(
        functools.partial(_attn_proj_kernel, dq=dq, dkv=dkv, dqi=dqi),
        out_shape=[jax.ShapeDtypeStruct((m, n), dt) for n, dt in outs],
        grid=(m // tm,),
        in_specs=[row(d), _const_spec((1, d)), _layer_spec(w, layer)],
        out_specs=[row(n) for n, _ in outs],
        compiler_params=_params("parallel"),
        name="attn_proj",
    )(x2, g.reshape(1, d), w)


def _sparse_attn_kernel(q_ref, qi_ref, wi_ref, k_ref, vt_ref, ki_ref, bt_ref, o_ref,
                        s_ref, sb_ref, dots_ref, qgt_ref, qit_ref, ot_ref, lg_ref, mb_ref, *state_refs,
                        q_off, n_keys, k_sel, group):
    tq = KEY_BLOCK
    hpt = HEADS_PER_TILE
    n_tiles = qgt_ref.shape[0]
    m_refs, acc_refs = state_refs[:n_tiles], state_refs[n_tiles:]
    i = pl.program_id(1)
    qs = q_off // KEY_BLOCK + i
    n_kb = qs + 1
    q_start = q_off + i * tq

    qt = q_ref[0].astype(F32).T * LOG2E
    for t in range(n_tiles):
        q2 = jnp.concatenate(
            [qt[(t * hpt + g) * HEAD_DIM:(t * hpt + g + 1) * HEAD_DIM, :] for g in range(hpt)], axis=1)
        q_hi = q2.astype(BF16)
        q_lo = (q2 - q_hi.astype(F32)).astype(BF16)
        qgt_ref[t] = jnp.concatenate([q_hi, q_lo], axis=0)
    qit = qi_ref[0].astype(F32).T
    qit_ref[...] = jnp.concatenate([qit[h * IDX_DIM:(h + 1) * IDX_DIM, :] for h in range(IDX_HEADS)],
                                   axis=1).astype(BF16)
    last_kv = k_ref.shape[1] // KEY_BLOCK - 1
    wit = wi_ref[0].T

    qpos = q_start + lax.broadcasted_iota(jnp.int32, (1, tq), 1)
    lim = jnp.minimum((qpos // CHUNK + 1) * CHUNK, n_keys)
    kidx = lax.broadcasted_iota(jnp.int32, (KEY_BLOCK, tq), 0)

    def key_to_f32(key):
        bits = key ^ ((key >> 31) & 0x7FFFFFFF)
        return lax.bitcast_convert_type(bits, F32)

    def put_scores(j, sc):
        s_ref[j] = sc
        bits = lax.bitcast_convert_type(jnp.where(sc == 0.0, 0.0, sc), jnp.int32)
        key = bits ^ ((bits >> 31) & 0x7FFFFFFF)
        down = jnp.maximum((key >> 16) << 16, KEY_NEG_INF)
        sb_ref[j] = key_to_f32(down).astype(BF16)

    def dots(j, slot):
        k0 = pl.multiple_of(jnp.minimum(j, last_kv) * KEY_BLOCK, KEY_BLOCK)
        dots_ref[slot] = _dot(ki_ref[0, pl.ds(k0, KEY_BLOCK), :], qit_ref[...])

    def finish(j, slot):
        acc = jnp.zeros((KEY_BLOCK, tq), F32)
        for h in range(IDX_HEADS):
            acc = acc + wit[h:h + 1, :] * jnp.maximum(dots_ref[slot, :, h * tq:(h + 1) * tq], 0.0)
        put_scores(j, jnp.where(kidx + j * KEY_BLOCK < lim, acc, -jnp.inf))

    def score_quad(quad, carry):
        j = 4 * quad
        dots(j + 2, 2)
        dots(j + 3, 3)
        finish(j, 0)
        finish(j + 1, 1)
        dots(j + 4, 0)
        dots(j + 5, 1)
        finish(j + 2, 2)
        finish(j + 3, 3)
        return carry

    n_pairs = (n_kb + 1) // 2
    dots(0, 0)
    dots(1, 1)
    lax.fori_loop(0, (n_kb + 3) // 4, score_quad, 0)
    for u in range(COUNT_UNROLL):
        put_scores(2 * n_pairs + u, jnp.full((KEY_BLOCK, tq), -jnp.inf, F32))
    n_count = (2 * n_pairs + COUNT_UNROLL - 1) // COUNT_UNROLL

    def count_ge(ref, cand):
        cb = jnp.broadcast_to(cand, (KEY_BLOCK, tq)).astype(ref.dtype)
        one, zero = jnp.ones((), ref.dtype), jnp.zeros((), ref.dtype)

        def body(p, c):
            for u in range(COUNT_UNROLL):
                c = c + jnp.where(ref[COUNT_UNROLL * p + u] >= cb, one, zero)
            return c
        c = lax.fori_loop(0, n_count, body, jnp.zeros((KEY_BLOCK, tq), ref.dtype))
        return jnp.sum(c.astype(F32), axis=0, keepdims=True)

    def bisect_step(count_at, carry):
        lo, hi, n_lo = carry
        mid = (lo + hi) >> 1
        n_mid = count_at(mid)
        ok = n_mid >= k_sel
        return jnp.where(ok, mid, lo), jnp.where(ok, hi, mid), jnp.where(ok, n_mid, n_lo)

    full = lambda v: jnp.full((1, tq), v, jnp.int32)
    count_hi = lambda v: count_ge(sb_ref, key_to_f32(v << 16))
    key_hi, _, n_sel = lax.fori_loop(
        0, 16, lambda _, c: bisect_step(count_hi, c),
        (full((KEY_NEG_INF + 1) >> 16), full((KEY_POS_INF >> 16) + 1), jnp.full((1, tq), jnp.inf, F32)))

    count_lo = lambda v: count_ge(s_ref, key_to_f32((key_hi << 16) + v))

    def low_steps(carry):
        step, c = carry[0], carry[1:]
        for _ in range(LOW_CHECK_EVERY):
            c = bisect_step(count_lo, c)
        return (step + LOW_CHECK_EVERY,) + c

    def low_unfinished(carry):
        step, n_lo = carry[0], carry[3]
        return (step < 16) & (jnp.max(jnp.abs(n_lo - k_sel)) > 0.0)

    _, key_lo, _, _ = lax.while_loop(low_unfinished, low_steps, (0, full(0), full(1 << 16), n_sel))
    thr = key_to_f32((key_hi << 16) + key_lo)

    def count(cand, strict):
        cb = jnp.broadcast_to(cand, (KEY_BLOCK, tq))
        hit = (lambda s: s > cb) if strict else (lambda s: s >= cb)

        def body(p, c):
            c = c + jnp.where(hit(s_ref[2 * p]), 1.0, 0.0)
            return c + jnp.where(hit(s_ref[2 * p + 1]), 1.0, 0.0)
        c = lax.fori_loop(0, n_pairs, body, jnp.zeros((KEY_BLOCK, tq), F32))
        return jnp.sum(c, axis=0, keepdims=True)

    surplus = count_ge(s_ref, thr) - k_sel

    @pl.when(jnp.max(surplus) > 0.0)
    def _():
        n_ties = k_sel - count(thr, True)
        row_i = lax.broadcasted_iota(jnp.int32, (KEY_BLOCK, KEY_BLOCK), 0)
        col_i = lax.broadcasted_iota(jnp.int32, (KEY_BLOCK, KEY_BLOCK), 1)
        lower = jnp.where(col_i < row_i, 1.0, 0.0).astype(BF16)
        ones = jnp.ones((KEY_BLOCK, KEY_BLOCK), BF16)

        def tie_body(j, seen):
            s = s_ref[j]
            eq = s == thr
            e = jnp.where(eq, 1.0, 0.0).astype(BF16)
            before = _dot(lower, e) + seen
            s_ref[j] = jnp.where(eq & (before >= n_ties), -jnp.inf, s)
            return seen + _dot(ones, e)

        lax.fori_loop(0, n_kb, tie_body, jnp.zeros((KEY_BLOCK, tq), F32))

    thr_sel = jnp.broadcast_to(jnp.maximum(thr, jnp.finfo(F32).min), (KEY_BLOCK, tq))

    for t in range(n_tiles):
        m_refs[t][...] = jnp.full(m_refs[t].shape, NEG_BIG, F32)
        acc_refs[t][...] = jnp.zeros(acc_refs[t].shape, F32)

    def far_blocks(db):
        out = []
        for j in (2 * db, 2 * db + 1):
            js = jnp.where(j < qs - 1, j, n_kb)
            out.append((js, jnp.minimum(js, last_kv)))
        return out

    near = [(jnp.where(qs >= 1, qs - 1, n_kb), jnp.maximum(qs - 1, 0)), (qs, qs)]

    def mask_of(blocks):
        m = jnp.concatenate([jnp.where(s_ref[js] >= thr_sel, 0.0, NEG_BIG) for js, _ in blocks], axis=0)
        return jnp.concatenate([m] * hpt, axis=1)

    def logits(blocks, mask, biased, slot, t):
        n = (t * hpt) // group
        kn = jnp.concatenate(
            [k_ref[0, pl.ds(pl.multiple_of(jk * KEY_BLOCK, KEY_BLOCK), KEY_BLOCK),
                   n * 2 * HEAD_DIM:(n + 1) * 2 * HEAD_DIM] for _, jk in blocks], axis=0)
        lg = _dot(kn, qgt_ref[t]) + mask
        if biased:
            lg = lg + bt_ref[t]
        lg_ref[slot, t] = lg
        mb_ref[slot, t] = jnp.max(lg, axis=0, keepdims=True)

    def softmax_update(blocks, slot, t):
        n = (t * hpt) // group
        m_old = m_refs[t][...]
        m_new = jnp.maximum(m_old, mb_ref[slot, t])
        alpha = jnp.exp2(m_old - m_new)
        p = jnp.exp2(lg_ref[slot, t] - m_new).astype(BF16)
        m_refs[t][...] = m_new
        vtn = jnp.concatenate([vt_ref[0, jk, n] for _, jk in blocks], axis=1)
        acc_refs[t][...] = alpha * acc_refs[t][...] + _dot(vtn, p)

    n_far = qs // 2
    mask_near = mask_of(near)
    for t in range(n_tiles):
        logits(near, mask_near, True, 0, t)

    def trip_body(trip, carry):
        d0 = 2 * trip
        blk1, blk2 = far_blocks(d0), far_blocks(d0 + 1)
        mask1, mask2 = mask_of(blk1), mask_of(blk2)
        far0 = far_blocks(d0 - 1)
        blk0 = [tuple(jnp.where(d0 == 0, a, b) for a, b in zip(near[h], far0[h])) for h in range(2)]
        for t in range(n_tiles):
            logits(blk1, mask1, False, 1, t)
            softmax_update(blk0, 0, t)
        for t in range(n_tiles):
            logits(blk2, mask2, False, 0, t)
            softmax_update(blk1, 1, t)
        return carry

    lax.fori_loop(0, (n_far + 2) // 2, trip_body, 0)

    for t in range(n_tiles):
        on = acc_refs[t][0:HEAD_DIM, :] / acc_refs[t][HEAD_DIM:HEAD_DIM + 1, :]
        for g in range(hpt):
            hh = t * hpt + g
            ot_ref[hh * HEAD_DIM:(hh + 1) * HEAD_DIM, :] = on[:, g * tq:(g + 1) * tq]
    o_ref[0] = ot_ref[...].T.astype(BF16)


def _sparse_attn(q, qi, wi, kb, vb, kib, bias_tiles, *, q_off, n_keys, k_sel):
    b, t, dq = q.shape
    lp = kb.shape[1]
    dkv = kb.shape[2]
    tq = KEY_BLOCK
    n_heads = dq // HEAD_DIM
    group = n_heads // N_KV_HEADS
    hpt = HEADS_PER_TILE
    n_tiles = n_heads // hpt
    assert group % hpt == 0
    nkb = lp // KEY_BLOCK
    assert t % tq == 0 and lp % KEY_BLOCK == 0 and q_off % KEY_BLOCK == 0 and q_off + t <= lp
    vt = jnp.transpose(vb.reshape(b, nkb, KEY_BLOCK, N_KV_HEADS, HEAD_DIM), (0, 1, 3, 4, 2))
    vt = jnp.concatenate(
        [vt, jnp.ones((b, nkb, N_KV_HEADS, 1, KEY_BLOCK), BF16),
         jnp.zeros((b, nkb, N_KV_HEADS, V_AUG_ROWS - HEAD_DIM - 1, KEY_BLOCK), BF16)], axis=3)
    k2 = jnp.concatenate([kb.reshape(b, lp, N_KV_HEADS, HEAD_DIM)] * 2, axis=-1).reshape(b, lp, 2 * dkv)
    qblk = lambda n: pl.BlockSpec((1, tq, n), lambda bi, i: (bi, i, 0))
    kblk = lambda n: pl.BlockSpec((1, lp, n), lambda bi, i: (bi, 0, 0))
    return pl.pallas_call(
        functools.partial(_sparse_attn_kernel, q_off=q_off, n_keys=n_keys, k_sel=k_sel, group=group),
        out_shape=jax.ShapeDtypeStruct((b, t, dq), BF16),
        grid=(b, t // tq),
        in_specs=[qblk(dq), qblk(qi.shape[2]), qblk(LANES), kblk(2 * dkv),
                  pl.BlockSpec((1, nkb, N_KV_HEADS, V_AUG_ROWS, KEY_BLOCK), lambda bi, i: (bi, 0, 0, 0, 0)),
                  kblk(kib.shape[2]), _const_spec(bias_tiles.shape)],
        out_specs=qblk(dq),
        scratch_shapes=[
            pltpu.VMEM((nkb + 1 + COUNT_UNROLL, KEY_BLOCK, tq), F32),
            pltpu.VMEM((nkb + 1 + COUNT_UNROLL, KEY_BLOCK, tq), BF16),
            pltpu.VMEM((4, KEY_BLOCK, IDX_HEADS * tq), F32),
            pltpu.VMEM((n_tiles, 2 * HEAD_DIM, hpt * tq), BF16),
            pltpu.VMEM((IDX_DIM, IDX_HEADS * tq), BF16),
            pltpu.VMEM((dq, tq), F32),
            pltpu.VMEM((2, n_tiles, 2 * KEY_BLOCK, hpt * tq), F32),
            pltpu.VMEM((2, n_tiles, 1, hpt * tq), F32),
        ] + [pltpu.VMEM((1, hpt * tq), F32)] * n_tiles
          + [pltpu.VMEM((V_AUG_ROWS, hpt * tq), F32)] * n_tiles,
        compiler_params=_params("parallel", "arbitrary"),
        name="sparse_attn",
    )(q, qi, wi, k2, vt, kib, bias_tiles)


def _matmul_res_kernel(x_ref, a_ref, w_ref, o_ref):
    o_ref[...] = x_ref[...] + _dot(a_ref[...], w_ref[...])


def _matmul_res(x2, a2, w, layer):
    m, d = x2.shape
    ka = a2.shape[1]
    tm = _row_tile(m, 512)
    return pl.pallas_call(
        _matmul_res_kernel,
        out_shape=jax.ShapeDtypeStruct((m, d), F32),
        grid=(m // tm,),
        in_specs=[pl.BlockSpec((tm, d), lambda i: (i, 0)), pl.BlockSpec((tm, ka), lambda i: (i, 0)),
                  _layer_spec(w, layer)],
        out_specs=pl.BlockSpec((tm, d), lambda i: (i, 0)),
        compiler_params=_params("parallel"),
        name="attn_out_proj",
    )(x2, a2, w)


def _mem_kv_kernel(x_ref, g_ref, w_ref, k_ref, v_ref, *, d):
    h = _rms(x_ref[...], g_ref[...]).astype(BF16)
    k_ref[...] = _dot(h, w_ref[:, :d])
    v_ref[...] = _dot(h, w_ref[:, d:])


def _mem_kv(mem2, g, w_kv, layer):
    m, d = mem2.shape
    tm = _row_tile(m, 512)
    row = pl.BlockSpec((tm, d), lambda i: (i, 0))
    return pl.pallas_call(
        functools.partial(_mem_kv_kernel, d=d),
        out_shape=[jax.ShapeDtypeStruct((m, d), F32)] * 2,
        grid=(m // tm,),
        in_specs=[row, _const_spec((1, d)), _layer_spec(w_kv, layer)],
        out_specs=[row, row],
        compiler_params=_params("parallel"),
        name="mem_kv",
    )(mem2, g.reshape(1, d), w_kv)


def _mem_attn_kernel(x_ref, g_ref, wq_ref, mk_ref, mv_ref, wo_ref, o_ref, *, hd):
    x = x_ref[...]
    h = _rms(x, g_ref[...]).astype(BF16)
    q = (_dot(h, wq_ref[...]) * (hd ** -0.5)).astype(BF16)
    heads = []
    for a in range(MEM_HEADS):
        cols = slice(a * hd, (a + 1) * hd)
        lg = _dot_nt(q[:, cols], mk_ref[0, :, cols].astype(BF16))
        p = jnp.exp(lg - jnp.max(lg, axis=1, keepdims=True))
        p = (p / jnp.sum(p, axis=1, keepdims=True)).astype(BF16)
        heads.append(_dot(p, mv_ref[0, :, cols].astype(BF16)).astype(BF16))
    o_ref[...] = x + _dot(jnp.concatenate(heads, axis=1), wo_ref[...])


def _mem_attn(x2, g, w_q, mk, mv, w_o, layer, rows_per_batch):
    m, d = x2.shape
    n_mem = mk.shape[1]
    tm = _row_tile(rows_per_batch, 512)
    per = rows_per_batch // tm
    row = pl.BlockSpec((tm, d), lambda i: (i, 0))
    mem = pl.BlockSpec((1, n_mem, d), lambda i: (i // per, 0, 0))
    return pl.pallas_call(
        functools.partial(_mem_attn_kernel, hd=d // MEM_HEADS),
        out_shape=jax.ShapeDtypeStruct((m, d), F32),
        grid=(m // tm,),
        in_specs=[row, _const_spec((1, d)), _layer_spec(w_q, layer), mem, mem, _layer_spec(w_o, layer)],
        out_specs=row,
        compiler_params=_params("parallel"),
        name="mem_attn",
    )(x2, g.reshape(1, d), w_q, mk, mv, w_o)


def _mlp_kernel(*refs, n_chunks, chunk, final):
    if final:
        x_ref, g_ref, w1_ref, w2_ref, gf_ref, o_ref = refs
    else:
        x_ref, g_ref, w1_ref, w2_ref, o_ref = refs
    x = x_ref[...]
    h = _rms(x, g_ref[...]).astype(BF16)
    acc = x
    for c in range(n_chunks):
        a = jnp.maximum(_dot(h, w1_ref[:, c * chunk:(c + 1) * chunk]), 0.0)
        acc = acc + _dot((a * a).astype(BF16), w2_ref[c * chunk:(c + 1) * chunk, :])
    if final:
        acc = _rms(acc, gf_ref[...])
    o_ref[...] = acc


def _mlp(x2, g, w1, w2, layer, g_final=None):
    m, d = x2.shape
    dff = w1.shape[2]
    chunk = min(dff, 1024)
    tm = _row_tile(m, 512)
    row = pl.BlockSpec((tm, d), lambda i: (i, 0))
    final = g_final is not None
    in_specs = [row, _const_spec((1, d)), _layer_spec(w1, layer), _layer_spec(w2, layer)]
    args = [x2, g.reshape(1, d), w1, w2]
    if final:
        in_specs.append(_const_spec((1, d)))
        args.append(g_final.reshape(1, d))
    return pl.pallas_call(
        functools.partial(_mlp_kernel, n_chunks=dff // chunk, chunk=chunk, final=final),
        out_shape=jax.ShapeDtypeStruct((m, d), F32),
        grid=(m // tm,),
        in_specs=in_specs,
        out_specs=row,
        compiler_params=_params("parallel"),
        name="mlp",
    )(*args)


def _conv_glu_kernel(x_ref, g_ref, w_ref, b_ref, u_ref, *, d):
    h = _rms(x_ref[...], g_ref[...]).astype(BF16)
    a = _dot(h, w_ref[:, :d]) + b_ref[:, :d]
    gate = _dot(h, w_ref[:, d:]) + b_ref[:, d:]
    u_ref[...] = a * (1.0 / (1.0 + jnp.exp(-gate)))


def _conv_glu(x2, g, w_pw1, layer, b_pw1):
    m, d = x2.shape
    tm = _row_tile(m, 512)
    row = pl.BlockSpec((tm, d), lambda i: (i, 0))
    return pl.pallas_call(
        functools.partial(_conv_glu_kernel, d=d),
        out_shape=jax.ShapeDtypeStruct((m, d), F32),
        grid=(m // tm,),
        in_specs=[row, _const_spec((1, d)), _layer_spec(w_pw1, layer), _const_spec((1, 2 * d))],
        out_specs=row,
        compiler_params=_params("parallel"),
        name="conv_glu",
    )(x2, g.reshape(1, d), w_pw1, b_pw1.reshape(1, 2 * d))


def _conv_rest_kernel(x_ref, u_ref, prev_ref, init_ref, wdw_ref, bdw_ref, lng_ref, lnb_ref,
                      w2_ref, b2_ref, o_ref, ext_ref, sh_ref, y_ref, *, tm, rc, lc):
    t = pl.program_id(1)
    d = u_ref.shape[2]
    pad = CONV_WIDTH - 1

    @pl.when(t == 0)
    def _():
        ext_ref[0:HALO, :] = init_ref[0]

    @pl.when(t > 0)
    def _():
        ext_ref[0:HALO, :] = prev_ref[0]

    ext_ref[HALO:HALO + tm, :] = u_ref[0]

    for s in range(SUBLANES):
        rows = tm + SUBLANES * ((CONV_WIDTH - 1 - s) // SUBLANES)
        sh_ref[s, 0:rows, :] = ext_ref[pl.ds(HALO - pad + s, rows), :]
    for r0 in range(0, tm, rc):
        for c0 in range(0, d, lc):
            cols = slice(c0, c0 + lc)
            y = jnp.broadcast_to(bdw_ref[:, cols], (rc, lc))
            for w in range(CONV_WIDTH):
                a, s = divmod(w, SUBLANES)
                y = y + sh_ref[s, r0 + SUBLANES * a:r0 + SUBLANES * a + rc, cols] * wdw_ref[w:w + 1, cols]
            y_ref[r0:r0 + rc, cols] = y

    y = y_ref[...]
    mu = jnp.mean(y, axis=-1, keepdims=True)
    yc = y - mu
    var = jnp.mean(yc * yc, axis=-1, keepdims=True)
    yn = yc * lax.rsqrt(var + EPS) * lng_ref[...] + lnb_ref[...]
    act = (yn * (1.0 / (1.0 + jnp.exp(-yn)))).astype(BF16)
    o_ref[0] = x_ref[0] + _dot(act, w2_ref[...]) + b2_ref[...]


def _conv_rest(x3, u3, init, w_dw, b_dw, ln_g, ln_b, w_pw2, layer, b_pw2):
    b, t, d = x3.shape
    tm = _row_tile(t, 256)
    rc = min(tm, 64)
    lc = min(d, 256)
    assert tm % HALO == 0 and tm % rc == 0 and d % lc == 0
    per = tm // HALO
    tile = pl.BlockSpec((1, tm, d), lambda bi, ti: (bi, ti, 0))
    prev = pl.BlockSpec((1, HALO, d), lambda bi, ti: (bi, jnp.maximum(ti * per - 1, 0), 0))
    first = pl.BlockSpec((1, HALO, d), lambda bi, ti: (bi, 0, 0))
    vec = _const_spec((1, d))
    wdw = jnp.pad(w_dw, ((0, HALO - CONV_WIDTH), (0, 0)))
    return pl.pallas_call(
        functools.partial(_conv_rest_kernel, tm=tm, rc=rc, lc=lc),
        out_shape=jax.ShapeDtypeStruct((b, t, d), F32),
        grid=(b, t // tm),
        in_specs=[tile, tile, prev, first, _const_spec((HALO, d)), vec, vec, vec,
                  _layer_spec(w_pw2, layer), vec],
        out_specs=tile,
        scratch_shapes=[pltpu.VMEM((HALO + tm, d), F32),
                        pltpu.VMEM((SUBLANES, tm + HALO - SUBLANES, d), F32),
                        pltpu.VMEM((tm, d), F32)],
        compiler_params=_params("parallel", "arbitrary"),
        name="conv_rest",
    )(x3, u3, u3, init, wdw, b_dw.reshape(1, d), ln_g.reshape(1, d), ln_b.reshape(1, d),
      w_pw2, b_pw2.reshape(1, d))


def _mixer_attn(x3, g, w_in, w_out, layer, bias_tiles, cache=None):
    b, t, d = x3.shape
    x2 = x3.reshape(b * t, d)
    q, k, v, ki, kb, vb, kib, qi, wi = _attn_proj(x2, g, w_in, layer)
    r3 = lambda a: a.reshape(b, t, a.shape[-1])
    kb, vb, kib = r3(kb), r3(vb), r3(kib)
    past = 0
    if cache is not None:
        ck, cv, cki = cache
        past = ck.shape[1]
        kb = jnp.concatenate([ck.reshape(b, past, -1).astype(BF16), kb], axis=1)
        vb = jnp.concatenate([cv.reshape(b, past, -1).astype(BF16), vb], axis=1)
        kib = jnp.concatenate([cki.astype(BF16), kib], axis=1)
    n_keys = past + t
    k_sel = min(TOPK_MAX, n_keys // 4)
    tpad = -(-t // KEY_BLOCK) * KEY_BLOCK
    lp = max(-(-n_keys // KEY_BLOCK) * KEY_BLOCK, past + tpad)
    padt = lambda a, n: a if a.shape[1] == n else jnp.pad(a, ((0, 0), (0, n - a.shape[1]), (0, 0)))
    o = _sparse_attn(padt(r3(q), tpad), padt(r3(qi), tpad), padt(r3(wi), tpad),
                     padt(kb, lp), padt(vb, lp), padt(kib, lp), bias_tiles,
                     q_off=past, n_keys=n_keys, k_sel=k_sel)[:, :t]
    x2 = _matmul_res(x2, o.reshape(b * t, d), w_out, layer)
    return (x2.reshape(b, t, d), k.reshape(b, t, N_KV_HEADS, HEAD_DIM),
            v.reshape(b, t, N_KV_HEADS, HEAD_DIM), ki.reshape(b, t, IDX_DIM))


def _mixer_conv(x3, g, w_pw1, b_pw1, w_dw, b_dw, ln_g, ln_b, w_pw2, b_pw2, layer, state=None):
    b, t, d = x3.shape
    pad = CONV_WIDTH - 1
    u3 = _conv_glu(x3.reshape(b * t, d), g, w_pw1, layer, b_pw1).reshape(b, t, d)
    if state is None:
        init = jnp.zeros((b, HALO, d), F32)
        tail = u3[:, -pad:] if t >= pad else jnp.pad(u3, ((0, 0), (pad - t, 0), (0, 0)))
    else:
        init = jnp.pad(state.astype(F32), ((0, 0), (HALO - pad, 0), (0, 0)))
        tail = jnp.concatenate([state.astype(F32), u3], axis=1)[:, -pad:]
    x3 = _conv_rest(x3, u3, init, w_dw, b_dw, ln_g, ln_b, w_pw2, layer, b_pw2)
    return x3, tail


def kernel(x_prompt, x_sample, cache_attn_k, cache_attn_v, cache_attn_kidx, state_conv, cache_mem_k, cache_mem_v, mem_prompt, rel_bias, g_mix, w_in_attn, w_out_attn, w_pw1, b_pw1, w_dw, b_dw, ln_g, ln_b, w_pw2, b_pw2, g_mem_q, g_mem_src, w_mem_q, w_mem_kv, w_mem_o, g_mlp, w_mlp1, w_mlp2, g_final):
    depth = g_mix.shape[0]
    bp, tp, d = x_prompt.shape
    bs, ts, _ = x_sample.shape
    n_mem = mem_prompt.shape[1]
    mem_hd = d // MEM_HEADS
    bias_tiles = _bias_tiles(rel_bias)
    w_in_b, w_out_b = _attn_proj_weights(w_in_attn), w_out_attn.astype(BF16)
    w_pw1_b, w_pw2_b = w_pw1.astype(BF16), w_pw2.astype(BF16)
    w_mq_b, w_mkv_b, w_mo_b = w_mem_q.astype(BF16), w_mem_kv.astype(BF16), w_mem_o.astype(BF16)
    w_mlp1_b, w_mlp2_b = w_mlp1.astype(BF16), w_mlp2.astype(BF16)
    xp, xs = x_prompt, x_sample
    kp_l, vp_l, kip_l, ks_l, vs_l, kis_l = [], [], [], [], [], []
    convp_l, convs_l, memk_l, memv_l = [], [], [], []
    for i in range(depth):
        j = i // 2
        if i % 2 == 0:
            xp, kp, vp, kip = _mixer_attn(xp, g_mix[i], w_in_b, w_out_b, j, bias_tiles)
            xs, ks, vs, kis = _mixer_attn(
                xs, g_mix[i], w_in_b, w_out_b, j, bias_tiles,
                cache=(cache_attn_k[j], cache_attn_v[j], cache_attn_kidx[j]))
            kp_l.append(kp); vp_l.append(vp); kip_l.append(kip)
            ks_l.append(ks); vs_l.append(vs); kis_l.append(kis)
        else:
            cw = (w_pw1_b, b_pw1[j], w_dw[j], b_dw[j], ln_g[j], ln_b[j], w_pw2_b, b_pw2[j], j)
            xp, cp = _mixer_conv(xp, g_mix[i], *cw)
            xs, cs = _mixer_conv(xs, g_mix[i], *cw, state=state_conv[j])
            convp_l.append(cp); convs_l.append(cs)
        mk, mv = _mem_kv(mem_prompt.reshape(bp * n_mem, d), g_mem_src[i], w_mkv_b, i)
        mk, mv = mk.reshape(bp, n_mem, d), mv.reshape(bp, n_mem, d)
        memk_l.append(mk.reshape(bp, n_mem, MEM_HEADS, mem_hd))
        memv_l.append(mv.reshape(bp, n_mem, MEM_HEADS, mem_hd))
        xp2 = _mem_attn(xp.reshape(bp * tp, d), g_mem_q[i], w_mq_b, mk, mv, w_mo_b, i, tp)
        xs2 = _mem_attn(xs.reshape(bs * ts, d), g_mem_q[i], w_mq_b,
                        cache_mem_k[i].reshape(bs, n_mem, d), cache_mem_v[i].reshape(bs, n_mem, d),
                        w_mo_b, i, ts)
        gf = g_final if i == depth - 1 else None
        xp = _mlp(xp2, g_mlp[i], w_mlp1_b, w_mlp2_b, i, gf).reshape(bp, tp, d)
        xs = _mlp(xs2, g_mlp[i], w_mlp1_b, w_mlp2_b, i, gf).reshape(bs, ts, d)
    return (xp, xs, jnp.stack(kp_l), jnp.stack(vp_l), jnp.stack(kip_l), jnp.stack(convp_l),
            jnp.stack(memk_l), jnp.stack(memv_l), jnp.stack(ks_l), jnp.stack(vs_l),
            jnp.stack(kis_l), jnp.stack(convs_l))
```

```python
import functools
import math

import jax
import jax.numpy as jnp
from jax import lax
from jax.experimental import pallas as pl
from jax.experimental.pallas import tpu as pltpu

CHUNK = 64
HEAD_DIM = 64
N_KV_HEADS = 4
IDX_HEADS = 8
IDX_DIM = 64
TOPK_MAX = 256
N_BUCKETS = 32
MAX_DISTANCE = 128
CONV_WIDTH = 31
MEM_HEADS = 4
EPS = 1e-6

LANES = 128
SUBLANES = 8
KEY_BLOCK = 128
HEADS_PER_TILE = 2
COUNT_UNROLL = 8
HI_BITS = 15
LO_BITS = 32 - HI_BITS
LOW_CHECK_EVERY = 4
V_AUG_ROWS = HEAD_DIM + 16
LOG2E = math.log2(math.e)
HALO = 32
VMEM_LIMIT = 56 * 1024 * 1024

NEG_BIG = -1e30
F32 = jnp.float32
BF16 = jnp.bfloat16

KEY_NEG_INF = -2139095041
KEY_POS_INF = 2139095040


def _const_spec(shape):
    nd = len(shape)
    return pl.BlockSpec(shape, lambda *_: (0,) * nd, pipeline_mode=pl.Buffered(1))


def _layer_spec(stacked, layer):
    nd = stacked.ndim
    return pl.BlockSpec((None,) + stacked.shape[1:], lambda *_: (layer,) + (0,) * (nd - 1),
                        pipeline_mode=pl.Buffered(1))


def _params(*sem):
    return pltpu.CompilerParams(dimension_semantics=sem, vmem_limit_bytes=VMEM_LIMIT)


def _rms(x, g):
    ms = jnp.mean(x * x, axis=-1, keepdims=True)
    return x * lax.rsqrt(ms + EPS) * g


def _dot(a, b):
    return jnp.dot(a, b, preferred_element_type=F32)


def _dot_nt(a, b):
    return lax.dot_general(a, b, (((1,), (1,)), ((), ())), preferred_element_type=F32)


def _row_tile(m, pref):
    t = min(m, pref)
    assert m % t == 0, (m, t)
    return t


def _bias_tiles_kernel(tab_ref, bt_ref, *, n_heads):
    nb = N_BUCKETS // 2
    max_exact = nb // 2
    c = lax.broadcasted_iota(jnp.int32, (KEY_BLOCK, KEY_BLOCK), 0)
    r = lax.broadcasted_iota(jnp.int32, (KEY_BLOCK, KEY_BLOCK), 1)
    for d in range(2):
        rel = c - r - d * KEY_BLOCK
        n = jnp.abs(rel)
        nf = jnp.maximum(n, 1).astype(F32)
        large = max_exact + (jnp.log(nf / max_exact) / math.log(MAX_DISTANCE / max_exact)
                             * (nb - max_exact)).astype(jnp.int32)
        large = jnp.minimum(large, nb - 1)
        bucket = jnp.where(rel > 0, nb, 0) + jnp.where(n < max_exact, n, large)
        for h in range(n_heads):
            acc = jnp.zeros((KEY_BLOCK, KEY_BLOCK), F32)
            for b in range(N_BUCKETS):
                acc = jnp.where(bucket == b, tab_ref[b, h], acc)
            g = h % HEADS_PER_TILE
            rows = slice((1 - d) * KEY_BLOCK, (2 - d) * KEY_BLOCK)
            bt_ref[h // HEADS_PER_TILE, rows, g * KEY_BLOCK:(g + 1) * KEY_BLOCK] = (
                (acc - tab_ref[nb - 1, h]) * LOG2E)


def _bias_tiles(rel_bias):
    n_heads = rel_bias.shape[1]
    assert n_heads % HEADS_PER_TILE == 0
    return pl.pallas_call(
        functools.partial(_bias_tiles_kernel, n_heads=n_heads),
        out_shape=jax.ShapeDtypeStruct(
            (n_heads // HEADS_PER_TILE, 2 * KEY_BLOCK, HEADS_PER_TILE * KEY_BLOCK), F32),
        in_specs=[pl.BlockSpec(memory_space=pltpu.SMEM)],
        out_specs=pl.BlockSpec(memory_space=pltpu.VMEM),
        name="bias_tiles",
    )(rel_bias)


def _attn_proj_kernel(x_ref, g_ref, w_ref, q_ref, k_ref, v_ref, ki_ref, kb_ref, vb_ref,
                      kib_ref, qi_ref, wi_ref, *, dq, dkv, dqi):
    h = _rms(x_ref[...], g_ref[...]).astype(BF16)
    o = 0
    q_ref[...] = (_dot(h, w_ref[:, o:o + dq]) * (HEAD_DIM ** -0.5)).astype(BF16)
    o += dq
    k = _dot(h, w_ref[:, o:o + dkv])
    k_ref[...] = k
    kb_ref[...] = k.astype(BF16)
    o += dkv
    v = _dot(h, w_ref[:, o:o + dkv])
    v_ref[...] = v
    vb_ref[...] = v.astype(BF16)
    o += dkv
    qi_ref[...] = (_dot(h, w_ref[:, o:o + dqi]) * (IDX_DIM ** -0.5)).astype(BF16)
    o += dqi
    ki = _dot(h, w_ref[:, o:o + LANES])[:, :IDX_DIM]
    ki_ref[...] = ki
    kib_ref[...] = ki.astype(BF16)
    o += LANES
    wi_ref[...] = _dot(h, w_ref[:, o:o + LANES]) * (IDX_HEADS ** -0.5)


def _attn_proj_weights(w_in):
    d = w_in.shape[1]
    base = d + 2 * N_KV_HEADS * HEAD_DIM + IDX_HEADS * IDX_DIM
    pad = lambda a: jnp.pad(a, ((0, 0), (0, 0), (0, LANES - a.shape[2])))
    return jnp.concatenate([w_in[:, :, :base], pad(w_in[:, :, base:base + IDX_DIM]),
                            pad(w_in[:, :, base + IDX_DIM:])], axis=2).astype(BF16)


def _attn_proj(x2, g, w, layer):
    m, d = x2.shape
    dq = d
    dkv = N_KV_HEADS * HEAD_DIM
    dqi = IDX_HEADS * IDX_DIM
    tm = _row_tile(m, 512)
    row = lambda n: pl.BlockSpec((tm, n), lambda i: (i, 0))
    outs = [(dq, BF16), (dkv, F32), (dkv, F32), (IDX_DIM, F32), (dkv, BF16), (dkv, BF16),
            (IDX_DIM, BF16), (dqi, BF16), (LANES, F32)]
    return pl.pallas_call(
        functools.partial(_attn_proj_kernel, dq=dq, dkv=dkv, dqi=dqi),
        out_shape=[jax.ShapeDtypeStruct((m, n), dt) for n, dt in outs],
        grid=(m // tm,),
        in_specs=[row(d), _const_spec((1, d)), _layer_spec(w, layer)],
        out_specs=[row(n) for n, _ in outs],
        compiler_params=_params("parallel"),
        name="attn_proj",
    )(x2, g.reshape(1, d), w)


def _sparse_attn_kernel(q_ref, qi_ref, wi_ref, k_ref, vt_ref, ki_ref, bt_ref, o_ref,
                        s_ref, w_ref, dots_ref, qgt_ref, qit_ref, ot_ref, lg_ref, mb_ref, *state_refs,
                        q_off, n_keys, k_sel, group):
    tq = KEY_BLOCK
    hpt = HEADS_PER_TILE
    n_tiles = qgt_ref.shape[0]
    m_refs, acc_refs = state_refs[:n_tiles], state_refs[n_tiles:]
    i = pl.program_id(1)
    qs = q_off // KEY_BLOCK + i
    n_kb = qs + 1
    q_start = q_off + i * tq

    qt = q_ref[0].astype(F32).T * LOG2E
    for t in range(n_tiles):
        q2 = jnp.concatenate(
            [qt[(t * hpt + g) * HEAD_DIM:(t * hpt + g + 1) * HEAD_DIM, :] for g in range(hpt)], axis=1)
        q_hi = q2.astype(BF16)
        q_lo = (q2 - q_hi.astype(F32)).astype(BF16)
        qgt_ref[t] = jnp.concatenate([q_hi, q_lo], axis=0)
    qit = qi_ref[0].astype(F32).T
    qit_ref[...] = jnp.concatenate([qit[h * IDX_DIM:(h + 1) * IDX_DIM, :] for h in range(IDX_HEADS)],
                                   axis=1).astype(BF16)
    last_kv = k_ref.shape[1] // KEY_BLOCK - 1
    wit = wi_ref[0].T

    qpos = q_start + lax.broadcasted_iota(jnp.int32, (1, tq), 1)
    lim = jnp.minimum((qpos // CHUNK + 1) * CHUNK, n_keys)
    kidx = lax.broadcasted_iota(jnp.int32, (KEY_BLOCK, tq), 0)

    def key_to_f32(key):
        bits = key ^ ((key >> 31) & 0x7FFFFFFF)
        return lax.bitcast_convert_type(bits, F32)

    def digit(sc):
        bits = lax.bitcast_convert_type(jnp.where(sc == 0.0, 0.0, sc), jnp.int32)
        key = bits ^ ((bits >> 31) & 0x7FFFFFFF)
        return (key >> LO_BITS) + (1 << (HI_BITS - 1))

    def dots(j, slot):
        k0 = pl.multiple_of(jnp.minimum(j, last_kv) * KEY_BLOCK, KEY_BLOCK)
        dots_ref[slot] = _dot(ki_ref[0, pl.ds(k0, KEY_BLOCK), :], qit_ref[...])

    def finish(j, slot):
        acc = jnp.zeros((KEY_BLOCK, tq), F32)
        for h in range(IDX_HEADS):
            acc = acc + wit[h:h + 1, :] * jnp.maximum(dots_ref[slot, :, h * tq:(h + 1) * tq], 0.0)
        sc = jnp.where(kidx + j * KEY_BLOCK < lim, acc, -jnp.inf)
        s_ref[j] = sc
        return digit(sc)

    def finish_pair(j, slot):
        w_ref[j // 2] = (finish(j, slot) << 16) | finish(j + 1, slot + 1)

    def score_quad(quad, carry):
        j = 4 * quad
        dots(j + 2, 2)
        dots(j + 3, 3)
        finish_pair(j, 0)
        dots(j + 4, 0)
        dots(j + 5, 1)
        finish_pair(j + 2, 2)
        return carry

    n_pairs = (n_kb + 1) // 2
    n_quads = (n_kb + 3) // 4
    dots(0, 0)
    dots(1, 1)
    lax.fori_loop(0, n_quads, score_quad, 0)
    for u in range(COUNT_UNROLL):
        s_ref[2 * n_pairs + u] = jnp.full((KEY_BLOCK, tq), -jnp.inf, F32)
    for u in range(COUNT_UNROLL // 2):
        w_ref[2 * n_quads + u] = jnp.zeros((KEY_BLOCK, tq), jnp.int32)
    n_count = (2 * n_pairs + COUNT_UNROLL - 1) // COUNT_UNROLL

    def count_digits(cand):
        comp = (1 << HI_BITS) - cand
        cw = jnp.broadcast_to((comp << 16) | comp, (KEY_BLOCK, tq))

        def body(p, c):
            for u in range(COUNT_UNROLL // 2):
                c = c + (((w_ref[(COUNT_UNROLL // 2) * p + u] + cw) >> HI_BITS) & 0x00010001)
            return c
        c = lax.fori_loop(0, n_count, body, jnp.zeros((KEY_BLOCK, tq), jnp.int32))
        return jnp.sum(((c & 0xFFFF) + (c >> 16)).astype(F32), axis=0, keepdims=True)

    def count_ge(cand):
        cb = jnp.broadcast_to(cand, (KEY_BLOCK, tq))

        def body(p, c):
            for u in range(COUNT_UNROLL):
                c = c + jnp.where(s_ref[COUNT_UNROLL * p + u] >= cb, 1.0, 0.0)
            return c
        c = lax.fori_loop(0, n_count, body, jnp.zeros((KEY_BLOCK, tq), F32))
        return jnp.sum(c, axis=0, keepdims=True)

    def bisect_step(count_at, carry):
        lo, hi, n_lo = carry
        mid = (lo + hi) >> 1
        n_mid = count_at(mid)
        ok = n_mid >= k_sel
        return jnp.where(ok, mid, lo), jnp.where(ok, hi, mid), jnp.where(ok, n_mid, n_lo)

    full = lambda v: jnp.full((1, tq), v, jnp.int32)
    half = 1 << (HI_BITS - 1)
    dig, _, n_sel = lax.fori_loop(
        0, HI_BITS, lambda _, c: bisect_step(count_digits, c),
        (full(((KEY_NEG_INF + 1) >> LO_BITS) + half), full((KEY_POS_INF >> LO_BITS) + half + 1),
         jnp.full((1, tq), jnp.inf, F32)))
    key_hi = (dig - half) << LO_BITS

    count_lo = lambda v: count_ge(key_to_f32(key_hi + v))

    def low_steps(carry):
        step, c = carry[0], carry[1:]
        for _ in range(LOW_CHECK_EVERY):
            c = bisect_step(count_lo, c)
        return (step + LOW_CHECK_EVERY,) + c

    def low_unfinished(carry):
        step, n_lo = carry[0], carry[3]
        return (step < LO_BITS) & (jnp.max(jnp.abs(n_lo - k_sel)) > 0.0)

    _, key_lo, _, _ = lax.while_loop(low_unfinished, low_steps, (0, full(0), full(1 << LO_BITS), n_sel))
    thr = key_to_f32(key_hi + key_lo)

    def count(cand, strict):
        cb = jnp.broadcast_to(cand, (KEY_BLOCK, tq))
        hit = (lambda s: s > cb) if strict else (lambda s: s >= cb)

        def body(p, c):
            c = c + jnp.where(hit(s_ref[2 * p]), 1.0, 0.0)
            return c + jnp.where(hit(s_ref[2 * p + 1]), 1.0, 0.0)
        c = lax.fori_loop(0, n_pairs, body, jnp.zeros((KEY_BLOCK, tq), F32))
        return jnp.sum(c, axis=0, keepdims=True)

    surplus = count_ge(thr) - k_sel

    @pl.when(jnp.max(surplus) > 0.0)
    def _():
        n_ties = k_sel - count(thr, True)
        row_i = lax.broadcasted_iota(jnp.int32, (KEY_BLOCK, KEY_BLOCK), 0)
        col_i = lax.broadcasted_iota(jnp.int32, (KEY_BLOCK, KEY_BLOCK), 1)
        lower = jnp.where(col_i < row_i, 1.0, 0.0).astype(BF16)
        ones = jnp.ones((KEY_BLOCK, KEY_BLOCK), BF16)

        def tie_body(j, seen):
            s = s_ref[j]
            eq = s == thr
            e = jnp.where(eq, 1.0, 0.0).astype(BF16)
            before = _dot(lower, e) + seen
            s_ref[j] = jnp.where(eq & (before >= n_ties), -jnp.inf, s)
            return seen + _dot(ones, e)

        lax.fori_loop(0, n_kb, tie_body, jnp.zeros((KEY_BLOCK, tq), F32))

    thr_sel = jnp.broadcast_to(jnp.maximum(thr, jnp.finfo(F32).min), (KEY_BLOCK, tq))

    for t in range(n_tiles):
        m_refs[t][...] = jnp.full(m_refs[t].shape, NEG_BIG, F32)
        acc_refs[t][...] = jnp.zeros(acc_refs[t].shape, F32)

    def far_blocks(db):
        out = []
        for j in (2 * db, 2 * db + 1):
            js = jnp.where(j < qs - 1, j, n_kb)
            out.append((js, jnp.minimum(js, last_kv)))
        return out

    near = [(jnp.where(qs >= 1, qs - 1, n_kb), jnp.maximum(qs - 1, 0)), (qs, qs)]

    def mask_of(blocks):
        m = jnp.concatenate([jnp.where(s_ref[js] >= thr_sel, 0.0, NEG_BIG) for js, _ in blocks], axis=0)
        return jnp.concatenate([m] * hpt, axis=1)

    def logits(blocks, mask, biased, slot, t):
        n = (t * hpt) // group
        kn = jnp.concatenate(
            [k_ref[0, pl.ds(pl.multiple_of(jk * KEY_BLOCK, KEY_BLOCK), KEY_BLOCK),
                   n * 2 * HEAD_DIM:(n + 1) * 2 * HEAD_DIM] for _, jk in blocks], axis=0)
        lg = _dot(kn, qgt_ref[t]) + mask
        if biased:
            lg = lg + bt_ref[t]
        lg_ref[slot, t] = lg
        mb_ref[slot, t] = jnp.max(lg, axis=0, keepdims=True)

    def softmax_update(blocks, slot, t):
        n = (t * hpt) // group
        m_old = m_refs[t][...]
        m_new = jnp.maximum(m_old, mb_ref[slot, t])
        alpha = jnp.exp2(m_old - m_new)
        p = jnp.exp2(lg_ref[slot, t] - m_new).astype(BF16)
        m_refs[t][...] = m_new
        vtn = jnp.concatenate([vt_ref[0, jk, n] for _, jk in blocks], axis=1)
        acc_refs[t][...] = alpha * acc_refs[t][...] + _dot(vtn, p)

    n_far = qs // 2
    mask_near = mask_of(near)
    for t in range(n_tiles):
        logits(near, mask_near, True, 0, t)

    def trip_body(trip, carry):
        d0 = 2 * trip
        blk1, blk2 = far_blocks(d0), far_blocks(d0 + 1)
        mask1, mask2 = mask_of(blk1), mask_of(blk2)
        far0 = far_blocks(d0 - 1)
        blk0 = [tuple(jnp.where(d0 == 0, a, b) for a, b in zip(near[h], far0[h])) for h in range(2)]
        for t in range(n_tiles):
            logits(blk1, mask1, False, 1, t)
            softmax_update(blk0, 0, t)
        for t in range(n_tiles):
            logits(blk2, mask2, False, 0, t)
            softmax_update(blk1, 1, t)
        return carry

    lax.fori_loop(0, (n_far + 2) // 2, trip_body, 0)

    for t in range(n_tiles):
        on = acc_refs[t][0:HEAD_DIM, :] / acc_refs[t][HEAD_DIM:HEAD_DIM + 1, :]
        for g in range(hpt):
            hh = t * hpt + g
            ot_ref[hh * HEAD_DIM:(hh + 1) * HEAD_DIM, :] = on[:, g * tq:(g + 1) * tq]
    o_ref[0] = ot_ref[...].T.astype(BF16)


def _sparse_attn(q, qi, wi, kb, vb, kib, bias_tiles, *, q_off, n_keys, k_sel):
    b, t, dq = q.shape
    lp = kb.shape[1]
    dkv = kb.shape[2]
    tq = KEY_BLOCK
    n_heads = dq // HEAD_DIM
    group = n_heads // N_KV_HEADS
    hpt = HEADS_PER_TILE
    n_tiles = n_heads // hpt
    assert group % hpt == 0
    nkb = lp // KEY_BLOCK
    assert t % tq == 0 and lp % KEY_BLOCK == 0 and q_off % KEY_BLOCK == 0 and q_off + t <= lp
    vt = jnp.transpose(vb.reshape(b, nkb, KEY_BLOCK, N_KV_HEADS, HEAD_DIM), (0, 1, 3, 4, 2))
    vt = jnp.concatenate(
        [vt, jnp.ones((b, nkb, N_KV_HEADS, 1, KEY_BLOCK), BF16),
         jnp.zeros((b, nkb, N_KV_HEADS, V_AUG_ROWS - HEAD_DIM - 1, KEY_BLOCK), BF16)], axis=3)
    k2 = jnp.concatenate([kb.reshape(b, lp, N_KV_HEADS, HEAD_DIM)] * 2, axis=-1).reshape(b, lp, 2 * dkv)
    qblk = lambda n: pl.BlockSpec((1, tq, n), lambda bi, i: (bi, i, 0))
    kblk = lambda n: pl.BlockSpec((1, lp, n), lambda bi, i: (bi, 0, 0))
    return pl.pallas_call(
        functools.partial(_sparse_attn_kernel, q_off=q_off, n_keys=n_keys, k_sel=k_sel, group=group),
        out_shape=jax.ShapeDtypeStruct((b, t, dq), BF16),
        grid=(b, t // tq),
        in_specs=[qblk(dq), qblk(qi.shape[2]), qblk(LANES), kblk(2 * dkv),
                  pl.BlockSpec((1, nkb, N_KV_HEADS, V_AUG_ROWS, KEY_BLOCK), lambda bi, i: (bi, 0, 0, 0, 0)),
                  kblk(kib.shape[2]), _const_spec(bias_tiles.shape)],
        out_specs=qblk(dq),
        scratch_shapes=[
            pltpu.VMEM((nkb + 1 + COUNT_UNROLL, KEY_BLOCK, tq), F32),
            pltpu.VMEM((nkb // 2 + COUNT_UNROLL, KEY_BLOCK, tq), jnp.int32),
            pltpu.VMEM((4, KEY_BLOCK, IDX_HEADS * tq), F32),
            pltpu.VMEM((n_tiles, 2 * HEAD_DIM, hpt * tq), BF16),
            pltpu.VMEM((IDX_DIM, IDX_HEADS * tq), BF16),
            pltpu.VMEM((dq, tq), F32),
            pltpu.VMEM((2, n_tiles, 2 * KEY_BLOCK, hpt * tq), F32),
            pltpu.VMEM((2, n_tiles, 1, hpt * tq), F32),
        ] + [pltpu.VMEM((1, hpt * tq), F32)] * n_tiles
          + [pltpu.VMEM((V_AUG_ROWS, hpt * tq), F32)] * n_tiles,
        compiler_params=_params("parallel", "arbitrary"),
        name="sparse_attn",
    )(q, qi, wi, k2, vt, kib, bias_tiles)


def _matmul_res_kernel(x_ref, a_ref, w_ref, o_ref):
    o_ref[...] = x_ref[...] + _dot(a_ref[...], w_ref[...])


def _matmul_res(x2, a2, w, layer):
    m, d = x2.shape
    ka = a2.shape[1]
    tm = _row_tile(m, 512)
    return pl.pallas_call(
        _matmul_res_kernel,
        out_shape=jax.ShapeDtypeStruct((m, d), F32),
        grid=(m // tm,),
        in_specs=[pl.BlockSpec((tm, d), lambda i: (i, 0)), pl.BlockSpec((tm, ka), lambda i: (i, 0)),
                  _layer_spec(w, layer)],
        out_specs=pl.BlockSpec((tm, d), lambda i: (i, 0)),
        compiler_params=_params("parallel"),
        name="attn_out_proj",
    )(x2, a2, w)


def _mem_kv_kernel(x_ref, g_ref, w_ref, k_ref, v_ref, *, d):
    h = _rms(x_ref[...], g_ref[...]).astype(BF16)
    k_ref[...] = _dot(h, w_ref[:, :d])
    v_ref[...] = _dot(h, w_ref[:, d:])


def _mem_kv(mem2, g, w_kv, layer):
    m, d = mem2.shape
    tm = _row_tile(m, 512)
    row = pl.BlockSpec((tm, d), lambda i: (i, 0))
    return pl.pallas_call(
        functools.partial(_mem_kv_kernel, d=d),
        out_shape=[jax.ShapeDtypeStruct((m, d), F32)] * 2,
        grid=(m // tm,),
        in_specs=[row, _const_spec((1, d)), _layer_spec(w_kv, layer)],
        out_specs=[row, row],
        compiler_params=_params("parallel"),
        name="mem_kv",
    )(mem2, g.reshape(1, d), w_kv)


def _mem_attn_kernel(x_ref, g_ref, wq_ref, mk_ref, mv_ref, wo_ref, o_ref, *, hd):
    x = x_ref[...]
    h = _rms(x, g_ref[...]).astype(BF16)
    q = (_dot(h, wq_ref[...]) * (hd ** -0.5)).astype(BF16)
    heads = []
    for a in range(MEM_HEADS):
        cols = slice(a * hd, (a + 1) * hd)
        lg = _dot_nt(q[:, cols], mk_ref[0, :, cols].astype(BF16))
        p = jnp.exp(lg - jnp.max(lg, axis=1, keepdims=True))
        p = (p / jnp.sum(p, axis=1, keepdims=True)).astype(BF16)
        heads.append(_dot(p, mv_ref[0, :, cols].astype(BF16)).astype(BF16))
    o_ref[...] = x + _dot(jnp.concatenate(heads, axis=1), wo_ref[...])


def _mem_attn(x2, g, w_q, mk, mv, w_o, layer, rows_per_batch):
    m, d = x2.shape
    n_mem = mk.shape[1]
    tm = _row_tile(rows_per_batch, 512)
    per = rows_per_batch // tm
    row = pl.BlockSpec((tm, d), lambda i: (i, 0))
    mem = pl.BlockSpec((1, n_mem, d), lambda i: (i // per, 0, 0))
    return pl.pallas_call(
        functools.partial(_mem_attn_kernel, hd=d // MEM_HEADS),
        out_shape=jax.ShapeDtypeStruct((m, d), F32),
        grid=(m // tm,),
        in_specs=[row, _const_spec((1, d)), _layer_spec(w_q, layer), mem, mem, _layer_spec(w_o, layer)],
        out_specs=row,
        compiler_params=_params("parallel"),
        name="mem_attn",
    )(x2, g.reshape(1, d), w_q, mk, mv, w_o)


def _mlp_kernel(*refs, n_chunks, chunk, final):
    if final:
        x_ref, g_ref, w1_ref, w2_ref, gf_ref, o_ref = refs
    else:
        x_ref, g_ref, w1_ref, w2_ref, o_ref = refs
    x = x_ref[...]
    h = _rms(x, g_ref[...]).astype(BF16)
    acc = x
    for c in range(n_chunks):
        a = jnp.maximum(_dot(h, w1_ref[:, c * chunk:(c + 1) * chunk]), 0.0)
        acc = acc + _dot((a * a).astype(BF16), w2_ref[c * chunk:(c + 1) * chunk, :])
    if final:
        acc = _rms(acc, gf_ref[...])
    o_ref[...] = acc


def _mlp(x2, g, w1, w2, layer, g_final=None):
    m, d = x2.shape
    dff = w1.shape[2]
    chunk = min(dff, 1024)
    tm = _row_tile(m, 512)
    row = pl.BlockSpec((tm, d), lambda i: (i, 0))
    final = g_final is not None
    in_specs = [row, _const_spec((1, d)), _layer_spec(w1, layer), _layer_spec(w2, layer)]
    args = [x2, g.reshape(1, d), w1, w2]
    if final:
        in_specs.append(_const_spec((1, d)))
        args.append(g_final.reshape(1, d))
    return pl.pallas_call(
        functools.partial(_mlp_kernel, n_chunks=dff // chunk, chunk=chunk, final=final),
        out_shape=jax.ShapeDtypeStruct((m, d), F32),
        grid=(m // tm,),
        in_specs=in_specs,
        out_specs=row,
        compiler_params=_params("parallel"),
        name="mlp",
    )(*args)


def _conv_glu_kernel(x_ref, g_ref, w_ref, b_ref, u_ref, *, d):
    h = _rms(x_ref[...], g_ref[...]).astype(BF16)
    a = _dot(h, w_ref[:, :d]) + b_ref[:, :d]
    gate = _dot(h, w_ref[:, d:]) + b_ref[:, d:]
    u_ref[...] = a * (1.0 / (1.0 + jnp.exp(-gate)))


def _conv_glu(x2, g, w_pw1, layer, b_pw1):
    m, d = x2.shape
    tm = _row_tile(m, 512)
    row = pl.BlockSpec((tm, d), lambda i: (i, 0))
    return pl.pallas_call(
        functools.partial(_conv_glu_kernel, d=d),
        out_shape=jax.ShapeDtypeStruct((m, d), F32),
        grid=(m // tm,),
        in_specs=[row, _const_spec((1, d)), _layer_spec(w_pw1, layer), _const_spec((1, 2 * d))],
        out_specs=row,
        compiler_params=_params("parallel"),
        name="conv_glu",
    )(x2, g.reshape(1, d), w_pw1, b_pw1.reshape(1, 2 * d))


def _conv_rest_kernel(x_ref, u_ref, prev_ref, init_ref, wdw_ref, bdw_ref, lng_ref, lnb_ref,
                      w2_ref, b2_ref, o_ref, ext_ref, sh_ref, y_ref, *, tm, rc, lc):
    t = pl.program_id(1)
    d = u_ref.shape[2]
    pad = CONV_WIDTH - 1

    @pl.when(t == 0)
    def _():
        ext_ref[0:HALO, :] = init_ref[0]

    @pl.when(t > 0)
    def _():
        ext_ref[0:HALO, :] = prev_ref[0]

    ext_ref[HALO:HALO + tm, :] = u_ref[0]

    for s in range(SUBLANES):
        rows = tm + SUBLANES * ((CONV_WIDTH - 1 - s) // SUBLANES)
        sh_ref[s, 0:rows, :] = ext_ref[pl.ds(HALO - pad + s, rows), :]
    for r0 in range(0, tm, rc):
        for c0 in range(0, d, lc):
            cols = slice(c0, c0 + lc)
            y = jnp.broadcast_to(bdw_ref[:, cols], (rc, lc))
            for w in range(CONV_WIDTH):
                a, s = divmod(w, SUBLANES)
                y = y + sh_ref[s, r0 + SUBLANES * a:r0 + SUBLANES * a + rc, cols] * wdw_ref[w:w + 1, cols]
            y_ref[r0:r0 + rc, cols] = y

    y = y_ref[...]
    mu = jnp.mean(y, axis=-1, keepdims=True)
    yc = y - mu
    var = jnp.mean(yc * yc, axis=-1, keepdims=True)
    yn = yc * lax.rsqrt(var + EPS) * lng_ref[...] + lnb_ref[...]
    act = (yn * (1.0 / (1.0 + jnp.exp(-yn)))).astype(BF16)
    o_ref[0] = x_ref[0] + _dot(act, w2_ref[...]) + b2_ref[...]


def _conv_rest(x3, u3, init, w_dw, b_dw, ln_g, ln_b, w_pw2, layer, b_pw2):
    b, t, d = x3.shape
    tm = _row_tile(t, 256)
    rc = min(tm, 64)
    lc = min(d, 256)
    assert tm % HALO == 0 and tm % rc == 0 and d % lc == 0
    per = tm // HALO
    tile = pl.BlockSpec((1, tm, d), lambda bi, ti: (bi, ti, 0))
    prev = pl.BlockSpec((1, HALO, d), lambda bi, ti: (bi, jnp.maximum(ti * per - 1, 0), 0))
    first = pl.BlockSpec((1, HALO, d), lambda bi, ti: (bi, 0, 0))
    vec = _const_spec((1, d))
    wdw = jnp.pad(w_dw, ((0, HALO - CONV_WIDTH), (0, 0)))
    return pl.pallas_call(
        functools.partial(_conv_rest_kernel, tm=tm, rc=rc, lc=lc),
        out_shape=jax.ShapeDtypeStruct((b, t, d), F32),
        grid=(b, t // tm),
        in_specs=[tile, tile, prev, first, _const_spec((HALO, d)), vec, vec, vec,
                  _layer_spec(w_pw2, layer), vec],
        out_specs=tile,
        scratch_shapes=[pltpu.VMEM((HALO + tm, d), F32),
                        pltpu.VMEM((SUBLANES, tm + HALO - SUBLANES, d), F32),
                        pltpu.VMEM((tm, d), F32)],
        compiler_params=_params("parallel", "arbitrary"),
        name="conv_rest",
    )(x3, u3, u3, init, wdw, b_dw.reshape(1, d), ln_g.reshape(1, d), ln_b.reshape(1, d),
      w_pw2, b_pw2.reshape(1, d))


def _mixer_attn(x3, g, w_in, w_out, layer, bias_tiles, cache=None):
    b, t, d = x3.shape
    x2 = x3.reshape(b * t, d)
    q, k, v, ki, kb, vb, kib, qi, wi = _attn_proj(x2, g, w_in, layer)
    r3 = lambda a: a.reshape(b, t, a.shape[-1])
    kb, vb, kib = r3(kb), r3(vb), r3(kib)
    past = 0
    if cache is not None:
        ck, cv, cki = cache
        past = ck.shape[1]
        kb = jnp.concatenate([ck.reshape(b, past, -1).astype(BF16), kb], axis=1)
        vb = jnp.concatenate([cv.reshape(b, past, -1).astype(BF16), vb], axis=1)
        kib = jnp.concatenate([cki.astype(BF16), kib], axis=1)
    n_keys = past + t
    k_sel = min(TOPK_MAX, n_keys // 4)
    tpad = -(-t // KEY_BLOCK) * KEY_BLOCK
    lp = max(-(-n_keys // KEY_BLOCK) * KEY_BLOCK, past + tpad)
    padt = lambda a, n: a if a.shape[1] == n else jnp.pad(a, ((0, 0), (0, n - a.shape[1]), (0, 0)))
    o = _sparse_attn(padt(r3(q), tpad), padt(r3(qi), tpad), padt(r3(wi), tpad),
                     padt(kb, lp), padt(vb, lp), padt(kib, lp), bias_tiles,
                     q_off=past, n_keys=n_keys, k_sel=k_sel)[:, :t]
    x2 = _matmul_res(x2, o.reshape(b * t, d), w_out, layer)
    return (x2.reshape(b, t, d), k.reshape(b, t, N_KV_HEADS, HEAD_DIM),
            v.reshape(b, t, N_KV_HEADS, HEAD_DIM), ki.reshape(b, t, IDX_DIM))


def _mixer_conv(x3, g, w_pw1, b_pw1, w_dw, b_dw, ln_g, ln_b, w_pw2, b_pw2, layer, state=None):
    b, t, d = x3.shape
    pad = CONV_WIDTH - 1
    u3 = _conv_glu(x3.reshape(b * t, d), g, w_pw1, layer, b_pw1).reshape(b, t, d)
    if state is None:
        init = jnp.zeros((b, HALO, d), F32)
        tail = u3[:, -pad:] if t >= pad else jnp.pad(u3, ((0, 0), (pad - t, 0), (0, 0)))
    else:
        init = jnp.pad(state.astype(F32), ((0, 0), (HALO - pad, 0), (0, 0)))
        tail = jnp.concatenate([state.astype(F32), u3], axis=1)[:, -pad:]
    x3 = _conv_rest(x3, u3, init, w_dw, b_dw, ln_g, ln_b, w_pw2, layer, b_pw2)
    return x3, tail


def kernel(x_prompt, x_sample, cache_attn_k, cache_attn_v, cache_attn_kidx, state_conv, cache_mem_k, cache_mem_v, mem_prompt, rel_bias, g_mix, w_in_attn, w_out_attn, w_pw1, b_pw1, w_dw, b_dw, ln_g, ln_b, w_pw2, b_pw2, g_mem_q, g_mem_src, w_mem_q, w_mem_kv, w_mem_o, g_mlp, w_mlp1, w_mlp2, g_final):
    depth = g_mix.shape[0]
    bp, tp, d = x_prompt.shape
    bs, ts, _ = x_sample.shape
    n_mem = mem_prompt.shape[1]
    mem_hd = d // MEM_HEADS
    bias_tiles = _bias_tiles(rel_bias)
    w_in_b, w_out_b = _attn_proj_weights(w_in_attn), w_out_attn.astype(BF16)
    w_pw1_b, w_pw2_b = w_pw1.astype(BF16), w_pw2.astype(BF16)
    w_mq_b, w_mkv_b, w_mo_b = w_mem_q.astype(BF16), w_mem_kv.astype(BF16), w_mem_o.astype(BF16)
    w_mlp1_b, w_mlp2_b = w_mlp1.astype(BF16), w_mlp2.astype(BF16)
    xp, xs = x_prompt, x_sample
    kp_l, vp_l, kip_l, ks_l, vs_l, kis_l = [], [], [], [], [], []
    convp_l, convs_l, memk_l, memv_l = [], [], [], []
    for i in range(depth):
        j = i // 2
        if i % 2 == 0:
            xp, kp, vp, kip = _mixer_attn(xp, g_mix[i], w_in_b, w_out_b, j, bias_tiles)
            xs, ks, vs, kis = _mixer_attn(
                xs, g_mix[i], w_in_b, w_out_b, j, bias_tiles,
                cache=(cache_attn_k[j], cache_attn_v[j], cache_attn_kidx[j]))
            kp_l.append(kp); vp_l.append(vp); kip_l.append(kip)
            ks_l.append(ks); vs_l.append(vs); kis_l.append(kis)
        else:
            cw = (w_pw1_b, b_pw1[j], w_dw[j], b_dw[j], ln_g[j], ln_b[j], w_pw2_b, b_pw2[j], j)
            xp, cp = _mixer_conv(xp, g_mix[i], *cw)
            xs, cs = _mixer_conv(xs, g_mix[i], *cw, state=state_conv[j])
            convp_l.append(cp); convs_l.append(cs)
        mk, mv = _mem_kv(mem_prompt.reshape(bp * n_mem, d), g_mem_src[i], w_mkv_b, i)
        mk, mv = mk.reshape(bp, n_mem, d), mv.reshape(bp, n_mem, d)
        memk_l.append(mk.reshape(bp, n_mem, MEM_HEADS, mem_hd))
        memv_l.append(mv.reshape(bp, n_mem, MEM_HEADS, mem_hd))
        xp2 = _mem_attn(xp.reshape(bp * tp, d), g_mem_q[i], w_mq_b, mk, mv, w_mo_b, i, tp)
        xs2 = _mem_attn(xs.reshape(bs * ts, d), g_mem_q[i], w_mq_b,
                        cache_mem_k[i].reshape(bs, n_mem, d), cache_mem_v[i].reshape(bs, n_mem, d),
                        w_mo_b, i, ts)
        gf = g_final if i == depth - 1 else None
        xp = _mlp(xp2, g_mlp[i], w_mlp1_b, w_mlp2_b, i, gf).reshape(bp, tp, d)
        xs = _mlp(xs2, g_mlp[i], w_mlp1_b, w_mlp2_b, i, gf).reshape(bs, ts, d)
    return (xp, xs, jnp.stack(kp_l), jnp.stack(vp_l), jnp.stack(kip_l), jnp.stack(convp_l),
            jnp.stack(memk_l), jnp.stack(memv_l), jnp.stack(ks_l), jnp.stack(vs_l),
            jnp.stack(kis_l), jnp.stack(convs_l))
```

```python
import functools
import math

import jax
import jax.numpy as jnp
from jax import lax
from jax.experimental import pallas as pl
from jax.experimental.pallas import tpu as pltpu

CHUNK = 64
HEAD_DIM = 64
N_KV_HEADS = 4
IDX_HEADS = 8
IDX_DIM = 64
TOPK_MAX = 256
N_BUCKETS = 32
MAX_DISTANCE = 128
CONV_WIDTH = 31
MEM_HEADS = 4
EPS = 1e-6

LANES = 128
SUBLANES = 8
KEY_BLOCK = 128
HEADS_PER_TILE = 2
COUNT_UNROLL = 8
HI_BITS = 15
LO_BITS = 32 - HI_BITS
LOW_CHECK_EVERY = 4
V_AUG_ROWS = HEAD_DIM + 16
LOG2E = math.log2(math.e)
HALO = 32
VMEM_LIMIT = 56 * 1024 * 1024

NEG_BIG = -1e30
F32 = jnp.float32
BF16 = jnp.bfloat16

KEY_NEG_INF = -2139095041
KEY_POS_INF = 2139095040


def _const_spec(shape):
    nd = len(shape)
    return pl.BlockSpec(shape, lambda *_: (0,) * nd, pipeline_mode=pl.Buffered(1))


def _layer_spec(stacked, layer):
    nd = stacked.ndim
    return pl.BlockSpec((None,) + stacked.shape[1:], lambda *_: (layer,) + (0,) * (nd - 1),
                        pipeline_mode=pl.Buffered(1))


def _params(*sem):
    return pltpu.CompilerParams(dimension_semantics=sem, vmem_limit_bytes=VMEM_LIMIT)


def _rms(x, g):
    ms = jnp.mean(x * x, axis=-1, keepdims=True)
    return x * lax.rsqrt(ms + EPS) * g


def _dot(a, b):
    return jnp.dot(a, b, preferred_element_type=F32)


def _dot_nt(a, b):
    return lax.dot_general(a, b, (((1,), (1,)), ((), ())), preferred_element_type=F32)


def _row_tile(m, pref):
    t = min(m, pref)
    assert m % t == 0, (m, t)
    return t


def _bias_tiles_kernel(tab_ref, bt_ref, *, n_heads):
    nb = N_BUCKETS // 2
    max_exact = nb // 2
    c = lax.broadcasted_iota(jnp.int32, (KEY_BLOCK, KEY_BLOCK), 0)
    r = lax.broadcasted_iota(jnp.int32, (KEY_BLOCK, KEY_BLOCK), 1)
    for d in range(2):
        rel = c - r - d * KEY_BLOCK
        n = jnp.abs(rel)
        nf = jnp.maximum(n, 1).astype(F32)
        large = max_exact + (jnp.log(nf / max_exact) / math.log(MAX_DISTANCE / max_exact)
                             * (nb - max_exact)).astype(jnp.int32)
        large = jnp.minimum(large, nb - 1)
        bucket = jnp.where(rel > 0, nb, 0) + jnp.where(n < max_exact, n, large)
        for h in range(n_heads):
            acc = jnp.zeros((KEY_BLOCK, KEY_BLOCK), F32)
            for b in range(N_BUCKETS):
                acc = jnp.where(bucket == b, tab_ref[b, h], acc)
            g = h % HEADS_PER_TILE
            rows = slice((1 - d) * KEY_BLOCK, (2 - d) * KEY_BLOCK)
            bt_ref[h // HEADS_PER_TILE, rows, g * KEY_BLOCK:(g + 1) * KEY_BLOCK] = (
                (acc - tab_ref[nb - 1, h]) * LOG2E)


def _bias_tiles(rel_bias):
    n_heads = rel_bias.shape[1]
    assert n_heads % HEADS_PER_TILE == 0
    return pl.pallas_call(
        functools.partial(_bias_tiles_kernel, n_heads=n_heads),
        out_shape=jax.ShapeDtypeStruct(
            (n_heads // HEADS_PER_TILE, 2 * KEY_BLOCK, HEADS_PER_TILE * KEY_BLOCK), F32),
        in_specs=[pl.BlockSpec(memory_space=pltpu.SMEM)],
        out_specs=pl.BlockSpec(memory_space=pltpu.VMEM),
        name="bias_tiles",
    )(rel_bias)


def _attn_proj_kernel(x_ref, g_ref, w_ref, q_ref, k_ref, v_ref, ki_ref, kb_ref, vb_ref,
                      kib_ref, qi_ref, wi_ref, *, dq, dkv, dqi):
    h = _rms(x_ref[...], g_ref[...]).astype(BF16)
    o = 0
    q_ref[...] = (_dot(h, w_ref[:, o:o + dq]) * (HEAD_DIM ** -0.5)).astype(BF16)
    o += dq
    k = _dot(h, w_ref[:, o:o + dkv])
    k_ref[...] = k
    kb_ref[...] = k.astype(BF16)
    o += dkv
    v = _dot(h, w_ref[:, o:o + dkv])
    v_ref[...] = v
    vb_ref[...] = v.astype(BF16)
    o += dkv
    qi_ref[...] = (_dot(h, w_ref[:, o:o + dqi]) * (IDX_DIM ** -0.5)).astype(BF16)
    o += dqi
    ki = _dot(h, w_ref[:, o:o + LANES])[:, :IDX_DIM]
    ki_ref[...] = ki
    kib_ref[...] = ki.astype(BF16)
    o += LANES
    wi_ref[...] = _dot(h, w_ref[:, o:o + LANES]) * (IDX_HEADS ** -0.5)


def _attn_proj_weights(w_in):
    d = w_in.shape[1]
    base = d + 2 * N_KV_HEADS * HEAD_DIM + IDX_HEADS * IDX_DIM
    pad = lambda a: jnp.pad(a, ((0, 0), (0, 0), (0, LANES - a.shape[2])))
    return jnp.concatenate([w_in[:, :, :base], pad(w_in[:, :, base:base + IDX_DIM]),
                            pad(w_in[:, :, base + IDX_DIM:])], axis=2).astype(BF16)


def _attn_proj(x2, g, w, layer):
    m, d = x2.shape
    dq = d
    dkv = N_KV_HEADS * HEAD_DIM
    dqi = IDX_HEADS * IDX_DIM
    tm = _row_tile(m, 512)
    row = lambda n: pl.BlockSpec((tm, n), lambda i: (i, 0))
    outs = [(dq, BF16), (dkv, F32), (dkv, F32), (IDX_DIM, F32), (dkv, BF16), (dkv, BF16),
            (IDX_DIM, BF16), (dqi, BF16), (LANES, F32)]
    return pl.pallas_call(
        functools.partial(_attn_proj_kernel, dq=dq, dkv=dkv, dqi=dqi),
        out_shape=[jax.ShapeDtypeStruct((m, n), dt) for n, dt in outs],
        grid=(m // tm,),
        in_specs=[row(d), _const_spec((1, d)), _layer_spec(w, layer)],
        out_specs=[row(n) for n, _ in outs],
        compiler_params=_params("parallel"),
        name="attn_proj",
    )(x2, g.reshape(1, d), w)


def _sparse_attn_kernel(q_ref, qi_ref, wi_ref, k_ref, vt_ref, ki_ref, bt_ref, o_ref,
                        s_ref, w_ref, dots_ref, qgt_ref, qit_ref, ot_ref, lg_ref, mb_ref, *state_refs,
                        q_off, n_keys, k_sel, group):
    tq = KEY_BLOCK
    hpt = HEADS_PER_TILE
    n_tiles = qgt_ref.shape[0]
    m_refs, acc_refs = state_refs[:n_tiles], state_refs[n_tiles:]
    i = pl.program_id(1)
    qs = q_off // KEY_BLOCK + i
    n_kb = qs + 1
    q_start = q_off + i * tq

    qt = q_ref[0].astype(F32).T * LOG2E
    for t in range(n_tiles):
        q2 = jnp.concatenate(
            [qt[(t * hpt + g) * HEAD_DIM:(t * hpt + g + 1) * HEAD_DIM, :] for g in range(hpt)], axis=1)
        q_hi = q2.astype(BF16)
        q_lo = (q2 - q_hi.astype(F32)).astype(BF16)
        qgt_ref[t] = jnp.concatenate([q_hi, q_lo], axis=0)
    qit = qi_ref[0].astype(F32).T
    qit_ref[...] = jnp.concatenate([qit[h * IDX_DIM:(h + 1) * IDX_DIM, :] for h in range(IDX_HEADS)],
                                   axis=1).astype(BF16)
    last_kv = k_ref.shape[1] // KEY_BLOCK - 1
    wit = wi_ref[0].T

    qpos = q_start + lax.broadcasted_iota(jnp.int32, (1, tq), 1)
    lim = jnp.minimum((qpos // CHUNK + 1) * CHUNK, n_keys)
    kidx = lax.broadcasted_iota(jnp.int32, (KEY_BLOCK, tq), 0)

    def key_to_f32(key):
        bits = key ^ ((key >> 31) & 0x7FFFFFFF)
        return lax.bitcast_convert_type(bits, F32)

    def digit(sc):
        bits = lax.bitcast_convert_type(jnp.where(sc == 0.0, 0.0, sc), jnp.int32)
        key = bits ^ ((bits >> 31) & 0x7FFFFFFF)
        return (key >> LO_BITS) + (1 << (HI_BITS - 1))

    def dots(j, slot):
        k0 = pl.multiple_of(jnp.minimum(j, last_kv) * KEY_BLOCK, KEY_BLOCK)
        dots_ref[slot] = _dot(ki_ref[0, pl.ds(k0, KEY_BLOCK), :], qit_ref[...])

    def finish(j, slot):
        acc = jnp.zeros((KEY_BLOCK, tq), F32)
        for h in range(IDX_HEADS):
            acc = acc + wit[h:h + 1, :] * jnp.maximum(dots_ref[slot, :, h * tq:(h + 1) * tq], 0.0)
        sc = jnp.where(kidx + j * KEY_BLOCK < lim, acc, -jnp.inf)
        s_ref[j] = sc
        return digit(sc)

    def finish_pair(j, slot):
        w_ref[j // 2] = (finish(j, slot) << 16) | finish(j + 1, slot + 1)

    def score_quad(quad, carry):
        j = 4 * quad
        dots(j + 2, 2)
        dots(j + 3, 3)
        finish_pair(j, 0)
        dots(j + 4, 0)
        dots(j + 5, 1)
        finish_pair(j + 2, 2)
        return carry

    n_pairs = (n_kb + 1) // 2
    n_quads = (n_kb + 3) // 4
    dots(0, 0)
    dots(1, 1)
    lax.fori_loop(0, n_quads, score_quad, 0)
    for u in range(COUNT_UNROLL):
        s_ref[2 * n_pairs + u] = jnp.full((KEY_BLOCK, tq), -jnp.inf, F32)
    for u in range(COUNT_UNROLL // 2):
        w_ref[2 * n_quads + u] = jnp.zeros((KEY_BLOCK, tq), jnp.int32)
    n_count = (2 * n_pairs + COUNT_UNROLL - 1) // COUNT_UNROLL

    def count_digits(cand):
        comp = (1 << HI_BITS) - cand
        cw = jnp.broadcast_to((comp << 16) | comp, (KEY_BLOCK, tq))

        def body(p, c):
            for u in range(COUNT_UNROLL // 2):
                c = c + (((w_ref[(COUNT_UNROLL // 2) * p + u] + cw) >> HI_BITS) & 0x00010001)
            return c
        c = lax.fori_loop(0, n_count, body, jnp.zeros((KEY_BLOCK, tq), jnp.int32))
        return jnp.sum(((c & 0xFFFF) + (c >> 16)).astype(F32), axis=0, keepdims=True)

    def count_ge(cand):
        cb = jnp.broadcast_to(cand, (KEY_BLOCK, tq))

        def body(p, c):
            for u in range(COUNT_UNROLL):
                c = c + jnp.where(s_ref[COUNT_UNROLL * p + u] >= cb, 1.0, 0.0)
            return c
        c = lax.fori_loop(0, n_count, body, jnp.zeros((KEY_BLOCK, tq), F32))
        return jnp.sum(c, axis=0, keepdims=True)

    def bisect_step(count_at, carry):
        lo, hi, n_lo = carry
        mid = (lo + hi) >> 1
        n_mid = count_at(mid)
        ok = n_mid >= k_sel
        return jnp.where(ok, mid, lo), jnp.where(ok, hi, mid), jnp.where(ok, n_mid, n_lo)

    full = lambda v: jnp.full((1, tq), v, jnp.int32)
    half = 1 << (HI_BITS - 1)
    dig, _, n_sel = lax.fori_loop(
        0, HI_BITS, lambda _, c: bisect_step(count_digits, c),
        (full(((KEY_NEG_INF + 1) >> LO_BITS) + half), full((KEY_POS_INF >> LO_BITS) + half + 1),
         jnp.full((1, tq), jnp.inf, F32)))
    key_hi = (dig - half) << LO_BITS

    count_lo = lambda v: count_ge(key_to_f32(key_hi + v))

    def low_steps(carry):
        step, c = carry[0], carry[1:]
        for _ in range(LOW_CHECK_EVERY):
            c = bisect_step(count_lo, c)
        return (step + LOW_CHECK_EVERY,) + c

    def low_unfinished(carry):
        step, n_lo = carry[0], carry[3]
        return (step < LO_BITS) & (jnp.max(jnp.abs(n_lo - k_sel)) > 0.0)

    _, key_lo, _, n_sel = lax.while_loop(low_unfinished, low_steps, (0, full(0), full(1 << LO_BITS), n_sel))
    thr = key_to_f32(key_hi + key_lo)

    def count(cand, strict):
        cb = jnp.broadcast_to(cand, (KEY_BLOCK, tq))
        hit = (lambda s: s > cb) if strict else (lambda s: s >= cb)

        def body(p, c):
            c = c + jnp.where(hit(s_ref[2 * p]), 1.0, 0.0)
            return c + jnp.where(hit(s_ref[2 * p + 1]), 1.0, 0.0)
        c = lax.fori_loop(0, n_pairs, body, jnp.zeros((KEY_BLOCK, tq), F32))
        return jnp.sum(c, axis=0, keepdims=True)

    surplus = jnp.where(n_sel < jnp.inf, n_sel - k_sel, 0.0)

    @pl.when(jnp.max(surplus) > 0.0)
    def _():
        n_ties = k_sel - count(thr, True)
        row_i = lax.broadcasted_iota(jnp.int32, (KEY_BLOCK, KEY_BLOCK), 0)
        col_i = lax.broadcasted_iota(jnp.int32, (KEY_BLOCK, KEY_BLOCK), 1)
        lower = jnp.where(col_i < row_i, 1.0, 0.0).astype(BF16)
        ones = jnp.ones((KEY_BLOCK, KEY_BLOCK), BF16)

        def tie_body(j, seen):
            s = s_ref[j]
            eq = s == thr
            e = jnp.where(eq, 1.0, 0.0).astype(BF16)
            before = _dot(lower, e) + seen
            s_ref[j] = jnp.where(eq & (before >= n_ties), -jnp.inf, s)
            return seen + _dot(ones, e)

        lax.fori_loop(0, n_kb, tie_body, jnp.zeros((KEY_BLOCK, tq), F32))

    thr_sel = jnp.broadcast_to(jnp.maximum(thr, jnp.finfo(F32).min), (KEY_BLOCK, tq))

    for t in range(n_tiles):
        m_refs[t][...] = jnp.full(m_refs[t].shape, NEG_BIG, F32)
        acc_refs[t][...] = jnp.zeros(acc_refs[t].shape, F32)

    def far_blocks(db):
        out = []
        for j in (2 * db, 2 * db + 1):
            js = jnp.where(j < qs - 1, j, n_kb)
            out.append((js, jnp.minimum(js, last_kv)))
        return out

    near = [(jnp.where(qs >= 1, qs - 1, n_kb), jnp.maximum(qs - 1, 0)), (qs, qs)]

    def mask_of(blocks):
        m = jnp.concatenate([jnp.where(s_ref[js] >= thr_sel, 0.0, NEG_BIG) for js, _ in blocks], axis=0)
        return jnp.concatenate([m] * hpt, axis=1)

    def logits(blocks, mask, biased, slot, t):
        n = (t * hpt) // group
        kn = jnp.concatenate(
            [k_ref[0, pl.ds(pl.multiple_of(jk * KEY_BLOCK, KEY_BLOCK), KEY_BLOCK),
                   n * 2 * HEAD_DIM:(n + 1) * 2 * HEAD_DIM] for _, jk in blocks], axis=0)
        lg = _dot(kn, qgt_ref[t]) + mask
        if biased:
            lg = lg + bt_ref[t]
        lg_ref[slot, t] = lg
        mb_ref[slot, t] = jnp.max(lg, axis=0, keepdims=True)

    def softmax_update(blocks, slot, t):
        n = (t * hpt) // group
        m_old = m_refs[t][...]
        m_new = jnp.maximum(m_old, mb_ref[slot, t])
        alpha = jnp.exp2(m_old - m_new)
        p = jnp.exp2(lg_ref[slot, t] - m_new).astype(BF16)
        m_refs[t][...] = m_new
        vtn = jnp.concatenate([vt_ref[0, jk, n] for _, jk in blocks], axis=1)
        acc_refs[t][...] = alpha * acc_refs[t][...] + _dot(vtn, p)

    n_far = qs // 2
    mask_near = mask_of(near)
    for t in range(n_tiles):
        logits(near, mask_near, True, 0, t)

    def trip_body(trip, carry):
        d0 = 2 * trip
        blk1, blk2 = far_blocks(d0), far_blocks(d0 + 1)
        mask1, mask2 = mask_of(blk1), mask_of(blk2)
        far0 = far_blocks(d0 - 1)
        blk0 = [tuple(jnp.where(d0 == 0, a, b) for a, b in zip(near[h], far0[h])) for h in range(2)]
        for t in range(n_tiles):
            logits(blk1, mask1, False, 1, t)
            softmax_update(blk0, 0, t)
        for t in range(n_tiles):
            logits(blk2, mask2, False, 0, t)
            softmax_update(blk1, 1, t)
        return carry

    lax.fori_loop(0, (n_far + 2) // 2, trip_body, 0)

    for t in range(n_tiles):
        on = acc_refs[t][0:HEAD_DIM, :] / acc_refs[t][HEAD_DIM:HEAD_DIM + 1, :]
        for g in range(hpt):
            hh = t * hpt + g
            ot_ref[hh * HEAD_DIM:(hh + 1) * HEAD_DIM, :] = on[:, g * tq:(g + 1) * tq]
    o_ref[0] = ot_ref[...].T.astype(BF16)


def _sparse_attn(q, qi, wi, kb, vb, kib, bias_tiles, *, q_off, n_keys, k_sel):
    b, t, dq = q.shape
    lp = kb.shape[1]
    dkv = kb.shape[2]
    tq = KEY_BLOCK
    n_heads = dq // HEAD_DIM
    group = n_heads // N_KV_HEADS
    hpt = HEADS_PER_TILE
    n_tiles = n_heads // hpt
    assert group % hpt == 0
    nkb = lp // KEY_BLOCK
    assert t % tq == 0 and lp % KEY_BLOCK == 0 and q_off % KEY_BLOCK == 0 and q_off + t <= lp
    vt = jnp.transpose(vb.reshape(b, nkb, KEY_BLOCK, N_KV_HEADS, HEAD_DIM), (0, 1, 3, 4, 2))
    vt = jnp.concatenate(
        [vt, jnp.ones((b, nkb, N_KV_HEADS, 1, KEY_BLOCK), BF16),
         jnp.zeros((b, nkb, N_KV_HEADS, V_AUG_ROWS - HEAD_DIM - 1, KEY_BLOCK), BF16)], axis=3)
    k2 = jnp.concatenate([kb.reshape(b, lp, N_KV_HEADS, HEAD_DIM)] * 2, axis=-1).reshape(b, lp, 2 * dkv)
    qblk = lambda n: pl.BlockSpec((1, tq, n), lambda bi, i: (bi, i, 0))
    kblk = lambda n: pl.BlockSpec((1, lp, n), lambda bi, i: (bi, 0, 0))
    return pl.pallas_call(
        functools.partial(_sparse_attn_kernel, q_off=q_off, n_keys=n_keys, k_sel=k_sel, group=group),
        out_shape=jax.ShapeDtypeStruct((b, t, dq), BF16),
        grid=(b, t // tq),
        in_specs=[qblk(dq), qblk(qi.shape[2]), qblk(LANES), kblk(2 * dkv),
                  pl.BlockSpec((1, nkb, N_KV_HEADS, V_AUG_ROWS, KEY_BLOCK), lambda bi, i: (bi, 0, 0, 0, 0)),
                  kblk(kib.shape[2]), _const_spec(bias_tiles.shape)],
        out_specs=qblk(dq),
        scratch_shapes=[
            pltpu.VMEM((nkb + 1 + COUNT_UNROLL, KEY_BLOCK, tq), F32),
            pltpu.VMEM((nkb // 2 + COUNT_UNROLL, KEY_BLOCK, tq), jnp.int32),
            pltpu.VMEM((4, KEY_BLOCK, IDX_HEADS * tq), F32),
            pltpu.VMEM((n_tiles, 2 * HEAD_DIM, hpt * tq), BF16),
            pltpu.VMEM((IDX_DIM, IDX_HEADS * tq), BF16),
            pltpu.VMEM((dq, tq), F32),
            pltpu.VMEM((2, n_tiles, 2 * KEY_BLOCK, hpt * tq), F32),
            pltpu.VMEM((2, n_tiles, 1, hpt * tq), F32),
        ] + [pltpu.VMEM((1, hpt * tq), F32)] * n_tiles
          + [pltpu.VMEM((V_AUG_ROWS, hpt * tq), F32)] * n_tiles,
        compiler_params=_params("parallel", "arbitrary"),
        name="sparse_attn",
    )(q, qi, wi, k2, vt, kib, bias_tiles)


def _matmul_res_kernel(x_ref, a_ref, w_ref, o_ref):
    o_ref[...] = x_ref[...] + _dot(a_ref[...], w_ref[...])


def _matmul_res(x2, a2, w, layer):
    m, d = x2.shape
    ka = a2.shape[1]
    tm = _row_tile(m, 512)
    return pl.pallas_call(
        _matmul_res_kernel,
        out_shape=jax.ShapeDtypeStruct((m, d), F32),
        grid=(m // tm,),
        in_specs=[pl.BlockSpec((tm, d), lambda i: (i, 0)), pl.BlockSpec((tm, ka), lambda i: (i, 0)),
                  _layer_spec(w, layer)],
        out_specs=pl.BlockSpec((tm, d), lambda i: (i, 0)),
        compiler_params=_params("parallel"),
        name="attn_out_proj",
    )(x2, a2, w)


def _mem_kv_kernel(x_ref, g_ref, w_ref, k_ref, v_ref, *, d):
    h = _rms(x_ref[...], g_ref[...]).astype(BF16)
    k_ref[...] = _dot(h, w_ref[:, :d])
    v_ref[...] = _dot(h, w_ref[:, d:])


def _mem_kv(mem2, g, w_kv, layer):
    m, d = mem2.shape
    tm = _row_tile(m, 512)
    row = pl.BlockSpec((tm, d), lambda i: (i, 0))
    return pl.pallas_call(
        functools.partial(_mem_kv_kernel, d=d),
        out_shape=[jax.ShapeDtypeStruct((m, d), F32)] * 2,
        grid=(m // tm,),
        in_specs=[row, _const_spec((1, d)), _layer_spec(w_kv, layer)],
        out_specs=[row, row],
        compiler_params=_params("parallel"),
        name="mem_kv",
    )(mem2, g.reshape(1, d), w_kv)


def _mem_attn_kernel(x_ref, g_ref, wq_ref, mk_ref, mv_ref, wo_ref, o_ref, *, hd):
    x = x_ref[...]
    h = _rms(x, g_ref[...]).astype(BF16)
    q = (_dot(h, wq_ref[...]) * (hd ** -0.5)).astype(BF16)
    heads = []
    for a in range(MEM_HEADS):
        cols = slice(a * hd, (a + 1) * hd)
        lg = _dot_nt(q[:, cols], mk_ref[0, :, cols].astype(BF16))
        p = jnp.exp(lg - jnp.max(lg, axis=1, keepdims=True))
        p = (p / jnp.sum(p, axis=1, keepdims=True)).astype(BF16)
        heads.append(_dot(p, mv_ref[0, :, cols].astype(BF16)).astype(BF16))
    o_ref[...] = x + _dot(jnp.concatenate(heads, axis=1), wo_ref[...])


def _mem_attn(x2, g, w_q, mk, mv, w_o, layer, rows_per_batch):
    m, d = x2.shape
    n_mem = mk.shape[1]
    tm = _row_tile(rows_per_batch, 512)
    per = rows_per_batch // tm
    row = pl.BlockSpec((tm, d), lambda i: (i, 0))
    mem = pl.BlockSpec((1, n_mem, d), lambda i: (i // per, 0, 0))
    return pl.pallas_call(
        functools.partial(_mem_attn_kernel, hd=d // MEM_HEADS),
        out_shape=jax.ShapeDtypeStruct((m, d), F32),
        grid=(m // tm,),
        in_specs=[row, _const_spec((1, d)), _layer_spec(w_q, layer), mem, mem, _layer_spec(w_o, layer)],
        out_specs=row,
        compiler_params=_params("parallel"),
        name="mem_attn",
    )(x2, g.reshape(1, d), w_q, mk, mv, w_o)


def _mlp_kernel(*refs, n_chunks, chunk, final):
    if final:
        x_ref, g_ref, w1_ref, w2_ref, gf_ref, o_ref = refs
    else:
        x_ref, g_ref, w1_ref, w2_ref, o_ref = refs
    x = x_ref[...]
    h = _rms(x, g_ref[...]).astype(BF16)
    acc = x
    for c in range(n_chunks):
        a = jnp.maximum(_dot(h, w1_ref[:, c * chunk:(c + 1) * chunk]), 0.0)
        acc = acc + _dot((a * a).astype(BF16), w2_ref[c * chunk:(c + 1) * chunk, :])
    if final:
        acc = _rms(acc, gf_ref[...])
    o_ref[...] = acc


def _mlp(x2, g, w1, w2, layer, g_final=None):
    m, d = x2.shape
    dff = w1.shape[2]
    chunk = min(dff, 1024)
    tm = _row_tile(m, 512)
    row = pl.BlockSpec((tm, d), lambda i: (i, 0))
    final = g_final is not None
    in_specs = [row, _const_spec((1, d)), _layer_spec(w1, layer), _layer_spec(w2, layer)]
    args = [x2, g.reshape(1, d), w1, w2]
    if final:
        in_specs.append(_const_spec((1, d)))
        args.append(g_final.reshape(1, d))
    return pl.pallas_call(
        functools.partial(_mlp_kernel, n_chunks=dff // chunk, chunk=chunk, final=final),
        out_shape=jax.ShapeDtypeStruct((m, d), F32),
        grid=(m // tm,),
        in_specs=in_specs,
        out_specs=row,
        compiler_params=_params("parallel"),
        name="mlp",
    )(*args)


def _conv_glu_kernel(x_ref, g_ref, w_ref, b_ref, u_ref, *, d):
    h = _rms(x_ref[...], g_ref[...]).astype(BF16)
    a = _dot(h, w_ref[:, :d]) + b_ref[:, :d]
    gate = _dot(h, w_ref[:, d:]) + b_ref[:, d:]
    u_ref[...] = a * (1.0 / (1.0 + jnp.exp(-gate)))


def _conv_glu(x2, g, w_pw1, layer, b_pw1):
    m, d = x2.shape
    tm = _row_tile(m, 512)
    row = pl.BlockSpec((tm, d), lambda i: (i, 0))
    return pl.pallas_call(
        functools.partial(_conv_glu_kernel, d=d),
        out_shape=jax.ShapeDtypeStruct((m, d), F32),
        grid=(m // tm,),
        in_specs=[row, _const_spec((1, d)), _layer_spec(w_pw1, layer), _const_spec((1, 2 * d))],
        out_specs=row,
        compiler_params=_params("parallel"),
        name="conv_glu",
    )(x2, g.reshape(1, d), w_pw1, b_pw1.reshape(1, 2 * d))


def _conv_rest_kernel(x_ref, u_ref, prev_ref, init_ref, wdw_ref, bdw_ref, lng_ref, lnb_ref,
                      w2_ref, b2_ref, o_ref, ext_ref, sh_ref, y_ref, *, tm, rc, lc):
    t = pl.program_id(1)
    d = u_ref.shape[2]
    pad = CONV_WIDTH - 1

    @pl.when(t == 0)
    def _():
        ext_ref[0:HALO, :] = init_ref[0]

    @pl.when(t > 0)
    def _():
        ext_ref[0:HALO, :] = prev_ref[0]

    ext_ref[HALO:HALO + tm, :] = u_ref[0]

    for s in range(SUBLANES):
        rows = tm + SUBLANES * ((CONV_WIDTH - 1 - s) // SUBLANES)
        sh_ref[s, 0:rows, :] = ext_ref[pl.ds(HALO - pad + s, rows), :]
    for r0 in range(0, tm, rc):
        for c0 in range(0, d, lc):
            cols = slice(c0, c0 + lc)
            y = jnp.broadcast_to(bdw_ref[:, cols], (rc, lc))
            for w in range(CONV_WIDTH):
                a, s = divmod(w, SUBLANES)
                y = y + sh_ref[s, r0 + SUBLANES * a:r0 + SUBLANES * a + rc, cols] * wdw_ref[w:w + 1, cols]
            y_ref[r0:r0 + rc, cols] = y

    y = y_ref[...]
    mu = jnp.mean(y, axis=-1, keepdims=True)
    yc = y - mu
    var = jnp.mean(yc * yc, axis=-1, keepdims=True)
    yn = yc * lax.rsqrt(var + EPS) * lng_ref[...] + lnb_ref[...]
    act = (yn * (1.0 / (1.0 + jnp.exp(-yn)))).astype(BF16)
    o_ref[0] = x_ref[0] + _dot(act, w2_ref[...]) + b2_ref[...]


def _conv_rest(x3, u3, init, w_dw, b_dw, ln_g, ln_b, w_pw2, layer, b_pw2):
    b, t, d = x3.shape
    tm = _row_tile(t, 256)
    rc = min(tm, 64)
    lc = min(d, 256)
    assert tm % HALO == 0 and tm % rc == 0 and d % lc == 0
    per = tm // HALO
    tile = pl.BlockSpec((1, tm, d), lambda bi, ti: (bi, ti, 0))
    prev = pl.BlockSpec((1, HALO, d), lambda bi, ti: (bi, jnp.maximum(ti * per - 1, 0), 0))
    first = pl.BlockSpec((1, HALO, d), lambda bi, ti: (bi, 0, 0))
    vec = _const_spec((1, d))
    wdw = jnp.pad(w_dw, ((0, HALO - CONV_WIDTH), (0, 0)))
    return pl.pallas_call(
        functools.partial(_conv_rest_kernel, tm=tm, rc=rc, lc=lc),
        out_shape=jax.ShapeDtypeStruct((b, t, d), F32),
        grid=(b, t // tm),
        in_specs=[tile, tile, prev, first, _const_spec((HALO, d)), vec, vec, vec,
                  _layer_spec(w_pw2, layer), vec],
        out_specs=tile,
        scratch_shapes=[pltpu.VMEM((HALO + tm, d), F32),
                        pltpu.VMEM((SUBLANES, tm + HALO - SUBLANES, d), F32),
                        pltpu.VMEM((tm, d), F32)],
        compiler_params=_params("parallel", "arbitrary"),
        name="conv_rest",
    )(x3, u3, u3, init, wdw, b_dw.reshape(1, d), ln_g.reshape(1, d), ln_b.reshape(1, d),
      w_pw2, b_pw2.reshape(1, d))


def _mixer_attn(x3, g, w_in, w_out, layer, bias_tiles, cache=None):
    b, t, d = x3.shape
    x2 = x3.reshape(b * t, d)
    q, k, v, ki, kb, vb, kib, qi, wi = _attn_proj(x2, g, w_in, layer)
    r3 = lambda a: a.reshape(b, t, a.shape[-1])
    kb, vb, kib = r3(kb), r3(vb), r3(kib)
    past = 0
    if cache is not None:
        ck, cv, cki = cache
        past = ck.shape[1]
        kb = jnp.concatenate([ck.reshape(b, past, -1).astype(BF16), kb], axis=1)
        vb = jnp.concatenate([cv.reshape(b, past, -1).astype(BF16), vb], axis=1)
        kib = jnp.concatenate([cki.astype(BF16), kib], axis=1)
    n_keys = past + t
    k_sel = min(TOPK_MAX, n_keys // 4)
    tpad = -(-t // KEY_BLOCK) * KEY_BLOCK
    lp = max(-(-n_keys // KEY_BLOCK) * KEY_BLOCK, past + tpad)
    padt = lambda a, n: a if a.shape[1] == n else jnp.pad(a, ((0, 0), (0, n - a.shape[1]), (0, 0)))
    o = _sparse_attn(padt(r3(q), tpad), padt(r3(qi), tpad), padt(r3(wi), tpad),
                     padt(kb, lp), padt(vb, lp), padt(kib, lp), bias_tiles,
                     q_off=past, n_keys=n_keys, k_sel=k_sel)[:, :t]
    x2 = _matmul_res(x2, o.reshape(b * t, d), w_out, layer)
    return (x2.reshape(b, t, d), k.reshape(b, t, N_KV_HEADS, HEAD_DIM),
            v.reshape(b, t, N_KV_HEADS, HEAD_DIM), ki.reshape(b, t, IDX_DIM))


def _mixer_conv(x3, g, w_pw1, b_pw1, w_dw, b_dw, ln_g, ln_b, w_pw2, b_pw2, layer, state=None):
    b, t, d = x3.shape
    pad = CONV_WIDTH - 1
    u3 = _conv_glu(x3.reshape(b * t, d), g, w_pw1, layer, b_pw1).reshape(b, t, d)
    if state is None:
        init = jnp.zeros((b, HALO, d), F32)
        tail = u3[:, -pad:] if t >= pad else jnp.pad(u3, ((0, 0), (pad - t, 0), (0, 0)))
    else:
        init = jnp.pad(state.astype(F32), ((0, 0), (HALO - pad, 0), (0, 0)))
        tail = jnp.concatenate([state.astype(F32), u3], axis=1)[:, -pad:]
    x3 = _conv_rest(x3, u3, init, w_dw, b_dw, ln_g, ln_b, w_pw2, layer, b_pw2)
    return x3, tail


def kernel(x_prompt, x_sample, cache_attn_k, cache_attn_v, cache_attn_kidx, state_conv, cache_mem_k, cache_mem_v, mem_prompt, rel_bias, g_mix, w_in_attn, w_out_attn, w_pw1, b_pw1, w_dw, b_dw, ln_g, ln_b, w_pw2, b_pw2, g_mem_q, g_mem_src, w_mem_q, w_mem_kv, w_mem_o, g_mlp, w_mlp1, w_mlp2, g_final):
    depth = g_mix.shape[0]
    bp, tp, d = x_prompt.shape
    bs, ts, _ = x_sample.shape
    n_mem = mem_prompt.shape[1]
    mem_hd = d // MEM_HEADS
    bias_tiles = _bias_tiles(rel_bias)
    w_in_b, w_out_b = _attn_proj_weights(w_in_attn), w_out_attn.astype(BF16)
    w_pw1_b, w_pw2_b = w_pw1.astype(BF16), w_pw2.astype(BF16)
    w_mq_b, w_mkv_b, w_mo_b = w_mem_q.astype(BF16), w_mem_kv.astype(BF16), w_mem_o.astype(BF16)
    w_mlp1_b, w_mlp2_b = w_mlp1.astype(BF16), w_mlp2.astype(BF16)
    xp, xs = x_prompt, x_sample
    kp_l, vp_l, kip_l, ks_l, vs_l, kis_l = [], [], [], [], [], []
    convp_l, convs_l, memk_l, memv_l = [], [], [], []
    for i in range(depth):
        j = i // 2
        if i % 2 == 0:
            xp, kp, vp, kip = _mixer_attn(xp, g_mix[i], w_in_b, w_out_b, j, bias_tiles)
            xs, ks, vs, kis = _mixer_attn(
                xs, g_mix[i], w_in_b, w_out_b, j, bias_tiles,
                cache=(cache_attn_k[j], cache_attn_v[j], cache_attn_kidx[j]))
            kp_l.append(kp); vp_l.append(vp); kip_l.append(kip)
            ks_l.append(ks); vs_l.append(vs); kis_l.append(kis)
        else:
            cw = (w_pw1_b, b_pw1[j], w_dw[j], b_dw[j], ln_g[j], ln_b[j], w_pw2_b, b_pw2[j], j)
            xp, cp = _mixer_conv(xp, g_mix[i], *cw)
            xs, cs = _mixer_conv(xs, g_mix[i], *cw, state=state_conv[j])
            convp_l.append(cp); convs_l.append(cs)
        mk, mv = _mem_kv(mem_prompt.reshape(bp * n_mem, d), g_mem_src[i], w_mkv_b, i)
        mk, mv = mk.reshape(bp, n_mem, d), mv.reshape(bp, n_mem, d)
        memk_l.append(mk.reshape(bp, n_mem, MEM_HEADS, mem_hd))
        memv_l.append(mv.reshape(bp, n_mem, MEM_HEADS, mem_hd))
        xp2 = _mem_attn(xp.reshape(bp * tp, d), g_mem_q[i], w_mq_b, mk, mv, w_mo_b, i, tp)
        xs2 = _mem_attn(xs.reshape(bs * ts, d), g_mem_q[i], w_mq_b,
                        cache_mem_k[i].reshape(bs, n_mem, d), cache_mem_v[i].reshape(bs, n_mem, d),
                        w_mo_b, i, ts)
        gf = g_final if i == depth - 1 else None
        xp = _mlp(xp2, g_mlp[i], w_mlp1_b, w_mlp2_b, i, gf).reshape(bp, tp, d)
        xs = _mlp(xs2, g_mlp[i], w_mlp1_b, w_mlp2_b, i, gf).reshape(bs, ts, d)
    return (xp, xs, jnp.stack(kp_l), jnp.stack(vp_l), jnp.stack(kip_l), jnp.stack(convp_l),
            jnp.stack(memk_l), jnp.stack(memv_l), jnp.stack(ks_l), jnp.stack(vs_l),
            jnp.stack(kis_l), jnp.stack(convs_l))
```

```python
import functools
import math

import jax
import jax.numpy as jnp
from jax import lax
from jax.experimental import pallas as pl
from jax.experimental.pallas import tpu as pltpu

CHUNK = 64
HEAD_DIM = 64
N_KV_HEADS = 4
IDX_HEADS = 8
IDX_DIM = 64
TOPK_MAX = 256
N_BUCKETS = 32
MAX_DISTANCE = 128
CONV_WIDTH = 31
MEM_HEADS = 4
EPS = 1e-6

LANES = 128
SUBLANES = 8
KEY_BLOCK = 128
HEADS_PER_TILE = 2
COUNT_UNROLL = 8
HI_BITS = 15
LO_BITS = 32 - HI_BITS
LOW_CHECK_EVERY = 4
V_AUG_ROWS = HEAD_DIM + 16
LOG2E = math.log2(math.e)
ROW_TILE = 1024
HALO = 32
VMEM_LIMIT = 56 * 1024 * 1024

NEG_BIG = -1e30
F32 = jnp.float32
BF16 = jnp.bfloat16

KEY_NEG_INF = -2139095041
KEY_POS_INF = 2139095040


def _const_spec(shape):
    nd = len(shape)
    return pl.BlockSpec(shape, lambda *_: (0,) * nd, pipeline_mode=pl.Buffered(1))


def _layer_spec(stacked, layer):
    nd = stacked.ndim
    return pl.BlockSpec((None,) + stacked.shape[1:], lambda *_: (layer,) + (0,) * (nd - 1),
                        pipeline_mode=pl.Buffered(1))


def _params(*sem):
    return pltpu.CompilerParams(dimension_semantics=sem, vmem_limit_bytes=VMEM_LIMIT)


def _rms(x, g):
    ms = jnp.mean(x * x, axis=-1, keepdims=True)
    return x * lax.rsqrt(ms + EPS) * g


def _dot(a, b):
    return jnp.dot(a, b, preferred_element_type=F32)


def _dot_nt(a, b):
    return lax.dot_general(a, b, (((1,), (1,)), ((), ())), preferred_element_type=F32)


def _row_tile(m, pref):
    t = min(m, pref)
    assert m % t == 0, (m, t)
    return t


def _bias_tiles_kernel(tab_ref, bt_ref, *, n_heads):
    nb = N_BUCKETS // 2
    max_exact = nb // 2
    c = lax.broadcasted_iota(jnp.int32, (KEY_BLOCK, KEY_BLOCK), 0)
    r = lax.broadcasted_iota(jnp.int32, (KEY_BLOCK, KEY_BLOCK), 1)
    for d in range(2):
        rel = c - r - d * KEY_BLOCK
        n = jnp.abs(rel)
        nf = jnp.maximum(n, 1).astype(F32)
        large = max_exact + (jnp.log(nf / max_exact) / math.log(MAX_DISTANCE / max_exact)
                             * (nb - max_exact)).astype(jnp.int32)
        large = jnp.minimum(large, nb - 1)
        bucket = jnp.where(rel > 0, nb, 0) + jnp.where(n < max_exact, n, large)
        for h in range(n_heads):
            acc = jnp.zeros((KEY_BLOCK, KEY_BLOCK), F32)
            for b in range(N_BUCKETS):
                acc = jnp.where(bucket == b, tab_ref[b, h], acc)
            g = h % HEADS_PER_TILE
            rows = slice((1 - d) * KEY_BLOCK, (2 - d) * KEY_BLOCK)
            bt_ref[h // HEADS_PER_TILE, rows, g * KEY_BLOCK:(g + 1) * KEY_BLOCK] = (
                (acc - tab_ref[nb - 1, h]) * LOG2E)


def _bias_tiles(rel_bias):
    n_heads = rel_bias.shape[1]
    assert n_heads % HEADS_PER_TILE == 0
    return pl.pallas_call(
        functools.partial(_bias_tiles_kernel, n_heads=n_heads),
        out_shape=jax.ShapeDtypeStruct(
            (n_heads // HEADS_PER_TILE, 2 * KEY_BLOCK, HEADS_PER_TILE * KEY_BLOCK), F32),
        in_specs=[pl.BlockSpec(memory_space=pltpu.SMEM)],
        out_specs=pl.BlockSpec(memory_space=pltpu.VMEM),
        name="bias_tiles",
    )(rel_bias)


def _attn_proj_kernel(x_ref, g_ref, w_ref, q_ref, k_ref, v_ref, ki_ref, kb_ref, vb_ref,
                      kib_ref, qi_ref, wi_ref, *, dq, dkv, dqi):
    h = _rms(x_ref[...], g_ref[...]).astype(BF16)
    o = 0
    q_ref[...] = (_dot(h, w_ref[:, o:o + dq]) * (HEAD_DIM ** -0.5)).astype(BF16)
    o += dq
    k = _dot(h, w_ref[:, o:o + dkv])
    k_ref[...] = k
    kb_ref[...] = k.astype(BF16)
    o += dkv
    v = _dot(h, w_ref[:, o:o + dkv])
    v_ref[...] = v
    vb_ref[...] = v.astype(BF16)
    o += dkv
    qi_ref[...] = (_dot(h, w_ref[:, o:o + dqi]) * (IDX_DIM ** -0.5)).astype(BF16)
    o += dqi
    ki = _dot(h, w_ref[:, o:o + LANES])[:, :IDX_DIM]
    ki_ref[...] = ki
    kib_ref[...] = ki.astype(BF16)
    o += LANES
    wi_ref[...] = _dot(h, w_ref[:, o:o + LANES]) * (IDX_HEADS ** -0.5)


def _attn_proj_weights(w_in):
    d = w_in.shape[1]
    base = d + 2 * N_KV_HEADS * HEAD_DIM + IDX_HEADS * IDX_DIM
    pad = lambda a: jnp.pad(a, ((0, 0), (0, 0), (0, LANES - a.shape[2])))
    return jnp.concatenate([w_in[:, :, :base], pad(w_in[:, :, base:base + IDX_DIM]),
                            pad(w_in[:, :, base + IDX_DIM:])], axis=2).astype(BF16)


def _attn_proj(x2, g, w, layer):
    m, d = x2.shape
    dq = d
    dkv = N_KV_HEADS * HEAD_DIM
    dqi = IDX_HEADS * IDX_DIM
    tm = _row_tile(m, ROW_TILE)
    row = lambda n: pl.BlockSpec((tm, n), lambda i: (i, 0))
    outs = [(dq, BF16), (dkv, F32), (dkv, F32), (IDX_DIM, F32), (dkv, BF16), (dkv, BF16),
            (IDX_DIM, BF16), (dqi, BF16), (LANES, F32)]
    return pl.pallas_call(
        functools.partial(_attn_proj_kernel, dq=dq, dkv=dkv, dqi=dqi),
        out_shape=[jax.ShapeDtypeStruct((m, n), dt) for n, dt in outs],
        grid=(m // tm,),
        in_specs=[row(d), _const_spec((1, d)), _layer_spec(w, layer)],
        out_specs=[row(n) for n, _ in outs],
        compiler_params=_params("parallel"),
        name="attn_proj",
    )(x2, g.reshape(1, d), w)


def _sparse_attn_kernel(q_ref, qi_ref, wi_ref, k_ref, vt_ref, ki_ref, bt_ref, o_ref,
                        s_ref, w_ref, dots_ref, qgt_ref, qit_ref, ot_ref, lg_ref, mb_ref, *state_refs,
                        q_off, n_keys, k_sel, group):
    tq = KEY_BLOCK
    hpt = HEADS_PER_TILE
    n_tiles = qgt_ref.shape[0]
    m_refs, acc_refs = state_refs[:n_tiles], state_refs[n_tiles:]
    i = pl.program_id(1)
    qs = q_off // KEY_BLOCK + i
    n_kb = qs + 1
    q_start = q_off + i * tq

    qt = q_ref[0].astype(F32).T * LOG2E
    for t in range(n_tiles):
        q2 = jnp.concatenate(
            [qt[(t * hpt + g) * HEAD_DIM:(t * hpt + g + 1) * HEAD_DIM, :] for g in range(hpt)], axis=1)
        q_hi = q2.astype(BF16)
        q_lo = (q2 - q_hi.astype(F32)).astype(BF16)
        qgt_ref[t] = jnp.concatenate([q_hi, q_lo], axis=0)
    qit = qi_ref[0].astype(F32).T
    qit_ref[...] = jnp.concatenate([qit[h * IDX_DIM:(h + 1) * IDX_DIM, :] for h in range(IDX_HEADS)],
                                   axis=1).astype(BF16)
    last_kv = k_ref.shape[1] // KEY_BLOCK - 1
    wit = wi_ref[0].T

    qpos = q_start + lax.broadcasted_iota(jnp.int32, (1, tq), 1)
    lim = jnp.minimum((qpos // CHUNK + 1) * CHUNK, n_keys)
    kidx = lax.broadcasted_iota(jnp.int32, (KEY_BLOCK, tq), 0)

    def key_to_f32(key):
        bits = key ^ ((key >> 31) & 0x7FFFFFFF)
        return lax.bitcast_convert_type(bits, F32)

    def digit(sc):
        bits = lax.bitcast_convert_type(jnp.where(sc == 0.0, 0.0, sc), jnp.int32)
        key = bits ^ ((bits >> 31) & 0x7FFFFFFF)
        return (key >> LO_BITS) + (1 << (HI_BITS - 1))

    def dots(j, slot):
        k0 = pl.multiple_of(jnp.minimum(j, last_kv) * KEY_BLOCK, KEY_BLOCK)
        dots_ref[slot] = _dot(ki_ref[0, pl.ds(k0, KEY_BLOCK), :], qit_ref[...])

    def finish(j, slot):
        acc = jnp.zeros((KEY_BLOCK, tq), F32)
        for h in range(IDX_HEADS):
            acc = acc + wit[h:h + 1, :] * jnp.maximum(dots_ref[slot, :, h * tq:(h + 1) * tq], 0.0)
        sc = jnp.where(kidx + j * KEY_BLOCK < lim, acc, -jnp.inf)
        s_ref[j] = sc
        return digit(sc)

    def finish_pair(j, slot):
        w_ref[j // 2] = (finish(j, slot) << 16) | finish(j + 1, slot + 1)

    def score_quad(quad, carry):
        j = 4 * quad
        dots(j + 2, 2)
        dots(j + 3, 3)
        finish_pair(j, 0)
        dots(j + 4, 0)
        dots(j + 5, 1)
        finish_pair(j + 2, 2)
        return carry

    n_pairs = (n_kb + 1) // 2
    n_quads = (n_kb + 3) // 4
    dots(0, 0)
    dots(1, 1)
    lax.fori_loop(0, n_quads, score_quad, 0)
    for u in range(COUNT_UNROLL):
        s_ref[2 * n_pairs + u] = jnp.full((KEY_BLOCK, tq), -jnp.inf, F32)
    for u in range(COUNT_UNROLL // 2):
        w_ref[2 * n_quads + u] = jnp.zeros((KEY_BLOCK, tq), jnp.int32)
    n_count = (2 * n_pairs + COUNT_UNROLL - 1) // COUNT_UNROLL

    def count_digits(cand):
        comp = (1 << HI_BITS) - cand
        cw = jnp.broadcast_to((comp << 16) | comp, (KEY_BLOCK, tq))

        def body(p, c):
            for u in range(COUNT_UNROLL // 2):
                c = c + (((w_ref[(COUNT_UNROLL // 2) * p + u] + cw) >> HI_BITS) & 0x00010001)
            return c
        c = lax.fori_loop(0, n_count, body, jnp.zeros((KEY_BLOCK, tq), jnp.int32))
        return jnp.sum(((c & 0xFFFF) + (c >> 16)).astype(F32), axis=0, keepdims=True)

    def count_ge(cand):
        cb = jnp.broadcast_to(cand, (KEY_BLOCK, tq))

        def body(p, c):
            for u in range(COUNT_UNROLL):
                c = c + jnp.where(s_ref[COUNT_UNROLL * p + u] >= cb, 1.0, 0.0)
            return c
        c = lax.fori_loop(0, n_count, body, jnp.zeros((KEY_BLOCK, tq), F32))
        return jnp.sum(c, axis=0, keepdims=True)

    def bisect_step(count_at, carry):
        lo, hi, n_lo = carry
        mid = (lo + hi) >> 1
        n_mid = count_at(mid)
        ok = n_mid >= k_sel
        return jnp.where(ok, mid, lo), jnp.where(ok, hi, mid), jnp.where(ok, n_mid, n_lo)

    full = lambda v: jnp.full((1, tq), v, jnp.int32)
    half = 1 << (HI_BITS - 1)
    dig, _, n_sel = lax.fori_loop(
        0, HI_BITS, lambda _, c: bisect_step(count_digits, c),
        (full(((KEY_NEG_INF + 1) >> LO_BITS) + half), full((KEY_POS_INF >> LO_BITS) + half + 1),
         jnp.full((1, tq), jnp.inf, F32)))
    key_hi = (dig - half) << LO_BITS

    count_lo = lambda v: count_ge(key_to_f32(key_hi + v))

    def low_steps(carry):
        step, c = carry[0], carry[1:]
        for _ in range(LOW_CHECK_EVERY):
            c = bisect_step(count_lo, c)
        return (step + LOW_CHECK_EVERY,) + c

    def low_unfinished(carry):
        step, n_lo = carry[0], carry[3]
        return (step < LO_BITS) & (jnp.max(jnp.abs(n_lo - k_sel)) > 0.0)

    _, key_lo, _, n_sel = lax.while_loop(low_unfinished, low_steps, (0, full(0), full(1 << LO_BITS), n_sel))
    thr = key_to_f32(key_hi + key_lo)

    def count(cand, strict):
        cb = jnp.broadcast_to(cand, (KEY_BLOCK, tq))
        hit = (lambda s: s > cb) if strict else (lambda s: s >= cb)

        def body(p, c):
            c = c + jnp.where(hit(s_ref[2 * p]), 1.0, 0.0)
            return c + jnp.where(hit(s_ref[2 * p + 1]), 1.0, 0.0)
        c = lax.fori_loop(0, n_pairs, body, jnp.zeros((KEY_BLOCK, tq), F32))
        return jnp.sum(c, axis=0, keepdims=True)

    surplus = jnp.where(n_sel < jnp.inf, n_sel - k_sel, 0.0)

    @pl.when(jnp.max(surplus) > 0.0)
    def _():
        n_ties = k_sel - count(thr, True)
        row_i = lax.broadcasted_iota(jnp.int32, (KEY_BLOCK, KEY_BLOCK), 0)
        col_i = lax.broadcasted_iota(jnp.int32, (KEY_BLOCK, KEY_BLOCK), 1)
        lower = jnp.where(col_i < row_i, 1.0, 0.0).astype(BF16)
        ones = jnp.ones((KEY_BLOCK, KEY_BLOCK), BF16)

        def tie_body(j, seen):
            s = s_ref[j]
            eq = s == thr
            e = jnp.where(eq, 1.0, 0.0).astype(BF16)
            before = _dot(lower, e) + seen
            s_ref[j] = jnp.where(eq & (before >= n_ties), -jnp.inf, s)
            return seen + _dot(ones, e)

        lax.fori_loop(0, n_kb, tie_body, jnp.zeros((KEY_BLOCK, tq), F32))

    thr_sel = jnp.broadcast_to(jnp.maximum(thr, jnp.finfo(F32).min), (KEY_BLOCK, tq))

    for t in range(n_tiles):
        m_refs[t][...] = jnp.full(m_refs[t].shape, NEG_BIG, F32)
        acc_refs[t][...] = jnp.zeros(acc_refs[t].shape, F32)

    def far_blocks(db):
        out = []
        for j in (2 * db, 2 * db + 1):
            js = jnp.where(j < qs - 1, j, n_kb)
            out.append((js, jnp.minimum(js, last_kv)))
        return out

    near = [(jnp.where(qs >= 1, qs - 1, n_kb), jnp.maximum(qs - 1, 0)), (qs, qs)]

    def mask_of(blocks):
        m = jnp.concatenate([jnp.where(s_ref[js] >= thr_sel, 0.0, NEG_BIG) for js, _ in blocks], axis=0)
        return jnp.concatenate([m] * hpt, axis=1)

    def logits(blocks, mask, biased, slot, t):
        n = (t * hpt) // group
        kn = jnp.concatenate(
            [k_ref[0, pl.ds(pl.multiple_of(jk * KEY_BLOCK, KEY_BLOCK), KEY_BLOCK),
                   n * 2 * HEAD_DIM:(n + 1) * 2 * HEAD_DIM] for _, jk in blocks], axis=0)
        lg = _dot(kn, qgt_ref[t]) + mask
        if biased:
            lg = lg + bt_ref[t]
        lg_ref[slot, t] = lg
        mb_ref[slot, t] = jnp.max(lg, axis=0, keepdims=True)

    def softmax_update(blocks, slot, t):
        n = (t * hpt) // group
        m_old = m_refs[t][...]
        m_new = jnp.maximum(m_old, mb_ref[slot, t])
        alpha = jnp.exp2(m_old - m_new)
        p = jnp.exp2(lg_ref[slot, t] - m_new).astype(BF16)
        m_refs[t][...] = m_new
        vtn = jnp.concatenate([vt_ref[0, jk, n] for _, jk in blocks], axis=1)
        acc_refs[t][...] = alpha * acc_refs[t][...] + _dot(vtn, p)

    n_far = qs // 2
    mask_near = mask_of(near)
    for t in range(n_tiles):
        logits(near, mask_near, True, 0, t)

    def trip_body(trip, carry):
        d0 = 2 * trip
        blk1, blk2 = far_blocks(d0), far_blocks(d0 + 1)
        mask1, mask2 = mask_of(blk1), mask_of(blk2)
        far0 = far_blocks(d0 - 1)
        blk0 = [tuple(jnp.where(d0 == 0, a, b) for a, b in zip(near[h], far0[h])) for h in range(2)]
        for t in range(n_tiles):
            logits(blk1, mask1, False, 1, t)
            softmax_update(blk0, 0, t)
        for t in range(n_tiles):
            logits(blk2, mask2, False, 0, t)
            softmax_update(blk1, 1, t)
        return carry

    lax.fori_loop(0, (n_far + 2) // 2, trip_body, 0)

    for t in range(n_tiles):
        on = acc_refs[t][0:HEAD_DIM, :] / acc_refs[t][HEAD_DIM:HEAD_DIM + 1, :]
        for g in range(hpt):
            hh = t * hpt + g
            ot_ref[hh * HEAD_DIM:(hh + 1) * HEAD_DIM, :] = on[:, g * tq:(g + 1) * tq]
    o_ref[0] = ot_ref[...].T.astype(BF16)


def _sparse_attn(q, qi, wi, kb, vb, kib, bias_tiles, *, q_off, n_keys, k_sel):
    b, t, dq = q.shape
    lp = kb.shape[1]
    dkv = kb.shape[2]
    tq = KEY_BLOCK
    n_heads = dq // HEAD_DIM
    group = n_heads // N_KV_HEADS
    hpt = HEADS_PER_TILE
    n_tiles = n_heads // hpt
    assert group % hpt == 0
    nkb = lp // KEY_BLOCK
    assert t % tq == 0 and lp % KEY_BLOCK == 0 and q_off % KEY_BLOCK == 0 and q_off + t <= lp
    vt = jnp.transpose(vb.reshape(b, nkb, KEY_BLOCK, N_KV_HEADS, HEAD_DIM), (0, 1, 3, 4, 2))
    vt = jnp.concatenate(
        [vt, jnp.ones((b, nkb, N_KV_HEADS, 1, KEY_BLOCK), BF16),
         jnp.zeros((b, nkb, N_KV_HEADS, V_AUG_ROWS - HEAD_DIM - 1, KEY_BLOCK), BF16)], axis=3)
    k2 = jnp.concatenate([kb.reshape(b, lp, N_KV_HEADS, HEAD_DIM)] * 2, axis=-1).reshape(b, lp, 2 * dkv)
    qblk = lambda n: pl.BlockSpec((1, tq, n), lambda bi, i: (bi, i, 0))
    kblk = lambda n: pl.BlockSpec((1, lp, n), lambda bi, i: (bi, 0, 0))
    return pl.pallas_call(
        functools.partial(_sparse_attn_kernel, q_off=q_off, n_keys=n_keys, k_sel=k_sel, group=group),
        out_shape=jax.ShapeDtypeStruct((b, t, dq), BF16),
        grid=(b, t // tq),
        in_specs=[qblk(dq), qblk(qi.shape[2]), qblk(LANES), kblk(2 * dkv),
                  pl.BlockSpec((1, nkb, N_KV_HEADS, V_AUG_ROWS, KEY_BLOCK), lambda bi, i: (bi, 0, 0, 0, 0)),
                  kblk(kib.shape[2]), _const_spec(bias_tiles.shape)],
        out_specs=qblk(dq),
        scratch_shapes=[
            pltpu.VMEM((nkb + 1 + COUNT_UNROLL, KEY_BLOCK, tq), F32),
            pltpu.VMEM((nkb // 2 + COUNT_UNROLL, KEY_BLOCK, tq), jnp.int32),
            pltpu.VMEM((4, KEY_BLOCK, IDX_HEADS * tq), F32),
            pltpu.VMEM((n_tiles, 2 * HEAD_DIM, hpt * tq), BF16),
            pltpu.VMEM((IDX_DIM, IDX_HEADS * tq), BF16),
            pltpu.VMEM((dq, tq), F32),
            pltpu.VMEM((2, n_tiles, 2 * KEY_BLOCK, hpt * tq), F32),
            pltpu.VMEM((2, n_tiles, 1, hpt * tq), F32),
        ] + [pltpu.VMEM((1, hpt * tq), F32)] * n_tiles
          + [pltpu.VMEM((V_AUG_ROWS, hpt * tq), F32)] * n_tiles,
        compiler_params=_params("parallel", "arbitrary"),
        name="sparse_attn",
    )(q, qi, wi, k2, vt, kib, bias_tiles)


def _mem_kv_kernel(x_ref, g_ref, w_ref, k_ref, v_ref, *, d):
    h = _rms(x_ref[...], g_ref[...]).astype(BF16)
    k_ref[...] = _dot(h, w_ref[:, :d])
    v_ref[...] = _dot(h, w_ref[:, d:])


def _mem_kv(mem2, g, w_kv, layer):
    m, d = mem2.shape
    tm = _row_tile(m, ROW_TILE)
    row = pl.BlockSpec((tm, d), lambda i: (i, 0))
    return pl.pallas_call(
        functools.partial(_mem_kv_kernel, d=d),
        out_shape=[jax.ShapeDtypeStruct((m, d), F32)] * 2,
        grid=(m // tm,),
        in_specs=[row, _const_spec((1, d)), _layer_spec(w_kv, layer)],
        out_specs=[row, row],
        compiler_params=_params("parallel"),
        name="mem_kv",
    )(mem2, g.reshape(1, d), w_kv)


def _mem_attn_kernel(*refs, hd, pending):
    if pending:
        x_ref, a_ref, wa_ref, g_ref, wq_ref, mk_ref, mv_ref, wo_ref, o_ref = refs
        x = x_ref[...] + _dot(a_ref[...], wa_ref[...])
    else:
        x_ref, g_ref, wq_ref, mk_ref, mv_ref, wo_ref, o_ref = refs
        x = x_ref[...]
    h = _rms(x, g_ref[...]).astype(BF16)
    q = (_dot(h, wq_ref[...]) * (hd ** -0.5)).astype(BF16)
    heads = []
    for a in range(MEM_HEADS):
        cols = slice(a * hd, (a + 1) * hd)
        lg = _dot_nt(q[:, cols], mk_ref[0, :, cols].astype(BF16))
        p = jnp.exp(lg - jnp.max(lg, axis=1, keepdims=True))
        p = (p / jnp.sum(p, axis=1, keepdims=True)).astype(BF16)
        heads.append(_dot(p, mv_ref[0, :, cols].astype(BF16)).astype(BF16))
    o_ref[...] = x + _dot(jnp.concatenate(heads, axis=1), wo_ref[...])


def _mem_attn(x2, g, w_q, mk, mv, w_o, layer, rows_per_batch, pending=None):
    m, d = x2.shape
    n_mem = mk.shape[1]
    tm = _row_tile(rows_per_batch, ROW_TILE)
    per = rows_per_batch // tm
    row = pl.BlockSpec((tm, d), lambda i: (i, 0))
    mem = pl.BlockSpec((1, n_mem, d), lambda i: (i // per, 0, 0))
    in_specs = [row, _const_spec((1, d)), _layer_spec(w_q, layer), mem, mem, _layer_spec(w_o, layer)]
    args = [x2, g.reshape(1, d), w_q, mk, mv, w_o]
    if pending is not None:
        a2, w_a, layer_a = pending
        in_specs[1:1] = [pl.BlockSpec((tm, a2.shape[1]), lambda i: (i, 0)), _layer_spec(w_a, layer_a)]
        args[1:1] = [a2, w_a]
    return pl.pallas_call(
        functools.partial(_mem_attn_kernel, hd=d // MEM_HEADS, pending=pending is not None),
        out_shape=jax.ShapeDtypeStruct((m, d), F32),
        grid=(m // tm,),
        in_specs=in_specs,
        out_specs=row,
        compiler_params=_params("parallel"),
        name="mem_attn",
    )(*args)


def _mlp_kernel(*refs, n_chunks, chunk, final):
    if final:
        x_ref, g_ref, w1_ref, w2_ref, gf_ref, o_ref = refs
    else:
        x_ref, g_ref, w1_ref, w2_ref, o_ref = refs
    x = x_ref[...]
    h = _rms(x, g_ref[...]).astype(BF16)
    acc = x
    for c in range(n_chunks):
        a = jnp.maximum(_dot(h, w1_ref[:, c * chunk:(c + 1) * chunk]), 0.0)
        acc = acc + _dot((a * a).astype(BF16), w2_ref[c * chunk:(c + 1) * chunk, :])
    if final:
        acc = _rms(acc, gf_ref[...])
    o_ref[...] = acc


def _mlp(x2, g, w1, w2, layer, g_final=None):
    m, d = x2.shape
    dff = w1.shape[2]
    chunk = min(dff, 1024)
    tm = _row_tile(m, ROW_TILE)
    row = pl.BlockSpec((tm, d), lambda i: (i, 0))
    final = g_final is not None
    in_specs = [row, _const_spec((1, d)), _layer_spec(w1, layer), _layer_spec(w2, layer)]
    args = [x2, g.reshape(1, d), w1, w2]
    if final:
        in_specs.append(_const_spec((1, d)))
        args.append(g_final.reshape(1, d))
    return pl.pallas_call(
        functools.partial(_mlp_kernel, n_chunks=dff // chunk, chunk=chunk, final=final),
        out_shape=jax.ShapeDtypeStruct((m, d), F32),
        grid=(m // tm,),
        in_specs=in_specs,
        out_specs=row,
        compiler_params=_params("parallel"),
        name="mlp",
    )(*args)


def _conv_glu_kernel(x_ref, g_ref, w_ref, b_ref, u_ref, *, d):
    h = _rms(x_ref[...], g_ref[...]).astype(BF16)
    a = _dot(h, w_ref[:, :d]) + b_ref[:, :d]
    gate = _dot(h, w_ref[:, d:]) + b_ref[:, d:]
    u_ref[...] = a * (1.0 / (1.0 + jnp.exp(-gate)))


def _conv_glu(x2, g, w_pw1, layer, b_pw1):
    m, d = x2.shape
    tm = _row_tile(m, ROW_TILE)
    row = pl.BlockSpec((tm, d), lambda i: (i, 0))
    return pl.pallas_call(
        functools.partial(_conv_glu_kernel, d=d),
        out_shape=jax.ShapeDtypeStruct((m, d), F32),
        grid=(m // tm,),
        in_specs=[row, _const_spec((1, d)), _layer_spec(w_pw1, layer), _const_spec((1, 2 * d))],
        out_specs=row,
        compiler_params=_params("parallel"),
        name="conv_glu",
    )(x2, g.reshape(1, d), w_pw1, b_pw1.reshape(1, 2 * d))


def _conv_rest_kernel(x_ref, u_ref, prev_ref, init_ref, wdw_ref, bdw_ref, lng_ref, lnb_ref,
                      w2_ref, b2_ref, o_ref, ext_ref, sh_ref, y_ref, *, tm, rc, lc):
    t = pl.program_id(1)
    d = u_ref.shape[2]
    pad = CONV_WIDTH - 1

    @pl.when(t == 0)
    def _():
        ext_ref[0:HALO, :] = init_ref[0]

    @pl.when(t > 0)
    def _():
        ext_ref[0:HALO, :] = prev_ref[0]

    ext_ref[HALO:HALO + tm, :] = u_ref[0]

    for s in range(SUBLANES):
        rows = tm + SUBLANES * ((CONV_WIDTH - 1 - s) // SUBLANES)
        sh_ref[s, 0:rows, :] = ext_ref[pl.ds(HALO - pad + s, rows), :]
    for r0 in range(0, tm, rc):
        for c0 in range(0, d, lc):
            cols = slice(c0, c0 + lc)
            y = jnp.broadcast_to(bdw_ref[:, cols], (rc, lc))
            for w in range(CONV_WIDTH):
                a, s = divmod(w, SUBLANES)
                y = y + sh_ref[s, r0 + SUBLANES * a:r0 + SUBLANES * a + rc, cols] * wdw_ref[w:w + 1, cols]
            y_ref[r0:r0 + rc, cols] = y

    y = y_ref[...]
    mu = jnp.mean(y, axis=-1, keepdims=True)
    yc = y - mu
    var = jnp.mean(yc * yc, axis=-1, keepdims=True)
    yn = yc * lax.rsqrt(var + EPS) * lng_ref[...] + lnb_ref[...]
    act = (yn * (1.0 / (1.0 + jnp.exp(-yn)))).astype(BF16)
    o_ref[0] = x_ref[0] + _dot(act, w2_ref[...]) + b2_ref[...]


def _conv_rest(x3, u3, init, w_dw, b_dw, ln_g, ln_b, w_pw2, layer, b_pw2):
    b, t, d = x3.shape
    tm = _row_tile(t, 256)
    rc = min(tm, 64)
    lc = min(d, 256)
    assert tm % HALO == 0 and tm % rc == 0 and d % lc == 0
    per = tm // HALO
    tile = pl.BlockSpec((1, tm, d), lambda bi, ti: (bi, ti, 0))
    prev = pl.BlockSpec((1, HALO, d), lambda bi, ti: (bi, jnp.maximum(ti * per - 1, 0), 0))
    first = pl.BlockSpec((1, HALO, d), lambda bi, ti: (bi, 0, 0))
    vec = _const_spec((1, d))
    wdw = jnp.pad(w_dw, ((0, HALO - CONV_WIDTH), (0, 0)))
    return pl.pallas_call(
        functools.partial(_conv_rest_kernel, tm=tm, rc=rc, lc=lc),
        out_shape=jax.ShapeDtypeStruct((b, t, d), F32),
        grid=(b, t // tm),
        in_specs=[tile, tile, prev, first, _const_spec((HALO, d)), vec, vec, vec,
                  _layer_spec(w_pw2, layer), vec],
        out_specs=tile,
        scratch_shapes=[pltpu.VMEM((HALO + tm, d), F32),
                        pltpu.VMEM((SUBLANES, tm + HALO - SUBLANES, d), F32),
                        pltpu.VMEM((tm, d), F32)],
        compiler_params=_params("parallel", "arbitrary"),
        name="conv_rest",
    )(x3, u3, u3, init, wdw, b_dw.reshape(1, d), ln_g.reshape(1, d), ln_b.reshape(1, d),
      w_pw2, b_pw2.reshape(1, d))


def _mixer_attn(x3, g, w_in, layer, bias_tiles, cache=None):
    b, t, d = x3.shape
    x2 = x3.reshape(b * t, d)
    q, k, v, ki, kb, vb, kib, qi, wi = _attn_proj(x2, g, w_in, layer)
    r3 = lambda a: a.reshape(b, t, a.shape[-1])
    kb, vb, kib = r3(kb), r3(vb), r3(kib)
    past = 0
    if cache is not None:
        ck, cv, cki = cache
        past = ck.shape[1]
        kb = jnp.concatenate([ck.reshape(b, past, -1).astype(BF16), kb], axis=1)
        vb = jnp.concatenate([cv.reshape(b, past, -1).astype(BF16), vb], axis=1)
        kib = jnp.concatenate([cki.astype(BF16), kib], axis=1)
    n_keys = past + t
    k_sel = min(TOPK_MAX, n_keys // 4)
    tpad = -(-t // KEY_BLOCK) * KEY_BLOCK
    lp = max(-(-n_keys // KEY_BLOCK) * KEY_BLOCK, past + tpad)
    padt = lambda a, n: a if a.shape[1] == n else jnp.pad(a, ((0, 0), (0, n - a.shape[1]), (0, 0)))
    o = _sparse_attn(padt(r3(q), tpad), padt(r3(qi), tpad), padt(r3(wi), tpad),
                     padt(kb, lp), padt(vb, lp), padt(kib, lp), bias_tiles,
                     q_off=past, n_keys=n_keys, k_sel=k_sel)[:, :t]
    return (o.reshape(b * t, d), k.reshape(b, t, N_KV_HEADS, HEAD_DIM),
            v.reshape(b, t, N_KV_HEADS, HEAD_DIM), ki.reshape(b, t, IDX_DIM))


def _mixer_conv(x3, g, w_pw1, b_pw1, w_dw, b_dw, ln_g, ln_b, w_pw2, b_pw2, layer, state=None):
    b, t, d = x3.shape
    pad = CONV_WIDTH - 1
    u3 = _conv_glu(x3.reshape(b * t, d), g, w_pw1, layer, b_pw1).reshape(b, t, d)
    if state is None:
        init = jnp.zeros((b, HALO, d), F32)
        tail = u3[:, -pad:] if t >= pad else jnp.pad(u3, ((0, 0), (pad - t, 0), (0, 0)))
    else:
        init = jnp.pad(state.astype(F32), ((0, 0), (HALO - pad, 0), (0, 0)))
        tail = jnp.concatenate([state.astype(F32), u3], axis=1)[:, -pad:]
    x3 = _conv_rest(x3, u3, init, w_dw, b_dw, ln_g, ln_b, w_pw2, layer, b_pw2)
    return x3, tail


def kernel(x_prompt, x_sample, cache_attn_k, cache_attn_v, cache_attn_kidx, state_conv, cache_mem_k, cache_mem_v, mem_prompt, rel_bias, g_mix, w_in_attn, w_out_attn, w_pw1, b_pw1, w_dw, b_dw, ln_g, ln_b, w_pw2, b_pw2, g_mem_q, g_mem_src, w_mem_q, w_mem_kv, w_mem_o, g_mlp, w_mlp1, w_mlp2, g_final):
    depth = g_mix.shape[0]
    bp, tp, d = x_prompt.shape
    bs, ts, _ = x_sample.shape
    n_mem = mem_prompt.shape[1]
    mem_hd = d // MEM_HEADS
    bias_tiles = _bias_tiles(rel_bias)
    w_in_b, w_out_b = _attn_proj_weights(w_in_attn), w_out_attn.astype(BF16)
    w_pw1_b, w_pw2_b = w_pw1.astype(BF16), w_pw2.astype(BF16)
    w_mq_b, w_mkv_b, w_mo_b = w_mem_q.astype(BF16), w_mem_kv.astype(BF16), w_mem_o.astype(BF16)
    w_mlp1_b, w_mlp2_b = w_mlp1.astype(BF16), w_mlp2.astype(BF16)
    xp, xs = x_prompt, x_sample
    kp_l, vp_l, kip_l, ks_l, vs_l, kis_l = [], [], [], [], [], []
    convp_l, convs_l, memk_l, memv_l = [], [], [], []
    for i in range(depth):
        j = i // 2
        if i % 2 == 0:
            op, kp, vp, kip = _mixer_attn(xp, g_mix[i], w_in_b, j, bias_tiles)
            osm, ks, vs, kis = _mixer_attn(
                xs, g_mix[i], w_in_b, j, bias_tiles,
                cache=(cache_attn_k[j], cache_attn_v[j], cache_attn_kidx[j]))
            pend_p, pend_s = (op, w_out_b, j), (osm, w_out_b, j)
            kp_l.append(kp); vp_l.append(vp); kip_l.append(kip)
            ks_l.append(ks); vs_l.append(vs); kis_l.append(kis)
        else:
            cw = (w_pw1_b, b_pw1[j], w_dw[j], b_dw[j], ln_g[j], ln_b[j], w_pw2_b, b_pw2[j], j)
            xp, cp = _mixer_conv(xp, g_mix[i], *cw)
            xs, cs = _mixer_conv(xs, g_mix[i], *cw, state=state_conv[j])
            pend_p = pend_s = None
            convp_l.append(cp); convs_l.append(cs)
        mk, mv = _mem_kv(mem_prompt.reshape(bp * n_mem, d), g_mem_src[i], w_mkv_b, i)
        mk, mv = mk.reshape(bp, n_mem, d), mv.reshape(bp, n_mem, d)
        memk_l.append(mk.reshape(bp, n_mem, MEM_HEADS, mem_hd))
        memv_l.append(mv.reshape(bp, n_mem, MEM_HEADS, mem_hd))
        xp2 = _mem_attn(xp.reshape(bp * tp, d), g_mem_q[i], w_mq_b, mk, mv, w_mo_b, i, tp, pend_p)
        xs2 = _mem_attn(xs.reshape(bs * ts, d), g_mem_q[i], w_mq_b,
                        cache_mem_k[i].reshape(bs, n_mem, d), cache_mem_v[i].reshape(bs, n_mem, d),
                        w_mo_b, i, ts, pend_s)
        gf = g_final if i == depth - 1 else None
        xp = _mlp(xp2, g_mlp[i], w_mlp1_b, w_mlp2_b, i, gf).reshape(bp, tp, d)
        xs = _mlp(xs2, g_mlp[i], w_mlp1_b, w_mlp2_b, i, gf).reshape(bs, ts, d)
    return (xp, xs, jnp.stack(kp_l), jnp.stack(vp_l), jnp.stack(kip_l), jnp.stack(convp_l),
            jnp.stack(memk_l), jnp.stack(memv_l), jnp.stack(ks_l), jnp.stack(vs_l),
            jnp.stack(kis_l), jnp.stack(convs_l))
```

```python
import functools
import math

import jax
import jax.numpy as jnp
from jax import lax
from jax.experimental import pallas as pl
from jax.experimental.pallas import tpu as pltpu

CHUNK = 64
HEAD_DIM = 64
N_KV_HEADS = 4
IDX_HEADS = 8
IDX_DIM = 64
TOPK_MAX = 256
N_BUCKETS = 32
MAX_DISTANCE = 128
CONV_WIDTH = 31
MEM_HEADS = 4
EPS = 1e-6

LANES = 128
SUBLANES = 8
KEY_BLOCK = 128
HEADS_PER_TILE = 2
COUNT_UNROLL = 4
HI_BITS = 15
LO_BITS = 32 - HI_BITS
LOW_CHECK_EVERY = 4
V_AUG_ROWS = HEAD_DIM + 16
LOG2E = math.log2(math.e)
ROW_TILE = 1024
CONV_ROW_TILE = 512
HALO = 32
VMEM_LIMIT = 56 * 1024 * 1024

NEG_BIG = -1e30
F32 = jnp.float32
BF16 = jnp.bfloat16

KEY_NEG_INF = -2139095041
KEY_POS_INF = 2139095040


def _const_spec(shape):
    nd = len(shape)
    return pl.BlockSpec(shape, lambda *_: (0,) * nd, pipeline_mode=pl.Buffered(1))


def _layer_spec(stacked, layer):
    nd = stacked.ndim
    return pl.BlockSpec((None,) + stacked.shape[1:], lambda *_: (layer,) + (0,) * (nd - 1),
                        pipeline_mode=pl.Buffered(1))


def _params(*sem):
    return pltpu.CompilerParams(dimension_semantics=sem, vmem_limit_bytes=VMEM_LIMIT)


def _rms(x, g):
    ms = jnp.mean(x * x, axis=-1, keepdims=True)
    return x * lax.rsqrt(ms + EPS) * g


def _dot(a, b):
    return jnp.dot(a, b, preferred_element_type=F32)


def _dot_nt(a, b):
    return lax.dot_general(a, b, (((1,), (1,)), ((), ())), preferred_element_type=F32)


def _row_tile(m, pref):
    t = min(m, pref)
    assert m % t == 0, (m, t)
    return t


def _bias_tiles_kernel(tab_ref, bt_ref, *, n_heads):
    nb = N_BUCKETS // 2
    max_exact = nb // 2
    c = lax.broadcasted_iota(jnp.int32, (KEY_BLOCK, KEY_BLOCK), 0)
    r = lax.broadcasted_iota(jnp.int32, (KEY_BLOCK, KEY_BLOCK), 1)
    for d in range(2):
        rel = c - r - d * KEY_BLOCK
        n = jnp.abs(rel)
        nf = jnp.maximum(n, 1).astype(F32)
        large = max_exact + (jnp.log(nf / max_exact) / math.log(MAX_DISTANCE / max_exact)
                             * (nb - max_exact)).astype(jnp.int32)
        large = jnp.minimum(large, nb - 1)
        bucket = jnp.where(rel > 0, nb, 0) + jnp.where(n < max_exact, n, large)
        for h in range(n_heads):
            acc = jnp.zeros((KEY_BLOCK, KEY_BLOCK), F32)
            for b in range(N_BUCKETS):
                acc = jnp.where(bucket == b, tab_ref[b, h], acc)
            g = h % HEADS_PER_TILE
            rows = slice((1 - d) * KEY_BLOCK, (2 - d) * KEY_BLOCK)
            bt_ref[h // HEADS_PER_TILE, rows, g * KEY_BLOCK:(g + 1) * KEY_BLOCK] = (
                (acc - tab_ref[nb - 1, h]) * LOG2E)


def _bias_tiles(rel_bias):
    n_heads = rel_bias.shape[1]
    assert n_heads % HEADS_PER_TILE == 0
    return pl.pallas_call(
        functools.partial(_bias_tiles_kernel, n_heads=n_heads),
        out_shape=jax.ShapeDtypeStruct(
            (n_heads // HEADS_PER_TILE, 2 * KEY_BLOCK, HEADS_PER_TILE * KEY_BLOCK), F32),
        in_specs=[pl.BlockSpec(memory_space=pltpu.SMEM)],
        out_specs=pl.BlockSpec(memory_space=pltpu.VMEM),
        name="bias_tiles",
    )(rel_bias)


def _attn_proj_kernel(x_ref, g_ref, w_ref, q_ref, k_ref, v_ref, ki_ref, kb_ref, vb_ref,
                      kib_ref, qi_ref, wi_ref, *, dq, dkv, dqi):
    h = _rms(x_ref[...], g_ref[...]).astype(BF16)
    o = 0
    q_ref[...] = (_dot(h, w_ref[:, o:o + dq]) * (HEAD_DIM ** -0.5)).astype(BF16)
    o += dq
    k = _dot(h, w_ref[:, o:o + dkv])
    k_ref[...] = k
    kb_ref[...] = k.astype(BF16)
    o += dkv
    v = _dot(h, w_ref[:, o:o + dkv])
    v_ref[...] = v
    vb_ref[...] = v.astype(BF16)
    o += dkv
    qi_ref[...] = (_dot(h, w_ref[:, o:o + dqi]) * (IDX_DIM ** -0.5)).astype(BF16)
    o += dqi
    ki = _dot(h, w_ref[:, o:o + LANES])[:, :IDX_DIM]
    ki_ref[...] = ki
    kib_ref[...] = ki.astype(BF16)
    o += LANES
    wi_ref[...] = _dot(h, w_ref[:, o:o + LANES]) * (IDX_HEADS ** -0.5)


def _attn_proj_weights(w_in):
    d = w_in.shape[1]
    base = d + 2 * N_KV_HEADS * HEAD_DIM + IDX_HEADS * IDX_DIM
    pad = lambda a: jnp.pad(a, ((0, 0), (0, 0), (0, LANES - a.shape[2])))
    return jnp.concatenate([w_in[:, :, :base], pad(w_in[:, :, base:base + IDX_DIM]),
                            pad(w_in[:, :, base + IDX_DIM:])], axis=2).astype(BF16)


def _attn_proj(x2, g, w, layer):
    m, d = x2.shape
    dq = d
    dkv = N_KV_HEADS * HEAD_DIM
    dqi = IDX_HEADS * IDX_DIM
    tm = _row_tile(m, ROW_TILE)
    row = lambda n: pl.BlockSpec((tm, n), lambda i: (i, 0))
    outs = [(dq, BF16), (dkv, F32), (dkv, F32), (IDX_DIM, F32), (dkv, BF16), (dkv, BF16),
            (IDX_DIM, BF16), (dqi, BF16), (LANES, F32)]
    return pl.pallas_call(
        functools.partial(_attn_proj_kernel, dq=dq, dkv=dkv, dqi=dqi),
        out_shape=[jax.ShapeDtypeStruct((m, n), dt) for n, dt in outs],
        grid=(m // tm,),
        in_specs=[row(d), _const_spec((1, d)), _layer_spec(w, layer)],
        out_specs=[row(n) for n, _ in outs],
        compiler_params=_params("parallel"),
        name="attn_proj",
    )(x2, g.reshape(1, d), w)


def _sparse_attn_kernel(q_ref, qi_ref, wi_ref, k_ref, vt_ref, ki_ref, bt_ref, o_ref,
                        s_ref, w_ref, dots_ref, qgt_ref, qit_ref, ot_ref, lg_ref, mb_ref, *state_refs,
                        q_off, n_keys, k_sel, group):
    tq = KEY_BLOCK
    hpt = HEADS_PER_TILE
    n_tiles = qgt_ref.shape[0]
    m_refs, acc_refs = state_refs[:n_tiles], state_refs[n_tiles:]
    i = pl.program_id(1)
    qs = q_off // KEY_BLOCK + i
    n_kb = qs + 1
    q_start = q_off + i * tq

    qt = q_ref[0].astype(F32).T * LOG2E
    for t in range(n_tiles):
        q2 = jnp.concatenate(
            [qt[(t * hpt + g) * HEAD_DIM:(t * hpt + g + 1) * HEAD_DIM, :] for g in range(hpt)], axis=1)
        q_hi = q2.astype(BF16)
        q_lo = (q2 - q_hi.astype(F32)).astype(BF16)
        qgt_ref[t] = jnp.concatenate([q_hi, q_lo], axis=0)
    qit = qi_ref[0].astype(F32).T
    qit_ref[...] = jnp.concatenate([qit[h * IDX_DIM:(h + 1) * IDX_DIM, :] for h in range(IDX_HEADS)],
                                   axis=1).astype(BF16)
    last_kv = k_ref.shape[1] // KEY_BLOCK - 1
    wit = wi_ref[0].T

    qpos = q_start + lax.broadcasted_iota(jnp.int32, (1, tq), 1)
    lim = jnp.minimum((qpos // CHUNK + 1) * CHUNK, n_keys)
    kidx = lax.broadcasted_iota(jnp.int32, (KEY_BLOCK, tq), 0)

    def key_to_f32(key):
        bits = key ^ ((key >> 31) & 0x7FFFFFFF)
        return lax.bitcast_convert_type(bits, F32)

    def digit(sc):
        bits = lax.bitcast_convert_type(jnp.where(sc == 0.0, 0.0, sc), jnp.int32)
        key = bits ^ ((bits >> 31) & 0x7FFFFFFF)
        return (key >> LO_BITS) + (1 << (HI_BITS - 1))

    def dots(j, slot):
        k0 = pl.multiple_of(jnp.minimum(j, last_kv) * KEY_BLOCK, KEY_BLOCK)
        dots_ref[slot] = _dot(ki_ref[0, pl.ds(k0, KEY_BLOCK), :], qit_ref[...])

    def finish(j, slot):
        acc = jnp.zeros((KEY_BLOCK, tq), F32)
        for h in range(IDX_HEADS):
            acc = acc + wit[h:h + 1, :] * jnp.maximum(dots_ref[slot, :, h * tq:(h + 1) * tq], 0.0)
        sc = jnp.where(kidx + j * KEY_BLOCK < lim, acc, -jnp.inf)
        s_ref[j] = sc
        return digit(sc)

    def finish_pair(j, slot):
        w_ref[j // 2] = (finish(j, slot) << 16) | finish(j + 1, slot + 1)

    def score_quad(quad, carry):
        j = 4 * quad
        dots(j + 2, 2)
        dots(j + 3, 3)
        finish_pair(j, 0)
        dots(j + 4, 0)
        dots(j + 5, 1)
        finish_pair(j + 2, 2)
        return carry

    n_pairs = (n_kb + 1) // 2
    n_quads = (n_kb + 3) // 4
    dots(0, 0)
    dots(1, 1)
    lax.fori_loop(0, n_quads, score_quad, 0)
    for u in range(COUNT_UNROLL):
        s_ref[2 * n_pairs + u] = jnp.full((KEY_BLOCK, tq), -jnp.inf, F32)
    for u in range(COUNT_UNROLL // 2):
        w_ref[2 * n_quads + u] = jnp.zeros((KEY_BLOCK, tq), jnp.int32)
    n_count = (2 * n_pairs + COUNT_UNROLL - 1) // COUNT_UNROLL

    def count_digits(cand):
        comp = (1 << HI_BITS) - cand
        cw = jnp.broadcast_to((comp << 16) | comp, (KEY_BLOCK, tq))

        def body(p, c):
            for u in range(COUNT_UNROLL // 2):
                c = c + (((w_ref[(COUNT_UNROLL // 2) * p + u] + cw) >> HI_BITS) & 0x00010001)
            return c
        c = lax.fori_loop(0, n_count, body, jnp.zeros((KEY_BLOCK, tq), jnp.int32))
        return jnp.sum(((c & 0xFFFF) + (c >> 16)).astype(F32), axis=0, keepdims=True)

    def count_ge(cand):
        cb = jnp.broadcast_to(cand, (KEY_BLOCK, tq))

        def body(p, c):
            for u in range(COUNT_UNROLL):
                c = c + jnp.where(s_ref[COUNT_UNROLL * p + u] >= cb, 1.0, 0.0)
            return c
        c = lax.fori_loop(0, n_count, body, jnp.zeros((KEY_BLOCK, tq), F32))
        return jnp.sum(c, axis=0, keepdims=True)

    def bisect_step(count_at, carry):
        lo, hi, n_lo = carry
        mid = (lo + hi) >> 1
        n_mid = count_at(mid)
        ok = n_mid >= k_sel
        return jnp.where(ok, mid, lo), jnp.where(ok, hi, mid), jnp.where(ok, n_mid, n_lo)

    full = lambda v: jnp.full((1, tq), v, jnp.int32)
    half = 1 << (HI_BITS - 1)
    dig, _, n_sel = lax.fori_loop(
        0, HI_BITS, lambda _, c: bisect_step(count_digits, c),
        (full(((KEY_NEG_INF + 1) >> LO_BITS) + half), full((KEY_POS_INF >> LO_BITS) + half + 1),
         jnp.full((1, tq), jnp.inf, F32)))
    key_hi = (dig - half) << LO_BITS

    count_lo = lambda v: count_ge(key_to_f32(key_hi + v))

    def low_steps(carry):
        step, c = carry[0], carry[1:]
        for _ in range(LOW_CHECK_EVERY):
            c = bisect_step(count_lo, c)
        return (step + LOW_CHECK_EVERY,) + c

    def low_unfinished(carry):
        step, n_lo = carry[0], carry[3]
        return (step < LO_BITS) & (jnp.max(jnp.abs(n_lo - k_sel)) > 0.0)

    _, key_lo, _, n_sel = lax.while_loop(low_unfinished, low_steps, (0, full(0), full(1 << LO_BITS), n_sel))
    thr = key_to_f32(key_hi + key_lo)

    def count(cand, strict):
        cb = jnp.broadcast_to(cand, (KEY_BLOCK, tq))
        hit = (lambda s: s > cb) if strict else (lambda s: s >= cb)

        def body(p, c):
            c = c + jnp.where(hit(s_ref[2 * p]), 1.0, 0.0)
            return c + jnp.where(hit(s_ref[2 * p + 1]), 1.0, 0.0)
        c = lax.fori_loop(0, n_pairs, body, jnp.zeros((KEY_BLOCK, tq), F32))
        return jnp.sum(c, axis=0, keepdims=True)

    surplus = jnp.where(n_sel < jnp.inf, n_sel - k_sel, 0.0)

    @pl.when(jnp.max(surplus) > 0.0)
    def _():
        n_ties = k_sel - count(thr, True)
        row_i = lax.broadcasted_iota(jnp.int32, (KEY_BLOCK, KEY_BLOCK), 0)
        col_i = lax.broadcasted_iota(jnp.int32, (KEY_BLOCK, KEY_BLOCK), 1)
        lower = jnp.where(col_i < row_i, 1.0, 0.0).astype(BF16)
        ones = jnp.ones((KEY_BLOCK, KEY_BLOCK), BF16)

        def tie_body(j, seen):
            s = s_ref[j]
            eq = s == thr
            e = jnp.where(eq, 1.0, 0.0).astype(BF16)
            before = _dot(lower, e) + seen
            s_ref[j] = jnp.where(eq & (before >= n_ties), -jnp.inf, s)
            return seen + _dot(ones, e)

        lax.fori_loop(0, n_kb, tie_body, jnp.zeros((KEY_BLOCK, tq), F32))

    thr_sel = jnp.broadcast_to(jnp.maximum(thr, jnp.finfo(F32).min), (KEY_BLOCK, tq))

    for t in range(n_tiles):
        m_refs[t][...] = jnp.full(m_refs[t].shape, NEG_BIG, F32)
        acc_refs[t][...] = jnp.zeros(acc_refs[t].shape, F32)

    def far_blocks(db):
        out = []
        for j in (2 * db, 2 * db + 1):
            js = jnp.where(j < qs - 1, j, n_kb)
            out.append((js, jnp.minimum(js, last_kv)))
        return out

    near = [(jnp.where(qs >= 1, qs - 1, n_kb), jnp.maximum(qs - 1, 0)), (qs, qs)]

    def mask_of(blocks):
        m = jnp.concatenate([jnp.where(s_ref[js] >= thr_sel, 0.0, NEG_BIG) for js, _ in blocks], axis=0)
        return jnp.concatenate([m] * hpt, axis=1)

    def logits(blocks, mask, biased, slot, t):
        n = (t * hpt) // group
        kn = jnp.concatenate(
            [k_ref[0, pl.ds(pl.multiple_of(jk * KEY_BLOCK, KEY_BLOCK), KEY_BLOCK),
                   n * 2 * HEAD_DIM:(n + 1) * 2 * HEAD_DIM] for _, jk in blocks], axis=0)
        lg = _dot(kn, qgt_ref[t]) + mask
        if biased:
            lg = lg + bt_ref[t]
        lg_ref[slot, t] = lg
        mb_ref[slot, t] = jnp.max(lg, axis=0, keepdims=True)

    def softmax_update(blocks, slot, t):
        n = (t * hpt) // group
        m_old = m_refs[t][...]
        m_new = jnp.maximum(m_old, mb_ref[slot, t])
        alpha = jnp.exp2(m_old - m_new)
        p = jnp.exp2(lg_ref[slot, t] - m_new).astype(BF16)
        m_refs[t][...] = m_new
        vtn = jnp.concatenate([vt_ref[0, jk, n] for _, jk in blocks], axis=1)
        acc_refs[t][...] = alpha * acc_refs[t][...] + _dot(vtn, p)

    n_far = qs // 2
    mask_near = mask_of(near)
    for t in range(n_tiles):
        logits(near, mask_near, True, 0, t)

    def trip_body(trip, carry):
        d0 = 2 * trip
        blk1, blk2 = far_blocks(d0), far_blocks(d0 + 1)
        mask1, mask2 = mask_of(blk1), mask_of(blk2)
        far0 = far_blocks(d0 - 1)
        blk0 = [tuple(jnp.where(d0 == 0, a, b) for a, b in zip(near[h], far0[h])) for h in range(2)]
        for t in range(n_tiles):
            logits(blk1, mask1, False, 1, t)
            softmax_update(blk0, 0, t)
        for t in range(n_tiles):
            logits(blk2, mask2, False, 0, t)
            softmax_update(blk1, 1, t)
        return carry

    lax.fori_loop(0, (n_far + 2) // 2, trip_body, 0)

    for t in range(n_tiles):
        on = acc_refs[t][0:HEAD_DIM, :] / acc_refs[t][HEAD_DIM:HEAD_DIM + 1, :]
        for g in range(hpt):
            hh = t * hpt + g
            ot_ref[hh * HEAD_DIM:(hh + 1) * HEAD_DIM, :] = on[:, g * tq:(g + 1) * tq]
    o_ref[0] = ot_ref[...].T.astype(BF16)


def _sparse_attn(q, qi, wi, kb, vb, kib, bias_tiles, *, q_off, n_keys, k_sel):
    b, t, dq = q.shape
    lp = kb.shape[1]
    dkv = kb.shape[2]
    tq = KEY_BLOCK
    n_heads = dq // HEAD_DIM
    group = n_heads // N_KV_HEADS
    hpt = HEADS_PER_TILE
    n_tiles = n_heads // hpt
    assert group % hpt == 0
    nkb = lp // KEY_BLOCK
    assert t % tq == 0 and lp % KEY_BLOCK == 0 and q_off % KEY_BLOCK == 0 and q_off + t <= lp
    vt = jnp.transpose(vb.reshape(b, nkb, KEY_BLOCK, N_KV_HEADS, HEAD_DIM), (0, 1, 3, 4, 2))
    vt = jnp.concatenate(
        [vt, jnp.ones((b, nkb, N_KV_HEADS, 1, KEY_BLOCK), BF16),
         jnp.zeros((b, nkb, N_KV_HEADS, V_AUG_ROWS - HEAD_DIM - 1, KEY_BLOCK), BF16)], axis=3)
    k2 = jnp.concatenate([kb.reshape(b, lp, N_KV_HEADS, HEAD_DIM)] * 2, axis=-1).reshape(b, lp, 2 * dkv)
    qblk = lambda n: pl.BlockSpec((1, tq, n), lambda bi, i: (bi, i, 0))
    kblk = lambda n: pl.BlockSpec((1, lp, n), lambda bi, i: (bi, 0, 0))
    return pl.pallas_call(
        functools.partial(_sparse_attn_kernel, q_off=q_off, n_keys=n_keys, k_sel=k_sel, group=group),
        out_shape=jax.ShapeDtypeStruct((b, t, dq), BF16),
        grid=(b, t // tq),
        in_specs=[qblk(dq), qblk(qi.shape[2]), qblk(LANES), kblk(2 * dkv),
                  pl.BlockSpec((1, nkb, N_KV_HEADS, V_AUG_ROWS, KEY_BLOCK), lambda bi, i: (bi, 0, 0, 0, 0)),
                  kblk(kib.shape[2]), _const_spec(bias_tiles.shape)],
        out_specs=qblk(dq),
        scratch_shapes=[
            pltpu.VMEM((nkb + 1 + COUNT_UNROLL, KEY_BLOCK, tq), F32),
            pltpu.VMEM((nkb // 2 + COUNT_UNROLL, KEY_BLOCK, tq), jnp.int32),
            pltpu.VMEM((4, KEY_BLOCK, IDX_HEADS * tq), F32),
            pltpu.VMEM((n_tiles, 2 * HEAD_DIM, hpt * tq), BF16),
            pltpu.VMEM((IDX_DIM, IDX_HEADS * tq), BF16),
            pltpu.VMEM((dq, tq), F32),
            pltpu.VMEM((2, n_tiles, 2 * KEY_BLOCK, hpt * tq), F32),
            pltpu.VMEM((2, n_tiles, 1, hpt * tq), F32),
        ] + [pltpu.VMEM((1, hpt * tq), F32)] * n_tiles
          + [pltpu.VMEM((V_AUG_ROWS, hpt * tq), F32)] * n_tiles,
        compiler_params=_params("parallel", "arbitrary"),
        name="sparse_attn",
    )(q, qi, wi, k2, vt, kib, bias_tiles)


def _mem_kv_kernel(x_ref, g_ref, w_ref, k_ref, v_ref, *, d):
    h = _rms(x_ref[...], g_ref[...]).astype(BF16)
    k_ref[...] = _dot(h, w_ref[:, :d])
    v_ref[...] = _dot(h, w_ref[:, d:])


def _mem_kv(mem2, g, w_kv, layer):
    m, d = mem2.shape
    tm = _row_tile(m, ROW_TILE)
    row = pl.BlockSpec((tm, d), lambda i: (i, 0))
    return pl.pallas_call(
        functools.partial(_mem_kv_kernel, d=d),
        out_shape=[jax.ShapeDtypeStruct((m, d), F32)] * 2,
        grid=(m // tm,),
        in_specs=[row, _const_spec((1, d)), _layer_spec(w_kv, layer)],
        out_specs=[row, row],
        compiler_params=_params("parallel"),
        name="mem_kv",
    )(mem2, g.reshape(1, d), w_kv)


def _mem_attn_kernel(*refs, hd, pending):
    if pending:
        x_ref, a_ref, wa_ref, g_ref, wq_ref, mk_ref, mv_ref, wo_ref, o_ref = refs
        x = x_ref[...] + _dot(a_ref[...], wa_ref[...])
    else:
        x_ref, g_ref, wq_ref, mk_ref, mv_ref, wo_ref, o_ref = refs
        x = x_ref[...]
    h = _rms(x, g_ref[...]).astype(BF16)
    q = (_dot(h, wq_ref[...]) * (hd ** -0.5)).astype(BF16)
    heads = []
    for a in range(MEM_HEADS):
        cols = slice(a * hd, (a + 1) * hd)
        lg = _dot_nt(q[:, cols], mk_ref[0, :, cols].astype(BF16))
        p = jnp.exp(lg - jnp.max(lg, axis=1, keepdims=True))
        p = (p / jnp.sum(p, axis=1, keepdims=True)).astype(BF16)
        heads.append(_dot(p, mv_ref[0, :, cols].astype(BF16)).astype(BF16))
    o_ref[...] = x + _dot(jnp.concatenate(heads, axis=1), wo_ref[...])


def _mem_attn(x2, g, w_q, mk, mv, w_o, layer, rows_per_batch, pending=None):
    m, d = x2.shape
    n_mem = mk.shape[1]
    tm = _row_tile(rows_per_batch, ROW_TILE)
    per = rows_per_batch // tm
    row = pl.BlockSpec((tm, d), lambda i: (i, 0))
    mem = pl.BlockSpec((1, n_mem, d), lambda i: (i // per, 0, 0))
    in_specs = [row, _const_spec((1, d)), _layer_spec(w_q, layer), mem, mem, _layer_spec(w_o, layer)]
    args = [x2, g.reshape(1, d), w_q, mk, mv, w_o]
    if pending is not None:
        a2, w_a, layer_a = pending
        in_specs[1:1] = [pl.BlockSpec((tm, a2.shape[1]), lambda i: (i, 0)), _layer_spec(w_a, layer_a)]
        args[1:1] = [a2, w_a]
    return pl.pallas_call(
        functools.partial(_mem_attn_kernel, hd=d // MEM_HEADS, pending=pending is not None),
        out_shape=jax.ShapeDtypeStruct((m, d), F32),
        grid=(m // tm,),
        in_specs=in_specs,
        out_specs=row,
        compiler_params=_params("parallel"),
        name="mem_attn",
    )(*args)


def _mlp_kernel(*refs, n_chunks, chunk, final):
    if final:
        x_ref, g_ref, w1_ref, w2_ref, gf_ref, o_ref = refs
    else:
        x_ref, g_ref, w1_ref, w2_ref, o_ref = refs
    x = x_ref[...]
    h = _rms(x, g_ref[...]).astype(BF16)
    acc = x
    for c in range(n_chunks):
        a = jnp.maximum(_dot(h, w1_ref[:, c * chunk:(c + 1) * chunk]), 0.0)
        acc = acc + _dot((a * a).astype(BF16), w2_ref[c * chunk:(c + 1) * chunk, :])
    if final:
        acc = _rms(acc, gf_ref[...])
    o_ref[...] = acc


def _mlp(x2, g, w1, w2, layer, g_final=None):
    m, d = x2.shape
    dff = w1.shape[2]
    chunk = min(dff, 1024)
    tm = _row_tile(m, ROW_TILE)
    row = pl.BlockSpec((tm, d), lambda i: (i, 0))
    final = g_final is not None
    in_specs = [row, _const_spec((1, d)), _layer_spec(w1, layer), _layer_spec(w2, layer)]
    args = [x2, g.reshape(1, d), w1, w2]
    if final:
        in_specs.append(_const_spec((1, d)))
        args.append(g_final.reshape(1, d))
    return pl.pallas_call(
        functools.partial(_mlp_kernel, n_chunks=dff // chunk, chunk=chunk, final=final),
        out_shape=jax.ShapeDtypeStruct((m, d), F32),
        grid=(m // tm,),
        in_specs=in_specs,
        out_specs=row,
        compiler_params=_params("parallel"),
        name="mlp",
    )(*args)


def _conv_glu_kernel(x_ref, g_ref, w_ref, b_ref, u_ref, *, d):
    h = _rms(x_ref[...], g_ref[...]).astype(BF16)
    a = _dot(h, w_ref[:, :d]) + b_ref[:, :d]
    gate = _dot(h, w_ref[:, d:]) + b_ref[:, d:]
    u_ref[...] = a * (1.0 / (1.0 + jnp.exp(-gate)))


def _conv_glu(x2, g, w_pw1, layer, b_pw1):
    m, d = x2.shape
    tm = _row_tile(m, ROW_TILE)
    row = pl.BlockSpec((tm, d), lambda i: (i, 0))
    return pl.pallas_call(
        functools.partial(_conv_glu_kernel, d=d),
        out_shape=jax.ShapeDtypeStruct((m, d), F32),
        grid=(m // tm,),
        in_specs=[row, _const_spec((1, d)), _layer_spec(w_pw1, layer), _const_spec((1, 2 * d))],
        out_specs=row,
        compiler_params=_params("parallel"),
        name="conv_glu",
    )(x2, g.reshape(1, d), w_pw1, b_pw1.reshape(1, 2 * d))


def _conv_rest_kernel(x_ref, u_ref, prev_ref, init_ref, wdw_ref, bdw_ref, lng_ref, lnb_ref,
                      w2_ref, b2_ref, o_ref, ext_ref, sh_ref, y_ref, *, tm, rc, lc):
    t = pl.program_id(1)
    d = u_ref.shape[2]
    pad = CONV_WIDTH - 1

    @pl.when(t == 0)
    def _():
        ext_ref[0:HALO, :] = init_ref[0]

    @pl.when(t > 0)
    def _():
        ext_ref[0:HALO, :] = prev_ref[0]

    ext_ref[HALO:HALO + tm, :] = u_ref[0]

    for s in range(SUBLANES):
        rows = tm + SUBLANES * ((CONV_WIDTH - 1 - s) // SUBLANES)
        sh_ref[s, 0:rows, :] = ext_ref[pl.ds(HALO - pad + s, rows), :]
    for r0 in range(0, tm, rc):
        for c0 in range(0, d, lc):
            cols = slice(c0, c0 + lc)
            y = jnp.broadcast_to(bdw_ref[:, cols], (rc, lc))
            for w in range(CONV_WIDTH):
                a, s = divmod(w, SUBLANES)
                y = y + sh_ref[s, r0 + SUBLANES * a:r0 + SUBLANES * a + rc, cols] * wdw_ref[w:w + 1, cols]
            y_ref[r0:r0 + rc, cols] = y

    y = y_ref[...]
    mu = jnp.mean(y, axis=-1, keepdims=True)
    yc = y - mu
    var = jnp.mean(yc * yc, axis=-1, keepdims=True)
    yn = yc * lax.rsqrt(var + EPS) * lng_ref[...] + lnb_ref[...]
    act = (yn * (1.0 / (1.0 + jnp.exp(-yn)))).astype(BF16)
    o_ref[0] = x_ref[0] + _dot(act, w2_ref[...]) + b2_ref[...]


def _conv_rest(x3, u3, init, w_dw, b_dw, ln_g, ln_b, w_pw2, layer, b_pw2):
    b, t, d = x3.shape
    tm = _row_tile(t, CONV_ROW_TILE)
    rc = min(tm, 64)
    lc = min(d, 256)
    assert tm % HALO == 0 and tm % rc == 0 and d % lc == 0
    per = tm // HALO
    tile = pl.BlockSpec((1, tm, d), lambda bi, ti: (bi, ti, 0))
    prev = pl.BlockSpec((1, HALO, d), lambda bi, ti: (bi, jnp.maximum(ti * per - 1, 0), 0))
    first = pl.BlockSpec((1, HALO, d), lambda bi, ti: (bi, 0, 0))
    vec = _const_spec((1, d))
    wdw = jnp.pad(w_dw, ((0, HALO - CONV_WIDTH), (0, 0)))
    return pl.pallas_call(
        functools.partial(_conv_rest_kernel, tm=tm, rc=rc, lc=lc),
        out_shape=jax.ShapeDtypeStruct((b, t, d), F32),
        grid=(b, t // tm),
        in_specs=[tile, tile, prev, first, _const_spec((HALO, d)), vec, vec, vec,
                  _layer_spec(w_pw2, layer), vec],
        out_specs=tile,
        scratch_shapes=[pltpu.VMEM((HALO + tm, d), F32),
                        pltpu.VMEM((SUBLANES, tm + HALO - SUBLANES, d), F32),
                        pltpu.VMEM((tm, d), F32)],
        compiler_params=_params("parallel", "arbitrary"),
        name="conv_rest",
    )(x3, u3, u3, init, wdw, b_dw.reshape(1, d), ln_g.reshape(1, d), ln_b.reshape(1, d),
      w_pw2, b_pw2.reshape(1, d))


def _mixer_attn(x3, g, w_in, layer, bias_tiles, cache=None):
    b, t, d = x3.shape
    x2 = x3.reshape(b * t, d)
    q, k, v, ki, kb, vb, kib, qi, wi = _attn_proj(x2, g, w_in, layer)
    r3 = lambda a: a.reshape(b, t, a.shape[-1])
    kb, vb, kib = r3(kb), r3(vb), r3(kib)
    past = 0
    if cache is not None:
        ck, cv, cki = cache
        past = ck.shape[1]
        kb = jnp.concatenate([ck.reshape(b, past, -1).astype(BF16), kb], axis=1)
        vb = jnp.concatenate([cv.reshape(b, past, -1).astype(BF16), vb], axis=1)
        kib = jnp.concatenate([cki.astype(BF16), kib], axis=1)
    n_keys = past + t
    k_sel = min(TOPK_MAX, n_keys // 4)
    tpad = -(-t // KEY_BLOCK) * KEY_BLOCK
    lp = max(-(-n_keys // KEY_BLOCK) * KEY_BLOCK, past + tpad)
    padt = lambda a, n: a if a.shape[1] == n else jnp.pad(a, ((0, 0), (0, n - a.shape[1]), (0, 0)))
    o = _sparse_attn(padt(r3(q), tpad), padt(r3(qi), tpad), padt(r3(wi), tpad),
                     padt(kb, lp), padt(vb, lp), padt(kib, lp), bias_tiles,
                     q_off=past, n_keys=n_keys, k_sel=k_sel)[:, :t]
    return (o.reshape(b * t, d), k.reshape(b, t, N_KV_HEADS, HEAD_DIM),
            v.reshape(b, t, N_KV_HEADS, HEAD_DIM), ki.reshape(b, t, IDX_DIM))


def _mixer_conv(x3, g, w_pw1, b_pw1, w_dw, b_dw, ln_g, ln_b, w_pw2, b_pw2, layer, state=None):
    b, t, d = x3.shape
    pad = CONV_WIDTH - 1
    u3 = _conv_glu(x3.reshape(b * t, d), g, w_pw1, layer, b_pw1).reshape(b, t, d)
    if state is None:
        init = jnp.zeros((b, HALO, d), F32)
        tail = u3[:, -pad:] if t >= pad else jnp.pad(u3, ((0, 0), (pad - t, 0), (0, 0)))
    else:
        init = jnp.pad(state.astype(F32), ((0, 0), (HALO - pad, 0), (0, 0)))
        tail = jnp.concatenate([state.astype(F32), u3], axis=1)[:, -pad:]
    x3 = _conv_rest(x3, u3, init, w_dw, b_dw, ln_g, ln_b, w_pw2, layer, b_pw2)
    return x3, tail


def kernel(x_prompt, x_sample, cache_attn_k, cache_attn_v, cache_attn_kidx, state_conv, cache_mem_k, cache_mem_v, mem_prompt, rel_bias, g_mix, w_in_attn, w_out_attn, w_pw1, b_pw1, w_dw, b_dw, ln_g, ln_b, w_pw2, b_pw2, g_mem_q, g_mem_src, w_mem_q, w_mem_kv, w_mem_o, g_mlp, w_mlp1, w_mlp2, g_final):
    depth = g_mix.shape[0]
    bp, tp, d = x_prompt.shape
    bs, ts, _ = x_sample.shape
    n_mem = mem_prompt.shape[1]
    mem_hd = d // MEM_HEADS
    bias_tiles = _bias_tiles(rel_bias)
    w_in_b, w_out_b = _attn_proj_weights(w_in_attn), w_out_attn.astype(BF16)
    w_pw1_b, w_pw2_b = w_pw1.astype(BF16), w_pw2.astype(BF16)
    w_mq_b, w_mkv_b, w_mo_b = w_mem_q.astype(BF16), w_mem_kv.astype(BF16), w_mem_o.astype(BF16)
    w_mlp1_b, w_mlp2_b = w_mlp1.astype(BF16), w_mlp2.astype(BF16)
    xp, xs = x_prompt, x_sample
    kp_l, vp_l, kip_l, ks_l, vs_l, kis_l = [], [], [], [], [], []
    convp_l, convs_l, memk_l, memv_l = [], [], [], []
    for i in range(depth):
        j = i // 2
        if i % 2 == 0:
            op, kp, vp, kip = _mixer_attn(xp, g_mix[i], w_in_b, j, bias_tiles)
            osm, ks, vs, kis = _mixer_attn(
                xs, g_mix[i], w_in_b, j, bias_tiles,
                cache=(cache_attn_k[j], cache_attn_v[j], cache_attn_kidx[j]))
            pend_p, pend_s = (op, w_out_b, j), (osm, w_out_b, j)
            kp_l.append(kp); vp_l.append(vp); kip_l.append(kip)
            ks_l.append(ks); vs_l.append(vs); kis_l.append(kis)
        else:
            cw = (w_pw1_b, b_pw1[j], w_dw[j], b_dw[j], ln_g[j], ln_b[j], w_pw2_b, b_pw2[j], j)
            xp, cp = _mixer_conv(xp, g_mix[i], *cw)
            xs, cs = _mixer_conv(xs, g_mix[i], *cw, state=state_conv[j])
            pend_p = pend_s = None
            convp_l.append(cp); convs_l.append(cs)
        mk, mv = _mem_kv(mem_prompt.reshape(bp * n_mem, d), g_mem_src[i], w_mkv_b, i)
        mk, mv = mk.reshape(bp, n_mem, d), mv.reshape(bp, n_mem, d)
        memk_l.append(mk.reshape(bp, n_mem, MEM_HEADS, mem_hd))
        memv_l.append(mv.reshape(bp, n_mem, MEM_HEADS, mem_hd))
        xp2 = _mem_attn(xp.reshape(bp * tp, d), g_mem_q[i], w_mq_b, mk, mv, w_mo_b, i, tp, pend_p)
        xs2 = _mem_attn(xs.reshape(bs * ts, d), g_mem_q[i], w_mq_b,
                        cache_mem_k[i].reshape(bs, n_mem, d), cache_mem_v[i].reshape(bs, n_mem, d),
                        w_mo_b, i, ts, pend_s)
        gf = g_final if i == depth - 1 else None
        xp = _mlp(xp2, g_mlp[i], w_mlp1_b, w_mlp2_b, i, gf).reshape(bp, tp, d)
        xs = _mlp(xs2, g_mlp[i], w_mlp1_b, w_mlp2_b, i, gf).reshape(bs, ts, d)
    return (xp, xs, jnp.stack(kp_l), jnp.stack(vp_l), jnp.stack(kip_l), jnp.stack(convp_l),
            jnp.stack(memk_l), jnp.stack(memv_l), jnp.stack(ks_l), jnp.stack(vs_l),
            jnp.stack(kis_l), jnp.stack(convs_l))
```

```python
import functools
import math

import jax
import jax.numpy as jnp
from jax import lax
from jax.experimental import pallas as pl
from jax.experimental.pallas import tpu as pltpu

CHUNK = 64
HEAD_DIM = 64
N_KV_HEADS = 4
IDX_HEADS = 8
IDX_DIM = 64
TOPK_MAX = 256
N_BUCKETS = 32
MAX_DISTANCE = 128
CONV_WIDTH = 31
MEM_HEADS = 4
EPS = 1e-6

LANES = 128
SUBLANES = 8
KEY_BLOCK = 128
HEADS_PER_TILE = 2
COUNT_UNROLL = 4
HI_BITS = 15
LO_BITS = 32 - HI_BITS
LOW_CHECK_EVERY = 4
V_AUG_ROWS = HEAD_DIM + 16
LOG2E = math.log2(math.e)
ROW_TILE = 1024
CONV_ROW_TILE = 512
HALO = 32
VMEM_LIMIT = 56 * 1024 * 1024

NEG_BIG = -1e30
F32 = jnp.float32
BF16 = jnp.bfloat16

KEY_NEG_INF = -2139095041
KEY_POS_INF = 2139095040


def _const_spec(shape):
    nd = len(shape)
    return pl.BlockSpec(shape, lambda *_: (0,) * nd, pipeline_mode=pl.Buffered(1))


def _layer_spec(stacked, layer):
    nd = stacked.ndim
    return pl.BlockSpec((None,) + stacked.shape[1:], lambda *_: (layer,) + (0,) * (nd - 1),
                        pipeline_mode=pl.Buffered(1))


def _params(*sem):
    return pltpu.CompilerParams(dimension_semantics=sem, vmem_limit_bytes=VMEM_LIMIT)


def _rms(x, g):
    ms = jnp.mean(x * x, axis=-1, keepdims=True)
    return x * lax.rsqrt(ms + EPS) * g


def _dot(a, b):
    return jnp.dot(a, b, preferred_element_type=F32)


def _dot_nt(a, b):
    return lax.dot_general(a, b, (((1,), (1,)), ((), ())), preferred_element_type=F32)


def _row_tile(m, pref):
    t = min(m, pref)
    assert m % t == 0, (m, t)
    return t


def _bias_tiles_kernel(tab_ref, bt_ref, *, n_heads):
    nb = N_BUCKETS // 2
    max_exact = nb // 2
    c = lax.broadcasted_iota(jnp.int32, (KEY_BLOCK, KEY_BLOCK), 0)
    r = lax.broadcasted_iota(jnp.int32, (KEY_BLOCK, KEY_BLOCK), 1)
    for d in range(2):
        rel = c - r - d * KEY_BLOCK
        n = jnp.abs(rel)
        nf = jnp.maximum(n, 1).astype(F32)
        large = max_exact + (jnp.log(nf / max_exact) / math.log(MAX_DISTANCE / max_exact)
                             * (nb - max_exact)).astype(jnp.int32)
        large = jnp.minimum(large, nb - 1)
        bucket = jnp.where(rel > 0, nb, 0) + jnp.where(n < max_exact, n, large)
        for h in range(n_heads):
            acc = jnp.zeros((KEY_BLOCK, KEY_BLOCK), F32)
            for b in range(N_BUCKETS):
                acc = jnp.where(bucket == b, tab_ref[b, h], acc)
            g = h % HEADS_PER_TILE
            rows = slice((1 - d) * KEY_BLOCK, (2 - d) * KEY_BLOCK)
            bt_ref[h // HEADS_PER_TILE, rows, g * KEY_BLOCK:(g + 1) * KEY_BLOCK] = (
                (acc - tab_ref[nb - 1, h]) * LOG2E)


def _bias_tiles(rel_bias):
    n_heads = rel_bias.shape[1]
    assert n_heads % HEADS_PER_TILE == 0
    return pl.pallas_call(
        functools.partial(_bias_tiles_kernel, n_heads=n_heads),
        out_shape=jax.ShapeDtypeStruct(
            (n_heads // HEADS_PER_TILE, 2 * KEY_BLOCK, HEADS_PER_TILE * KEY_BLOCK), F32),
        in_specs=[pl.BlockSpec(memory_space=pltpu.SMEM)],
        out_specs=pl.BlockSpec(memory_space=pltpu.VMEM),
        name="bias_tiles",
    )(rel_bias)


def _attn_proj_kernel(x_ref, g_ref, w_ref, q_ref, k_ref, v_ref, ki_ref, kb_ref, vb_ref,
                      kib_ref, qi_ref, wi_ref, *, dq, dkv, dqi):
    h = _rms(x_ref[...], g_ref[...]).astype(BF16)
    o = 0
    q_ref[...] = (_dot(h, w_ref[:, o:o + dq]) * (HEAD_DIM ** -0.5)).astype(BF16)
    o += dq
    k = _dot(h, w_ref[:, o:o + dkv])
    kb_ref[...] = k.astype(BF16)
    o += dkv
    v = _dot(h, w_ref[:, o:o + dkv])
    vb_ref[...] = v.astype(BF16)
    for n in range(N_KV_HEADS):
        k_ref[:, n, :] = k[:, n * HEAD_DIM:(n + 1) * HEAD_DIM]
        v_ref[:, n, :] = v[:, n * HEAD_DIM:(n + 1) * HEAD_DIM]
    o += dkv
    qi_ref[...] = (_dot(h, w_ref[:, o:o + dqi]) * (IDX_DIM ** -0.5)).astype(BF16)
    o += dqi
    ki = _dot(h, w_ref[:, o:o + LANES])[:, :IDX_DIM]
    ki_ref[...] = ki
    kib_ref[...] = ki.astype(BF16)
    o += LANES
    wi_ref[...] = _dot(h, w_ref[:, o:o + LANES]) * (IDX_HEADS ** -0.5)


def _attn_proj_weights(w_in):
    d = w_in.shape[1]
    base = d + 2 * N_KV_HEADS * HEAD_DIM + IDX_HEADS * IDX_DIM
    pad = lambda a: jnp.pad(a, ((0, 0), (0, 0), (0, LANES - a.shape[2])))
    return jnp.concatenate([w_in[:, :, :base], pad(w_in[:, :, base:base + IDX_DIM]),
                            pad(w_in[:, :, base + IDX_DIM:])], axis=2).astype(BF16)


def _attn_proj(x2, g, w, layer):
    m, d = x2.shape
    dq = d
    dkv = N_KV_HEADS * HEAD_DIM
    dqi = IDX_HEADS * IDX_DIM
    tm = _row_tile(m, ROW_TILE)
    row = lambda *n: pl.BlockSpec((tm,) + n, lambda i: (i,) + (0,) * len(n))
    heads = (N_KV_HEADS, HEAD_DIM)
    outs = [((dq,), BF16), (heads, F32), (heads, F32), ((IDX_DIM,), F32), ((dkv,), BF16), ((dkv,), BF16),
            ((IDX_DIM,), BF16), ((dqi,), BF16), ((LANES,), F32)]
    return pl.pallas_call(
        functools.partial(_attn_proj_kernel, dq=dq, dkv=dkv, dqi=dqi),
        out_shape=[jax.ShapeDtypeStruct((m,) + n, dt) for n, dt in outs],
        grid=(m // tm,),
        in_specs=[row(d), _const_spec((1, d)), _layer_spec(w, layer)],
        out_specs=[row(*n) for n, _ in outs],
        compiler_params=_params("parallel"),
        name="attn_proj",
    )(x2, g.reshape(1, d), w)


def _sparse_attn_kernel(q_ref, qi_ref, wi_ref, k_ref, vt_ref, ki_ref, bt_ref, o_ref,
                        s_ref, w_ref, dots_ref, qgt_ref, qit_ref, ot_ref, lg_ref, mb_ref, *state_refs,
                        q_off, n_keys, k_sel, group):
    tq = KEY_BLOCK
    hpt = HEADS_PER_TILE
    n_tiles = qgt_ref.shape[0]
    m_refs, acc_refs = state_refs[:n_tiles], state_refs[n_tiles:]
    i = pl.program_id(1)
    qs = q_off // KEY_BLOCK + i
    n_kb = qs + 1
    q_start = q_off + i * tq

    qt = q_ref[0].astype(F32).T * LOG2E
    for t in range(n_tiles):
        q2 = jnp.concatenate(
            [qt[(t * hpt + g) * HEAD_DIM:(t * hpt + g + 1) * HEAD_DIM, :] for g in range(hpt)], axis=1)
        q_hi = q2.astype(BF16)
        q_lo = (q2 - q_hi.astype(F32)).astype(BF16)
        qgt_ref[t] = jnp.concatenate([q_hi, q_lo], axis=0)
    qit = qi_ref[0].astype(F32).T
    qit_ref[...] = jnp.concatenate([qit[h * IDX_DIM:(h + 1) * IDX_DIM, :] for h in range(IDX_HEADS)],
                                   axis=1).astype(BF16)
    last_kv = k_ref.shape[1] // KEY_BLOCK - 1
    wit = wi_ref[0].T

    qpos = q_start + lax.broadcasted_iota(jnp.int32, (1, tq), 1)
    lim = jnp.minimum((qpos // CHUNK + 1) * CHUNK, n_keys)
    kidx = lax.broadcasted_iota(jnp.int32, (KEY_BLOCK, tq), 0)

    def key_to_f32(key):
        bits = key ^ ((key >> 31) & 0x7FFFFFFF)
        return lax.bitcast_convert_type(bits, F32)

    def digit(sc):
        bits = lax.bitcast_convert_type(jnp.where(sc == 0.0, 0.0, sc), jnp.int32)
        key = bits ^ ((bits >> 31) & 0x7FFFFFFF)
        return (key >> LO_BITS) + (1 << (HI_BITS - 1))

    def dots(j, slot):
        k0 = pl.multiple_of(jnp.minimum(j, last_kv) * KEY_BLOCK, KEY_BLOCK)
        dots_ref[slot] = _dot(ki_ref[0, pl.ds(k0, KEY_BLOCK), :], qit_ref[...])

    def finish(j, slot):
        acc = jnp.zeros((KEY_BLOCK, tq), F32)
        for h in range(IDX_HEADS):
            acc = acc + wit[h:h + 1, :] * jnp.maximum(dots_ref[slot, :, h * tq:(h + 1) * tq], 0.0)
        sc = jnp.where(kidx + j * KEY_BLOCK < lim, acc, -jnp.inf)
        s_ref[j] = sc
        return digit(sc)

    def finish_pair(j, slot):
        w_ref[j // 2] = (finish(j, slot) << 16) | finish(j + 1, slot + 1)

    def score_quad(quad, carry):
        j = 4 * quad
        dots(j + 2, 2)
        dots(j + 3, 3)
        finish_pair(j, 0)
        dots(j + 4, 0)
        dots(j + 5, 1)
        finish_pair(j + 2, 2)
        return carry

    n_pairs = (n_kb + 1) // 2
    n_quads = (n_kb + 3) // 4
    dots(0, 0)
    dots(1, 1)
    lax.fori_loop(0, n_quads, score_quad, 0)
    for u in range(COUNT_UNROLL):
        s_ref[2 * n_pairs + u] = jnp.full((KEY_BLOCK, tq), -jnp.inf, F32)
    for u in range(COUNT_UNROLL // 2):
        w_ref[2 * n_quads + u] = jnp.zeros((KEY_BLOCK, tq), jnp.int32)
    n_count = (2 * n_pairs + COUNT_UNROLL - 1) // COUNT_UNROLL

    def count_digits(cand):
        comp = (1 << HI_BITS) - cand
        cw = jnp.broadcast_to((comp << 16) | comp, (KEY_BLOCK, tq))

        def body(p, c):
            for u in range(COUNT_UNROLL // 2):
                c = c + (((w_ref[(COUNT_UNROLL // 2) * p + u] + cw) >> HI_BITS) & 0x00010001)
            return c
        c = lax.fori_loop(0, n_count, body, jnp.zeros((KEY_BLOCK, tq), jnp.int32))
        return jnp.sum(((c & 0xFFFF) + (c >> 16)).astype(F32), axis=0, keepdims=True)

    def count_ge(cand):
        cb = jnp.broadcast_to(cand, (KEY_BLOCK, tq))

        def body(p, c):
            for u in range(COUNT_UNROLL):
                c = c + jnp.where(s_ref[COUNT_UNROLL * p + u] >= cb, 1.0, 0.0)
            return c
        c = lax.fori_loop(0, n_count, body, jnp.zeros((KEY_BLOCK, tq), F32))
        return jnp.sum(c, axis=0, keepdims=True)

    def bisect_step(count_at, carry):
        lo, hi, n_lo = carry
        mid = (lo + hi) >> 1
        n_mid = count_at(mid)
        ok = n_mid >= k_sel
        return jnp.where(ok, mid, lo), jnp.where(ok, hi, mid), jnp.where(ok, n_mid, n_lo)

    full = lambda v: jnp.full((1, tq), v, jnp.int32)
    half = 1 << (HI_BITS - 1)
    dig, _, n_sel = lax.fori_loop(
        0, HI_BITS, lambda _, c: bisect_step(count_digits, c),
        (full(((KEY_NEG_INF + 1) >> LO_BITS) + half), full((KEY_POS_INF >> LO_BITS) + half + 1),
         jnp.full((1, tq), jnp.inf, F32)))
    key_hi = (dig - half) << LO_BITS

    count_lo = lambda v: count_ge(key_to_f32(key_hi + v))

    def low_steps(carry):
        step, c = carry[0], carry[1:]
        for _ in range(LOW_CHECK_EVERY):
            c = bisect_step(count_lo, c)
        return (step + LOW_CHECK_EVERY,) + c

    def low_unfinished(carry):
        step, n_lo = carry[0], carry[3]
        return (step < LO_BITS) & (jnp.max(jnp.abs(n_lo - k_sel)) > 0.0)

    _, key_lo, _, n_sel = lax.while_loop(low_unfinished, low_steps, (0, full(0), full(1 << LO_BITS), n_sel))
    thr = key_to_f32(key_hi + key_lo)

    def count(cand, strict):
        cb = jnp.broadcast_to(cand, (KEY_BLOCK, tq))
        hit = (lambda s: s > cb) if strict else (lambda s: s >= cb)

        def body(p, c):
            c = c + jnp.where(hit(s_ref[2 * p]), 1.0, 0.0)
            return c + jnp.where(hit(s_ref[2 * p + 1]), 1.0, 0.0)
        c = lax.fori_loop(0, n_pairs, body, jnp.zeros((KEY_BLOCK, tq), F32))
        return jnp.sum(c, axis=0, keepdims=True)

    surplus = jnp.where(n_sel < jnp.inf, n_sel - k_sel, 0.0)

    @pl.when(jnp.max(surplus) > 0.0)
    def _():
        n_ties = k_sel - count(thr, True)
        row_i = lax.broadcasted_iota(jnp.int32, (KEY_BLOCK, KEY_BLOCK), 0)
        col_i = lax.broadcasted_iota(jnp.int32, (KEY_BLOCK, KEY_BLOCK), 1)
        lower = jnp.where(col_i < row_i, 1.0, 0.0).astype(BF16)
        ones = jnp.ones((KEY_BLOCK, KEY_BLOCK), BF16)

        def tie_body(j, seen):
            s = s_ref[j]
            eq = s == thr
            e = jnp.where(eq, 1.0, 0.0).astype(BF16)
            before = _dot(lower, e) + seen
            s_ref[j] = jnp.where(eq & (before >= n_ties), -jnp.inf, s)
            return seen + _dot(ones, e)

        lax.fori_loop(0, n_kb, tie_body, jnp.zeros((KEY_BLOCK, tq), F32))

    thr_sel = jnp.broadcast_to(jnp.maximum(thr, jnp.finfo(F32).min), (KEY_BLOCK, tq))

    for t in range(n_tiles):
        m_refs[t][...] = jnp.full(m_refs[t].shape, NEG_BIG, F32)
        acc_refs[t][...] = jnp.zeros(acc_refs[t].shape, F32)

    def far_blocks(db):
        out = []
        for j in (2 * db, 2 * db + 1):
            js = jnp.where(j < qs - 1, j, n_kb)
            out.append((js, jnp.minimum(js, last_kv)))
        return out

    near = [(jnp.where(qs >= 1, qs - 1, n_kb), jnp.maximum(qs - 1, 0)), (qs, qs)]

    def mask_of(blocks):
        m = jnp.concatenate([jnp.where(s_ref[js] >= thr_sel, 0.0, NEG_BIG) for js, _ in blocks], axis=0)
        return jnp.concatenate([m] * hpt, axis=1)

    def logits(blocks, mask, biased, slot, t):
        n = (t * hpt) // group
        kn = jnp.concatenate(
            [k_ref[0, pl.ds(pl.multiple_of(jk * KEY_BLOCK, KEY_BLOCK), KEY_BLOCK),
                   n * 2 * HEAD_DIM:(n + 1) * 2 * HEAD_DIM] for _, jk in blocks], axis=0)
        lg = _dot(kn, qgt_ref[t]) + mask
        if biased:
            lg = lg + bt_ref[t]
        lg_ref[slot, t] = lg
        mb_ref[slot, t] = jnp.max(lg, axis=0, keepdims=True)

    def softmax_update(blocks, slot, t):
        n = (t * hpt) // group
        m_old = m_refs[t][...]
        m_new = jnp.maximum(m_old, mb_ref[slot, t])
        alpha = jnp.exp2(m_old - m_new)
        p = jnp.exp2(lg_ref[slot, t] - m_new).astype(BF16)
        m_refs[t][...] = m_new
        vtn = jnp.concatenate([vt_ref[0, jk, n] for _, jk in blocks], axis=1)
        acc_refs[t][...] = alpha * acc_refs[t][...] + _dot(vtn, p)

    n_far = qs // 2
    mask_near = mask_of(near)
    for t in range(n_tiles):
        logits(near, mask_near, True, 0, t)

    def trip_body(trip, carry):
        d0 = 2 * trip
        blk1, blk2 = far_blocks(d0), far_blocks(d0 + 1)
        mask1, mask2 = mask_of(blk1), mask_of(blk2)
        far0 = far_blocks(d0 - 1)
        blk0 = [tuple(jnp.where(d0 == 0, a, b) for a, b in zip(near[h], far0[h])) for h in range(2)]
        for t in range(n_tiles):
            logits(blk1, mask1, False, 1, t)
            softmax_update(blk0, 0, t)
        for t in range(n_tiles):
            logits(blk2, mask2, False, 0, t)
            softmax_update(blk1, 1, t)
        return carry

    lax.fori_loop(0, (n_far + 2) // 2, trip_body, 0)

    for t in range(n_tiles):
        on = acc_refs[t][0:HEAD_DIM, :] / acc_refs[t][HEAD_DIM:HEAD_DIM + 1, :]
        for g in range(hpt):
            hh = t * hpt + g
            ot_ref[hh * HEAD_DIM:(hh + 1) * HEAD_DIM, :] = on[:, g * tq:(g + 1) * tq]
    o_ref[0] = ot_ref[...].T.astype(BF16)


def _sparse_attn(q, qi, wi, kb, vb, kib, bias_tiles, *, q_off, n_keys, k_sel):
    b, t, dq = q.shape
    lp = kb.shape[1]
    dkv = kb.shape[2]
    tq = KEY_BLOCK
    n_heads = dq // HEAD_DIM
    group = n_heads // N_KV_HEADS
    hpt = HEADS_PER_TILE
    n_tiles = n_heads // hpt
    assert group % hpt == 0
    nkb = lp // KEY_BLOCK
    assert t % tq == 0 and lp % KEY_BLOCK == 0 and q_off % KEY_BLOCK == 0 and q_off + t <= lp
    vt = jnp.swapaxes(vb.reshape(b, nkb, KEY_BLOCK, dkv), 2, 3).reshape(
        b, nkb, N_KV_HEADS, HEAD_DIM, KEY_BLOCK)
    vt = jnp.concatenate(
        [vt, jnp.ones((b, nkb, N_KV_HEADS, 1, KEY_BLOCK), BF16),
         jnp.zeros((b, nkb, N_KV_HEADS, V_AUG_ROWS - HEAD_DIM - 1, KEY_BLOCK), BF16)], axis=3)
    k2 = jnp.concatenate([kb[:, :, n * HEAD_DIM:(n + 1) * HEAD_DIM]
                          for n in range(N_KV_HEADS) for _ in range(2)], axis=-1)
    qblk = lambda n: pl.BlockSpec((1, tq, n), lambda bi, i: (bi, i, 0))
    kblk = lambda n: pl.BlockSpec((1, lp, n), lambda bi, i: (bi, 0, 0))
    return pl.pallas_call(
        functools.partial(_sparse_attn_kernel, q_off=q_off, n_keys=n_keys, k_sel=k_sel, group=group),
        out_shape=jax.ShapeDtypeStruct((b, t, dq), BF16),
        grid=(b, t // tq),
        in_specs=[qblk(dq), qblk(qi.shape[2]), qblk(LANES), kblk(2 * dkv),
                  pl.BlockSpec((1, nkb, N_KV_HEADS, V_AUG_ROWS, KEY_BLOCK), lambda bi, i: (bi, 0, 0, 0, 0)),
                  kblk(kib.shape[2]), _const_spec(bias_tiles.shape)],
        out_specs=qblk(dq),
        scratch_shapes=[
            pltpu.VMEM((nkb + 1 + COUNT_UNROLL, KEY_BLOCK, tq), F32),
            pltpu.VMEM((nkb // 2 + COUNT_UNROLL, KEY_BLOCK, tq), jnp.int32),
            pltpu.VMEM((4, KEY_BLOCK, IDX_HEADS * tq), F32),
            pltpu.VMEM((n_tiles, 2 * HEAD_DIM, hpt * tq), BF16),
            pltpu.VMEM((IDX_DIM, IDX_HEADS * tq), BF16),
            pltpu.VMEM((dq, tq), F32),
            pltpu.VMEM((2, n_tiles, 2 * KEY_BLOCK, hpt * tq), F32),
            pltpu.VMEM((2, n_tiles, 1, hpt * tq), F32),
        ] + [pltpu.VMEM((1, hpt * tq), F32)] * n_tiles
          + [pltpu.VMEM((V_AUG_ROWS, hpt * tq), F32)] * n_tiles,
        compiler_params=_params("parallel", "arbitrary"),
        name="sparse_attn",
    )(q, qi, wi, k2, vt, kib, bias_tiles)


def _mem_kv_kernel(x_ref, g_ref, w_ref, k_ref, v_ref, *, d):
    h = _rms(x_ref[...], g_ref[...]).astype(BF16)
    k_ref[...] = _dot(h, w_ref[:, :d])
    v_ref[...] = _dot(h, w_ref[:, d:])


def _mem_kv(mem2, g, w_kv, layer):
    m, d = mem2.shape
    tm = _row_tile(m, ROW_TILE)
    row = pl.BlockSpec((tm, d), lambda i: (i, 0))
    return pl.pallas_call(
        functools.partial(_mem_kv_kernel, d=d),
        out_shape=[jax.ShapeDtypeStruct((m, d), F32)] * 2,
        grid=(m // tm,),
        in_specs=[row, _const_spec((1, d)), _layer_spec(w_kv, layer)],
        out_specs=[row, row],
        compiler_params=_params("parallel"),
        name="mem_kv",
    )(mem2, g.reshape(1, d), w_kv)


def _mem_attn_kernel(*refs, hd, pending):
    if pending:
        x_ref, a_ref, wa_ref, g_ref, wq_ref, mk_ref, mv_ref, wo_ref, o_ref = refs
        x = x_ref[...] + _dot(a_ref[...], wa_ref[...])
    else:
        x_ref, g_ref, wq_ref, mk_ref, mv_ref, wo_ref, o_ref = refs
        x = x_ref[...]
    h = _rms(x, g_ref[...]).astype(BF16)
    q = (_dot(h, wq_ref[...]) * (hd ** -0.5)).astype(BF16)
    heads = []
    for a in range(MEM_HEADS):
        cols = slice(a * hd, (a + 1) * hd)
        lg = _dot_nt(q[:, cols], mk_ref[0, :, cols].astype(BF16))
        p = jnp.exp(lg - jnp.max(lg, axis=1, keepdims=True))
        p = (p / jnp.sum(p, axis=1, keepdims=True)).astype(BF16)
        heads.append(_dot(p, mv_ref[0, :, cols].astype(BF16)).astype(BF16))
    o_ref[...] = x + _dot(jnp.concatenate(heads, axis=1), wo_ref[...])


def _mem_attn(x2, g, w_q, mk, mv, w_o, layer, rows_per_batch, pending=None):
    m, d = x2.shape
    n_mem = mk.shape[1]
    tm = _row_tile(rows_per_batch, ROW_TILE)
    per = rows_per_batch // tm
    row = pl.BlockSpec((tm, d), lambda i: (i, 0))
    mem = pl.BlockSpec((1, n_mem, d), lambda i: (i // per, 0, 0))
    in_specs = [row, _const_spec((1, d)), _layer_spec(w_q, layer), mem, mem, _layer_spec(w_o, layer)]
    args = [x2, g.reshape(1, d), w_q, mk, mv, w_o]
    if pending is not None:
        a2, w_a, layer_a = pending
        in_specs[1:1] = [pl.BlockSpec((tm, a2.shape[1]), lambda i: (i, 0)), _layer_spec(w_a, layer_a)]
        args[1:1] = [a2, w_a]
    return pl.pallas_call(
        functools.partial(_mem_attn_kernel, hd=d // MEM_HEADS, pending=pending is not None),
        out_shape=jax.ShapeDtypeStruct((m, d), F32),
        grid=(m // tm,),
        in_specs=in_specs,
        out_specs=row,
        compiler_params=_params("parallel"),
        name="mem_attn",
    )(*args)


def _mlp_kernel(*refs, n_chunks, chunk, final):
    if final:
        x_ref, g_ref, w1_ref, w2_ref, gf_ref, o_ref = refs
    else:
        x_ref, g_ref, w1_ref, w2_ref, o_ref = refs
    x = x_ref[...]
    h = _rms(x, g_ref[...]).astype(BF16)
    acc = x
    for c in range(n_chunks):
        a = jnp.maximum(_dot(h, w1_ref[:, c * chunk:(c + 1) * chunk]), 0.0)
        acc = acc + _dot((a * a).astype(BF16), w2_ref[c * chunk:(c + 1) * chunk, :])
    if final:
        acc = _rms(acc, gf_ref[...])
    o_ref[...] = acc


def _mlp(x2, g, w1, w2, layer, g_final=None):
    m, d = x2.shape
    dff = w1.shape[2]
    chunk = min(dff, 1024)
    tm = _row_tile(m, ROW_TILE)
    row = pl.BlockSpec((tm, d), lambda i: (i, 0))
    final = g_final is not None
    in_specs = [row, _const_spec((1, d)), _layer_spec(w1, layer), _layer_spec(w2, layer)]
    args = [x2, g.reshape(1, d), w1, w2]
    if final:
        in_specs.append(_const_spec((1, d)))
        args.append(g_final.reshape(1, d))
    return pl.pallas_call(
        functools.partial(_mlp_kernel, n_chunks=dff // chunk, chunk=chunk, final=final),
        out_shape=jax.ShapeDtypeStruct((m, d), F32),
        grid=(m // tm,),
        in_specs=in_specs,
        out_specs=row,
        compiler_params=_params("parallel"),
        name="mlp",
    )(*args)


def _conv_glu_kernel(x_ref, g_ref, w_ref, b_ref, u_ref, *, d):
    h = _rms(x_ref[...], g_ref[...]).astype(BF16)
    a = _dot(h, w_ref[:, :d]) + b_ref[:, :d]
    gate = _dot(h, w_ref[:, d:]) + b_ref[:, d:]
    u_ref[...] = a * (1.0 / (1.0 + jnp.exp(-gate)))


def _conv_glu(x2, g, w_pw1, layer, b_pw1):
    m, d = x2.shape
    tm = _row_tile(m, ROW_TILE)
    row = pl.BlockSpec((tm, d), lambda i: (i, 0))
    return pl.pallas_call(
        functools.partial(_conv_glu_kernel, d=d),
        out_shape=jax.ShapeDtypeStruct((m, d), F32),
        grid=(m // tm,),
        in_specs=[row, _const_spec((1, d)), _layer_spec(w_pw1, layer), _const_spec((1, 2 * d))],
        out_specs=row,
        compiler_params=_params("parallel"),
        name="conv_glu",
    )(x2, g.reshape(1, d), w_pw1, b_pw1.reshape(1, 2 * d))


def _conv_rest_kernel(x_ref, u_ref, prev_ref, init_ref, wdw_ref, bdw_ref, lng_ref, lnb_ref,
                      w2_ref, b2_ref, o_ref, ext_ref, sh_ref, y_ref, *, tm, rc, lc):
    t = pl.program_id(1)
    d = u_ref.shape[2]
    pad = CONV_WIDTH - 1

    @pl.when(t == 0)
    def _():
        ext_ref[0:HALO, :] = init_ref[0]

    @pl.when(t > 0)
    def _():
        ext_ref[0:HALO, :] = prev_ref[0]

    ext_ref[HALO:HALO + tm, :] = u_ref[0]

    for s in range(SUBLANES):
        rows = tm + SUBLANES * ((CONV_WIDTH - 1 - s) // SUBLANES)
        sh_ref[s, 0:rows, :] = ext_ref[pl.ds(HALO - pad + s, rows), :]
    for r0 in range(0, tm, rc):
        for c0 in range(0, d, lc):
            cols = slice(c0, c0 + lc)
            y = jnp.broadcast_to(bdw_ref[:, cols], (rc, lc))
            for w in range(CONV_WIDTH):
                a, s = divmod(w, SUBLANES)
                y = y + sh_ref[s, r0 + SUBLANES * a:r0 + SUBLANES * a + rc, cols] * wdw_ref[w:w + 1, cols]
            y_ref[r0:r0 + rc, cols] = y

    y = y_ref[...]
    mu = jnp.mean(y, axis=-1, keepdims=True)
    yc = y - mu
    var = jnp.mean(yc * yc, axis=-1, keepdims=True)
    yn = yc * lax.rsqrt(var + EPS) * lng_ref[...] + lnb_ref[...]
    act = (yn * (1.0 / (1.0 + jnp.exp(-yn)))).astype(BF16)
    o_ref[0] = x_ref[0] + _dot(act, w2_ref[...]) + b2_ref[...]


def _conv_rest(x3, u3, init, w_dw, b_dw, ln_g, ln_b, w_pw2, layer, b_pw2):
    b, t, d = x3.shape
    tm = _row_tile(t, CONV_ROW_TILE)
    rc = min(tm, 64)
    lc = min(d, 256)
    assert tm % HALO == 0 and tm % rc == 0 and d % lc == 0
    per = tm // HALO
    tile = pl.BlockSpec((1, tm, d), lambda bi, ti: (bi, ti, 0))
    prev = pl.BlockSpec((1, HALO, d), lambda bi, ti: (bi, jnp.maximum(ti * per - 1, 0), 0))
    first = pl.BlockSpec((1, HALO, d), lambda bi, ti: (bi, 0, 0))
    vec = _const_spec((1, d))
    wdw = jnp.pad(w_dw, ((0, HALO - CONV_WIDTH), (0, 0)))
    return pl.pallas_call(
        functools.partial(_conv_rest_kernel, tm=tm, rc=rc, lc=lc),
        out_shape=jax.ShapeDtypeStruct((b, t, d), F32),
        grid=(b, t // tm),
        in_specs=[tile, tile, prev, first, _const_spec((HALO, d)), vec, vec, vec,
                  _layer_spec(w_pw2, layer), vec],
        out_specs=tile,
        scratch_shapes=[pltpu.VMEM((HALO + tm, d), F32),
                        pltpu.VMEM((SUBLANES, tm + HALO - SUBLANES, d), F32),
                        pltpu.VMEM((tm, d), F32)],
        compiler_params=_params("parallel", "arbitrary"),
        name="conv_rest",
    )(x3, u3, u3, init, wdw, b_dw.reshape(1, d), ln_g.reshape(1, d), ln_b.reshape(1, d),
      w_pw2, b_pw2.reshape(1, d))


def _mixer_attn(x3, g, w_in, layer, bias_tiles, cache=None):
    b, t, d = x3.shape
    x2 = x3.reshape(b * t, d)
    q, k, v, ki, kb, vb, kib, qi, wi = _attn_proj(x2, g, w_in, layer)
    r3 = lambda a: a.reshape(b, t, a.shape[-1])
    kb, vb, kib = r3(kb), r3(vb), r3(kib)
    past = 0
    if cache is not None:
        ck, cv, cki = cache
        past = ck.shape[1]
        kb = jnp.concatenate([ck.reshape(b, past, -1).astype(BF16), kb], axis=1)
        vb = jnp.concatenate([cv.reshape(b, past, -1).astype(BF16), vb], axis=1)
        kib = jnp.concatenate([cki.astype(BF16), kib], axis=1)
    n_keys = past + t
    k_sel = min(TOPK_MAX, n_keys // 4)
    tpad = -(-t // KEY_BLOCK) * KEY_BLOCK
    lp = max(-(-n_keys // KEY_BLOCK) * KEY_BLOCK, past + tpad)
    padt = lambda a, n: a if a.shape[1] == n else jnp.pad(a, ((0, 0), (0, n - a.shape[1]), (0, 0)))
    o = _sparse_attn(padt(r3(q), tpad), padt(r3(qi), tpad), padt(r3(wi), tpad),
                     padt(kb, lp), padt(vb, lp), padt(kib, lp), bias_tiles,
                     q_off=past, n_keys=n_keys, k_sel=k_sel)[:, :t]
    return (o.reshape(b * t, d), k.reshape(b, t, N_KV_HEADS, HEAD_DIM),
            v.reshape(b, t, N_KV_HEADS, HEAD_DIM), ki.reshape(b, t, IDX_DIM))


def _mixer_conv(x3, g, w_pw1, b_pw1, w_dw, b_dw, ln_g, ln_b, w_pw2, b_pw2, layer, state=None):
    b, t, d = x3.shape
    pad = CONV_WIDTH - 1
    u3 = _conv_glu(x3.reshape(b * t, d), g, w_pw1, layer, b_pw1).reshape(b, t, d)
    if state is None:
        init = jnp.zeros((b, HALO, d), F32)
        tail = u3[:, -pad:] if t >= pad else jnp.pad(u3, ((0, 0), (pad - t, 0), (0, 0)))
    else:
        init = jnp.pad(state.astype(F32), ((0, 0), (HALO - pad, 0), (0, 0)))
        tail = jnp.concatenate([state.astype(F32), u3], axis=1)[:, -pad:]
    x3 = _conv_rest(x3, u3, init, w_dw, b_dw, ln_g, ln_b, w_pw2, layer, b_pw2)
    return x3, tail


def kernel(x_prompt, x_sample, cache_attn_k, cache_attn_v, cache_attn_kidx, state_conv, cache_mem_k, cache_mem_v, mem_prompt, rel_bias, g_mix, w_in_attn, w_out_attn, w_pw1, b_pw1, w_dw, b_dw, ln_g, ln_b, w_pw2, b_pw2, g_mem_q, g_mem_src, w_mem_q, w_mem_kv, w_mem_o, g_mlp, w_mlp1, w_mlp2, g_final):
    depth = g_mix.shape[0]
    bp, tp, d = x_prompt.shape
    bs, ts, _ = x_sample.shape
    n_mem = mem_prompt.shape[1]
    mem_hd = d // MEM_HEADS
    bias_tiles = _bias_tiles(rel_bias)
    w_in_b, w_out_b = _attn_proj_weights(w_in_attn), w_out_attn.astype(BF16)
    w_pw1_b, w_pw2_b = w_pw1.astype(BF16), w_pw2.astype(BF16)
    w_mq_b, w_mkv_b, w_mo_b = w_mem_q.astype(BF16), w_mem_kv.astype(BF16), w_mem_o.astype(BF16)
    w_mlp1_b, w_mlp2_b = w_mlp1.astype(BF16), w_mlp2.astype(BF16)
    xp, xs = x_prompt, x_sample
    kp_l, vp_l, kip_l, ks_l, vs_l, kis_l = [], [], [], [], [], []
    convp_l, convs_l, memk_l, memv_l = [], [], [], []
    for i in range(depth):
        j = i // 2
        if i % 2 == 0:
            op, kp, vp, kip = _mixer_attn(xp, g_mix[i], w_in_b, j, bias_tiles)
            osm, ks, vs, kis = _mixer_attn(
                xs, g_mix[i], w_in_b, j, bias_tiles,
                cache=(cache_attn_k[j], cache_attn_v[j], cache_attn_kidx[j]))
            pend_p, pend_s = (op, w_out_b, j), (osm, w_out_b, j)
            kp_l.append(kp); vp_l.append(vp); kip_l.append(kip)
            ks_l.append(ks); vs_l.append(vs); kis_l.append(kis)
        else:
            cw = (w_pw1_b, b_pw1[j], w_dw[j], b_dw[j], ln_g[j], ln_b[j], w_pw2_b, b_pw2[j], j)
            xp, cp = _mixer_conv(xp, g_mix[i], *cw)
            xs, cs = _mixer_conv(xs, g_mix[i], *cw, state=state_conv[j])
            pend_p = pend_s = None
            convp_l.append(cp); convs_l.append(cs)
        mk, mv = _mem_kv(mem_prompt.reshape(bp * n_mem, d), g_mem_src[i], w_mkv_b, i)
        mk, mv = mk.reshape(bp, n_mem, d), mv.reshape(bp, n_mem, d)
        memk_l.append(mk.reshape(bp, n_mem, MEM_HEADS, mem_hd))
        memv_l.append(mv.reshape(bp, n_mem, MEM_HEADS, mem_hd))
        xp2 = _mem_attn(xp.reshape(bp * tp, d), g_mem_q[i], w_mq_b, mk, mv, w_mo_b, i, tp, pend_p)
        xs2 = _mem_attn(xs.reshape(bs * ts, d), g_mem_q[i], w_mq_b,
                        cache_mem_k[i].reshape(bs, n_mem, d), cache_mem_v[i].reshape(bs, n_mem, d),
                        w_mo_b, i, ts, pend_s)
        gf = g_final if i == depth - 1 else None
        xp = _mlp(xp2, g_mlp[i], w_mlp1_b, w_mlp2_b, i, gf).reshape(bp, tp, d)
        xs = _mlp(xs2, g_mlp[i], w_mlp1_b, w_mlp2_b, i, gf).reshape(bs, ts, d)
    return (xp, xs, jnp.stack(kp_l), jnp.stack(vp_l), jnp.stack(kip_l), jnp.stack(convp_l),
            jnp.stack(memk_l), jnp.stack(memv_l), jnp.stack(ks_l), jnp.stack(vs_l),
            jnp.stack(kis_l), jnp.stack(convs_l))
```

```python
import functools
import math

import jax
import jax.numpy as jnp
from jax import lax
from jax.experimental import pallas as pl
from jax.experimental.pallas import tpu as pltpu

CHUNK = 64
HEAD_DIM = 64
N_KV_HEADS = 4
IDX_HEADS = 8
IDX_DIM = 64
TOPK_MAX = 256
N_BUCKETS = 32
MAX_DISTANCE = 128
CONV_WIDTH = 31
MEM_HEADS = 4
EPS = 1e-6

LANES = 128
SUBLANES = 8
KEY_BLOCK = 128
HEADS_PER_TILE = 2
COUNT_UNROLL = 4
HI_BITS = 15
LO_BITS = 32 - HI_BITS
LOW_CHECK_EVERY = 4
V_AUG_ROWS = HEAD_DIM + 16
LOG2E = math.log2(math.e)
ROW_TILE = 1024
CONV_ROW_TILE = 512
HALO = 32
VMEM_LIMIT = 56 * 1024 * 1024

NEG_BIG = -1e30
F32 = jnp.float32
BF16 = jnp.bfloat16

KEY_NEG_INF = -2139095041
KEY_POS_INF = 2139095040


def _const_spec(shape):
    nd = len(shape)
    return pl.BlockSpec(shape, lambda *_: (0,) * nd, pipeline_mode=pl.Buffered(1))


def _layer_spec(stacked, layer):
    nd = stacked.ndim
    return pl.BlockSpec((None,) + stacked.shape[1:], lambda *_: (layer,) + (0,) * (nd - 1),
                        pipeline_mode=pl.Buffered(1))


def _params(*sem):
    return pltpu.CompilerParams(dimension_semantics=sem, vmem_limit_bytes=VMEM_LIMIT)


def _rms(x, g):
    ms = jnp.mean(x * x, axis=-1, keepdims=True)
    return x * lax.rsqrt(ms + EPS) * g


def _dot(a, b):
    return jnp.dot(a, b, preferred_element_type=F32)


def _dot_nt(a, b):
    return lax.dot_general(a, b, (((1,), (1,)), ((), ())), preferred_element_type=F32)


def _row_tile(m, pref):
    t = min(m, pref)
    assert m % t == 0, (m, t)
    return t


def _bias_tiles_kernel(tab_ref, bt_ref, *, n_heads):
    nb = N_BUCKETS // 2
    max_exact = nb // 2
    c = lax.broadcasted_iota(jnp.int32, (KEY_BLOCK, KEY_BLOCK), 0)
    r = lax.broadcasted_iota(jnp.int32, (KEY_BLOCK, KEY_BLOCK), 1)
    for d in range(2):
        rel = c - r - d * KEY_BLOCK
        n = jnp.abs(rel)
        nf = jnp.maximum(n, 1).astype(F32)
        large = max_exact + (jnp.log(nf / max_exact) / math.log(MAX_DISTANCE / max_exact)
                             * (nb - max_exact)).astype(jnp.int32)
        large = jnp.minimum(large, nb - 1)
        bucket = jnp.where(rel > 0, nb, 0) + jnp.where(n < max_exact, n, large)
        for h in range(n_heads):
            acc = jnp.zeros((KEY_BLOCK, KEY_BLOCK), F32)
            for b in range(N_BUCKETS):
                acc = jnp.where(bucket == b, tab_ref[b, h], acc)
            g = h % HEADS_PER_TILE
            rows = slice((1 - d) * KEY_BLOCK, (2 - d) * KEY_BLOCK)
            bt_ref[h // HEADS_PER_TILE, rows, g * KEY_BLOCK:(g + 1) * KEY_BLOCK] = (
                (acc - tab_ref[nb - 1, h]) * LOG2E)


def _bias_tiles(rel_bias):
    n_heads = rel_bias.shape[1]
    assert n_heads % HEADS_PER_TILE == 0
    return pl.pallas_call(
        functools.partial(_bias_tiles_kernel, n_heads=n_heads),
        out_shape=jax.ShapeDtypeStruct(
            (n_heads // HEADS_PER_TILE, 2 * KEY_BLOCK, HEADS_PER_TILE * KEY_BLOCK), F32),
        in_specs=[pl.BlockSpec(memory_space=pltpu.SMEM)],
        out_specs=pl.BlockSpec(memory_space=pltpu.VMEM),
        name="bias_tiles",
    )(rel_bias)


def _attn_proj_kernel(x_ref, g_ref, w_ref, q_ref, k_ref, v_ref, ki_ref, kb_ref, vb_ref,
                      kib_ref, qi_ref, wi_ref, *, dq, dkv, dqi):
    h = _rms(x_ref[...], g_ref[...]).astype(BF16)
    o = 0
    q_ref[...] = (_dot(h, w_ref[:, o:o + dq]) * (HEAD_DIM ** -0.5)).astype(BF16)
    o += dq
    k = _dot(h, w_ref[:, o:o + dkv])
    o += dkv
    v = _dot(h, w_ref[:, o:o + dkv])
    vb_ref[...] = v.astype(BF16)
    for n in range(N_KV_HEADS):
        k_ref[:, n, :] = k[:, n * HEAD_DIM:(n + 1) * HEAD_DIM]
        v_ref[:, n, :] = v[:, n * HEAD_DIM:(n + 1) * HEAD_DIM]
    o += dkv
    qi_ref[...] = (_dot(h, w_ref[:, o:o + dqi]) * (IDX_DIM ** -0.5)).astype(BF16)
    o += dqi
    ki = _dot(h, w_ref[:, o:o + LANES])[:, :IDX_DIM]
    ki_ref[...] = ki
    kib_ref[...] = ki.astype(BF16)
    o += LANES
    wi_ref[...] = _dot(h, w_ref[:, o:o + LANES]) * (IDX_HEADS ** -0.5)
    o += LANES
    kb_ref[...] = _dot(h, w_ref[:, o:o + 2 * dkv]).astype(BF16)


def _attn_proj_weights(w_in):
    d = w_in.shape[1]
    base = d + 2 * N_KV_HEADS * HEAD_DIM + IDX_HEADS * IDX_DIM
    pad = lambda a: jnp.pad(a, ((0, 0), (0, 0), (0, LANES - a.shape[2])))
    w_k = [w_in[:, :, d + n * HEAD_DIM:d + (n + 1) * HEAD_DIM] for n in range(N_KV_HEADS)]
    return jnp.concatenate([w_in[:, :, :base], pad(w_in[:, :, base:base + IDX_DIM]),
                            pad(w_in[:, :, base + IDX_DIM:])] + [w for w in w_k for _ in range(2)],
                           axis=2).astype(BF16)


def _attn_proj(x2, g, w, layer):
    m, d = x2.shape
    dq = d
    dkv = N_KV_HEADS * HEAD_DIM
    dqi = IDX_HEADS * IDX_DIM
    tm = _row_tile(m, ROW_TILE)
    row = lambda *n: pl.BlockSpec((tm,) + n, lambda i: (i,) + (0,) * len(n))
    heads = (N_KV_HEADS, HEAD_DIM)
    outs = [((dq,), BF16), (heads, F32), (heads, F32), ((IDX_DIM,), F32), ((2 * dkv,), BF16), ((dkv,), BF16),
            ((IDX_DIM,), BF16), ((dqi,), BF16), ((LANES,), F32)]
    return pl.pallas_call(
        functools.partial(_attn_proj_kernel, dq=dq, dkv=dkv, dqi=dqi),
        out_shape=[jax.ShapeDtypeStruct((m,) + n, dt) for n, dt in outs],
        grid=(m // tm,),
        in_specs=[row(d), _const_spec((1, d)), _layer_spec(w, layer)],
        out_specs=[row(*n) for n, _ in outs],
        compiler_params=_params("parallel"),
        name="attn_proj",
    )(x2, g.reshape(1, d), w)


def _sparse_attn_kernel(q_ref, qi_ref, wi_ref, k_ref, vt_ref, ki_ref, bt_ref, o_ref,
                        s_ref, w_ref, dots_ref, qgt_ref, qit_ref, ot_ref, lg_ref, mb_ref, *state_refs,
                        q_off, n_keys, k_sel, group):
    tq = KEY_BLOCK
    hpt = HEADS_PER_TILE
    n_tiles = qgt_ref.shape[0]
    m_refs, acc_refs = state_refs[:n_tiles], state_refs[n_tiles:]
    i = pl.program_id(1)
    qs = q_off // KEY_BLOCK + i
    n_kb = qs + 1
    q_start = q_off + i * tq

    qt = q_ref[0].astype(F32).T * LOG2E
    for t in range(n_tiles):
        q2 = jnp.concatenate(
            [qt[(t * hpt + g) * HEAD_DIM:(t * hpt + g + 1) * HEAD_DIM, :] for g in range(hpt)], axis=1)
        q_hi = q2.astype(BF16)
        q_lo = (q2 - q_hi.astype(F32)).astype(BF16)
        qgt_ref[t] = jnp.concatenate([q_hi, q_lo], axis=0)
    qit = qi_ref[0].astype(F32).T
    qit_ref[...] = jnp.concatenate([qit[h * IDX_DIM:(h + 1) * IDX_DIM, :] for h in range(IDX_HEADS)],
                                   axis=1).astype(BF16)
    last_kv = k_ref.shape[1] // KEY_BLOCK - 1
    wit = wi_ref[0].T

    qpos = q_start + lax.broadcasted_iota(jnp.int32, (1, tq), 1)
    lim = jnp.minimum((qpos // CHUNK + 1) * CHUNK, n_keys)
    kidx = lax.broadcasted_iota(jnp.int32, (KEY_BLOCK, tq), 0)

    def key_to_f32(key):
        bits = key ^ ((key >> 31) & 0x7FFFFFFF)
        return lax.bitcast_convert_type(bits, F32)

    def digit(sc):
        bits = lax.bitcast_convert_type(jnp.where(sc == 0.0, 0.0, sc), jnp.int32)
        key = bits ^ ((bits >> 31) & 0x7FFFFFFF)
        return (key >> LO_BITS) + (1 << (HI_BITS - 1))

    def dots(j, slot):
        k0 = pl.multiple_of(jnp.minimum(j, last_kv) * KEY_BLOCK, KEY_BLOCK)
        dots_ref[slot] = _dot(ki_ref[0, pl.ds(k0, KEY_BLOCK), :], qit_ref[...])

    def finish(j, slot):
        acc = jnp.zeros((KEY_BLOCK, tq), F32)
        for h in range(IDX_HEADS):
            acc = acc + wit[h:h + 1, :] * jnp.maximum(dots_ref[slot, :, h * tq:(h + 1) * tq], 0.0)
        sc = jnp.where(kidx + j * KEY_BLOCK < lim, acc, -jnp.inf)
        s_ref[j] = sc
        return digit(sc)

    def finish_pair(j, slot):
        w_ref[j // 2] = (finish(j, slot) << 16) | finish(j + 1, slot + 1)

    def score_quad(quad, carry):
        j = 4 * quad
        dots(j + 2, 2)
        dots(j + 3, 3)
        finish_pair(j, 0)
        dots(j + 4, 0)
        dots(j + 5, 1)
        finish_pair(j + 2, 2)
        return carry

    n_pairs = (n_kb + 1) // 2
    n_quads = (n_kb + 3) // 4
    dots(0, 0)
    dots(1, 1)
    lax.fori_loop(0, n_quads, score_quad, 0)
    for u in range(COUNT_UNROLL):
        s_ref[2 * n_pairs + u] = jnp.full((KEY_BLOCK, tq), -jnp.inf, F32)
    for u in range(COUNT_UNROLL // 2):
        w_ref[2 * n_quads + u] = jnp.zeros((KEY_BLOCK, tq), jnp.int32)
    n_count = (2 * n_pairs + COUNT_UNROLL - 1) // COUNT_UNROLL

    def count_digits(cand):
        comp = (1 << HI_BITS) - cand
        cw = jnp.broadcast_to((comp << 16) | comp, (KEY_BLOCK, tq))

        def body(p, c):
            for u in range(COUNT_UNROLL // 2):
                c = c + (((w_ref[(COUNT_UNROLL // 2) * p + u] + cw) >> HI_BITS) & 0x00010001)
            return c
        c = lax.fori_loop(0, n_count, body, jnp.zeros((KEY_BLOCK, tq), jnp.int32))
        return jnp.sum(((c & 0xFFFF) + (c >> 16)).astype(F32), axis=0, keepdims=True)

    def count_ge(cand):
        cb = jnp.broadcast_to(cand, (KEY_BLOCK, tq))

        def body(p, c):
            for u in range(COUNT_UNROLL):
                c = c + jnp.where(s_ref[COUNT_UNROLL * p + u] >= cb, 1.0, 0.0)
            return c
        c = lax.fori_loop(0, n_count, body, jnp.zeros((KEY_BLOCK, tq), F32))
        return jnp.sum(c, axis=0, keepdims=True)

    def bisect_step(count_at, carry):
        lo, hi, n_lo = carry
        mid = (lo + hi) >> 1
        n_mid = count_at(mid)
        ok = n_mid >= k_sel
        return jnp.where(ok, mid, lo), jnp.where(ok, hi, mid), jnp.where(ok, n_mid, n_lo)

    full = lambda v: jnp.full((1, tq), v, jnp.int32)
    half = 1 << (HI_BITS - 1)
    dig, _, n_sel = lax.fori_loop(
        0, HI_BITS, lambda _, c: bisect_step(count_digits, c),
        (full(((KEY_NEG_INF + 1) >> LO_BITS) + half), full((KEY_POS_INF >> LO_BITS) + half + 1),
         jnp.full((1, tq), jnp.inf, F32)))
    key_hi = (dig - half) << LO_BITS

    count_lo = lambda v: count_ge(key_to_f32(key_hi + v))

    def low_steps(carry):
        step, c = carry[0], carry[1:]
        for _ in range(LOW_CHECK_EVERY):
            c = bisect_step(count_lo, c)
        return (step + LOW_CHECK_EVERY,) + c

    def low_unfinished(carry):
        step, n_lo = carry[0], carry[3]
        return (step < LO_BITS) & (jnp.max(jnp.abs(n_lo - k_sel)) > 0.0)

    _, key_lo, _, n_sel = lax.while_loop(low_unfinished, low_steps, (0, full(0), full(1 << LO_BITS), n_sel))
    thr = key_to_f32(key_hi + key_lo)

    def count(cand, strict):
        cb = jnp.broadcast_to(cand, (KEY_BLOCK, tq))
        hit = (lambda s: s > cb) if strict else (lambda s: s >= cb)

        def body(p, c):
            c = c + jnp.where(hit(s_ref[2 * p]), 1.0, 0.0)
            return c + jnp.where(hit(s_ref[2 * p + 1]), 1.0, 0.0)
        c = lax.fori_loop(0, n_pairs, body, jnp.zeros((KEY_BLOCK, tq), F32))
        return jnp.sum(c, axis=0, keepdims=True)

    surplus = jnp.where(n_sel < jnp.inf, n_sel - k_sel, 0.0)

    @pl.when(jnp.max(surplus) > 0.0)
    def _():
        n_ties = k_sel - count(thr, True)
        row_i = lax.broadcasted_iota(jnp.int32, (KEY_BLOCK, KEY_BLOCK), 0)
        col_i = lax.broadcasted_iota(jnp.int32, (KEY_BLOCK, KEY_BLOCK), 1)
        lower = jnp.where(col_i < row_i, 1.0, 0.0).astype(BF16)
        ones = jnp.ones((KEY_BLOCK, KEY_BLOCK), BF16)

        def tie_body(j, seen):
            s = s_ref[j]
            eq = s == thr
            e = jnp.where(eq, 1.0, 0.0).astype(BF16)
            before = _dot(lower, e) + seen
            s_ref[j] = jnp.where(eq & (before >= n_ties), -jnp.inf, s)
            return seen + _dot(ones, e)

        lax.fori_loop(0, n_kb, tie_body, jnp.zeros((KEY_BLOCK, tq), F32))

    thr_sel = jnp.broadcast_to(jnp.maximum(thr, jnp.finfo(F32).min), (KEY_BLOCK, tq))

    for t in range(n_tiles):
        m_refs[t][...] = jnp.full(m_refs[t].shape, NEG_BIG, F32)
        acc_refs[t][...] = jnp.zeros(acc_refs[t].shape, F32)

    def far_blocks(db):
        out = []
        for j in (2 * db, 2 * db + 1):
            js = jnp.where(j < qs - 1, j, n_kb)
            out.append((js, jnp.minimum(js, last_kv)))
        return out

    near = [(jnp.where(qs >= 1, qs - 1, n_kb), jnp.maximum(qs - 1, 0)), (qs, qs)]

    def mask_of(blocks):
        m = jnp.concatenate([jnp.where(s_ref[js] >= thr_sel, 0.0, NEG_BIG) for js, _ in blocks], axis=0)
        return jnp.concatenate([m] * hpt, axis=1)

    def logits(blocks, mask, biased, slot, t):
        n = (t * hpt) // group
        kn = jnp.concatenate(
            [k_ref[0, pl.ds(pl.multiple_of(jk * KEY_BLOCK, KEY_BLOCK), KEY_BLOCK),
                   n * 2 * HEAD_DIM:(n + 1) * 2 * HEAD_DIM] for _, jk in blocks], axis=0)
        lg = _dot(kn, qgt_ref[t]) + mask
        if biased:
            lg = lg + bt_ref[t]
        lg_ref[slot, t] = lg
        mb_ref[slot, t] = jnp.max(lg, axis=0, keepdims=True)

    def softmax_update(blocks, slot, t):
        n = (t * hpt) // group
        m_old = m_refs[t][...]
        m_new = jnp.maximum(m_old, mb_ref[slot, t])
        alpha = jnp.exp2(m_old - m_new)
        p = jnp.exp2(lg_ref[slot, t] - m_new).astype(BF16)
        m_refs[t][...] = m_new
        vtn = jnp.concatenate([vt_ref[0, jk, n] for _, jk in blocks], axis=1)
        acc_refs[t][...] = alpha * acc_refs[t][...] + _dot(vtn, p)

    n_far = qs // 2
    mask_near = mask_of(near)
    for t in range(n_tiles):
        logits(near, mask_near, True, 0, t)

    def trip_body(trip, carry):
        d0 = 2 * trip
        blk1, blk2 = far_blocks(d0), far_blocks(d0 + 1)
        mask1, mask2 = mask_of(blk1), mask_of(blk2)
        far0 = far_blocks(d0 - 1)
        blk0 = [tuple(jnp.where(d0 == 0, a, b) for a, b in zip(near[h], far0[h])) for h in range(2)]
        for t in range(n_tiles):
            logits(blk1, mask1, False, 1, t)
            softmax_update(blk0, 0, t)
        for t in range(n_tiles):
            logits(blk2, mask2, False, 0, t)
            softmax_update(blk1, 1, t)
        return carry

    lax.fori_loop(0, (n_far + 2) // 2, trip_body, 0)

    for t in range(n_tiles):
        on = acc_refs[t][0:HEAD_DIM, :] / acc_refs[t][HEAD_DIM:HEAD_DIM + 1, :]
        for g in range(hpt):
            hh = t * hpt + g
            ot_ref[hh * HEAD_DIM:(hh + 1) * HEAD_DIM, :] = on[:, g * tq:(g + 1) * tq]
    o_ref[0] = ot_ref[...].T.astype(BF16)


def _sparse_attn(q, qi, wi, k2, vb, kib, bias_tiles, *, q_off, n_keys, k_sel):
    b, t, dq = q.shape
    lp = vb.shape[1]
    dkv = vb.shape[2]
    tq = KEY_BLOCK
    n_heads = dq // HEAD_DIM
    group = n_heads // N_KV_HEADS
    hpt = HEADS_PER_TILE
    n_tiles = n_heads // hpt
    assert group % hpt == 0
    nkb = lp // KEY_BLOCK
    assert t % tq == 0 and lp % KEY_BLOCK == 0 and q_off % KEY_BLOCK == 0 and q_off + t <= lp
    vt = jnp.swapaxes(vb.reshape(b, nkb, KEY_BLOCK, dkv), 2, 3).reshape(
        b, nkb, N_KV_HEADS, HEAD_DIM, KEY_BLOCK)
    vt = jnp.concatenate(
        [vt, jnp.ones((b, nkb, N_KV_HEADS, 1, KEY_BLOCK), BF16),
         jnp.zeros((b, nkb, N_KV_HEADS, V_AUG_ROWS - HEAD_DIM - 1, KEY_BLOCK), BF16)], axis=3)
    qblk = lambda n: pl.BlockSpec((1, tq, n), lambda bi, i: (bi, i, 0))
    kblk = lambda n: pl.BlockSpec((1, lp, n), lambda bi, i: (bi, 0, 0))
    return pl.pallas_call(
        functools.partial(_sparse_attn_kernel, q_off=q_off, n_keys=n_keys, k_sel=k_sel, group=group),
        out_shape=jax.ShapeDtypeStruct((b, t, dq), BF16),
        grid=(b, t // tq),
        in_specs=[qblk(dq), qblk(qi.shape[2]), qblk(LANES), kblk(2 * dkv),
                  pl.BlockSpec((1, nkb, N_KV_HEADS, V_AUG_ROWS, KEY_BLOCK), lambda bi, i: (bi, 0, 0, 0, 0)),
                  kblk(kib.shape[2]), _const_spec(bias_tiles.shape)],
        out_specs=qblk(dq),
        scratch_shapes=[
            pltpu.VMEM((nkb + 1 + COUNT_UNROLL, KEY_BLOCK, tq), F32),
            pltpu.VMEM((nkb // 2 + COUNT_UNROLL, KEY_BLOCK, tq), jnp.int32),
            pltpu.VMEM((4, KEY_BLOCK, IDX_HEADS * tq), F32),
            pltpu.VMEM((n_tiles, 2 * HEAD_DIM, hpt * tq), BF16),
            pltpu.VMEM((IDX_DIM, IDX_HEADS * tq), BF16),
            pltpu.VMEM((dq, tq), F32),
            pltpu.VMEM((2, n_tiles, 2 * KEY_BLOCK, hpt * tq), F32),
            pltpu.VMEM((2, n_tiles, 1, hpt * tq), F32),
        ] + [pltpu.VMEM((1, hpt * tq), F32)] * n_tiles
          + [pltpu.VMEM((V_AUG_ROWS, hpt * tq), F32)] * n_tiles,
        compiler_params=_params("parallel", "arbitrary"),
        name="sparse_attn",
    )(q, qi, wi, k2, vt, kib, bias_tiles)


def _mem_kv_kernel(x_ref, g_ref, w_ref, k_ref, v_ref, *, d):
    h = _rms(x_ref[...], g_ref[...]).astype(BF16)
    k_ref[...] = _dot(h, w_ref[:, :d])
    v_ref[...] = _dot(h, w_ref[:, d:])


def _mem_kv(mem2, g, w_kv, layer):
    m, d = mem2.shape
    tm = _row_tile(m, ROW_TILE)
    row = pl.BlockSpec((tm, d), lambda i: (i, 0))
    return pl.pallas_call(
        functools.partial(_mem_kv_kernel, d=d),
        out_shape=[jax.ShapeDtypeStruct((m, d), F32)] * 2,
        grid=(m // tm,),
        in_specs=[row, _const_spec((1, d)), _layer_spec(w_kv, layer)],
        out_specs=[row, row],
        compiler_params=_params("parallel"),
        name="mem_kv",
    )(mem2, g.reshape(1, d), w_kv)


def _mem_attn_kernel(*refs, hd, pending):
    if pending:
        x_ref, a_ref, wa_ref, g_ref, wq_ref, mk_ref, mv_ref, wo_ref, o_ref = refs
        x = x_ref[...] + _dot(a_ref[...], wa_ref[...])
    else:
        x_ref, g_ref, wq_ref, mk_ref, mv_ref, wo_ref, o_ref = refs
        x = x_ref[...]
    h = _rms(x, g_ref[...]).astype(BF16)
    q = (_dot(h, wq_ref[...]) * (hd ** -0.5)).astype(BF16)
    heads = []
    for a in range(MEM_HEADS):
        cols = slice(a * hd, (a + 1) * hd)
        lg = _dot_nt(q[:, cols], mk_ref[0, :, cols].astype(BF16))
        p = jnp.exp(lg - jnp.max(lg, axis=1, keepdims=True))
        p = (p / jnp.sum(p, axis=1, keepdims=True)).astype(BF16)
        heads.append(_dot(p, mv_ref[0, :, cols].astype(BF16)).astype(BF16))
    o_ref[...] = x + _dot(jnp.concatenate(heads, axis=1), wo_ref[...])


def _mem_attn(x2, g, w_q, mk, mv, w_o, layer, rows_per_batch, pending=None):
    m, d = x2.shape
    n_mem = mk.shape[1]
    tm = _row_tile(rows_per_batch, ROW_TILE)
    per = rows_per_batch // tm
    row = pl.BlockSpec((tm, d), lambda i: (i, 0))
    mem = pl.BlockSpec((1, n_mem, d), lambda i: (i // per, 0, 0))
    in_specs = [row, _const_spec((1, d)), _layer_spec(w_q, layer), mem, mem, _layer_spec(w_o, layer)]
    args = [x2, g.reshape(1, d), w_q, mk, mv, w_o]
    if pending is not None:
        a2, w_a, layer_a = pending
        in_specs[1:1] = [pl.BlockSpec((tm, a2.shape[1]), lambda i: (i, 0)), _layer_spec(w_a, layer_a)]
        args[1:1] = [a2, w_a]
    return pl.pallas_call(
        functools.partial(_mem_attn_kernel, hd=d // MEM_HEADS, pending=pending is not None),
        out_shape=jax.ShapeDtypeStruct((m, d), F32),
        grid=(m // tm,),
        in_specs=in_specs,
        out_specs=row,
        compiler_params=_params("parallel"),
        name="mem_attn",
    )(*args)


def _mlp_kernel(*refs, n_chunks, chunk, final):
    if final:
        x_ref, g_ref, w1_ref, w2_ref, gf_ref, o_ref = refs
    else:
        x_ref, g_ref, w1_ref, w2_ref, o_ref = refs
    x = x_ref[...]
    h = _rms(x, g_ref[...]).astype(BF16)
    acc = x
    for c in range(n_chunks):
        a = jnp.maximum(_dot(h, w1_ref[:, c * chunk:(c + 1) * chunk]), 0.0)
        acc = acc + _dot((a * a).astype(BF16), w2_ref[c * chunk:(c + 1) * chunk, :])
    if final:
        acc = _rms(acc, gf_ref[...])
    o_ref[...] = acc


def _mlp(x2, g, w1, w2, layer, g_final=None):
    m, d = x2.shape
    dff = w1.shape[2]
    chunk = min(dff, 1024)
    tm = _row_tile(m, ROW_TILE)
    row = pl.BlockSpec((tm, d), lambda i: (i, 0))
    final = g_final is not None
    in_specs = [row, _const_spec((1, d)), _layer_spec(w1, layer), _layer_spec(w2, layer)]
    args = [x2, g.reshape(1, d), w1, w2]
    if final:
        in_specs.append(_const_spec((1, d)))
        args.append(g_final.reshape(1, d))
    return pl.pallas_call(
        functools.partial(_mlp_kernel, n_chunks=dff // chunk, chunk=chunk, final=final),
        out_shape=jax.ShapeDtypeStruct((m, d), F32),
        grid=(m // tm,),
        in_specs=in_specs,
        out_specs=row,
        compiler_params=_params("parallel"),
        name="mlp",
    )(*args)


def _conv_glu_kernel(x_ref, g_ref, w_ref, b_ref, u_ref, *, d):
    h = _rms(x_ref[...], g_ref[...]).astype(BF16)
    a = _dot(h, w_ref[:, :d]) + b_ref[:, :d]
    gate = _dot(h, w_ref[:, d:]) + b_ref[:, d:]
    u_ref[...] = a * (1.0 / (1.0 + jnp.exp(-gate)))


def _conv_glu(x2, g, w_pw1, layer, b_pw1):
    m, d = x2.shape
    tm = _row_tile(m, ROW_TILE)
    row = pl.BlockSpec((tm, d), lambda i: (i, 0))
    return pl.pallas_call(
        functools.partial(_conv_glu_kernel, d=d),
        out_shape=jax.ShapeDtypeStruct((m, d), F32),
        grid=(m // tm,),
        in_specs=[row, _const_spec((1, d)), _layer_spec(w_pw1, layer), _const_spec((1, 2 * d))],
        out_specs=row,
        compiler_params=_params("parallel"),
        name="conv_glu",
    )(x2, g.reshape(1, d), w_pw1, b_pw1.reshape(1, 2 * d))


def _conv_rest_kernel(x_ref, u_ref, prev_ref, init_ref, wdw_ref, bdw_ref, lng_ref, lnb_ref,
                      w2_ref, b2_ref, o_ref, ext_ref, sh_ref, y_ref, *, tm, rc, lc):
    t = pl.program_id(1)
    d = u_ref.shape[2]
    pad = CONV_WIDTH - 1

    @pl.when(t == 0)
    def _():
        ext_ref[0:HALO, :] = init_ref[0]

    @pl.when(t > 0)
    def _():
        ext_ref[0:HALO, :] = prev_ref[0]

    ext_ref[HALO:HALO + tm, :] = u_ref[0]

    for s in range(SUBLANES):
        rows = tm + SUBLANES * ((CONV_WIDTH - 1 - s) // SUBLANES)
        sh_ref[s, 0:rows, :] = ext_ref[pl.ds(HALO - pad + s, rows), :]
    for r0 in range(0, tm, rc):
        for c0 in range(0, d, lc):
            cols = slice(c0, c0 + lc)
            y = jnp.broadcast_to(bdw_ref[:, cols], (rc, lc))
            for w in range(CONV_WIDTH):
                a, s = divmod(w, SUBLANES)
                y = y + sh_ref[s, r0 + SUBLANES * a:r0 + SUBLANES * a + rc, cols] * wdw_ref[w:w + 1, cols]
            y_ref[r0:r0 + rc, cols] = y

    y = y_ref[...]
    mu = jnp.mean(y, axis=-1, keepdims=True)
    yc = y - mu
    var = jnp.mean(yc * yc, axis=-1, keepdims=True)
    yn = yc * lax.rsqrt(var + EPS) * lng_ref[...] + lnb_ref[...]
    act = (yn * (1.0 / (1.0 + jnp.exp(-yn)))).astype(BF16)
    o_ref[0] = x_ref[0] + _dot(act, w2_ref[...]) + b2_ref[...]


def _conv_rest(x3, u3, init, w_dw, b_dw, ln_g, ln_b, w_pw2, layer, b_pw2):
    b, t, d = x3.shape
    tm = _row_tile(t, CONV_ROW_TILE)
    rc = min(tm, 64)
    lc = min(d, 256)
    assert tm % HALO == 0 and tm % rc == 0 and d % lc == 0
    per = tm // HALO
    tile = pl.BlockSpec((1, tm, d), lambda bi, ti: (bi, ti, 0))
    prev = pl.BlockSpec((1, HALO, d), lambda bi, ti: (bi, jnp.maximum(ti * per - 1, 0), 0))
    first = pl.BlockSpec((1, HALO, d), lambda bi, ti: (bi, 0, 0))
    vec = _const_spec((1, d))
    wdw = jnp.pad(w_dw, ((0, HALO - CONV_WIDTH), (0, 0)))
    return pl.pallas_call(
        functools.partial(_conv_rest_kernel, tm=tm, rc=rc, lc=lc),
        out_shape=jax.ShapeDtypeStruct((b, t, d), F32),
        grid=(b, t // tm),
        in_specs=[tile, tile, prev, first, _const_spec((HALO, d)), vec, vec, vec,
                  _layer_spec(w_pw2, layer), vec],
        out_specs=tile,
        scratch_shapes=[pltpu.VMEM((HALO + tm, d), F32),
                        pltpu.VMEM((SUBLANES, tm + HALO - SUBLANES, d), F32),
                        pltpu.VMEM((tm, d), F32)],
        compiler_params=_params("parallel", "arbitrary"),
        name="conv_rest",
    )(x3, u3, u3, init, wdw, b_dw.reshape(1, d), ln_g.reshape(1, d), ln_b.reshape(1, d),
      w_pw2, b_pw2.reshape(1, d))


def _mixer_attn(x3, g, w_in, layer, bias_tiles, cache=None):
    b, t, d = x3.shape
    x2 = x3.reshape(b * t, d)
    q, k, v, ki, kb, vb, kib, qi, wi = _attn_proj(x2, g, w_in, layer)
    r3 = lambda a: a.reshape(b, t, a.shape[-1])
    kb, vb, kib = r3(kb), r3(vb), r3(kib)
    past = 0
    if cache is not None:
        ck, cv, cki = cache
        past = ck.shape[1]
        ck2 = jnp.concatenate([ck[:, :, n].astype(BF16) for n in range(N_KV_HEADS) for _ in range(2)], axis=-1)
        kb = jnp.concatenate([ck2, kb], axis=1)
        vb = jnp.concatenate([cv.reshape(b, past, -1).astype(BF16), vb], axis=1)
        kib = jnp.concatenate([cki.astype(BF16), kib], axis=1)
    n_keys = past + t
    k_sel = min(TOPK_MAX, n_keys // 4)
    tpad = -(-t // KEY_BLOCK) * KEY_BLOCK
    lp = max(-(-n_keys // KEY_BLOCK) * KEY_BLOCK, past + tpad)
    padt = lambda a, n: a if a.shape[1] == n else jnp.pad(a, ((0, 0), (0, n - a.shape[1]), (0, 0)))
    o = _sparse_attn(padt(r3(q), tpad), padt(r3(qi), tpad), padt(r3(wi), tpad),
                     padt(kb, lp), padt(vb, lp), padt(kib, lp), bias_tiles,
                     q_off=past, n_keys=n_keys, k_sel=k_sel)[:, :t]
    return (o.reshape(b * t, d), k.reshape(b, t, N_KV_HEADS, HEAD_DIM),
            v.reshape(b, t, N_KV_HEADS, HEAD_DIM), ki.reshape(b, t, IDX_DIM))


def _mixer_conv(x3, g, w_pw1, b_pw1, w_dw, b_dw, ln_g, ln_b, w_pw2, b_pw2, layer, state=None):
    b, t, d = x3.shape
    pad = CONV_WIDTH - 1
    u3 = _conv_glu(x3.reshape(b * t, d), g, w_pw1, layer, b_pw1).reshape(b, t, d)
    if state is None:
        init = jnp.zeros((b, HALO, d), F32)
        tail = u3[:, -pad:] if t >= pad else jnp.pad(u3, ((0, 0), (pad - t, 0), (0, 0)))
    else:
        init = jnp.pad(state.astype(F32), ((0, 0), (HALO - pad, 0), (0, 0)))
        tail = jnp.concatenate([state.astype(F32), u3], axis=1)[:, -pad:]
    x3 = _conv_rest(x3, u3, init, w_dw, b_dw, ln_g, ln_b, w_pw2, layer, b_pw2)
    return x3, tail


def kernel(x_prompt, x_sample, cache_attn_k, cache_attn_v, cache_attn_kidx, state_conv, cache_mem_k, cache_mem_v, mem_prompt, rel_bias, g_mix, w_in_attn, w_out_attn, w_pw1, b_pw1, w_dw, b_dw, ln_g, ln_b, w_pw2, b_pw2, g_mem_q, g_mem_src, w_mem_q, w_mem_kv, w_mem_o, g_mlp, w_mlp1, w_mlp2, g_final):
    depth = g_mix.shape[0]
    bp, tp, d = x_prompt.shape
    bs, ts, _ = x_sample.shape
    n_mem = mem_prompt.shape[1]
    mem_hd = d // MEM_HEADS
    bias_tiles = _bias_tiles(rel_bias)
    w_in_b, w_out_b = _attn_proj_weights(w_in_attn), w_out_attn.astype(BF16)
    w_pw1_b, w_pw2_b = w_pw1.astype(BF16), w_pw2.astype(BF16)
    w_mq_b, w_mkv_b, w_mo_b = w_mem_q.astype(BF16), w_mem_kv.astype(BF16), w_mem_o.astype(BF16)
    w_mlp1_b, w_mlp2_b = w_mlp1.astype(BF16), w_mlp2.astype(BF16)
    xp, xs = x_prompt, x_sample
    kp_l, vp_l, kip_l, ks_l, vs_l, kis_l = [], [], [], [], [], []
    convp_l, convs_l, memk_l, memv_l = [], [], [], []
    for i in range(depth):
        j = i // 2
        if i % 2 == 0:
            op, kp, vp, kip = _mixer_attn(xp, g_mix[i], w_in_b, j, bias_tiles)
            osm, ks, vs, kis = _mixer_attn(
                xs, g_mix[i], w_in_b, j, bias_tiles,
                cache=(cache_attn_k[j], cache_attn_v[j], cache_attn_kidx[j]))
            pend_p, pend_s = (op, w_out_b, j), (osm, w_out_b, j)
            kp_l.append(kp); vp_l.append(vp); kip_l.append(kip)
            ks_l.append(ks); vs_l.append(vs); kis_l.append(kis)
        else:
            cw = (w_pw1_b, b_pw1[j], w_dw[j], b_dw[j], ln_g[j], ln_b[j], w_pw2_b, b_pw2[j], j)
            xp, cp = _mixer_conv(xp, g_mix[i], *cw)
            xs, cs = _mixer_conv(xs, g_mix[i], *cw, state=state_conv[j])
            pend_p = pend_s = None
            convp_l.append(cp); convs_l.append(cs)
        mk, mv = _mem_kv(mem_prompt.reshape(bp * n_mem, d), g_mem_src[i], w_mkv_b, i)
        mk, mv = mk.reshape(bp, n_mem, d), mv.reshape(bp, n_mem, d)
        memk_l.append(mk.reshape(bp, n_mem, MEM_HEADS, mem_hd))
        memv_l.append(mv.reshape(bp, n_mem, MEM_HEADS, mem_hd))
        xp2 = _mem_attn(xp.reshape(bp * tp, d), g_mem_q[i], w_mq_b, mk, mv, w_mo_b, i, tp, pend_p)
        xs2 = _mem_attn(xs.reshape(bs * ts, d), g_mem_q[i], w_mq_b,
                        cache_mem_k[i].reshape(bs, n_mem, d), cache_mem_v[i].reshape(bs, n_mem, d),
                        w_mo_b, i, ts, pend_s)
        gf = g_final if i == depth - 1 else None
        xp = _mlp(xp2, g_mlp[i], w_mlp1_b, w_mlp2_b, i, gf).reshape(bp, tp, d)
        xs = _mlp(xs2, g_mlp[i], w_mlp1_b, w_mlp2_b, i, gf).reshape(bs, ts, d)
    return (xp, xs, jnp.stack(kp_l), jnp.stack(vp_l), jnp.stack(kip_l), jnp.stack(convp_l),
            jnp.stack(memk_l), jnp.stack(memv_l), jnp.stack(ks_l), jnp.stack(vs_l),
            jnp.stack(kis_l), jnp.stack(convs_l))
```

```python
import functools
import math

import jax
import jax.numpy as jnp
from jax import lax
from jax.experimental import pallas as pl
from jax.experimental.pallas import tpu as pltpu

CHUNK = 64
HEAD_DIM = 64
N_KV_HEADS = 4
IDX_HEADS = 8
IDX_DIM = 64
TOPK_MAX = 256
N_BUCKETS = 32
MAX_DISTANCE = 128
CONV_WIDTH = 31
MEM_HEADS = 4
EPS = 1e-6

LANES = 128
SUBLANES = 8
KEY_BLOCK = 128
HEADS_PER_TILE = 2
COUNT_UNROLL = 4
HI_BITS = 15
LO_BITS = 32 - HI_BITS
LOW_CHECK_EVERY = 4
V_AUG_ROWS = HEAD_DIM + 16
LOG2E = math.log2(math.e)
ROW_TILE = 1024
CONV_ROW_TILE = 512
HALO = 32
VMEM_LIMIT = 56 * 1024 * 1024

NEG_BIG = -1e30
F32 = jnp.float32
BF16 = jnp.bfloat16

KEY_NEG_INF = -2139095041
KEY_POS_INF = 2139095040


def _const_spec(shape):
    nd = len(shape)
    return pl.BlockSpec(shape, lambda *_: (0,) * nd, pipeline_mode=pl.Buffered(1))


def _layer_spec(stacked, layer):
    nd = stacked.ndim
    return pl.BlockSpec((None,) + stacked.shape[1:], lambda *_: (layer,) + (0,) * (nd - 1),
                        pipeline_mode=pl.Buffered(1))


def _params(*sem):
    return pltpu.CompilerParams(dimension_semantics=sem, vmem_limit_bytes=VMEM_LIMIT)


def _rms(x, g):
    ms = jnp.mean(x * x, axis=-1, keepdims=True)
    return x * lax.rsqrt(ms + EPS) * g


def _dot(a, b):
    return jnp.dot(a, b, preferred_element_type=F32)


def _dot_nt(a, b):
    return lax.dot_general(a, b, (((1,), (1,)), ((), ())), preferred_element_type=F32)


def _row_tile(m, pref):
    t = min(m, pref)
    assert m % t == 0, (m, t)
    return t


def _bias_tiles_kernel(tab_ref, bt_ref, *, n_heads):
    nb = N_BUCKETS // 2
    max_exact = nb // 2
    c = lax.broadcasted_iota(jnp.int32, (KEY_BLOCK, KEY_BLOCK), 0)
    r = lax.broadcasted_iota(jnp.int32, (KEY_BLOCK, KEY_BLOCK), 1)
    for d in range(2):
        rel = c - r - d * KEY_BLOCK
        n = jnp.abs(rel)
        nf = jnp.maximum(n, 1).astype(F32)
        large = max_exact + (jnp.log(nf / max_exact) / math.log(MAX_DISTANCE / max_exact)
                             * (nb - max_exact)).astype(jnp.int32)
        large = jnp.minimum(large, nb - 1)
        bucket = jnp.where(rel > 0, nb, 0) + jnp.where(n < max_exact, n, large)
        for h in range(n_heads):
            acc = jnp.zeros((KEY_BLOCK, KEY_BLOCK), F32)
            for b in range(N_BUCKETS):
                acc = jnp.where(bucket == b, tab_ref[b, h], acc)
            g = h % HEADS_PER_TILE
            rows = slice((1 - d) * KEY_BLOCK, (2 - d) * KEY_BLOCK)
            bt_ref[h // HEADS_PER_TILE, rows, g * KEY_BLOCK:(g + 1) * KEY_BLOCK] = (
                (acc - tab_ref[nb - 1, h]) * LOG2E)


def _bias_tiles(rel_bias):
    n_heads = rel_bias.shape[1]
    assert n_heads % HEADS_PER_TILE == 0
    return pl.pallas_call(
        functools.partial(_bias_tiles_kernel, n_heads=n_heads),
        out_shape=jax.ShapeDtypeStruct(
            (n_heads // HEADS_PER_TILE, 2 * KEY_BLOCK, HEADS_PER_TILE * KEY_BLOCK), F32),
        in_specs=[pl.BlockSpec(memory_space=pltpu.SMEM)],
        out_specs=pl.BlockSpec(memory_space=pltpu.VMEM),
        name="bias_tiles",
    )(rel_bias)


def _attn_proj_kernel(x_ref, g_ref, w_ref, q_ref, k_ref, v_ref, ki_ref, kb_ref, vb_ref,
                      kib_ref, qi_ref, wi_ref, *, dq, dkv, dqi):
    h = _rms(x_ref[...], g_ref[...]).astype(BF16)
    o = 0
    q_ref[...] = (_dot(h, w_ref[:, o:o + dq]) * (HEAD_DIM ** -0.5)).astype(BF16)
    o += dq
    k = _dot(h, w_ref[:, o:o + dkv])
    o += dkv
    v = _dot(h, w_ref[:, o:o + dkv])
    vb_ref[...] = v.astype(BF16)
    for n in range(N_KV_HEADS):
        k_ref[:, n, :] = k[:, n * HEAD_DIM:(n + 1) * HEAD_DIM]
        v_ref[:, n, :] = v[:, n * HEAD_DIM:(n + 1) * HEAD_DIM]
    o += dkv
    qi_ref[...] = (_dot(h, w_ref[:, o:o + dqi]) * (IDX_DIM ** -0.5)).astype(BF16)
    o += dqi
    ki = _dot(h, w_ref[:, o:o + LANES])[:, :IDX_DIM]
    ki_ref[...] = ki
    kib_ref[...] = ki.astype(BF16)
    o += LANES
    wi_ref[...] = _dot(h, w_ref[:, o:o + LANES]) * (IDX_HEADS ** -0.5)
    o += LANES
    kb_ref[...] = _dot(h, w_ref[:, o:o + 2 * dkv]).astype(BF16)


def _attn_proj_weights(w_in):
    d = w_in.shape[1]
    base = d + 2 * N_KV_HEADS * HEAD_DIM + IDX_HEADS * IDX_DIM
    pad = lambda a: jnp.pad(a, ((0, 0), (0, 0), (0, LANES - a.shape[2])))
    w_k = [w_in[:, :, d + n * HEAD_DIM:d + (n + 1) * HEAD_DIM] for n in range(N_KV_HEADS)]
    return jnp.concatenate([w_in[:, :, :base], pad(w_in[:, :, base:base + IDX_DIM]),
                            pad(w_in[:, :, base + IDX_DIM:])] + [w for w in w_k for _ in range(2)],
                           axis=2).astype(BF16)


def _attn_proj(x2, g, w, layer):
    m, d = x2.shape
    dq = d
    dkv = N_KV_HEADS * HEAD_DIM
    dqi = IDX_HEADS * IDX_DIM
    tm = _row_tile(m, ROW_TILE)
    row = lambda *n: pl.BlockSpec((tm,) + n, lambda i: (i,) + (0,) * len(n))
    heads = (N_KV_HEADS, HEAD_DIM)
    outs = [((dq,), BF16), (heads, F32), (heads, F32), ((IDX_DIM,), F32), ((2 * dkv,), BF16), ((dkv,), BF16),
            ((IDX_DIM,), BF16), ((dqi,), BF16), ((LANES,), F32)]
    return pl.pallas_call(
        functools.partial(_attn_proj_kernel, dq=dq, dkv=dkv, dqi=dqi),
        out_shape=[jax.ShapeDtypeStruct((m,) + n, dt) for n, dt in outs],
        grid=(m // tm,),
        in_specs=[row(d), _const_spec((1, d)), _layer_spec(w, layer)],
        out_specs=[row(*n) for n, _ in outs],
        compiler_params=_params("parallel"),
        name="attn_proj",
    )(x2, g.reshape(1, d), w)


def _sparse_attn_kernel(q_ref, qi_ref, wi_ref, k_ref, vt_ref, ki_ref, bt_ref, o_ref,
                        s_ref, w_ref, dots_ref, qgt_ref, qit_ref, ot_ref, lg_ref, mb_ref, *state_refs,
                        q_off, n_keys, k_sel, group):
    tq = KEY_BLOCK
    hpt = HEADS_PER_TILE
    n_tiles = qgt_ref.shape[0]
    m_refs, acc_refs = state_refs[:n_tiles], state_refs[n_tiles:]
    i = pl.program_id(1)
    qs = q_off // KEY_BLOCK + i
    n_kb = qs + 1
    q_start = q_off + i * tq

    qt = q_ref[0].astype(F32).T * LOG2E
    for t in range(n_tiles):
        q2 = jnp.concatenate(
            [qt[(t * hpt + g) * HEAD_DIM:(t * hpt + g + 1) * HEAD_DIM, :] for g in range(hpt)], axis=1)
        q_hi = q2.astype(BF16)
        q_lo = (q2 - q_hi.astype(F32)).astype(BF16)
        qgt_ref[t] = jnp.concatenate([q_hi, q_lo], axis=0)
    qit = qi_ref[0].astype(F32).T
    qit_ref[...] = jnp.concatenate([qit[h * IDX_DIM:(h + 1) * IDX_DIM, :] for h in range(IDX_HEADS)],
                                   axis=1).astype(BF16)
    last_kv = k_ref.shape[1] // KEY_BLOCK - 1
    wit = wi_ref[0].T

    qpos = q_start + lax.broadcasted_iota(jnp.int32, (1, tq), 1)
    lim = jnp.minimum((qpos // CHUNK + 1) * CHUNK, n_keys)
    kidx = lax.broadcasted_iota(jnp.int32, (KEY_BLOCK, tq), 0)

    def key_to_f32(key):
        bits = key ^ ((key >> 31) & 0x7FFFFFFF)
        return lax.bitcast_convert_type(bits, F32)

    def digit(sc):
        bits = lax.bitcast_convert_type(jnp.where(sc == 0.0, 0.0, sc), jnp.int32)
        key = bits ^ ((bits >> 31) & 0x7FFFFFFF)
        return (key >> LO_BITS) + (1 << (HI_BITS - 1))

    def dots(j, slot):
        k0 = pl.multiple_of(jnp.minimum(j, last_kv) * KEY_BLOCK, KEY_BLOCK)
        dots_ref[slot] = _dot(ki_ref[0, pl.ds(k0, KEY_BLOCK), :], qit_ref[...])

    def finish(j, slot):
        acc = jnp.zeros((KEY_BLOCK, tq), F32)
        for h in range(IDX_HEADS):
            acc = acc + wit[h:h + 1, :] * jnp.maximum(dots_ref[slot, :, h * tq:(h + 1) * tq], 0.0)
        sc = jnp.where(kidx + j * KEY_BLOCK < lim, acc, -jnp.inf)
        s_ref[j] = sc
        return digit(sc)

    def finish_pair(j, slot):
        w_ref[j // 2] = (finish(j, slot) << 16) | finish(j + 1, slot + 1)

    def score_quad(quad, carry):
        j = 4 * quad
        dots(j + 2, 2)
        dots(j + 3, 3)
        finish_pair(j, 0)
        dots(j + 4, 0)
        dots(j + 5, 1)
        finish_pair(j + 2, 2)
        return carry

    n_pairs = (n_kb + 1) // 2
    n_quads = (n_kb + 3) // 4
    dots(0, 0)
    dots(1, 1)
    lax.fori_loop(0, n_quads, score_quad, 0)
    for u in range(COUNT_UNROLL):
        s_ref[2 * n_pairs + u] = jnp.full((KEY_BLOCK, tq), -jnp.inf, F32)
    for u in range(COUNT_UNROLL // 2):
        w_ref[2 * n_quads + u] = jnp.zeros((KEY_BLOCK, tq), jnp.int32)
    n_count = (2 * n_pairs + COUNT_UNROLL - 1) // COUNT_UNROLL

    def count_digits(cand):
        comp = (1 << HI_BITS) - cand
        cw = jnp.broadcast_to((comp << 16) | comp, (KEY_BLOCK, tq))

        def body(p, c):
            for u in range(COUNT_UNROLL // 2):
                c = c + (((w_ref[(COUNT_UNROLL // 2) * p + u] + cw) >> HI_BITS) & 0x00010001)
            return c
        c = lax.fori_loop(0, n_count, body, jnp.zeros((KEY_BLOCK, tq), jnp.int32))
        return jnp.sum(((c & 0xFFFF) + (c >> 16)).astype(F32), axis=0, keepdims=True)

    def count_ge(cand):
        cb = jnp.broadcast_to(cand, (KEY_BLOCK, tq))

        def body(p, c):
            for u in range(COUNT_UNROLL):
                c = c + jnp.where(s_ref[COUNT_UNROLL * p + u] >= cb, 1.0, 0.0)
            return c
        c = lax.fori_loop(0, n_count, body, jnp.zeros((KEY_BLOCK, tq), F32))
        return jnp.sum(c, axis=0, keepdims=True)

    def bisect_step(count_at, carry):
        lo, hi, n_lo = carry
        mid = (lo + hi) >> 1
        n_mid = count_at(mid)
        ok = n_mid >= k_sel
        return jnp.where(ok, mid, lo), jnp.where(ok, hi, mid), jnp.where(ok, n_mid, n_lo)

    full = lambda v: jnp.full((1, tq), v, jnp.int32)
    half = 1 << (HI_BITS - 1)
    dig, _, n_sel = lax.fori_loop(
        0, HI_BITS, lambda _, c: bisect_step(count_digits, c),
        (full(((KEY_NEG_INF + 1) >> LO_BITS) + half), full((KEY_POS_INF >> LO_BITS) + half + 1),
         jnp.full((1, tq), jnp.inf, F32)))
    key_hi = (dig - half) << LO_BITS

    count_lo = lambda v: count_ge(key_to_f32(key_hi + v))

    def low_steps(carry):
        step, c = carry[0], carry[1:]
        for _ in range(LOW_CHECK_EVERY):
            c = bisect_step(count_lo, c)
        return (step + LOW_CHECK_EVERY,) + c

    def low_unfinished(carry):
        step, n_lo = carry[0], carry[3]
        return (step < LO_BITS) & (jnp.max(jnp.abs(n_lo - k_sel)) > 0.0)

    _, key_lo, _, n_sel = lax.while_loop(low_unfinished, low_steps, (0, full(0), full(1 << LO_BITS), n_sel))
    thr = key_to_f32(key_hi + key_lo)

    def count(cand, strict):
        cb = jnp.broadcast_to(cand, (KEY_BLOCK, tq))
        hit = (lambda s: s > cb) if strict else (lambda s: s >= cb)

        def body(p, c):
            c = c + jnp.where(hit(s_ref[2 * p]), 1.0, 0.0)
            return c + jnp.where(hit(s_ref[2 * p + 1]), 1.0, 0.0)
        c = lax.fori_loop(0, n_pairs, body, jnp.zeros((KEY_BLOCK, tq), F32))
        return jnp.sum(c, axis=0, keepdims=True)

    surplus = jnp.where(n_sel < jnp.inf, n_sel - k_sel, 0.0)

    @pl.when(jnp.max(surplus) > 0.0)
    def _():
        n_ties = k_sel - count(thr, True)
        row_i = lax.broadcasted_iota(jnp.int32, (KEY_BLOCK, KEY_BLOCK), 0)
        col_i = lax.broadcasted_iota(jnp.int32, (KEY_BLOCK, KEY_BLOCK), 1)
        lower = jnp.where(col_i < row_i, 1.0, 0.0).astype(BF16)
        ones = jnp.ones((KEY_BLOCK, KEY_BLOCK), BF16)

        def tie_body(j, seen):
            s = s_ref[j]
            eq = s == thr
            e = jnp.where(eq, 1.0, 0.0).astype(BF16)
            before = _dot(lower, e) + seen
            s_ref[j] = jnp.where(eq & (before >= n_ties), -jnp.inf, s)
            return seen + _dot(ones, e)

        lax.fori_loop(0, n_kb, tie_body, jnp.zeros((KEY_BLOCK, tq), F32))

    thr_sel = jnp.broadcast_to(jnp.maximum(thr, jnp.finfo(F32).min), (KEY_BLOCK, tq))

    for t in range(n_tiles):
        m_refs[t][...] = jnp.full(m_refs[t].shape, NEG_BIG, F32)
        acc_refs[t][...] = jnp.zeros(acc_refs[t].shape, F32)

    def far_blocks(db):
        out = []
        for j in (2 * db, 2 * db + 1):
            js = jnp.where(j < qs - 1, j, n_kb)
            out.append((js, jnp.minimum(js, last_kv)))
        return out

    near = [(jnp.where(qs >= 1, qs - 1, n_kb), jnp.maximum(qs - 1, 0)), (qs, qs)]

    def mask_of(blocks):
        m = jnp.concatenate([jnp.where(s_ref[js] >= thr_sel, 0.0, NEG_BIG) for js, _ in blocks], axis=0)
        return jnp.concatenate([m] * hpt, axis=1)

    def logits(blocks, mask, biased, slot, t):
        n = (t * hpt) // group
        kn = jnp.concatenate(
            [k_ref[0, pl.ds(pl.multiple_of(jk * KEY_BLOCK, KEY_BLOCK), KEY_BLOCK),
                   n * 2 * HEAD_DIM:(n + 1) * 2 * HEAD_DIM] for _, jk in blocks], axis=0)
        lg = _dot(kn, qgt_ref[t]) + mask
        if biased:
            lg = lg + bt_ref[t]
        lg_ref[slot, t] = lg
        mb_ref[slot, t] = jnp.max(lg, axis=0, keepdims=True)

    def softmax_update(blocks, slot, t):
        n = (t * hpt) // group
        m_old = m_refs[t][...]
        m_new = jnp.maximum(m_old, mb_ref[slot, t])
        alpha = jnp.exp2(m_old - m_new)
        p = jnp.exp2(lg_ref[slot, t] - m_new).astype(BF16)
        m_refs[t][...] = m_new
        vtn = jnp.concatenate([vt_ref[0, jk, n] for _, jk in blocks], axis=1)
        acc_refs[t][...] = alpha * acc_refs[t][...] + _dot(jnp.concatenate([vtn, ones_rows], axis=0), p)

    row_id = lax.broadcasted_iota(jnp.int32, (V_AUG_ROWS - HEAD_DIM, 2 * KEY_BLOCK), 0)
    ones_rows = jnp.where(row_id == 0, 1.0, 0.0).astype(BF16)

    n_far = qs // 2
    mask_near = mask_of(near)
    for t in range(n_tiles):
        logits(near, mask_near, True, 0, t)

    def trip_body(trip, carry):
        d0 = 2 * trip
        blk1, blk2 = far_blocks(d0), far_blocks(d0 + 1)
        mask1, mask2 = mask_of(blk1), mask_of(blk2)
        far0 = far_blocks(d0 - 1)
        blk0 = [tuple(jnp.where(d0 == 0, a, b) for a, b in zip(near[h], far0[h])) for h in range(2)]
        for t in range(n_tiles):
            logits(blk1, mask1, False, 1, t)
            softmax_update(blk0, 0, t)
        for t in range(n_tiles):
            logits(blk2, mask2, False, 0, t)
            softmax_update(blk1, 1, t)
        return carry

    lax.fori_loop(0, (n_far + 2) // 2, trip_body, 0)

    for t in range(n_tiles):
        on = acc_refs[t][0:HEAD_DIM, :] / acc_refs[t][HEAD_DIM:HEAD_DIM + 1, :]
        for g in range(hpt):
            hh = t * hpt + g
            ot_ref[hh * HEAD_DIM:(hh + 1) * HEAD_DIM, :] = on[:, g * tq:(g + 1) * tq]
    o_ref[0] = ot_ref[...].T.astype(BF16)


def _sparse_attn(q, qi, wi, k2, vb, kib, bias_tiles, *, q_off, n_keys, k_sel):
    b, t, dq = q.shape
    lp = vb.shape[1]
    dkv = vb.shape[2]
    tq = KEY_BLOCK
    n_heads = dq // HEAD_DIM
    group = n_heads // N_KV_HEADS
    hpt = HEADS_PER_TILE
    n_tiles = n_heads // hpt
    assert group % hpt == 0
    nkb = lp // KEY_BLOCK
    assert t % tq == 0 and lp % KEY_BLOCK == 0 and q_off % KEY_BLOCK == 0 and q_off + t <= lp
    vt = jnp.swapaxes(vb.reshape(b, nkb, KEY_BLOCK, dkv), 2, 3).reshape(
        b, nkb, N_KV_HEADS, HEAD_DIM, KEY_BLOCK)
    qblk = lambda n: pl.BlockSpec((1, tq, n), lambda bi, i: (bi, i, 0))
    kblk = lambda n: pl.BlockSpec((1, lp, n), lambda bi, i: (bi, 0, 0))
    return pl.pallas_call(
        functools.partial(_sparse_attn_kernel, q_off=q_off, n_keys=n_keys, k_sel=k_sel, group=group),
        out_shape=jax.ShapeDtypeStruct((b, t, dq), BF16),
        grid=(b, t // tq),
        in_specs=[qblk(dq), qblk(qi.shape[2]), qblk(LANES), kblk(2 * dkv),
                  pl.BlockSpec((1, nkb, N_KV_HEADS, HEAD_DIM, KEY_BLOCK), lambda bi, i: (bi, 0, 0, 0, 0)),
                  kblk(kib.shape[2]), _const_spec(bias_tiles.shape)],
        out_specs=qblk(dq),
        scratch_shapes=[
            pltpu.VMEM((nkb + 1 + COUNT_UNROLL, KEY_BLOCK, tq), F32),
            pltpu.VMEM((nkb // 2 + COUNT_UNROLL, KEY_BLOCK, tq), jnp.int32),
            pltpu.VMEM((4, KEY_BLOCK, IDX_HEADS * tq), F32),
            pltpu.VMEM((n_tiles, 2 * HEAD_DIM, hpt * tq), BF16),
            pltpu.VMEM((IDX_DIM, IDX_HEADS * tq), BF16),
            pltpu.VMEM((dq, tq), F32),
            pltpu.VMEM((2, n_tiles, 2 * KEY_BLOCK, hpt * tq), F32),
            pltpu.VMEM((2, n_tiles, 1, hpt * tq), F32),
        ] + [pltpu.VMEM((1, hpt * tq), F32)] * n_tiles
          + [pltpu.VMEM((V_AUG_ROWS, hpt * tq), F32)] * n_tiles,
        compiler_params=_params("parallel", "arbitrary"),
        name="sparse_attn",
    )(q, qi, wi, k2, vt, kib, bias_tiles)


def _mem_kv_kernel(x_ref, g_ref, w_ref, k_ref, v_ref, *, d):
    h = _rms(x_ref[...], g_ref[...]).astype(BF16)
    k_ref[...] = _dot(h, w_ref[:, :d])
    v_ref[...] = _dot(h, w_ref[:, d:])


def _mem_kv(mem2, g, w_kv, layer):
    m, d = mem2.shape
    tm = _row_tile(m, ROW_TILE)
    row = pl.BlockSpec((tm, d), lambda i: (i, 0))
    return pl.pallas_call(
        functools.partial(_mem_kv_kernel, d=d),
        out_shape=[jax.ShapeDtypeStruct((m, d), F32)] * 2,
        grid=(m // tm,),
        in_specs=[row, _const_spec((1, d)), _layer_spec(w_kv, layer)],
        out_specs=[row, row],
        compiler_params=_params("parallel"),
        name="mem_kv",
    )(mem2, g.reshape(1, d), w_kv)


def _mem_attn_kernel(*refs, hd, pending):
    if pending:
        x_ref, a_ref, wa_ref, g_ref, wq_ref, mk_ref, mv_ref, wo_ref, o_ref = refs
        x = x_ref[...] + _dot(a_ref[...], wa_ref[...])
    else:
        x_ref, g_ref, wq_ref, mk_ref, mv_ref, wo_ref, o_ref = refs
        x = x_ref[...]
    h = _rms(x, g_ref[...]).astype(BF16)
    q = (_dot(h, wq_ref[...]) * (hd ** -0.5)).astype(BF16)
    heads = []
    for a in range(MEM_HEADS):
        cols = slice(a * hd, (a + 1) * hd)
        lg = _dot_nt(q[:, cols], mk_ref[0, :, cols].astype(BF16))
        p = jnp.exp(lg - jnp.max(lg, axis=1, keepdims=True))
        p = (p / jnp.sum(p, axis=1, keepdims=True)).astype(BF16)
        heads.append(_dot(p, mv_ref[0, :, cols].astype(BF16)).astype(BF16))
    o_ref[...] = x + _dot(jnp.concatenate(heads, axis=1), wo_ref[...])


def _mem_attn(x2, g, w_q, mk, mv, w_o, layer, rows_per_batch, pending=None):
    m, d = x2.shape
    n_mem = mk.shape[1]
    tm = _row_tile(rows_per_batch, ROW_TILE)
    per = rows_per_batch // tm
    row = pl.BlockSpec((tm, d), lambda i: (i, 0))
    mem = pl.BlockSpec((1, n_mem, d), lambda i: (i // per, 0, 0))
    in_specs = [row, _const_spec((1, d)), _layer_spec(w_q, layer), mem, mem, _layer_spec(w_o, layer)]
    args = [x2, g.reshape(1, d), w_q, mk, mv, w_o]
    if pending is not None:
        a2, w_a, layer_a = pending
        in_specs[1:1] = [pl.BlockSpec((tm, a2.shape[1]), lambda i: (i, 0)), _layer_spec(w_a, layer_a)]
        args[1:1] = [a2, w_a]
    return pl.pallas_call(
        functools.partial(_mem_attn_kernel, hd=d // MEM_HEADS, pending=pending is not None),
        out_shape=jax.ShapeDtypeStruct((m, d), F32),
        grid=(m // tm,),
        in_specs=in_specs,
        out_specs=row,
        compiler_params=_params("parallel"),
        name="mem_attn",
    )(*args)


def _mlp_kernel(*refs, n_chunks, chunk, final):
    if final:
        x_ref, g_ref, w1_ref, w2_ref, gf_ref, o_ref = refs
    else:
        x_ref, g_ref, w1_ref, w2_ref, o_ref = refs
    x = x_ref[...]
    h = _rms(x, g_ref[...]).astype(BF16)
    acc = x
    for c in range(n_chunks):
        a = jnp.maximum(_dot(h, w1_ref[:, c * chunk:(c + 1) * chunk]), 0.0)
        acc = acc + _dot((a * a).astype(BF16), w2_ref[c * chunk:(c + 1) * chunk, :])
    if final:
        acc = _rms(acc, gf_ref[...])
    o_ref[...] = acc


def _mlp(x2, g, w1, w2, layer, g_final=None):
    m, d = x2.shape
    dff = w1.shape[2]
    chunk = min(dff, 1024)
    tm = _row_tile(m, ROW_TILE)
    row = pl.BlockSpec((tm, d), lambda i: (i, 0))
    final = g_final is not None
    in_specs = [row, _const_spec((1, d)), _layer_spec(w1, layer), _layer_spec(w2, layer)]
    args = [x2, g.reshape(1, d), w1, w2]
    if final:
        in_specs.append(_const_spec((1, d)))
        args.append(g_final.reshape(1, d))
    return pl.pallas_call(
        functools.partial(_mlp_kernel, n_chunks=dff // chunk, chunk=chunk, final=final),
        out_shape=jax.ShapeDtypeStruct((m, d), F32),
        grid=(m // tm,),
        in_specs=in_specs,
        out_specs=row,
        compiler_params=_params("parallel"),
        name="mlp",
    )(*args)


def _conv_glu_kernel(x_ref, g_ref, w_ref, b_ref, u_ref, *, d):
    h = _rms(x_ref[...], g_ref[...]).astype(BF16)
    a = _dot(h, w_ref[:, :d]) + b_ref[:, :d]
    gate = _dot(h, w_ref[:, d:]) + b_ref[:, d:]
    u_ref[...] = a * (1.0 / (1.0 + jnp.exp(-gate)))


def _conv_glu(x2, g, w_pw1, layer, b_pw1):
    m, d = x2.shape
    tm = _row_tile(m, ROW_TILE)
    row = pl.BlockSpec((tm, d), lambda i: (i, 0))
    return pl.pallas_call(
        functools.partial(_conv_glu_kernel, d=d),
        out_shape=jax.ShapeDtypeStruct((m, d), F32),
        grid=(m // tm,),
        in_specs=[row, _const_spec((1, d)), _layer_spec(w_pw1, layer), _const_spec((1, 2 * d))],
        out_specs=row,
        compiler_params=_params("parallel"),
        name="conv_glu",
    )(x2, g.reshape(1, d), w_pw1, b_pw1.reshape(1, 2 * d))


def _conv_rest_kernel(x_ref, u_ref, prev_ref, init_ref, wdw_ref, bdw_ref, lng_ref, lnb_ref,
                      w2_ref, b2_ref, o_ref, ext_ref, sh_ref, y_ref, *, tm, rc, lc):
    t = pl.program_id(1)
    d = u_ref.shape[2]
    pad = CONV_WIDTH - 1

    @pl.when(t == 0)
    def _():
        ext_ref[0:HALO, :] = init_ref[0]

    @pl.when(t > 0)
    def _():
        ext_ref[0:HALO, :] = prev_ref[0]

    ext_ref[HALO:HALO + tm, :] = u_ref[0]

    for s in range(SUBLANES):
        rows = tm + SUBLANES * ((CONV_WIDTH - 1 - s) // SUBLANES)
        sh_ref[s, 0:rows, :] = ext_ref[pl.ds(HALO - pad + s, rows), :]
    for r0 in range(0, tm, rc):
        for c0 in range(0, d, lc):
            cols = slice(c0, c0 + lc)
            y = jnp.broadcast_to(bdw_ref[:, cols], (rc, lc))
            for w in range(CONV_WIDTH):
                a, s = divmod(w, SUBLANES)
                y = y + sh_ref[s, r0 + SUBLANES * a:r0 + SUBLANES * a + rc, cols] * wdw_ref[w:w + 1, cols]
            y_ref[r0:r0 + rc, cols] = y

    y = y_ref[...]
    mu = jnp.mean(y, axis=-1, keepdims=True)
    yc = y - mu
    var = jnp.mean(yc * yc, axis=-1, keepdims=True)
    yn = yc * lax.rsqrt(var + EPS) * lng_ref[...] + lnb_ref[...]
    act = (yn * (1.0 / (1.0 + jnp.exp(-yn)))).astype(BF16)
    o_ref[0] = x_ref[0] + _dot(act, w2_ref[...]) + b2_ref[...]


def _conv_rest(x3, u3, init, w_dw, b_dw, ln_g, ln_b, w_pw2, layer, b_pw2):
    b, t, d = x3.shape
    tm = _row_tile(t, CONV_ROW_TILE)
    rc = min(tm, 64)
    lc = min(d, 256)
    assert tm % HALO == 0 and tm % rc == 0 and d % lc == 0
    per = tm // HALO
    tile = pl.BlockSpec((1, tm, d), lambda bi, ti: (bi, ti, 0))
    prev = pl.BlockSpec((1, HALO, d), lambda bi, ti: (bi, jnp.maximum(ti * per - 1, 0), 0))
    first = pl.BlockSpec((1, HALO, d), lambda bi, ti: (bi, 0, 0))
    vec = _const_spec((1, d))
    wdw = jnp.pad(w_dw, ((0, HALO - CONV_WIDTH), (0, 0)))
    return pl.pallas_call(
        functools.partial(_conv_rest_kernel, tm=tm, rc=rc, lc=lc),
        out_shape=jax.ShapeDtypeStruct((b, t, d), F32),
        grid=(b, t // tm),
        in_specs=[tile, tile, prev, first, _const_spec((HALO, d)), vec, vec, vec,
                  _layer_spec(w_pw2, layer), vec],
        out_specs=tile,
        scratch_shapes=[pltpu.VMEM((HALO + tm, d), F32),
                        pltpu.VMEM((SUBLANES, tm + HALO - SUBLANES, d), F32),
                        pltpu.VMEM((tm, d), F32)],
        compiler_params=_params("parallel", "arbitrary"),
        name="conv_rest",
    )(x3, u3, u3, init, wdw, b_dw.reshape(1, d), ln_g.reshape(1, d), ln_b.reshape(1, d),
      w_pw2, b_pw2.reshape(1, d))


def _mixer_attn(x3, g, w_in, layer, bias_tiles, cache=None):
    b, t, d = x3.shape
    x2 = x3.reshape(b * t, d)
    q, k, v, ki, kb, vb, kib, qi, wi = _attn_proj(x2, g, w_in, layer)
    r3 = lambda a: a.reshape(b, t, a.shape[-1])
    kb, vb, kib = r3(kb), r3(vb), r3(kib)
    past = 0
    if cache is not None:
        ck, cv, cki = cache
        past = ck.shape[1]
        ck2 = jnp.concatenate([ck[:, :, n].astype(BF16) for n in range(N_KV_HEADS) for _ in range(2)], axis=-1)
        kb = jnp.concatenate([ck2, kb], axis=1)
        vb = jnp.concatenate([cv.reshape(b, past, -1).astype(BF16), vb], axis=1)
        kib = jnp.concatenate([cki.astype(BF16), kib], axis=1)
    n_keys = past + t
    k_sel = min(TOPK_MAX, n_keys // 4)
    tpad = -(-t // KEY_BLOCK) * KEY_BLOCK
    lp = max(-(-n_keys // KEY_BLOCK) * KEY_BLOCK, past + tpad)
    padt = lambda a, n: a if a.shape[1] == n else jnp.pad(a, ((0, 0), (0, n - a.shape[1]), (0, 0)))
    o = _sparse_attn(padt(r3(q), tpad), padt(r3(qi), tpad), padt(r3(wi), tpad),
                     padt(kb, lp), padt(vb, lp), padt(kib, lp), bias_tiles,
                     q_off=past, n_keys=n_keys, k_sel=k_sel)[:, :t]
    return (o.reshape(b * t, d), k.reshape(b, t, N_KV_HEADS, HEAD_DIM),
            v.reshape(b, t, N_KV_HEADS, HEAD_DIM), ki.reshape(b, t, IDX_DIM))


def _mixer_conv(x3, g, w_pw1, b_pw1, w_dw, b_dw, ln_g, ln_b, w_pw2, b_pw2, layer, state=None):
    b, t, d = x3.shape
    pad = CONV_WIDTH - 1
    u3 = _conv_glu(x3.reshape(b * t, d), g, w_pw1, layer, b_pw1).reshape(b, t, d)
    if state is None:
        init = jnp.zeros((b, HALO, d), F32)
        tail = u3[:, -pad:] if t >= pad else jnp.pad(u3, ((0, 0), (pad - t, 0), (0, 0)))
    else:
        init = jnp.pad(state.astype(F32), ((0, 0), (HALO - pad, 0), (0, 0)))
        tail = jnp.concatenate([state.astype(F32), u3], axis=1)[:, -pad:]
    x3 = _conv_rest(x3, u3, init, w_dw, b_dw, ln_g, ln_b, w_pw2, layer, b_pw2)
    return x3, tail


def kernel(x_prompt, x_sample, cache_attn_k, cache_attn_v, cache_attn_kidx, state_conv, cache_mem_k, cache_mem_v, mem_prompt, rel_bias, g_mix, w_in_attn, w_out_attn, w_pw1, b_pw1, w_dw, b_dw, ln_g, ln_b, w_pw2, b_pw2, g_mem_q, g_mem_src, w_mem_q, w_mem_kv, w_mem_o, g_mlp, w_mlp1, w_mlp2, g_final):
    depth = g_mix.shape[0]
    bp, tp, d = x_prompt.shape
    bs, ts, _ = x_sample.shape
    n_mem = mem_prompt.shape[1]
    mem_hd = d // MEM_HEADS
    bias_tiles = _bias_tiles(rel_bias)
    w_in_b, w_out_b = _attn_proj_weights(w_in_attn), w_out_attn.astype(BF16)
    w_pw1_b, w_pw2_b = w_pw1.astype(BF16), w_pw2.astype(BF16)
    w_mq_b, w_mkv_b, w_mo_b = w_mem_q.astype(BF16), w_mem_kv.astype(BF16), w_mem_o.astype(BF16)
    w_mlp1_b, w_mlp2_b = w_mlp1.astype(BF16), w_mlp2.astype(BF16)
    xp, xs = x_prompt, x_sample
    kp_l, vp_l, kip_l, ks_l, vs_l, kis_l = [], [], [], [], [], []
    convp_l, convs_l, memk_l, memv_l = [], [], [], []
    for i in range(depth):
        j = i // 2
        if i % 2 == 0:
            op, kp, vp, kip = _mixer_attn(xp, g_mix[i], w_in_b, j, bias_tiles)
            osm, ks, vs, kis = _mixer_attn(
                xs, g_mix[i], w_in_b, j, bias_tiles,
                cache=(cache_attn_k[j], cache_attn_v[j], cache_attn_kidx[j]))
            pend_p, pend_s = (op, w_out_b, j), (osm, w_out_b, j)
            kp_l.append(kp); vp_l.append(vp); kip_l.append(kip)
            ks_l.append(ks); vs_l.append(vs); kis_l.append(kis)
        else:
            cw = (w_pw1_b, b_pw1[j], w_dw[j], b_dw[j], ln_g[j], ln_b[j], w_pw2_b, b_pw2[j], j)
            xp, cp = _mixer_conv(xp, g_mix[i], *cw)
            xs, cs = _mixer_conv(xs, g_mix[i], *cw, state=state_conv[j])
            pend_p = pend_s = None
            convp_l.append(cp); convs_l.append(cs)
        mk, mv = _mem_kv(mem_prompt.reshape(bp * n_mem, d), g_mem_src[i], w_mkv_b, i)
        mk, mv = mk.reshape(bp, n_mem, d), mv.reshape(bp, n_mem, d)
        memk_l.append(mk.reshape(bp, n_mem, MEM_HEADS, mem_hd))
        memv_l.append(mv.reshape(bp, n_mem, MEM_HEADS, mem_hd))
        xp2 = _mem_attn(xp.reshape(bp * tp, d), g_mem_q[i], w_mq_b, mk, mv, w_mo_b, i, tp, pend_p)
        xs2 = _mem_attn(xs.reshape(bs * ts, d), g_mem_q[i], w_mq_b,
                        cache_mem_k[i].reshape(bs, n_mem, d), cache_mem_v[i].reshape(bs, n_mem, d),
                        w_mo_b, i, ts, pend_s)
        gf = g_final if i == depth - 1 else None
        xp = _mlp(xp2, g_mlp[i], w_mlp1_b, w_mlp2_b, i, gf).reshape(bp, tp, d)
        xs = _mlp(xs2, g_mlp[i], w_mlp1_b, w_mlp2_b, i, gf).reshape(bs, ts, d)
    return (xp, xs, jnp.stack(kp_l), jnp.stack(vp_l), jnp.stack(kip_l), jnp.stack(convp_l),
            jnp.stack(memk_l), jnp.stack(memv_l), jnp.stack(ks_l), jnp.stack(vs_l),
            jnp.stack(kis_l), jnp.stack(convs_l))
```

```python
import functools
import math

import jax
import jax.numpy as jnp
from jax import lax
from jax.experimental import pallas as pl
from jax.experimental.pallas import tpu as pltpu

CHUNK = 64
HEAD_DIM = 64
N_KV_HEADS = 4
IDX_HEADS = 8
IDX_DIM = 64
TOPK_MAX = 256
N_BUCKETS = 32
MAX_DISTANCE = 128
CONV_WIDTH = 31
MEM_HEADS = 4
EPS = 1e-6

LANES = 128
SUBLANES = 8
KEY_BLOCK = 128
HEADS_PER_TILE = 2
COUNT_UNROLL = 4
HI_BITS = 15
LO_BITS = 32 - HI_BITS
LOW_CHECK_EVERY = 4
V_AUG_ROWS = HEAD_DIM + 16
LOG2E = math.log2(math.e)
ROW_TILE = 1024
CONV_ROW_TILE = 512
HALO = 32
VMEM_LIMIT = 56 * 1024 * 1024

NEG_BIG = -1e30
F32 = jnp.float32
BF16 = jnp.bfloat16

KEY_NEG_INF = -2139095041
KEY_POS_INF = 2139095040


def _const_spec(shape):
    nd = len(shape)
    return pl.BlockSpec(shape, lambda *_: (0,) * nd, pipeline_mode=pl.Buffered(1))


def _layer_spec(stacked, layer):
    nd = stacked.ndim
    return pl.BlockSpec((None,) + stacked.shape[1:], lambda *_: (layer,) + (0,) * (nd - 1),
                        pipeline_mode=pl.Buffered(1))


def _params(*sem):
    return pltpu.CompilerParams(dimension_semantics=sem, vmem_limit_bytes=VMEM_LIMIT)


def _rms(x, g):
    ms = jnp.mean(x * x, axis=-1, keepdims=True)
    return x * lax.rsqrt(ms + EPS) * g


def _dot(a, b):
    return jnp.dot(a, b, preferred_element_type=F32)


def _dot_nt(a, b):
    return lax.dot_general(a, b, (((1,), (1,)), ((), ())), preferred_element_type=F32)


def _row_tile(m, pref):
    t = min(m, pref)
    assert m % t == 0, (m, t)
    return t


def _bias_tiles_kernel(tab_ref, bt_ref, *, n_heads):
    nb = N_BUCKETS // 2
    max_exact = nb // 2
    c = lax.broadcasted_iota(jnp.int32, (KEY_BLOCK, KEY_BLOCK), 0)
    r = lax.broadcasted_iota(jnp.int32, (KEY_BLOCK, KEY_BLOCK), 1)
    for d in range(2):
        rel = c - r - d * KEY_BLOCK
        n = jnp.abs(rel)
        nf = jnp.maximum(n, 1).astype(F32)
        large = max_exact + (jnp.log(nf / max_exact) / math.log(MAX_DISTANCE / max_exact)
                             * (nb - max_exact)).astype(jnp.int32)
        large = jnp.minimum(large, nb - 1)
        bucket = jnp.where(rel > 0, nb, 0) + jnp.where(n < max_exact, n, large)
        for h in range(n_heads):
            acc = jnp.zeros((KEY_BLOCK, KEY_BLOCK), F32)
            for b in range(N_BUCKETS):
                acc = jnp.where(bucket == b, tab_ref[b, h], acc)
            g = h % HEADS_PER_TILE
            rows = slice((1 - d) * KEY_BLOCK, (2 - d) * KEY_BLOCK)
            bt_ref[h // HEADS_PER_TILE, rows, g * KEY_BLOCK:(g + 1) * KEY_BLOCK] = (
                (acc - tab_ref[nb - 1, h]) * LOG2E)


def _bias_tiles(rel_bias):
    n_heads = rel_bias.shape[1]
    assert n_heads % HEADS_PER_TILE == 0
    return pl.pallas_call(
        functools.partial(_bias_tiles_kernel, n_heads=n_heads),
        out_shape=jax.ShapeDtypeStruct(
            (n_heads // HEADS_PER_TILE, 2 * KEY_BLOCK, HEADS_PER_TILE * KEY_BLOCK), F32),
        in_specs=[pl.BlockSpec(memory_space=pltpu.SMEM)],
        out_specs=pl.BlockSpec(memory_space=pltpu.VMEM),
        name="bias_tiles",
    )(rel_bias)


def _attn_proj_kernel(x_ref, g_ref, w_ref, q_ref, k_ref, v_ref, ki_ref, kb_ref, vb_ref,
                      kib_ref, qi_ref, wi_ref, *, dq, dkv, dqi):
    h = _rms(x_ref[...], g_ref[...]).astype(BF16)
    o = 0
    q_ref[...] = (_dot(h, w_ref[:, o:o + dq]) * (HEAD_DIM ** -0.5)).astype(BF16)
    o += dq
    k = _dot(h, w_ref[:, o:o + dkv])
    o += dkv
    v = _dot(h, w_ref[:, o:o + dkv])
    vb_ref[...] = v.astype(BF16)
    k_ref[...] = k.reshape(k_ref.shape)
    v_ref[...] = v.reshape(v_ref.shape)
    o += dkv
    qi_ref[...] = (_dot(h, w_ref[:, o:o + dqi]) * (IDX_DIM ** -0.5)).astype(BF16)
    o += dqi
    ki = _dot(h, w_ref[:, o:o + LANES])[:, :IDX_DIM]
    ki_ref[...] = ki
    kib_ref[...] = ki.astype(BF16)
    o += LANES
    wi_ref[...] = _dot(h, w_ref[:, o:o + LANES]) * (IDX_HEADS ** -0.5)
    o += LANES
    kb_ref[...] = _dot(h, w_ref[:, o:o + 2 * dkv]).astype(BF16)


def _attn_proj_weights(w_in):
    d = w_in.shape[1]
    base = d + 2 * N_KV_HEADS * HEAD_DIM + IDX_HEADS * IDX_DIM
    pad = lambda a: jnp.pad(a, ((0, 0), (0, 0), (0, LANES - a.shape[2])))
    w_k = [w_in[:, :, d + n * HEAD_DIM:d + (n + 1) * HEAD_DIM] for n in range(N_KV_HEADS)]
    return jnp.concatenate([w_in[:, :, :base], pad(w_in[:, :, base:base + IDX_DIM]),
                            pad(w_in[:, :, base + IDX_DIM:])] + [w for w in w_k for _ in range(2)],
                           axis=2).astype(BF16)


def _attn_proj(x2, g, w, layer):
    m, d = x2.shape
    dq = d
    dkv = N_KV_HEADS * HEAD_DIM
    dqi = IDX_HEADS * IDX_DIM
    tm = _row_tile(m, ROW_TILE)
    row = lambda *n: pl.BlockSpec((tm,) + n, lambda i: (i,) + (0,) * len(n))
    heads = (N_KV_HEADS, HEAD_DIM)
    outs = [((dq,), BF16), (heads, F32), (heads, F32), ((IDX_DIM,), F32), ((2 * dkv,), BF16), ((dkv,), BF16),
            ((IDX_DIM,), BF16), ((dqi,), BF16), ((LANES,), F32)]
    return pl.pallas_call(
        functools.partial(_attn_proj_kernel, dq=dq, dkv=dkv, dqi=dqi),
        out_shape=[jax.ShapeDtypeStruct((m,) + n, dt) for n, dt in outs],
        grid=(m // tm,),
        in_specs=[row(d), _const_spec((1, d)), _layer_spec(w, layer)],
        out_specs=[row(*n) for n, _ in outs],
        compiler_params=_params("parallel"),
        name="attn_proj",
    )(x2, g.reshape(1, d), w)


def _sparse_attn_kernel(q_ref, qi_ref, wi_ref, k_ref, vt_ref, ki_ref, bt_ref, o_ref,
                        s_ref, w_ref, dots_ref, qgt_ref, qit_ref, ot_ref, lg_ref, mb_ref, *state_refs,
                        q_off, n_keys, k_sel, group):
    tq = KEY_BLOCK
    hpt = HEADS_PER_TILE
    n_tiles = qgt_ref.shape[0]
    m_refs, acc_refs = state_refs[:n_tiles], state_refs[n_tiles:]
    i = pl.program_id(1)
    qs = q_off // KEY_BLOCK + i
    n_kb = qs + 1
    q_start = q_off + i * tq

    qt = q_ref[0].astype(F32).T * LOG2E
    for t in range(n_tiles):
        q2 = jnp.concatenate(
            [qt[(t * hpt + g) * HEAD_DIM:(t * hpt + g + 1) * HEAD_DIM, :] for g in range(hpt)], axis=1)
        q_hi = q2.astype(BF16)
        q_lo = (q2 - q_hi.astype(F32)).astype(BF16)
        qgt_ref[t] = jnp.concatenate([q_hi, q_lo], axis=0)
    qit = qi_ref[0].astype(F32).T
    qit_ref[...] = jnp.concatenate([qit[h * IDX_DIM:(h + 1) * IDX_DIM, :] for h in range(IDX_HEADS)],
                                   axis=1).astype(BF16)
    last_kv = k_ref.shape[1] // KEY_BLOCK - 1
    wit = wi_ref[0].T

    qpos = q_start + lax.broadcasted_iota(jnp.int32, (1, tq), 1)
    lim = jnp.minimum((qpos // CHUNK + 1) * CHUNK, n_keys)
    kidx = lax.broadcasted_iota(jnp.int32, (KEY_BLOCK, tq), 0)

    def key_to_f32(key):
        bits = key ^ ((key >> 31) & 0x7FFFFFFF)
        return lax.bitcast_convert_type(bits, F32)

    def digit(sc):
        bits = lax.bitcast_convert_type(jnp.where(sc == 0.0, 0.0, sc), jnp.int32)
        key = bits ^ ((bits >> 31) & 0x7FFFFFFF)
        return (key >> LO_BITS) + (1 << (HI_BITS - 1))

    def dots(j, slot):
        k0 = pl.multiple_of(jnp.minimum(j, last_kv) * KEY_BLOCK, KEY_BLOCK)
        dots_ref[slot] = _dot(ki_ref[0, pl.ds(k0, KEY_BLOCK), :], qit_ref[...])

    def finish(j, slot):
        acc = jnp.zeros((KEY_BLOCK, tq), F32)
        for h in range(IDX_HEADS):
            acc = acc + wit[h:h + 1, :] * jnp.maximum(dots_ref[slot, :, h * tq:(h + 1) * tq], 0.0)
        sc = jnp.where(kidx + j * KEY_BLOCK < lim, acc, -jnp.inf)
        s_ref[j] = sc
        return digit(sc)

    def finish_pair(j, slot):
        w_ref[j // 2] = (finish(j, slot) << 16) | finish(j + 1, slot + 1)

    def score_quad(quad, carry):
        j = 4 * quad
        dots(j + 2, 2)
        dots(j + 3, 3)
        finish_pair(j, 0)
        dots(j + 4, 0)
        dots(j + 5, 1)
        finish_pair(j + 2, 2)
        return carry

    n_pairs = (n_kb + 1) // 2
    n_quads = (n_kb + 3) // 4
    dots(0, 0)
    dots(1, 1)
    lax.fori_loop(0, n_quads, score_quad, 0)
    for u in range(COUNT_UNROLL):
        s_ref[2 * n_pairs + u] = jnp.full((KEY_BLOCK, tq), -jnp.inf, F32)
    for u in range(COUNT_UNROLL // 2):
        w_ref[2 * n_quads + u] = jnp.zeros((KEY_BLOCK, tq), jnp.int32)
    n_count = (2 * n_pairs + COUNT_UNROLL - 1) // COUNT_UNROLL

    def count_digits(cand):
        comp = (1 << HI_BITS) - cand
        cw = jnp.broadcast_to((comp << 16) | comp, (KEY_BLOCK, tq))

        def body(p, c):
            for u in range(COUNT_UNROLL // 2):
                c = c + (((w_ref[(COUNT_UNROLL // 2) * p + u] + cw) >> HI_BITS) & 0x00010001)
            return c
        c = lax.fori_loop(0, n_count, body, jnp.zeros((KEY_BLOCK, tq), jnp.int32))
        return jnp.sum(((c & 0xFFFF) + (c >> 16)).astype(F32), axis=0, keepdims=True)

    def count_ge(cand):
        cb = jnp.broadcast_to(cand, (KEY_BLOCK, tq))

        def body(p, c):
            for u in range(COUNT_UNROLL):
                c = c + jnp.where(s_ref[COUNT_UNROLL * p + u] >= cb, 1.0, 0.0)
            return c
        c = lax.fori_loop(0, n_count, body, jnp.zeros((KEY_BLOCK, tq), F32))
        return jnp.sum(c, axis=0, keepdims=True)

    def bisect_step(count_at, carry):
        lo, hi, n_lo = carry
        mid = (lo + hi) >> 1
        n_mid = count_at(mid)
        ok = n_mid >= k_sel
        return jnp.where(ok, mid, lo), jnp.where(ok, hi, mid), jnp.where(ok, n_mid, n_lo)

    full = lambda v: jnp.full((1, tq), v, jnp.int32)
    half = 1 << (HI_BITS - 1)
    dig, _, n_sel = lax.fori_loop(
        0, HI_BITS, lambda _, c: bisect_step(count_digits, c),
        (full(((KEY_NEG_INF + 1) >> LO_BITS) + half), full((KEY_POS_INF >> LO_BITS) + half + 1),
         jnp.full((1, tq), jnp.inf, F32)))
    key_hi = (dig - half) << LO_BITS

    count_lo = lambda v: count_ge(key_to_f32(key_hi + v))

    def low_steps(carry):
        step, c = carry[0], carry[1:]
        for _ in range(LOW_CHECK_EVERY):
            c = bisect_step(count_lo, c)
        return (step + LOW_CHECK_EVERY,) + c

    def low_unfinished(carry):
        step, n_lo = carry[0], carry[3]
        return (step < LO_BITS) & (jnp.max(jnp.abs(n_lo - k_sel)) > 0.0)

    _, key_lo, _, n_sel = lax.while_loop(low_unfinished, low_steps, (0, full(0), full(1 << LO_BITS), n_sel))
    thr = key_to_f32(key_hi + key_lo)

    def count(cand, strict):
        cb = jnp.broadcast_to(cand, (KEY_BLOCK, tq))
        hit = (lambda s: s > cb) if strict else (lambda s: s >= cb)

        def body(p, c):
            c = c + jnp.where(hit(s_ref[2 * p]), 1.0, 0.0)
            return c + jnp.where(hit(s_ref[2 * p + 1]), 1.0, 0.0)
        c = lax.fori_loop(0, n_pairs, body, jnp.zeros((KEY_BLOCK, tq), F32))
        return jnp.sum(c, axis=0, keepdims=True)

    surplus = jnp.where(n_sel < jnp.inf, n_sel - k_sel, 0.0)

    @pl.when(jnp.max(surplus) > 0.0)
    def _():
        n_ties = k_sel - count(thr, True)
        row_i = lax.broadcasted_iota(jnp.int32, (KEY_BLOCK, KEY_BLOCK), 0)
        col_i = lax.broadcasted_iota(jnp.int32, (KEY_BLOCK, KEY_BLOCK), 1)
        lower = jnp.where(col_i < row_i, 1.0, 0.0).astype(BF16)
        ones = jnp.ones((KEY_BLOCK, KEY_BLOCK), BF16)

        def tie_body(j, seen):
            s = s_ref[j]
            eq = s == thr
            e = jnp.where(eq, 1.0, 0.0).astype(BF16)
            before = _dot(lower, e) + seen
            s_ref[j] = jnp.where(eq & (before >= n_ties), -jnp.inf, s)
            return seen + _dot(ones, e)

        lax.fori_loop(0, n_kb, tie_body, jnp.zeros((KEY_BLOCK, tq), F32))

    thr_sel = jnp.broadcast_to(jnp.maximum(thr, jnp.finfo(F32).min), (KEY_BLOCK, tq))

    for t in range(n_tiles):
        m_refs[t][...] = jnp.full(m_refs[t].shape, NEG_BIG, F32)
        acc_refs[t][...] = jnp.zeros(acc_refs[t].shape, F32)

    def far_blocks(db):
        out = []
        for j in (2 * db, 2 * db + 1):
            js = jnp.where(j < qs - 1, j, n_kb)
            out.append((js, jnp.minimum(js, last_kv)))
        return out

    near = [(jnp.where(qs >= 1, qs - 1, n_kb), jnp.maximum(qs - 1, 0)), (qs, qs)]

    def mask_of(blocks):
        m = jnp.concatenate([jnp.where(s_ref[js] >= thr_sel, 0.0, NEG_BIG) for js, _ in blocks], axis=0)
        return jnp.concatenate([m] * hpt, axis=1)

    def logits(blocks, mask, biased, slot, t):
        n = (t * hpt) // group
        kn = jnp.concatenate(
            [k_ref[0, pl.ds(pl.multiple_of(jk * KEY_BLOCK, KEY_BLOCK), KEY_BLOCK),
                   n * 2 * HEAD_DIM:(n + 1) * 2 * HEAD_DIM] for _, jk in blocks], axis=0)
        lg = _dot(kn, qgt_ref[t]) + mask
        if biased:
            lg = lg + bt_ref[t]
        lg_ref[slot, t] = lg
        mb_ref[slot, t] = jnp.max(lg, axis=0, keepdims=True)

    def softmax_update(blocks, slot, t):
        n = (t * hpt) // group
        m_old = m_refs[t][...]
        m_new = jnp.maximum(m_old, mb_ref[slot, t])
        alpha = jnp.exp2(m_old - m_new)
        p = jnp.exp2(lg_ref[slot, t] - m_new).astype(BF16)
        m_refs[t][...] = m_new
        vtn = jnp.concatenate([vt_ref[0, jk, n] for _, jk in blocks], axis=1)
        acc_refs[t][...] = alpha * acc_refs[t][...] + _dot(jnp.concatenate([vtn, ones_rows], axis=0), p)

    row_id = lax.broadcasted_iota(jnp.int32, (V_AUG_ROWS - HEAD_DIM, 2 * KEY_BLOCK), 0)
    ones_rows = jnp.where(row_id == 0, 1.0, 0.0).astype(BF16)

    n_far = qs // 2
    mask_near = mask_of(near)
    for t in range(n_tiles):
        logits(near, mask_near, True, 0, t)

    def trip_body(trip, carry):
        d0 = 2 * trip
        blk1, blk2 = far_blocks(d0), far_blocks(d0 + 1)
        mask1, mask2 = mask_of(blk1), mask_of(blk2)
        far0 = far_blocks(d0 - 1)
        blk0 = [tuple(jnp.where(d0 == 0, a, b) for a, b in zip(near[h], far0[h])) for h in range(2)]
        for t in range(n_tiles):
            logits(blk1, mask1, False, 1, t)
            softmax_update(blk0, 0, t)
        for t in range(n_tiles):
            logits(blk2, mask2, False, 0, t)
            softmax_update(blk1, 1, t)
        return carry

    lax.fori_loop(0, (n_far + 2) // 2, trip_body, 0)

    for t in range(n_tiles):
        on = acc_refs[t][0:HEAD_DIM, :] / acc_refs[t][HEAD_DIM:HEAD_DIM + 1, :]
        for g in range(hpt):
            hh = t * hpt + g
            ot_ref[hh * HEAD_DIM:(hh + 1) * HEAD_DIM, :] = on[:, g * tq:(g + 1) * tq]
    o_ref[0] = ot_ref[...].T.astype(BF16)


def _sparse_attn(q, qi, wi, k2, vb, kib, bias_tiles, *, q_off, n_keys, k_sel):
    b, t, dq = q.shape
    lp = vb.shape[1]
    dkv = vb.shape[2]
    tq = KEY_BLOCK
    n_heads = dq // HEAD_DIM
    group = n_heads // N_KV_HEADS
    hpt = HEADS_PER_TILE
    n_tiles = n_heads // hpt
    assert group % hpt == 0
    nkb = lp // KEY_BLOCK
    assert t % tq == 0 and lp % KEY_BLOCK == 0 and q_off % KEY_BLOCK == 0 and q_off + t <= lp
    vt = jnp.swapaxes(vb.reshape(b, nkb, KEY_BLOCK, dkv), 2, 3).reshape(
        b, nkb, N_KV_HEADS, HEAD_DIM, KEY_BLOCK)
    qblk = lambda n: pl.BlockSpec((1, tq, n), lambda bi, i: (bi, i, 0))
    kblk = lambda n: pl.BlockSpec((1, lp, n), lambda bi, i: (bi, 0, 0))
    return pl.pallas_call(
        functools.partial(_sparse_attn_kernel, q_off=q_off, n_keys=n_keys, k_sel=k_sel, group=group),
        out_shape=jax.ShapeDtypeStruct((b, t, dq), BF16),
        grid=(b, t // tq),
        in_specs=[qblk(dq), qblk(qi.shape[2]), qblk(LANES), kblk(2 * dkv),
                  pl.BlockSpec((1, nkb, N_KV_HEADS, HEAD_DIM, KEY_BLOCK), lambda bi, i: (bi, 0, 0, 0, 0)),
                  kblk(kib.shape[2]), _const_spec(bias_tiles.shape)],
        out_specs=qblk(dq),
        scratch_shapes=[
            pltpu.VMEM((nkb + 1 + COUNT_UNROLL, KEY_BLOCK, tq), F32),
            pltpu.VMEM((nkb // 2 + COUNT_UNROLL, KEY_BLOCK, tq), jnp.int32),
            pltpu.VMEM((4, KEY_BLOCK, IDX_HEADS * tq), F32),
            pltpu.VMEM((n_tiles, 2 * HEAD_DIM, hpt * tq), BF16),
            pltpu.VMEM((IDX_DIM, IDX_HEADS * tq), BF16),
            pltpu.VMEM((dq, tq), F32),
            pltpu.VMEM((2, n_tiles, 2 * KEY_BLOCK, hpt * tq), F32),
            pltpu.VMEM((2, n_tiles, 1, hpt * tq), F32),
        ] + [pltpu.VMEM((1, hpt * tq), F32)] * n_tiles
          + [pltpu.VMEM((V_AUG_ROWS, hpt * tq), F32)] * n_tiles,
        compiler_params=_params("parallel", "arbitrary"),
        name="sparse_attn",
    )(q, qi, wi, k2, vt, kib, bias_tiles)


def _mem_kv_kernel(x_ref, g_ref, w_ref, k_ref, v_ref, *, d):
    h = _rms(x_ref[...], g_ref[...]).astype(BF16)
    k_ref[...] = _dot(h, w_ref[:, :d])
    v_ref[...] = _dot(h, w_ref[:, d:])


def _mem_kv(mem2, g, w_kv, layer):
    m, d = mem2.shape
    tm = _row_tile(m, ROW_TILE)
    row = pl.BlockSpec((tm, d), lambda i: (i, 0))
    return pl.pallas_call(
        functools.partial(_mem_kv_kernel, d=d),
        out_shape=[jax.ShapeDtypeStruct((m, d), F32)] * 2,
        grid=(m // tm,),
        in_specs=[row, _const_spec((1, d)), _layer_spec(w_kv, layer)],
        out_specs=[row, row],
        compiler_params=_params("parallel"),
        name="mem_kv",
    )(mem2, g.reshape(1, d), w_kv)


def _mem_attn_kernel(*refs, hd, pending):
    if pending:
        x_ref, a_ref, wa_ref, g_ref, wq_ref, mk_ref, mv_ref, wo_ref, o_ref = refs
        x = x_ref[...] + _dot(a_ref[...], wa_ref[...])
    else:
        x_ref, g_ref, wq_ref, mk_ref, mv_ref, wo_ref, o_ref = refs
        x = x_ref[...]
    h = _rms(x, g_ref[...]).astype(BF16)
    q = (_dot(h, wq_ref[...]) * (hd ** -0.5)).astype(BF16)
    heads = []
    for a in range(MEM_HEADS):
        cols = slice(a * hd, (a + 1) * hd)
        lg = _dot_nt(q[:, cols], mk_ref[0, :, cols].astype(BF16))
        p = jnp.exp(lg - jnp.max(lg, axis=1, keepdims=True))
        p = (p / jnp.sum(p, axis=1, keepdims=True)).astype(BF16)
        heads.append(_dot(p, mv_ref[0, :, cols].astype(BF16)).astype(BF16))
    o_ref[...] = x + _dot(jnp.concatenate(heads, axis=1), wo_ref[...])


def _mem_attn(x2, g, w_q, mk, mv, w_o, layer, rows_per_batch, pending=None):
    m, d = x2.shape
    n_mem = mk.shape[1]
    tm = _row_tile(rows_per_batch, ROW_TILE)
    per = rows_per_batch // tm
    row = pl.BlockSpec((tm, d), lambda i: (i, 0))
    mem = pl.BlockSpec((1, n_mem, d), lambda i: (i // per, 0, 0))
    in_specs = [row, _const_spec((1, d)), _layer_spec(w_q, layer), mem, mem, _layer_spec(w_o, layer)]
    args = [x2, g.reshape(1, d), w_q, mk, mv, w_o]
    if pending is not None:
        a2, w_a, layer_a = pending
        in_specs[1:1] = [pl.BlockSpec((tm, a2.shape[1]), lambda i: (i, 0)), _layer_spec(w_a, layer_a)]
        args[1:1] = [a2, w_a]
    return pl.pallas_call(
        functools.partial(_mem_attn_kernel, hd=d // MEM_HEADS, pending=pending is not None),
        out_shape=jax.ShapeDtypeStruct((m, d), F32),
        grid=(m // tm,),
        in_specs=in_specs,
        out_specs=row,
        compiler_params=_params("parallel"),
        name="mem_attn",
    )(*args)


def _mlp_kernel(*refs, n_chunks, chunk, final):
    if final:
        x_ref, g_ref, w1_ref, w2_ref, gf_ref, o_ref = refs
    else:
        x_ref, g_ref, w1_ref, w2_ref, o_ref = refs
    x = x_ref[...]
    h = _rms(x, g_ref[...]).astype(BF16)
    acc = x
    for c in range(n_chunks):
        a = jnp.maximum(_dot(h, w1_ref[:, c * chunk:(c + 1) * chunk]), 0.0)
        acc = acc + _dot((a * a).astype(BF16), w2_ref[c * chunk:(c + 1) * chunk, :])
    if final:
        acc = _rms(acc, gf_ref[...])
    o_ref[...] = acc


def _mlp(x2, g, w1, w2, layer, g_final=None):
    m, d = x2.shape
    dff = w1.shape[2]
    chunk = min(dff, 1024)
    tm = _row_tile(m, ROW_TILE)
    row = pl.BlockSpec((tm, d), lambda i: (i, 0))
    final = g_final is not None
    in_specs = [row, _const_spec((1, d)), _layer_spec(w1, layer), _layer_spec(w2, layer)]
    args = [x2, g.reshape(1, d), w1, w2]
    if final:
        in_specs.append(_const_spec((1, d)))
        args.append(g_final.reshape(1, d))
    return pl.pallas_call(
        functools.partial(_mlp_kernel, n_chunks=dff // chunk, chunk=chunk, final=final),
        out_shape=jax.ShapeDtypeStruct((m, d), F32),
        grid=(m // tm,),
        in_specs=in_specs,
        out_specs=row,
        compiler_params=_params("parallel"),
        name="mlp",
    )(*args)


def _conv_glu_kernel(x_ref, g_ref, w_ref, b_ref, u_ref, *, d):
    h = _rms(x_ref[...], g_ref[...]).astype(BF16)
    a = _dot(h, w_ref[:, :d]) + b_ref[:, :d]
    gate = _dot(h, w_ref[:, d:]) + b_ref[:, d:]
    u_ref[...] = a * (1.0 / (1.0 + jnp.exp(-gate)))


def _conv_glu(x2, g, w_pw1, layer, b_pw1):
    m, d = x2.shape
    tm = _row_tile(m, ROW_TILE)
    row = pl.BlockSpec((tm, d), lambda i: (i, 0))
    return pl.pallas_call(
        functools.partial(_conv_glu_kernel, d=d),
        out_shape=jax.ShapeDtypeStruct((m, d), F32),
        grid=(m // tm,),
        in_specs=[row, _const_spec((1, d)), _layer_spec(w_pw1, layer), _const_spec((1, 2 * d))],
        out_specs=row,
        compiler_params=_params("parallel"),
        name="conv_glu",
    )(x2, g.reshape(1, d), w_pw1, b_pw1.reshape(1, 2 * d))


def _conv_rest_kernel(x_ref, u_ref, prev_ref, init_ref, wdw_ref, bdw_ref, lng_ref, lnb_ref,
                      w2_ref, b2_ref, o_ref, ext_ref, sh_ref, y_ref, *, tm, rc, lc):
    t = pl.program_id(1)
    d = u_ref.shape[2]
    pad = CONV_WIDTH - 1

    @pl.when(t == 0)
    def _():
        ext_ref[0:HALO, :] = init_ref[0]

    @pl.when(t > 0)
    def _():
        ext_ref[0:HALO, :] = prev_ref[0]

    ext_ref[HALO:HALO + tm, :] = u_ref[0]

    for s in range(SUBLANES):
        rows = tm + SUBLANES * ((CONV_WIDTH - 1 - s) // SUBLANES)
        sh_ref[s, 0:rows, :] = ext_ref[pl.ds(HALO - pad + s, rows), :]
    for r0 in range(0, tm, rc):
        for c0 in range(0, d, lc):
            cols = slice(c0, c0 + lc)
            y = jnp.broadcast_to(bdw_ref[:, cols], (rc, lc))
            for w in range(CONV_WIDTH):
                a, s = divmod(w, SUBLANES)
                y = y + sh_ref[s, r0 + SUBLANES * a:r0 + SUBLANES * a + rc, cols] * wdw_ref[w:w + 1, cols]
            y_ref[r0:r0 + rc, cols] = y

    y = y_ref[...]
    mu = jnp.mean(y, axis=-1, keepdims=True)
    yc = y - mu
    var = jnp.mean(yc * yc, axis=-1, keepdims=True)
    yn = yc * lax.rsqrt(var + EPS) * lng_ref[...] + lnb_ref[...]
    act = (yn * (1.0 / (1.0 + jnp.exp(-yn)))).astype(BF16)
    o_ref[0] = x_ref[0] + _dot(act, w2_ref[...]) + b2_ref[...]


def _conv_rest(x3, u3, init, w_dw, b_dw, ln_g, ln_b, w_pw2, layer, b_pw2):
    b, t, d = x3.shape
    tm = _row_tile(t, CONV_ROW_TILE)
    rc = min(tm, 64)
    lc = min(d, 256)
    assert tm % HALO == 0 and tm % rc == 0 and d % lc == 0
    per = tm // HALO
    tile = pl.BlockSpec((1, tm, d), lambda bi, ti: (bi, ti, 0))
    prev = pl.BlockSpec((1, HALO, d), lambda bi, ti: (bi, jnp.maximum(ti * per - 1, 0), 0))
    first = pl.BlockSpec((1, HALO, d), lambda bi, ti: (bi, 0, 0))
    vec = _const_spec((1, d))
    wdw = jnp.pad(w_dw, ((0, HALO - CONV_WIDTH), (0, 0)))
    return pl.pallas_call(
        functools.partial(_conv_rest_kernel, tm=tm, rc=rc, lc=lc),
        out_shape=jax.ShapeDtypeStruct((b, t, d), F32),
        grid=(b, t // tm),
        in_specs=[tile, tile, prev, first, _const_spec((HALO, d)), vec, vec, vec,
                  _layer_spec(w_pw2, layer), vec],
        out_specs=tile,
        scratch_shapes=[pltpu.VMEM((HALO + tm, d), F32),
                        pltpu.VMEM((SUBLANES, tm + HALO - SUBLANES, d), F32),
                        pltpu.VMEM((tm, d), F32)],
        compiler_params=_params("parallel", "arbitrary"),
        name="conv_rest",
    )(x3, u3, u3, init, wdw, b_dw.reshape(1, d), ln_g.reshape(1, d), ln_b.reshape(1, d),
      w_pw2, b_pw2.reshape(1, d))


def _mixer_attn(x3, g, w_in, layer, bias_tiles, cache=None):
    b, t, d = x3.shape
    x2 = x3.reshape(b * t, d)
    q, k, v, ki, kb, vb, kib, qi, wi = _attn_proj(x2, g, w_in, layer)
    r3 = lambda a: a.reshape(b, t, a.shape[-1])
    kb, vb, kib = r3(kb), r3(vb), r3(kib)
    past = 0
    if cache is not None:
        ck, cv, cki = cache
        past = ck.shape[1]
        ck2 = jnp.concatenate([ck[:, :, n].astype(BF16) for n in range(N_KV_HEADS) for _ in range(2)], axis=-1)
        kb = jnp.concatenate([ck2, kb], axis=1)
        vb = jnp.concatenate([cv.reshape(b, past, -1).astype(BF16), vb], axis=1)
        kib = jnp.concatenate([cki.astype(BF16), kib], axis=1)
    n_keys = past + t
    k_sel = min(TOPK_MAX, n_keys // 4)
    tpad = -(-t // KEY_BLOCK) * KEY_BLOCK
    lp = max(-(-n_keys // KEY_BLOCK) * KEY_BLOCK, past + tpad)
    padt = lambda a, n: a if a.shape[1] == n else jnp.pad(a, ((0, 0), (0, n - a.shape[1]), (0, 0)))
    o = _sparse_attn(padt(r3(q), tpad), padt(r3(qi), tpad), padt(r3(wi), tpad),
                     padt(kb, lp), padt(vb, lp), padt(kib, lp), bias_tiles,
                     q_off=past, n_keys=n_keys, k_sel=k_sel)[:, :t]
    return (o.reshape(b * t, d), k.reshape(b, t, N_KV_HEADS, HEAD_DIM),
            v.reshape(b, t, N_KV_HEADS, HEAD_DIM), ki.reshape(b, t, IDX_DIM))


def _mixer_conv(x3, g, w_pw1, b_pw1, w_dw, b_dw, ln_g, ln_b, w_pw2, b_pw2, layer, state=None):
    b, t, d = x3.shape
    pad = CONV_WIDTH - 1
    u3 = _conv_glu(x3.reshape(b * t, d), g, w_pw1, layer, b_pw1).reshape(b, t, d)
    if state is None:
        init = jnp.zeros((b, HALO, d), F32)
        tail = u3[:, -pad:] if t >= pad else jnp.pad(u3, ((0, 0), (pad - t, 0), (0, 0)))
    else:
        init = jnp.pad(state.astype(F32), ((0, 0), (HALO - pad, 0), (0, 0)))
        tail = jnp.concatenate([state.astype(F32), u3], axis=1)[:, -pad:]
    x3 = _conv_rest(x3, u3, init, w_dw, b_dw, ln_g, ln_b, w_pw2, layer, b_pw2)
    return x3, tail


def kernel(x_prompt, x_sample, cache_attn_k, cache_attn_v, cache_attn_kidx, state_conv, cache_mem_k, cache_mem_v, mem_prompt, rel_bias, g_mix, w_in_attn, w_out_attn, w_pw1, b_pw1, w_dw, b_dw, ln_g, ln_b, w_pw2, b_pw2, g_mem_q, g_mem_src, w_mem_q, w_mem_kv, w_mem_o, g_mlp, w_mlp1, w_mlp2, g_final):
    depth = g_mix.shape[0]
    bp, tp, d = x_prompt.shape
    bs, ts, _ = x_sample.shape
    n_mem = mem_prompt.shape[1]
    mem_hd = d // MEM_HEADS
    bias_tiles = _bias_tiles(rel_bias)
    w_in_b, w_out_b = _attn_proj_weights(w_in_attn), w_out_attn.astype(BF16)
    w_pw1_b, w_pw2_b = w_pw1.astype(BF16), w_pw2.astype(BF16)
    w_mq_b, w_mkv_b, w_mo_b = w_mem_q.astype(BF16), w_mem_kv.astype(BF16), w_mem_o.astype(BF16)
    w_mlp1_b, w_mlp2_b = w_mlp1.astype(BF16), w_mlp2.astype(BF16)
    xp, xs = x_prompt, x_sample
    kp_l, vp_l, kip_l, ks_l, vs_l, kis_l = [], [], [], [], [], []
    convp_l, convs_l, memk_l, memv_l = [], [], [], []
    for i in range(depth):
        j = i // 2
        if i % 2 == 0:
            op, kp, vp, kip = _mixer_attn(xp, g_mix[i], w_in_b, j, bias_tiles)
            osm, ks, vs, kis = _mixer_attn(
                xs, g_mix[i], w_in_b, j, bias_tiles,
                cache=(cache_attn_k[j], cache_attn_v[j], cache_attn_kidx[j]))
            pend_p, pend_s = (op, w_out_b, j), (osm, w_out_b, j)
            kp_l.append(kp); vp_l.append(vp); kip_l.append(kip)
            ks_l.append(ks); vs_l.append(vs); kis_l.append(kis)
        else:
            cw = (w_pw1_b, b_pw1[j], w_dw[j], b_dw[j], ln_g[j], ln_b[j], w_pw2_b, b_pw2[j], j)
            xp, cp = _mixer_conv(xp, g_mix[i], *cw)
            xs, cs = _mixer_conv(xs, g_mix[i], *cw, state=state_conv[j])
            pend_p = pend_s = None
            convp_l.append(cp); convs_l.append(cs)
        mk, mv = _mem_kv(mem_prompt.reshape(bp * n_mem, d), g_mem_src[i], w_mkv_b, i)
        mk, mv = mk.reshape(bp, n_mem, d), mv.reshape(bp, n_mem, d)
        memk_l.append(mk.reshape(bp, n_mem, MEM_HEADS, mem_hd))
        memv_l.append(mv.reshape(bp, n_mem, MEM_HEADS, mem_hd))
        xp2 = _mem_attn(xp.reshape(bp * tp, d), g_mem_q[i], w_mq_b, mk, mv, w_mo_b, i, tp, pend_p)
        xs2 = _mem_attn(xs.reshape(bs * ts, d), g_mem_q[i], w_mq_b,
                        cache_mem_k[i].reshape(bs, n_mem, d), cache_mem_v[i].reshape(bs, n_mem, d),
                        w_mo_b, i, ts, pend_s)
        gf = g_final if i == depth - 1 else None
        xp = _mlp(xp2, g_mlp[i], w_mlp1_b, w_mlp2_b, i, gf).reshape(bp, tp, d)
        xs = _mlp(xs2, g_mlp[i], w_mlp1_b, w_mlp2_b, i, gf).reshape(bs, ts, d)
    return (xp, xs, jnp.stack(kp_l), jnp.stack(vp_l), jnp.stack(kip_l), jnp.stack(convp_l),
            jnp.stack(memk_l), jnp.stack(memv_l), jnp.stack(ks_l), jnp.stack(vs_l),
            jnp.stack(kis_l), jnp.stack(convs_l))
```

```python
import functools
import math

import jax
import jax.numpy as jnp
from jax import lax
from jax.experimental import pallas as pl
from jax.experimental.pallas import tpu as pltpu

CHUNK = 64
HEAD_DIM = 64
N_KV_HEADS = 4
IDX_HEADS = 8
IDX_DIM = 64
TOPK_MAX = 256
N_BUCKETS = 32
MAX_DISTANCE = 128
CONV_WIDTH = 31
MEM_HEADS = 4
EPS = 1e-6

LANES = 128
SUBLANES = 8
KEY_BLOCK = 128
HEADS_PER_TILE = 2
COUNT_UNROLL = 4
HI_BITS = 15
LO_BITS = 32 - HI_BITS
LOW_CHECK_EVERY = 4
V_AUG_ROWS = HEAD_DIM + 16
LOG2E = math.log2(math.e)
ROW_TILE = 1024
CONV_ROW_TILE = 512
HALO = 32
VMEM_LIMIT = 56 * 1024 * 1024

NEG_BIG = -1e30
F32 = jnp.float32
BF16 = jnp.bfloat16

KEY_NEG_INF = -2139095041
KEY_POS_INF = 2139095040


def _const_spec(shape):
    nd = len(shape)
    return pl.BlockSpec(shape, lambda *_: (0,) * nd, pipeline_mode=pl.Buffered(1))


def _layer_spec(stacked, layer):
    nd = stacked.ndim
    return pl.BlockSpec((None,) + stacked.shape[1:], lambda *_: (layer,) + (0,) * (nd - 1),
                        pipeline_mode=pl.Buffered(1))


def _params(*sem):
    return pltpu.CompilerParams(dimension_semantics=sem, vmem_limit_bytes=VMEM_LIMIT)


def _rms(x, g):
    ms = jnp.mean(x * x, axis=-1, keepdims=True)
    return x * lax.rsqrt(ms + EPS) * g


def _dot(a, b):
    return jnp.dot(a, b, preferred_element_type=F32)


def _dot_nt(a, b):
    return lax.dot_general(a, b, (((1,), (1,)), ((), ())), preferred_element_type=F32)


def _row_tile(m, pref):
    t = min(m, pref)
    assert m % t == 0, (m, t)
    return t


def _bias_tiles_kernel(tab_ref, bt_ref, *, n_heads):
    nb = N_BUCKETS // 2
    max_exact = nb // 2
    c = lax.broadcasted_iota(jnp.int32, (KEY_BLOCK, KEY_BLOCK), 0)
    r = lax.broadcasted_iota(jnp.int32, (KEY_BLOCK, KEY_BLOCK), 1)
    for d in range(2):
        rel = c - r - d * KEY_BLOCK
        n = jnp.abs(rel)
        nf = jnp.maximum(n, 1).astype(F32)
        large = max_exact + (jnp.log(nf / max_exact) / math.log(MAX_DISTANCE / max_exact)
                             * (nb - max_exact)).astype(jnp.int32)
        large = jnp.minimum(large, nb - 1)
        bucket = jnp.where(rel > 0, nb, 0) + jnp.where(n < max_exact, n, large)
        for h in range(n_heads):
            acc = jnp.zeros((KEY_BLOCK, KEY_BLOCK), F32)
            for b in range(N_BUCKETS):
                acc = jnp.where(bucket == b, tab_ref[b, h], acc)
            g = h % HEADS_PER_TILE
            rows = slice((1 - d) * KEY_BLOCK, (2 - d) * KEY_BLOCK)
            bt_ref[h // HEADS_PER_TILE, rows, g * KEY_BLOCK:(g + 1) * KEY_BLOCK] = (
                (acc - tab_ref[nb - 1, h]) * LOG2E)


def _bias_tiles(rel_bias):
    n_heads = rel_bias.shape[1]
    assert n_heads % HEADS_PER_TILE == 0
    return pl.pallas_call(
        functools.partial(_bias_tiles_kernel, n_heads=n_heads),
        out_shape=jax.ShapeDtypeStruct(
            (n_heads // HEADS_PER_TILE, 2 * KEY_BLOCK, HEADS_PER_TILE * KEY_BLOCK), F32),
        in_specs=[pl.BlockSpec(memory_space=pltpu.SMEM)],
        out_specs=pl.BlockSpec(memory_space=pltpu.VMEM),
        name="bias_tiles",
    )(rel_bias)


def _attn_proj_kernel(x_ref, g_ref, w_ref, q_ref, k_ref, v_ref, ki_ref, kb_ref, vb_ref,
                      kib_ref, qi_ref, wi_ref, *, dq, dkv, dqi):
    h = _rms(x_ref[...], g_ref[...]).astype(BF16)
    o = 0
    q_ref[...] = (_dot(h, w_ref[:, o:o + dq]) * (HEAD_DIM ** -0.5)).astype(BF16)
    o += dq
    k = _dot(h, w_ref[:, o:o + dkv])
    o += dkv
    v = _dot(h, w_ref[:, o:o + dkv])
    vb_ref[...] = v.astype(BF16)
    k_ref[...] = k.reshape(k_ref.shape)
    v_ref[...] = v.reshape(v_ref.shape)
    o += dkv
    qi_ref[...] = (_dot(h, w_ref[:, o:o + dqi]) * (IDX_DIM ** -0.5)).astype(BF16)
    o += dqi
    ki = _dot(h, w_ref[:, o:o + LANES])[:, :IDX_DIM]
    ki_ref[...] = ki
    kib_ref[...] = ki.astype(BF16)
    o += LANES
    wi_ref[...] = _dot(h, w_ref[:, o:o + LANES]) * (IDX_HEADS ** -0.5)
    o += LANES
    kb_ref[...] = _dot(h, w_ref[:, o:o + 2 * dkv]).astype(BF16)


def _attn_proj_weights(w_in):
    d = w_in.shape[1]
    base = d + 2 * N_KV_HEADS * HEAD_DIM + IDX_HEADS * IDX_DIM
    pad = lambda a: jnp.pad(a, ((0, 0), (0, 0), (0, LANES - a.shape[2])))
    w_k = [w_in[:, :, d + n * HEAD_DIM:d + (n + 1) * HEAD_DIM] for n in range(N_KV_HEADS)]
    return jnp.concatenate([w_in[:, :, :base], pad(w_in[:, :, base:base + IDX_DIM]),
                            pad(w_in[:, :, base + IDX_DIM:])] + [w for w in w_k for _ in range(2)],
                           axis=2).astype(BF16)


def _attn_proj(x2, g, w, layer):
    m, d = x2.shape
    dq = d
    dkv = N_KV_HEADS * HEAD_DIM
    dqi = IDX_HEADS * IDX_DIM
    tm = _row_tile(m, ROW_TILE)
    row = lambda *n: pl.BlockSpec((tm,) + n, lambda i: (i,) + (0,) * len(n))
    heads = (N_KV_HEADS, HEAD_DIM)
    outs = [((dq,), BF16), (heads, F32), (heads, F32), ((IDX_DIM,), F32), ((2 * dkv,), BF16), ((dkv,), BF16),
            ((IDX_DIM,), BF16), ((dqi,), BF16), ((LANES,), F32)]
    return pl.pallas_call(
        functools.partial(_attn_proj_kernel, dq=dq, dkv=dkv, dqi=dqi),
        out_shape=[jax.ShapeDtypeStruct((m,) + n, dt) for n, dt in outs],
        grid=(m // tm,),
        in_specs=[row(d), _const_spec((1, d)), _layer_spec(w, layer)],
        out_specs=[row(*n) for n, _ in outs],
        compiler_params=_params("parallel"),
        name="attn_proj",
    )(x2, g.reshape(1, d), w)


def _sparse_attn_kernel(q_ref, qi_ref, wi_ref, k_ref, vt_ref, ki_ref, bt_ref, o_ref,
                        s_ref, w_ref, dots_ref, qgt_ref, qit_ref, ot_ref, lg_ref, mb_ref, *state_refs,
                        q_off, n_keys, n_queries, k_sel, group):
    tq = KEY_BLOCK
    hpt = HEADS_PER_TILE
    n_tiles = qgt_ref.shape[0]
    m_refs, acc_refs = state_refs[:n_tiles], state_refs[n_tiles:]
    i = pl.program_id(1)
    qs = q_off // KEY_BLOCK + i
    n_kb = qs + 1
    q_start = q_off + i * tq

    qt = q_ref[0].astype(F32).T * LOG2E
    for t in range(n_tiles):
        q2 = jnp.concatenate(
            [qt[(t * hpt + g) * HEAD_DIM:(t * hpt + g + 1) * HEAD_DIM, :] for g in range(hpt)], axis=1)
        q_hi = q2.astype(BF16)
        q_lo = (q2 - q_hi.astype(F32)).astype(BF16)
        qgt_ref[t] = jnp.concatenate([q_hi, q_lo], axis=0)
    qit = qi_ref[0].astype(F32).T
    qit_ref[...] = jnp.concatenate([qit[h * IDX_DIM:(h + 1) * IDX_DIM, :] for h in range(IDX_HEADS)],
                                   axis=1).astype(BF16)
    last_kv = k_ref.shape[1] // KEY_BLOCK - 1
    wit = wi_ref[0].T

    qpos = q_start + lax.broadcasted_iota(jnp.int32, (1, tq), 1)
    lim = jnp.minimum((qpos // CHUNK + 1) * CHUNK, n_keys)
    real_q = qpos < q_off + n_queries
    kidx = lax.broadcasted_iota(jnp.int32, (KEY_BLOCK, tq), 0)

    def key_to_f32(key):
        bits = key ^ ((key >> 31) & 0x7FFFFFFF)
        return lax.bitcast_convert_type(bits, F32)

    def digit(sc):
        bits = lax.bitcast_convert_type(jnp.where(sc == 0.0, 0.0, sc), jnp.int32)
        key = bits ^ ((bits >> 31) & 0x7FFFFFFF)
        return (key >> LO_BITS) + (1 << (HI_BITS - 1))

    def dots(j, slot):
        k0 = pl.multiple_of(jnp.minimum(j, last_kv) * KEY_BLOCK, KEY_BLOCK)
        dots_ref[slot] = _dot(ki_ref[0, pl.ds(k0, KEY_BLOCK), :], qit_ref[...])

    def finish(j, slot):
        acc = jnp.zeros((KEY_BLOCK, tq), F32)
        for h in range(IDX_HEADS):
            acc = acc + wit[h:h + 1, :] * jnp.maximum(dots_ref[slot, :, h * tq:(h + 1) * tq], 0.0)
        sc = jnp.where(kidx + j * KEY_BLOCK < lim, acc, -jnp.inf)
        s_ref[j] = sc
        return digit(sc)

    def finish_pair(j, slot):
        w_ref[j // 2] = (finish(j, slot) << 16) | finish(j + 1, slot + 1)

    def score_quad(quad, carry):
        j = 4 * quad
        dots(j + 2, 2)
        dots(j + 3, 3)
        finish_pair(j, 0)
        dots(j + 4, 0)
        dots(j + 5, 1)
        finish_pair(j + 2, 2)
        return carry

    n_pairs = (n_kb + 1) // 2
    n_quads = (n_kb + 3) // 4
    dots(0, 0)
    dots(1, 1)
    lax.fori_loop(0, n_quads, score_quad, 0)
    for u in range(COUNT_UNROLL):
        s_ref[2 * n_pairs + u] = jnp.full((KEY_BLOCK, tq), -jnp.inf, F32)
    for u in range(COUNT_UNROLL // 2):
        w_ref[2 * n_quads + u] = jnp.zeros((KEY_BLOCK, tq), jnp.int32)
    n_count = (2 * n_pairs + COUNT_UNROLL - 1) // COUNT_UNROLL

    def count_digits(cand):
        comp = (1 << HI_BITS) - cand
        cw = jnp.broadcast_to((comp << 16) | comp, (KEY_BLOCK, tq))

        def body(p, c):
            for u in range(COUNT_UNROLL // 2):
                c = c + (((w_ref[(COUNT_UNROLL // 2) * p + u] + cw) >> HI_BITS) & 0x00010001)
            return c
        c = lax.fori_loop(0, n_count, body, jnp.zeros((KEY_BLOCK, tq), jnp.int32))
        return jnp.sum(((c & 0xFFFF) + (c >> 16)).astype(F32), axis=0, keepdims=True)

    def count_ge(cand):
        cb = jnp.broadcast_to(cand, (KEY_BLOCK, tq))

        def body(p, c):
            for u in range(COUNT_UNROLL):
                c = c + jnp.where(s_ref[COUNT_UNROLL * p + u] >= cb, 1.0, 0.0)
            return c
        c = lax.fori_loop(0, n_count, body, jnp.zeros((KEY_BLOCK, tq), F32))
        return jnp.sum(c, axis=0, keepdims=True)

    def bisect_step(count_at, carry):
        lo, hi, n_lo = carry
        mid = (lo + hi) >> 1
        n_mid = count_at(mid)
        ok = n_mid >= k_sel
        return jnp.where(ok, mid, lo), jnp.where(ok, hi, mid), jnp.where(ok, n_mid, n_lo)

    full = lambda v: jnp.full((1, tq), v, jnp.int32)
    half = 1 << (HI_BITS - 1)
    dig, _, n_sel = lax.fori_loop(
        0, HI_BITS, lambda _, c: bisect_step(count_digits, c),
        (full(((KEY_NEG_INF + 1) >> LO_BITS) + half), full((KEY_POS_INF >> LO_BITS) + half + 1),
         jnp.full((1, tq), jnp.inf, F32)))
    key_hi = (dig - half) << LO_BITS

    count_lo = lambda v: count_ge(key_to_f32(key_hi + v))

    def low_steps(carry):
        step, c = carry[0], carry[1:]
        for _ in range(LOW_CHECK_EVERY):
            c = bisect_step(count_lo, c)
        return (step + LOW_CHECK_EVERY,) + c

    def low_unfinished(carry):
        step, n_lo = carry[0], carry[3]
        return (step < LO_BITS) & (jnp.max(jnp.where(real_q, jnp.abs(n_lo - k_sel), 0.0)) > 0.0)

    _, key_lo, _, n_sel = lax.while_loop(low_unfinished, low_steps, (0, full(0), full(1 << LO_BITS), n_sel))
    thr = key_to_f32(key_hi + key_lo)

    def count(cand, strict):
        cb = jnp.broadcast_to(cand, (KEY_BLOCK, tq))
        hit = (lambda s: s > cb) if strict else (lambda s: s >= cb)

        def body(p, c):
            c = c + jnp.where(hit(s_ref[2 * p]), 1.0, 0.0)
            return c + jnp.where(hit(s_ref[2 * p + 1]), 1.0, 0.0)
        c = lax.fori_loop(0, n_pairs, body, jnp.zeros((KEY_BLOCK, tq), F32))
        return jnp.sum(c, axis=0, keepdims=True)

    surplus = jnp.where((n_sel < jnp.inf) & real_q, n_sel - k_sel, 0.0)

    @pl.when(jnp.max(surplus) > 0.0)
    def _():
        n_ties = k_sel - count(thr, True)
        row_i = lax.broadcasted_iota(jnp.int32, (KEY_BLOCK, KEY_BLOCK), 0)
        col_i = lax.broadcasted_iota(jnp.int32, (KEY_BLOCK, KEY_BLOCK), 1)
        lower = jnp.where(col_i < row_i, 1.0, 0.0).astype(BF16)
        ones = jnp.ones((KEY_BLOCK, KEY_BLOCK), BF16)

        def tie_body(j, seen):
            s = s_ref[j]
            eq = s == thr
            e = jnp.where(eq, 1.0, 0.0).astype(BF16)
            before = _dot(lower, e) + seen
            s_ref[j] = jnp.where(eq & (before >= n_ties), -jnp.inf, s)
            return seen + _dot(ones, e)

        lax.fori_loop(0, n_kb, tie_body, jnp.zeros((KEY_BLOCK, tq), F32))

    thr_sel = jnp.broadcast_to(jnp.maximum(thr, jnp.finfo(F32).min), (KEY_BLOCK, tq))

    for t in range(n_tiles):
        m_refs[t][...] = jnp.full(m_refs[t].shape, NEG_BIG, F32)
        acc_refs[t][...] = jnp.zeros(acc_refs[t].shape, F32)

    def far_blocks(db):
        out = []
        for j in (2 * db, 2 * db + 1):
            js = jnp.where(j < qs - 1, j, n_kb)
            out.append((js, jnp.minimum(js, last_kv)))
        return out

    near = [(jnp.where(qs >= 1, qs - 1, n_kb), jnp.maximum(qs - 1, 0)), (qs, qs)]

    def mask_of(blocks):
        m = jnp.concatenate([jnp.where(s_ref[js] >= thr_sel, 0.0, NEG_BIG) for js, _ in blocks], axis=0)
        return jnp.concatenate([m] * hpt, axis=1)

    def logits(blocks, mask, biased, slot, t):
        n = (t * hpt) // group
        kn = jnp.concatenate(
            [k_ref[0, pl.ds(pl.multiple_of(jk * KEY_BLOCK, KEY_BLOCK), KEY_BLOCK),
                   n * 2 * HEAD_DIM:(n + 1) * 2 * HEAD_DIM] for _, jk in blocks], axis=0)
        lg = _dot(kn, qgt_ref[t]) + mask
        if biased:
            lg = lg + bt_ref[t]
        lg_ref[slot, t] = lg
        mb_ref[slot, t] = jnp.max(lg, axis=0, keepdims=True)

    def softmax_update(blocks, slot, t):
        n = (t * hpt) // group
        m_old = m_refs[t][...]
        m_new = jnp.maximum(m_old, mb_ref[slot, t])
        alpha = jnp.exp2(m_old - m_new)
        p = jnp.exp2(lg_ref[slot, t] - m_new).astype(BF16)
        m_refs[t][...] = m_new
        vtn = jnp.concatenate([vt_ref[0, jk, n] for _, jk in blocks], axis=1)
        acc_refs[t][...] = alpha * acc_refs[t][...] + _dot(jnp.concatenate([vtn, ones_rows], axis=0), p)

    row_id = lax.broadcasted_iota(jnp.int32, (V_AUG_ROWS - HEAD_DIM, 2 * KEY_BLOCK), 0)
    ones_rows = jnp.where(row_id == 0, 1.0, 0.0).astype(BF16)

    n_far = qs // 2
    mask_near = mask_of(near)
    for t in range(n_tiles):
        logits(near, mask_near, True, 0, t)

    def trip_body(trip, carry):
        d0 = 2 * trip
        blk1, blk2 = far_blocks(d0), far_blocks(d0 + 1)
        mask1, mask2 = mask_of(blk1), mask_of(blk2)
        far0 = far_blocks(d0 - 1)
        blk0 = [tuple(jnp.where(d0 == 0, a, b) for a, b in zip(near[h], far0[h])) for h in range(2)]
        for t in range(n_tiles):
            logits(blk1, mask1, False, 1, t)
            softmax_update(blk0, 0, t)
        for t in range(n_tiles):
            logits(blk2, mask2, False, 0, t)
            softmax_update(blk1, 1, t)
        return carry

    lax.fori_loop(0, (n_far + 2) // 2, trip_body, 0)

    for t in range(n_tiles):
        on = acc_refs[t][0:HEAD_DIM, :] / acc_refs[t][HEAD_DIM:HEAD_DIM + 1, :]
        for g in range(hpt):
            hh = t * hpt + g
            ot_ref[hh * HEAD_DIM:(hh + 1) * HEAD_DIM, :] = on[:, g * tq:(g + 1) * tq]
    o_ref[0] = ot_ref[...].T.astype(BF16)


def _sparse_attn(q, qi, wi, k2, vb, kib, bias_tiles, *, q_off, n_keys, n_queries, k_sel):
    b, t, dq = q.shape
    lp = vb.shape[1]
    dkv = vb.shape[2]
    tq = KEY_BLOCK
    n_heads = dq // HEAD_DIM
    group = n_heads // N_KV_HEADS
    hpt = HEADS_PER_TILE
    n_tiles = n_heads // hpt
    assert group % hpt == 0
    nkb = lp // KEY_BLOCK
    assert t % tq == 0 and lp % KEY_BLOCK == 0 and q_off % KEY_BLOCK == 0 and q_off + t <= lp
    vt = jnp.swapaxes(vb.reshape(b, nkb, KEY_BLOCK, dkv), 2, 3).reshape(
        b, nkb, N_KV_HEADS, HEAD_DIM, KEY_BLOCK)
    qblk = lambda n: pl.BlockSpec((1, tq, n), lambda bi, i: (bi, i, 0))
    kblk = lambda n: pl.BlockSpec((1, lp, n), lambda bi, i: (bi, 0, 0))
    return pl.pallas_call(
        functools.partial(_sparse_attn_kernel, q_off=q_off, n_keys=n_keys, n_queries=n_queries, k_sel=k_sel,
                          group=group),
        out_shape=jax.ShapeDtypeStruct((b, t, dq), BF16),
        grid=(b, t // tq),
        in_specs=[qblk(dq), qblk(qi.shape[2]), qblk(LANES), kblk(2 * dkv),
                  pl.BlockSpec((1, nkb, N_KV_HEADS, HEAD_DIM, KEY_BLOCK), lambda bi, i: (bi, 0, 0, 0, 0)),
                  kblk(kib.shape[2]), _const_spec(bias_tiles.shape)],
        out_specs=qblk(dq),
        scratch_shapes=[
            pltpu.VMEM((nkb + 1 + COUNT_UNROLL, KEY_BLOCK, tq), F32),
            pltpu.VMEM((nkb // 2 + COUNT_UNROLL, KEY_BLOCK, tq), jnp.int32),
            pltpu.VMEM((4, KEY_BLOCK, IDX_HEADS * tq), F32),
            pltpu.VMEM((n_tiles, 2 * HEAD_DIM, hpt * tq), BF16),
            pltpu.VMEM((IDX_DIM, IDX_HEADS * tq), BF16),
            pltpu.VMEM((dq, tq), F32),
            pltpu.VMEM((2, n_tiles, 2 * KEY_BLOCK, hpt * tq), F32),
            pltpu.VMEM((2, n_tiles, 1, hpt * tq), F32),
        ] + [pltpu.VMEM((1, hpt * tq), F32)] * n_tiles
          + [pltpu.VMEM((V_AUG_ROWS, hpt * tq), F32)] * n_tiles,
        compiler_params=_params("parallel", "arbitrary"),
        name="sparse_attn",
    )(q, qi, wi, k2, vt, kib, bias_tiles)


def _mem_kv_kernel(x_ref, g_ref, w_ref, k_ref, v_ref, *, d):
    h = _rms(x_ref[...], g_ref[...]).astype(BF16)
    k_ref[...] = _dot(h, w_ref[:, :d])
    v_ref[...] = _dot(h, w_ref[:, d:])


def _mem_kv(mem2, g, w_kv, layer):
    m, d = mem2.shape
    tm = _row_tile(m, ROW_TILE)
    row = pl.BlockSpec((tm, d), lambda i: (i, 0))
    return pl.pallas_call(
        functools.partial(_mem_kv_kernel, d=d),
        out_shape=[jax.ShapeDtypeStruct((m, d), F32)] * 2,
        grid=(m // tm,),
        in_specs=[row, _const_spec((1, d)), _layer_spec(w_kv, layer)],
        out_specs=[row, row],
        compiler_params=_params("parallel"),
        name="mem_kv",
    )(mem2, g.reshape(1, d), w_kv)


def _mem_attn_kernel(*refs, hd, pending):
    if pending:
        x_ref, a_ref, wa_ref, g_ref, wq_ref, mk_ref, mv_ref, wo_ref, o_ref = refs
        x = x_ref[...] + _dot(a_ref[...], wa_ref[...])
    else:
        x_ref, g_ref, wq_ref, mk_ref, mv_ref, wo_ref, o_ref = refs
        x = x_ref[...]
    h = _rms(x, g_ref[...]).astype(BF16)
    q = (_dot(h, wq_ref[...]) * (hd ** -0.5)).astype(BF16)
    heads = []
    for a in range(MEM_HEADS):
        cols = slice(a * hd, (a + 1) * hd)
        lg = _dot_nt(q[:, cols], mk_ref[0, :, cols].astype(BF16))
        p = jnp.exp(lg - jnp.max(lg, axis=1, keepdims=True))
        p = (p / jnp.sum(p, axis=1, keepdims=True)).astype(BF16)
        heads.append(_dot(p, mv_ref[0, :, cols].astype(BF16)).astype(BF16))
    o_ref[...] = x + _dot(jnp.concatenate(heads, axis=1), wo_ref[...])


def _mem_attn(x2, g, w_q, mk, mv, w_o, layer, rows_per_batch, pending=None):
    m, d = x2.shape
    n_mem = mk.shape[1]
    tm = _row_tile(rows_per_batch, ROW_TILE)
    per = rows_per_batch // tm
    row = pl.BlockSpec((tm, d), lambda i: (i, 0))
    mem = pl.BlockSpec((1, n_mem, d), lambda i: (i // per, 0, 0))
    in_specs = [row, _const_spec((1, d)), _layer_spec(w_q, layer), mem, mem, _layer_spec(w_o, layer)]
    args = [x2, g.reshape(1, d), w_q, mk, mv, w_o]
    if pending is not None:
        a2, w_a, layer_a = pending
        in_specs[1:1] = [pl.BlockSpec((tm, a2.shape[1]), lambda i: (i, 0)), _layer_spec(w_a, layer_a)]
        args[1:1] = [a2, w_a]
    return pl.pallas_call(
        functools.partial(_mem_attn_kernel, hd=d // MEM_HEADS, pending=pending is not None),
        out_shape=jax.ShapeDtypeStruct((m, d), F32),
        grid=(m // tm,),
        in_specs=in_specs,
        out_specs=row,
        compiler_params=_params("parallel"),
        name="mem_attn",
    )(*args)


def _mlp_kernel(*refs, n_chunks, chunk, final):
    if final:
        x_ref, g_ref, w1_ref, w2_ref, gf_ref, o_ref = refs
    else:
        x_ref, g_ref, w1_ref, w2_ref, o_ref = refs
    x = x_ref[...]
    h = _rms(x, g_ref[...]).astype(BF16)
    acc = x
    for c in range(n_chunks):
        a = jnp.maximum(_dot(h, w1_ref[:, c * chunk:(c + 1) * chunk]), 0.0)
        acc = acc + _dot((a * a).astype(BF16), w2_ref[c * chunk:(c + 1) * chunk, :])
    if final:
        acc = _rms(acc, gf_ref[...])
    o_ref[...] = acc


def _mlp(x2, g, w1, w2, layer, g_final=None):
    m, d = x2.shape
    dff = w1.shape[2]
    chunk = min(dff, 1024)
    tm = _row_tile(m, ROW_TILE)
    row = pl.BlockSpec((tm, d), lambda i: (i, 0))
    final = g_final is not None
    in_specs = [row, _const_spec((1, d)), _layer_spec(w1, layer), _layer_spec(w2, layer)]
    args = [x2, g.reshape(1, d), w1, w2]
    if final:
        in_specs.append(_const_spec((1, d)))
        args.append(g_final.reshape(1, d))
    return pl.pallas_call(
        functools.partial(_mlp_kernel, n_chunks=dff // chunk, chunk=chunk, final=final),
        out_shape=jax.ShapeDtypeStruct((m, d), F32),
        grid=(m // tm,),
        in_specs=in_specs,
        out_specs=row,
        compiler_params=_params("parallel"),
        name="mlp",
    )(*args)


def _conv_glu_kernel(x_ref, g_ref, w_ref, b_ref, u_ref, *, d):
    h = _rms(x_ref[...], g_ref[...]).astype(BF16)
    a = _dot(h, w_ref[:, :d]) + b_ref[:, :d]
    gate = _dot(h, w_ref[:, d:]) + b_ref[:, d:]
    u_ref[...] = a * (1.0 / (1.0 + jnp.exp(-gate)))


def _conv_glu(x2, g, w_pw1, layer, b_pw1):
    m, d = x2.shape
    tm = _row_tile(m, ROW_TILE)
    row = pl.BlockSpec((tm, d), lambda i: (i, 0))
    return pl.pallas_call(
        functools.partial(_conv_glu_kernel, d=d),
        out_shape=jax.ShapeDtypeStruct((m, d), F32),
        grid=(m // tm,),
        in_specs=[row, _const_spec((1, d)), _layer_spec(w_pw1, layer), _const_spec((1, 2 * d))],
        out_specs=row,
        compiler_params=_params("parallel"),
        name="conv_glu",
    )(x2, g.reshape(1, d), w_pw1, b_pw1.reshape(1, 2 * d))


def _conv_rest_kernel(x_ref, u_ref, prev_ref, init_ref, wdw_ref, bdw_ref, lng_ref, lnb_ref,
                      w2_ref, b2_ref, o_ref, ext_ref, sh_ref, y_ref, *, tm, rc, lc):
    t = pl.program_id(1)
    d = u_ref.shape[2]
    pad = CONV_WIDTH - 1

    @pl.when(t == 0)
    def _():
        ext_ref[0:HALO, :] = init_ref[0]

    @pl.when(t > 0)
    def _():
        ext_ref[0:HALO, :] = prev_ref[0]

    ext_ref[HALO:HALO + tm, :] = u_ref[0]

    for s in range(SUBLANES):
        rows = tm + SUBLANES * ((CONV_WIDTH - 1 - s) // SUBLANES)
        sh_ref[s, 0:rows, :] = ext_ref[pl.ds(HALO - pad + s, rows), :]
    for r0 in range(0, tm, rc):
        for c0 in range(0, d, lc):
            cols = slice(c0, c0 + lc)
            y = jnp.broadcast_to(bdw_ref[:, cols], (rc, lc))
            for w in range(CONV_WIDTH):
                a, s = divmod(w, SUBLANES)
                y = y + sh_ref[s, r0 + SUBLANES * a:r0 + SUBLANES * a + rc, cols] * wdw_ref[w:w + 1, cols]
            y_ref[r0:r0 + rc, cols] = y

    y = y_ref[...]
    mu = jnp.mean(y, axis=-1, keepdims=True)
    yc = y - mu
    var = jnp.mean(yc * yc, axis=-1, keepdims=True)
    yn = yc * lax.rsqrt(var + EPS) * lng_ref[...] + lnb_ref[...]
    act = (yn * (1.0 / (1.0 + jnp.exp(-yn)))).astype(BF16)
    o_ref[0] = x_ref[0] + _dot(act, w2_ref[...]) + b2_ref[...]


def _conv_rest(x3, u3, init, w_dw, b_dw, ln_g, ln_b, w_pw2, layer, b_pw2):
    b, t, d = x3.shape
    tm = _row_tile(t, CONV_ROW_TILE)
    rc = min(tm, 64)
    lc = min(d, 256)
    assert tm % HALO == 0 and tm % rc == 0 and d % lc == 0
    per = tm // HALO
    tile = pl.BlockSpec((1, tm, d), lambda bi, ti: (bi, ti, 0))
    prev = pl.BlockSpec((1, HALO, d), lambda bi, ti: (bi, jnp.maximum(ti * per - 1, 0), 0))
    first = pl.BlockSpec((1, HALO, d), lambda bi, ti: (bi, 0, 0))
    vec = _const_spec((1, d))
    wdw = jnp.pad(w_dw, ((0, HALO - CONV_WIDTH), (0, 0)))
    return pl.pallas_call(
        functools.partial(_conv_rest_kernel, tm=tm, rc=rc, lc=lc),
        out_shape=jax.ShapeDtypeStruct((b, t, d), F32),
        grid=(b, t // tm),
        in_specs=[tile, tile, prev, first, _const_spec((HALO, d)), vec, vec, vec,
                  _layer_spec(w_pw2, layer), vec],
        out_specs=tile,
        scratch_shapes=[pltpu.VMEM((HALO + tm, d), F32),
                        pltpu.VMEM((SUBLANES, tm + HALO - SUBLANES, d), F32),
                        pltpu.VMEM((tm, d), F32)],
        compiler_params=_params("parallel", "arbitrary"),
        name="conv_rest",
    )(x3, u3, u3, init, wdw, b_dw.reshape(1, d), ln_g.reshape(1, d), ln_b.reshape(1, d),
      w_pw2, b_pw2.reshape(1, d))


def _mixer_attn(x3, g, w_in, layer, bias_tiles, cache=None):
    b, t, d = x3.shape
    x2 = x3.reshape(b * t, d)
    q, k, v, ki, kb, vb, kib, qi, wi = _attn_proj(x2, g, w_in, layer)
    r3 = lambda a: a.reshape(b, t, a.shape[-1])
    kb, vb, kib = r3(kb), r3(vb), r3(kib)
    past = 0
    if cache is not None:
        ck, cv, cki = cache
        past = ck.shape[1]
        ck2 = jnp.concatenate([ck[:, :, n].astype(BF16) for n in range(N_KV_HEADS) for _ in range(2)], axis=-1)
        kb = jnp.concatenate([ck2, kb], axis=1)
        vb = jnp.concatenate([cv.reshape(b, past, -1).astype(BF16), vb], axis=1)
        kib = jnp.concatenate([cki.astype(BF16), kib], axis=1)
    n_keys = past + t
    k_sel = min(TOPK_MAX, n_keys // 4)
    tpad = -(-t // KEY_BLOCK) * KEY_BLOCK
    lp = max(-(-n_keys // KEY_BLOCK) * KEY_BLOCK, past + tpad)
    padt = lambda a, n: a if a.shape[1] == n else jnp.pad(a, ((0, 0), (0, n - a.shape[1]), (0, 0)))
    o = _sparse_attn(padt(r3(q), tpad), padt(r3(qi), tpad), padt(r3(wi), tpad),
                     padt(kb, lp), padt(vb, lp), padt(kib, lp), bias_tiles,
                     q_off=past, n_keys=n_keys, n_queries=t, k_sel=k_sel)[:, :t]
    return (o.reshape(b * t, d), k.reshape(b, t, N_KV_HEADS, HEAD_DIM),
            v.reshape(b, t, N_KV_HEADS, HEAD_DIM), ki.reshape(b, t, IDX_DIM))


def _mixer_conv(x3, g, w_pw1, b_pw1, w_dw, b_dw, ln_g, ln_b, w_pw2, b_pw2, layer, state=None):
    b, t, d = x3.shape
    pad = CONV_WIDTH - 1
    u3 = _conv_glu(x3.reshape(b * t, d), g, w_pw1, layer, b_pw1).reshape(b, t, d)
    if state is None:
        init = jnp.zeros((b, HALO, d), F32)
        tail = u3[:, -pad:] if t >= pad else jnp.pad(u3, ((0, 0), (pad - t, 0), (0, 0)))
    else:
        init = jnp.pad(state.astype(F32), ((0, 0), (HALO - pad, 0), (0, 0)))
        tail = jnp.concatenate([state.astype(F32), u3], axis=1)[:, -pad:]
    x3 = _conv_rest(x3, u3, init, w_dw, b_dw, ln_g, ln_b, w_pw2, layer, b_pw2)
    return x3, tail


def kernel(x_prompt, x_sample, cache_attn_k, cache_attn_v, cache_attn_kidx, state_conv, cache_mem_k, cache_mem_v, mem_prompt, rel_bias, g_mix, w_in_attn, w_out_attn, w_pw1, b_pw1, w_dw, b_dw, ln_g, ln_b, w_pw2, b_pw2, g_mem_q, g_mem_src, w_mem_q, w_mem_kv, w_mem_o, g_mlp, w_mlp1, w_mlp2, g_final):
    depth = g_mix.shape[0]
    bp, tp, d = x_prompt.shape
    bs, ts, _ = x_sample.shape
    n_mem = mem_prompt.shape[1]
    mem_hd = d // MEM_HEADS
    bias_tiles = _bias_tiles(rel_bias)
    w_in_b, w_out_b = _attn_proj_weights(w_in_attn), w_out_attn.astype(BF16)
    w_pw1_b, w_pw2_b = w_pw1.astype(BF16), w_pw2.astype(BF16)
    w_mq_b, w_mkv_b, w_mo_b = w_mem_q.astype(BF16), w_mem_kv.astype(BF16), w_mem_o.astype(BF16)
    w_mlp1_b, w_mlp2_b = w_mlp1.astype(BF16), w_mlp2.astype(BF16)
    xp, xs = x_prompt, x_sample
    kp_l, vp_l, kip_l, ks_l, vs_l, kis_l = [], [], [], [], [], []
    convp_l, convs_l, memk_l, memv_l = [], [], [], []
    for i in range(depth):
        j = i // 2
        if i % 2 == 0:
            op, kp, vp, kip = _mixer_attn(xp, g_mix[i], w_in_b, j, bias_tiles)
            osm, ks, vs, kis = _mixer_attn(
                xs, g_mix[i], w_in_b, j, bias_tiles,
                cache=(cache_attn_k[j], cache_attn_v[j], cache_attn_kidx[j]))
            pend_p, pend_s = (op, w_out_b, j), (osm, w_out_b, j)
            kp_l.append(kp); vp_l.append(vp); kip_l.append(kip)
            ks_l.append(ks); vs_l.append(vs); kis_l.append(kis)
        else:
            cw = (w_pw1_b, b_pw1[j], w_dw[j], b_dw[j], ln_g[j], ln_b[j], w_pw2_b, b_pw2[j], j)
            xp, cp = _mixer_conv(xp, g_mix[i], *cw)
            xs, cs = _mixer_conv(xs, g_mix[i], *cw, state=state_conv[j])
            pend_p = pend_s = None
            convp_l.append(cp); convs_l.append(cs)
        mk, mv = _mem_kv(mem_prompt.reshape(bp * n_mem, d), g_mem_src[i], w_mkv_b, i)
        mk, mv = mk.reshape(bp, n_mem, d), mv.reshape(bp, n_mem, d)
        memk_l.append(mk.reshape(bp, n_mem, MEM_HEADS, mem_hd))
        memv_l.append(mv.reshape(bp, n_mem, MEM_HEADS, mem_hd))
        xp2 = _mem_attn(xp.reshape(bp * tp, d), g_mem_q[i], w_mq_b, mk, mv, w_mo_b, i, tp, pend_p)
        xs2 = _mem_attn(xs.reshape(bs * ts, d), g_mem_q[i], w_mq_b,
                        cache_mem_k[i].reshape(bs, n_mem, d), cache_mem_v[i].reshape(bs, n_mem, d),
                        w_mo_b, i, ts, pend_s)
        gf = g_final if i == depth - 1 else None
        xp = _mlp(xp2, g_mlp[i], w_mlp1_b, w_mlp2_b, i, gf).reshape(bp, tp, d)
        xs = _mlp(xs2, g_mlp[i], w_mlp1_b, w_mlp2_b, i, gf).reshape(bs, ts, d)
    return (xp, xs, jnp.stack(kp_l), jnp.stack(vp_l), jnp.stack(kip_l), jnp.stack(convp_l),
            jnp.stack(memk_l), jnp.stack(memv_l), jnp.stack(ks_l), jnp.stack(vs_l),
            jnp.stack(kis_l), jnp.stack(convs_l))
```

```python
import functools
import math

import jax
import jax.numpy as jnp
from jax import lax
from jax.experimental import pallas as pl
from jax.experimental.pallas import tpu as pltpu

CHUNK = 64
HEAD_DIM = 64
N_KV_HEADS = 4
IDX_HEADS = 8
IDX_DIM = 64
TOPK_MAX = 256
N_BUCKETS = 32
MAX_DISTANCE = 128
CONV_WIDTH = 31
MEM_HEADS = 4
EPS = 1e-6

LANES = 128
SUBLANES = 8
KEY_BLOCK = 128
HEADS_PER_TILE = 2
COUNT_UNROLL = 4
HI_BITS = 15
LO_BITS = 32 - HI_BITS
LOW_CHECK_EVERY = 4
V_AUG_ROWS = HEAD_DIM + 16
LOG2E = math.log2(math.e)
ROW_TILE = 1024
CONV_ROW_TILE = 512
HALO = 32
VMEM_LIMIT = 56 * 1024 * 1024

NEG_BIG = -1e30
F32 = jnp.float32
BF16 = jnp.bfloat16

KEY_NEG_INF = -2139095041
KEY_POS_INF = 2139095040


def _const_spec(shape):
    nd = len(shape)
    return pl.BlockSpec(shape, lambda *_: (0,) * nd, pipeline_mode=pl.Buffered(1))


def _layer_spec(stacked, layer):
    nd = stacked.ndim
    return pl.BlockSpec((None,) + stacked.shape[1:], lambda *_: (layer,) + (0,) * (nd - 1),
                        pipeline_mode=pl.Buffered(1))


def _params(*sem):
    return pltpu.CompilerParams(dimension_semantics=sem, vmem_limit_bytes=VMEM_LIMIT)


def _rms(x, g):
    ms = jnp.mean(x * x, axis=-1, keepdims=True)
    return x * lax.rsqrt(ms + EPS) * g


def _dot(a, b):
    return jnp.dot(a, b, preferred_element_type=F32)


def _dot_nt(a, b):
    return lax.dot_general(a, b, (((1,), (1,)), ((), ())), preferred_element_type=F32)


def _row_tile(m, pref):
    t = min(m, pref)
    assert m % t == 0, (m, t)
    return t


def _bias_tiles_kernel(tab_ref, bt_ref, *, n_heads):
    nb = N_BUCKETS // 2
    max_exact = nb // 2
    c = lax.broadcasted_iota(jnp.int32, (KEY_BLOCK, KEY_BLOCK), 0)
    r = lax.broadcasted_iota(jnp.int32, (KEY_BLOCK, KEY_BLOCK), 1)
    for d in range(2):
        rel = c - r - d * KEY_BLOCK
        n = jnp.abs(rel)
        nf = jnp.maximum(n, 1).astype(F32)
        large = max_exact + (jnp.log(nf / max_exact) / math.log(MAX_DISTANCE / max_exact)
                             * (nb - max_exact)).astype(jnp.int32)
        large = jnp.minimum(large, nb - 1)
        bucket = jnp.where(rel > 0, nb, 0) + jnp.where(n < max_exact, n, large)
        for h in range(n_heads):
            acc = jnp.zeros((KEY_BLOCK, KEY_BLOCK), F32)
            for b in range(N_BUCKETS):
                acc = jnp.where(bucket == b, tab_ref[b, h], acc)
            g = h % HEADS_PER_TILE
            rows = slice((1 - d) * KEY_BLOCK, (2 - d) * KEY_BLOCK)
            bt_ref[h // HEADS_PER_TILE, rows, g * KEY_BLOCK:(g + 1) * KEY_BLOCK] = (
                (acc - tab_ref[nb - 1, h]) * LOG2E)


def _bias_tiles(rel_bias):
    n_heads = rel_bias.shape[1]
    assert n_heads % HEADS_PER_TILE == 0
    return pl.pallas_call(
        functools.partial(_bias_tiles_kernel, n_heads=n_heads),
        out_shape=jax.ShapeDtypeStruct(
            (n_heads // HEADS_PER_TILE, 2 * KEY_BLOCK, HEADS_PER_TILE * KEY_BLOCK), F32),
        in_specs=[pl.BlockSpec(memory_space=pltpu.SMEM)],
        out_specs=pl.BlockSpec(memory_space=pltpu.VMEM),
        name="bias_tiles",
    )(rel_bias)


def _attn_proj_kernel(x_ref, g_ref, w_ref, q_ref, k_ref, v_ref, ki_ref, kb_ref, vb_ref,
                      kib_ref, qi_ref, wi_ref, *, dq, dkv, dqi):
    h = _rms(x_ref[...], g_ref[...]).astype(BF16)
    o = 0
    q_ref[...] = (_dot(h, w_ref[:, o:o + dq]) * (HEAD_DIM ** -0.5)).astype(BF16)
    o += dq
    k = _dot(h, w_ref[:, o:o + dkv])
    o += dkv
    v = _dot(h, w_ref[:, o:o + dkv])
    vb_ref[...] = v.astype(BF16)
    k_ref[...] = k.reshape(k_ref.shape)
    v_ref[...] = v.reshape(v_ref.shape)
    o += dkv
    qi_ref[...] = (_dot(h, w_ref[:, o:o + dqi]) * (IDX_DIM ** -0.5)).astype(BF16)
    o += dqi
    ki = _dot(h, w_ref[:, o:o + LANES])[:, :IDX_DIM]
    ki_ref[...] = ki
    kib_ref[...] = ki.astype(BF16)
    o += LANES
    wi_ref[...] = _dot(h, w_ref[:, o:o + LANES]) * (IDX_HEADS ** -0.5)
    o += LANES
    kb_ref[...] = _dot(h, w_ref[:, o:o + 2 * dkv]).astype(BF16)


def _attn_proj_weights(w_in):
    d = w_in.shape[1]
    base = d + 2 * N_KV_HEADS * HEAD_DIM + IDX_HEADS * IDX_DIM
    pad = lambda a: jnp.pad(a, ((0, 0), (0, 0), (0, LANES - a.shape[2])))
    w_k = [w_in[:, :, d + n * HEAD_DIM:d + (n + 1) * HEAD_DIM] for n in range(N_KV_HEADS)]
    return jnp.concatenate([w_in[:, :, :base], pad(w_in[:, :, base:base + IDX_DIM]),
                            pad(w_in[:, :, base + IDX_DIM:])] + [w for w in w_k for _ in range(2)],
                           axis=2).astype(BF16)


def _attn_proj(x2, g, w, layer):
    m, d = x2.shape
    dq = d
    dkv = N_KV_HEADS * HEAD_DIM
    dqi = IDX_HEADS * IDX_DIM
    tm = _row_tile(m, ROW_TILE)
    row = lambda *n: pl.BlockSpec((tm,) + n, lambda i: (i,) + (0,) * len(n))
    heads = (N_KV_HEADS, HEAD_DIM)
    outs = [((dq,), BF16), (heads, F32), (heads, F32), ((IDX_DIM,), F32), ((2 * dkv,), BF16), ((dkv,), BF16),
            ((IDX_DIM,), BF16), ((dqi,), BF16), ((LANES,), F32)]
    return pl.pallas_call(
        functools.partial(_attn_proj_kernel, dq=dq, dkv=dkv, dqi=dqi),
        out_shape=[jax.ShapeDtypeStruct((m,) + n, dt) for n, dt in outs],
        grid=(m // tm,),
        in_specs=[row(d), _const_spec((1, d)), _layer_spec(w, layer)],
        out_specs=[row(*n) for n, _ in outs],
        compiler_params=_params("parallel"),
        name="attn_proj",
    )(x2, g.reshape(1, d), w)


def _sparse_attn_kernel(q_ref, qi_ref, wi_ref, k_ref, vt_ref, ki_ref, bt_ref, o_ref,
                        s_ref, w_ref, dots_ref, qgt_ref, qit_ref, ot_ref, lg_ref, mb_ref, *state_refs,
                        q_off, n_keys, n_queries, k_sel, group):
    tq = KEY_BLOCK
    hpt = HEADS_PER_TILE
    n_tiles = qgt_ref.shape[0]
    m_refs, acc_refs = state_refs[:n_tiles], state_refs[n_tiles:]
    i = pl.program_id(1)
    qs = q_off // KEY_BLOCK + i
    n_kb = qs + 1
    q_start = q_off + i * tq

    qt = q_ref[0].astype(F32).T * LOG2E
    for t in range(n_tiles):
        q2 = jnp.concatenate(
            [qt[(t * hpt + g) * HEAD_DIM:(t * hpt + g + 1) * HEAD_DIM, :] for g in range(hpt)], axis=1)
        q_hi = q2.astype(BF16)
        q_lo = (q2 - q_hi.astype(F32)).astype(BF16)
        qgt_ref[t] = jnp.concatenate([q_hi, q_lo], axis=0)
    qit = qi_ref[0].astype(F32).T
    qit_ref[...] = jnp.concatenate([qit[h * IDX_DIM:(h + 1) * IDX_DIM, :] for h in range(IDX_HEADS)],
                                   axis=1).astype(BF16)
    last_kv = k_ref.shape[1] // KEY_BLOCK - 1
    wit = wi_ref[0].T

    qpos = q_start + lax.broadcasted_iota(jnp.int32, (1, tq), 1)
    lim = jnp.minimum((qpos // CHUNK + 1) * CHUNK, n_keys)
    real_q = qpos < q_off + n_queries
    kidx = lax.broadcasted_iota(jnp.int32, (KEY_BLOCK, tq), 0)

    def key_to_f32(key):
        bits = key ^ ((key >> 31) & 0x7FFFFFFF)
        return lax.bitcast_convert_type(bits, F32)

    def digit(sc):
        bits = lax.bitcast_convert_type(jnp.where(sc == 0.0, 0.0, sc), jnp.int32)
        key = bits ^ ((bits >> 31) & 0x7FFFFFFF)
        return (key >> LO_BITS) + (1 << (HI_BITS - 1))

    def dots(j, slot):
        k0 = pl.multiple_of(jnp.minimum(j, last_kv) * KEY_BLOCK, KEY_BLOCK)
        dots_ref[slot] = _dot(ki_ref[0, pl.ds(k0, KEY_BLOCK), :], qit_ref[...])

    def finish(j, slot):
        acc = jnp.zeros((KEY_BLOCK, tq), F32)
        for h in range(IDX_HEADS):
            acc = acc + wit[h:h + 1, :] * jnp.maximum(dots_ref[slot, :, h * tq:(h + 1) * tq], 0.0)
        sc = jnp.where(kidx + j * KEY_BLOCK < lim, acc, -jnp.inf)
        s_ref[j] = sc
        return digit(sc)

    def finish_pair(j, slot):
        w_ref[j // 2] = (finish(j, slot) << 16) | finish(j + 1, slot + 1)

    def score_quad(quad, carry):
        j = 4 * quad
        dots(j + 2, 2)
        dots(j + 3, 3)
        finish_pair(j, 0)
        dots(j + 4, 0)
        dots(j + 5, 1)
        finish_pair(j + 2, 2)
        return carry

    n_pairs = (n_kb + 1) // 2
    n_quads = (n_kb + 3) // 4
    dots(0, 0)
    dots(1, 1)
    lax.fori_loop(0, n_quads, score_quad, 0)
    for u in range(COUNT_UNROLL):
        s_ref[2 * n_pairs + u] = jnp.full((KEY_BLOCK, tq), -jnp.inf, F32)
    for u in range(COUNT_UNROLL // 2):
        w_ref[2 * n_quads + u] = jnp.zeros((KEY_BLOCK, tq), jnp.int32)
    n_count = (2 * n_pairs + COUNT_UNROLL - 1) // COUNT_UNROLL

    def count_digits(cand):
        comp = (1 << HI_BITS) - cand
        cw = jnp.broadcast_to((comp << 16) | comp, (KEY_BLOCK, tq))

        def body(p, c):
            for u in range(COUNT_UNROLL // 2):
                c = c + (((w_ref[(COUNT_UNROLL // 2) * p + u] + cw) >> HI_BITS) & 0x00010001)
            return c
        c = lax.fori_loop(0, n_count, body, jnp.zeros((KEY_BLOCK, tq), jnp.int32))
        return jnp.sum(((c & 0xFFFF) + (c >> 16)).astype(F32), axis=0, keepdims=True)

    def count_ge(cand):
        cb = jnp.broadcast_to(cand, (KEY_BLOCK, tq))

        def body(p, c):
            for u in range(COUNT_UNROLL):
                c = c + jnp.where(s_ref[COUNT_UNROLL * p + u] >= cb, 1.0, 0.0)
            return c
        c = lax.fori_loop(0, n_count, body, jnp.zeros((KEY_BLOCK, tq), F32))
        return jnp.sum(c, axis=0, keepdims=True)

    def bisect_step(count_at, carry):
        lo, hi, n_lo = carry
        mid = (lo + hi) >> 1
        n_mid = count_at(mid)
        ok = n_mid >= k_sel
        return jnp.where(ok, mid, lo), jnp.where(ok, hi, mid), jnp.where(ok, n_mid, n_lo)

    full = lambda v: jnp.full((1, tq), v, jnp.int32)
    half = 1 << (HI_BITS - 1)
    dig, _, n_sel = lax.fori_loop(
        0, HI_BITS, lambda _, c: bisect_step(count_digits, c),
        (full(((KEY_NEG_INF + 1) >> LO_BITS) + half), full((KEY_POS_INF >> LO_BITS) + half + 1),
         jnp.full((1, tq), jnp.inf, F32)))
    key_hi = (dig - half) << LO_BITS

    count_lo = lambda v: count_ge(key_to_f32(key_hi + v))

    def low_steps(carry):
        step, c = carry[0], carry[1:]
        for _ in range(LOW_CHECK_EVERY):
            c = bisect_step(count_lo, c)
        return (step + LOW_CHECK_EVERY,) + c

    def low_unfinished(carry):
        step, n_lo = carry[0], carry[3]
        return (step < LO_BITS) & (jnp.max(jnp.where(real_q, jnp.abs(n_lo - k_sel), 0.0)) > 0.0)

    _, key_lo, _, n_sel = lax.while_loop(low_unfinished, low_steps, (0, full(0), full(1 << LO_BITS), n_sel))
    thr = key_to_f32(key_hi + key_lo)

    def count(cand, strict):
        cb = jnp.broadcast_to(cand, (KEY_BLOCK, tq))
        hit = (lambda s: s > cb) if strict else (lambda s: s >= cb)

        def body(p, c):
            c = c + jnp.where(hit(s_ref[2 * p]), 1.0, 0.0)
            return c + jnp.where(hit(s_ref[2 * p + 1]), 1.0, 0.0)
        c = lax.fori_loop(0, n_pairs, body, jnp.zeros((KEY_BLOCK, tq), F32))
        return jnp.sum(c, axis=0, keepdims=True)

    surplus = jnp.where((n_sel < jnp.inf) & real_q, n_sel - k_sel, 0.0)

    @pl.when(jnp.max(surplus) > 0.0)
    def _():
        n_ties = k_sel - count(thr, True)
        row_i = lax.broadcasted_iota(jnp.int32, (KEY_BLOCK, KEY_BLOCK), 0)
        col_i = lax.broadcasted_iota(jnp.int32, (KEY_BLOCK, KEY_BLOCK), 1)
        lower = jnp.where(col_i < row_i, 1.0, 0.0).astype(BF16)
        ones = jnp.ones((KEY_BLOCK, KEY_BLOCK), BF16)

        def tie_body(j, seen):
            s = s_ref[j]
            eq = s == thr
            e = jnp.where(eq, 1.0, 0.0).astype(BF16)
            before = _dot(lower, e) + seen
            s_ref[j] = jnp.where(eq & (before >= n_ties), -jnp.inf, s)
            return seen + _dot(ones, e)

        lax.fori_loop(0, n_kb, tie_body, jnp.zeros((KEY_BLOCK, tq), F32))

    thr_sel = jnp.broadcast_to(jnp.maximum(thr, jnp.finfo(F32).min), (KEY_BLOCK, tq))

    for t in range(n_tiles):
        m_refs[t][...] = jnp.full(m_refs[t].shape, NEG_BIG, F32)
        acc_refs[t][...] = jnp.zeros(acc_refs[t].shape, F32)

    def far_blocks(db):
        out = []
        for j in (2 * db, 2 * db + 1):
            js = jnp.where(j < qs - 1, j, n_kb)
            out.append((js, jnp.minimum(js, last_kv)))
        return out

    near = [(jnp.where(qs >= 1, qs - 1, n_kb), jnp.maximum(qs - 1, 0)), (qs, qs)]

    def mask_of(blocks):
        m = jnp.concatenate([jnp.where(s_ref[js] >= thr_sel, 0.0, NEG_BIG) for js, _ in blocks], axis=0)
        return jnp.concatenate([m] * hpt, axis=1)

    def logits(blocks, mask, biased, slot, t):
        n = (t * hpt) // group
        kn = jnp.concatenate(
            [k_ref[0, pl.ds(pl.multiple_of(jk * KEY_BLOCK, KEY_BLOCK), KEY_BLOCK),
                   n * 2 * HEAD_DIM:(n + 1) * 2 * HEAD_DIM] for _, jk in blocks], axis=0)
        lg = _dot(kn, qgt_ref[t]) + mask
        if biased:
            lg = lg + bt_ref[t]
        lg_ref[slot, t] = lg
        mb_ref[slot, t] = jnp.max(lg, axis=0, keepdims=True)

    def softmax_update(blocks, slot, t):
        n = (t * hpt) // group
        m_old = m_refs[t][...]
        m_new = jnp.maximum(m_old, mb_ref[slot, t])
        alpha = jnp.exp2(m_old - m_new)
        p = jnp.exp2(lg_ref[slot, t] - m_new).astype(BF16)
        m_refs[t][...] = m_new
        vtn = jnp.concatenate([vt_ref[0, jk, n] for _, jk in blocks], axis=1)
        acc_refs[t][...] = alpha * acc_refs[t][...] + _dot(jnp.concatenate([vtn, ones_rows], axis=0), p)

    row_id = lax.broadcasted_iota(jnp.int32, (V_AUG_ROWS - HEAD_DIM, 2 * KEY_BLOCK), 0)
    ones_rows = jnp.where(row_id == 0, 1.0, 0.0).astype(BF16)

    n_far = qs // 2
    mask_near = mask_of(near)
    for t in range(n_tiles):
        logits(near, mask_near, True, 0, t)

    def trip_body(trip, carry):
        d0 = 2 * trip
        blk1, blk2 = far_blocks(d0), far_blocks(d0 + 1)
        mask1, mask2 = mask_of(blk1), mask_of(blk2)
        far0 = far_blocks(d0 - 1)
        blk0 = [tuple(jnp.where(d0 == 0, a, b) for a, b in zip(near[h], far0[h])) for h in range(2)]
        for t in range(n_tiles):
            logits(blk1, mask1, False, 1, t)
            softmax_update(blk0, 0, t)
        for t in range(n_tiles):
            logits(blk2, mask2, False, 0, t)
            softmax_update(blk1, 1, t)
        return carry

    lax.fori_loop(0, (n_far + 2) // 2, trip_body, 0)

    for t in range(n_tiles):
        on = acc_refs[t][0:HEAD_DIM, :] / acc_refs[t][HEAD_DIM:HEAD_DIM + 1, :]
        for g in range(hpt):
            hh = t * hpt + g
            ot_ref[hh * HEAD_DIM:(hh + 1) * HEAD_DIM, :] = on[:, g * tq:(g + 1) * tq]
    o_ref[0] = ot_ref[...].T.astype(BF16)


def _sparse_attn(q, qi, wi, k2, vb, kib, bias_tiles, *, q_off, n_keys, n_queries, k_sel):
    b, t, dq = q.shape
    lp = vb.shape[1]
    dkv = vb.shape[2]
    tq = KEY_BLOCK
    n_heads = dq // HEAD_DIM
    group = n_heads // N_KV_HEADS
    hpt = HEADS_PER_TILE
    n_tiles = n_heads // hpt
    assert group % hpt == 0
    nkb = lp // KEY_BLOCK
    assert t % tq == 0 and lp % KEY_BLOCK == 0 and q_off % KEY_BLOCK == 0 and q_off + t <= lp
    vt = jnp.swapaxes(vb.reshape(b, nkb, KEY_BLOCK, dkv), 2, 3).reshape(
        b, nkb, N_KV_HEADS, HEAD_DIM, KEY_BLOCK)
    qblk = lambda n: pl.BlockSpec((1, tq, n), lambda bi, i: (bi, i, 0))
    kblk = lambda n: pl.BlockSpec((1, lp, n), lambda bi, i: (bi, 0, 0))
    return pl.pallas_call(
        functools.partial(_sparse_attn_kernel, q_off=q_off, n_keys=n_keys, n_queries=n_queries, k_sel=k_sel,
                          group=group),
        out_shape=jax.ShapeDtypeStruct((b, t, dq), BF16),
        grid=(b, t // tq),
        in_specs=[qblk(dq), qblk(qi.shape[2]), qblk(LANES), kblk(2 * dkv),
                  pl.BlockSpec((1, nkb, N_KV_HEADS, HEAD_DIM, KEY_BLOCK), lambda bi, i: (bi, 0, 0, 0, 0)),
                  kblk(kib.shape[2]), _const_spec(bias_tiles.shape)],
        out_specs=qblk(dq),
        scratch_shapes=[
            pltpu.VMEM((nkb + 1 + COUNT_UNROLL, KEY_BLOCK, tq), F32),
            pltpu.VMEM((nkb // 2 + COUNT_UNROLL, KEY_BLOCK, tq), jnp.int32),
            pltpu.VMEM((4, KEY_BLOCK, IDX_HEADS * tq), F32),
            pltpu.VMEM((n_tiles, 2 * HEAD_DIM, hpt * tq), BF16),
            pltpu.VMEM((IDX_DIM, IDX_HEADS * tq), BF16),
            pltpu.VMEM((dq, tq), F32),
            pltpu.VMEM((2, n_tiles, 2 * KEY_BLOCK, hpt * tq), F32),
            pltpu.VMEM((2, n_tiles, 1, hpt * tq), F32),
        ] + [pltpu.VMEM((1, hpt * tq), F32)] * n_tiles
          + [pltpu.VMEM((V_AUG_ROWS, hpt * tq), F32)] * n_tiles,
        compiler_params=_params("parallel", "arbitrary"),
        name="sparse_attn",
    )(q, qi, wi, k2, vt, kib, bias_tiles)


def _mem_kv_kernel(x_ref, g_ref, w_ref, k_ref, v_ref, kh_ref, vh_ref, *, d):
    h = _rms(x_ref[...], g_ref[...]).astype(BF16)
    k = _dot(h, w_ref[:, :d])
    v = _dot(h, w_ref[:, d:])
    k_ref[...] = k
    v_ref[...] = v
    kh_ref[...] = k.reshape(kh_ref.shape)
    vh_ref[...] = v.reshape(vh_ref.shape)


def _mem_kv(mem2, g, w_kv, layer):
    m, d = mem2.shape
    tm = _row_tile(m, ROW_TILE)
    row = pl.BlockSpec((tm, d), lambda i: (i, 0))
    hd = d // MEM_HEADS
    split = pl.BlockSpec((tm, MEM_HEADS, hd), lambda i: (i, 0, 0))
    return pl.pallas_call(
        functools.partial(_mem_kv_kernel, d=d),
        out_shape=[jax.ShapeDtypeStruct((m, d), F32)] * 2 + [jax.ShapeDtypeStruct((m, MEM_HEADS, hd), F32)] * 2,
        grid=(m // tm,),
        in_specs=[row, _const_spec((1, d)), _layer_spec(w_kv, layer)],
        out_specs=[row, row, split, split],
        compiler_params=_params("parallel"),
        name="mem_kv",
    )(mem2, g.reshape(1, d), w_kv)


def _mem_attn_kernel(*refs, hd, pending):
    if pending:
        x_ref, a_ref, wa_ref, g_ref, wq_ref, mk_ref, mv_ref, wo_ref, o_ref = refs
        x = x_ref[...] + _dot(a_ref[...], wa_ref[...])
    else:
        x_ref, g_ref, wq_ref, mk_ref, mv_ref, wo_ref, o_ref = refs
        x = x_ref[...]
    h = _rms(x, g_ref[...]).astype(BF16)
    q = (_dot(h, wq_ref[...]) * (hd ** -0.5)).astype(BF16)
    heads = []
    for a in range(MEM_HEADS):
        cols = slice(a * hd, (a + 1) * hd)
        lg = _dot_nt(q[:, cols], mk_ref[0, :, cols].astype(BF16))
        p = jnp.exp(lg - jnp.max(lg, axis=1, keepdims=True))
        p = (p / jnp.sum(p, axis=1, keepdims=True)).astype(BF16)
        heads.append(_dot(p, mv_ref[0, :, cols].astype(BF16)).astype(BF16))
    o_ref[...] = x + _dot(jnp.concatenate(heads, axis=1), wo_ref[...])


def _mem_attn(x2, g, w_q, mk, mv, w_o, layer, rows_per_batch, pending=None):
    m, d = x2.shape
    n_mem = mk.shape[1]
    tm = _row_tile(rows_per_batch, ROW_TILE)
    per = rows_per_batch // tm
    row = pl.BlockSpec((tm, d), lambda i: (i, 0))
    mem = pl.BlockSpec((1, n_mem, d), lambda i: (i // per, 0, 0))
    in_specs = [row, _const_spec((1, d)), _layer_spec(w_q, layer), mem, mem, _layer_spec(w_o, layer)]
    args = [x2, g.reshape(1, d), w_q, mk, mv, w_o]
    if pending is not None:
        a2, w_a, layer_a = pending
        in_specs[1:1] = [pl.BlockSpec((tm, a2.shape[1]), lambda i: (i, 0)), _layer_spec(w_a, layer_a)]
        args[1:1] = [a2, w_a]
    return pl.pallas_call(
        functools.partial(_mem_attn_kernel, hd=d // MEM_HEADS, pending=pending is not None),
        out_shape=jax.ShapeDtypeStruct((m, d), F32),
        grid=(m // tm,),
        in_specs=in_specs,
        out_specs=row,
        compiler_params=_params("parallel"),
        name="mem_attn",
    )(*args)


def _mlp_kernel(*refs, n_chunks, chunk, final):
    if final:
        x_ref, g_ref, w1_ref, w2_ref, gf_ref, o_ref = refs
    else:
        x_ref, g_ref, w1_ref, w2_ref, o_ref = refs
    x = x_ref[...]
    h = _rms(x, g_ref[...]).astype(BF16)
    acc = x
    for c in range(n_chunks):
        a = jnp.maximum(_dot(h, w1_ref[:, c * chunk:(c + 1) * chunk]), 0.0)
        acc = acc + _dot((a * a).astype(BF16), w2_ref[c * chunk:(c + 1) * chunk, :])
    if final:
        acc = _rms(acc, gf_ref[...])
    o_ref[...] = acc


def _mlp(x2, g, w1, w2, layer, g_final=None):
    m, d = x2.shape
    dff = w1.shape[2]
    chunk = min(dff, 1024)
    tm = _row_tile(m, ROW_TILE)
    row = pl.BlockSpec((tm, d), lambda i: (i, 0))
    final = g_final is not None
    in_specs = [row, _const_spec((1, d)), _layer_spec(w1, layer), _layer_spec(w2, layer)]
    args = [x2, g.reshape(1, d), w1, w2]
    if final:
        in_specs.append(_const_spec((1, d)))
        args.append(g_final.reshape(1, d))
    return pl.pallas_call(
        functools.partial(_mlp_kernel, n_chunks=dff // chunk, chunk=chunk, final=final),
        out_shape=jax.ShapeDtypeStruct((m, d), F32),
        grid=(m // tm,),
        in_specs=in_specs,
        out_specs=row,
        compiler_params=_params("parallel"),
        name="mlp",
    )(*args)


def _conv_glu_kernel(x_ref, g_ref, w_ref, b_ref, u_ref, *, d):
    h = _rms(x_ref[...], g_ref[...]).astype(BF16)
    a = _dot(h, w_ref[:, :d]) + b_ref[:, :d]
    gate = _dot(h, w_ref[:, d:]) + b_ref[:, d:]
    u_ref[...] = a * (1.0 / (1.0 + jnp.exp(-gate)))


def _conv_glu(x2, g, w_pw1, layer, b_pw1):
    m, d = x2.shape
    tm = _row_tile(m, ROW_TILE)
    row = pl.BlockSpec((tm, d), lambda i: (i, 0))
    return pl.pallas_call(
        functools.partial(_conv_glu_kernel, d=d),
        out_shape=jax.ShapeDtypeStruct((m, d), F32),
        grid=(m // tm,),
        in_specs=[row, _const_spec((1, d)), _layer_spec(w_pw1, layer), _const_spec((1, 2 * d))],
        out_specs=row,
        compiler_params=_params("parallel"),
        name="conv_glu",
    )(x2, g.reshape(1, d), w_pw1, b_pw1.reshape(1, 2 * d))


def _conv_rest_kernel(x_ref, u_ref, prev_ref, init_ref, wdw_ref, bdw_ref, lng_ref, lnb_ref,
                      w2_ref, b2_ref, o_ref, ext_ref, sh_ref, y_ref, *, tm, rc, lc):
    t = pl.program_id(1)
    d = u_ref.shape[2]
    pad = CONV_WIDTH - 1

    @pl.when(t == 0)
    def _():
        ext_ref[0:HALO, :] = init_ref[0]

    @pl.when(t > 0)
    def _():
        ext_ref[0:HALO, :] = prev_ref[0]

    ext_ref[HALO:HALO + tm, :] = u_ref[0]

    for s in range(SUBLANES):
        rows = tm + SUBLANES * ((CONV_WIDTH - 1 - s) // SUBLANES)
        sh_ref[s, 0:rows, :] = ext_ref[pl.ds(HALO - pad + s, rows), :]
    for r0 in range(0, tm, rc):
        for c0 in range(0, d, lc):
            cols = slice(c0, c0 + lc)
            y = jnp.broadcast_to(bdw_ref[:, cols], (rc, lc))
            for w in range(CONV_WIDTH):
                a, s = divmod(w, SUBLANES)
                y = y + sh_ref[s, r0 + SUBLANES * a:r0 + SUBLANES * a + rc, cols] * wdw_ref[w:w + 1, cols]
            y_ref[r0:r0 + rc, cols] = y

    y = y_ref[...]
    mu = jnp.mean(y, axis=-1, keepdims=True)
    yc = y - mu
    var = jnp.mean(yc * yc, axis=-1, keepdims=True)
    yn = yc * lax.rsqrt(var + EPS) * lng_ref[...] + lnb_ref[...]
    act = (yn * (1.0 / (1.0 + jnp.exp(-yn)))).astype(BF16)
    o_ref[0] = x_ref[0] + _dot(act, w2_ref[...]) + b2_ref[...]


def _conv_rest(x3, u3, init, w_dw, b_dw, ln_g, ln_b, w_pw2, layer, b_pw2):
    b, t, d = x3.shape
    tm = _row_tile(t, CONV_ROW_TILE)
    rc = min(tm, 64)
    lc = min(d, 256)
    assert tm % HALO == 0 and tm % rc == 0 and d % lc == 0
    per = tm // HALO
    tile = pl.BlockSpec((1, tm, d), lambda bi, ti: (bi, ti, 0))
    prev = pl.BlockSpec((1, HALO, d), lambda bi, ti: (bi, jnp.maximum(ti * per - 1, 0), 0))
    first = pl.BlockSpec((1, HALO, d), lambda bi, ti: (bi, 0, 0))
    vec = _const_spec((1, d))
    wdw = jnp.pad(w_dw, ((0, HALO - CONV_WIDTH), (0, 0)))
    return pl.pallas_call(
        functools.partial(_conv_rest_kernel, tm=tm, rc=rc, lc=lc),
        out_shape=jax.ShapeDtypeStruct((b, t, d), F32),
        grid=(b, t // tm),
        in_specs=[tile, tile, prev, first, _const_spec((HALO, d)), vec, vec, vec,
                  _layer_spec(w_pw2, layer), vec],
        out_specs=tile,
        scratch_shapes=[pltpu.VMEM((HALO + tm, d), F32),
                        pltpu.VMEM((SUBLANES, tm + HALO - SUBLANES, d), F32),
                        pltpu.VMEM((tm, d), F32)],
        compiler_params=_params("parallel", "arbitrary"),
        name="conv_rest",
    )(x3, u3, u3, init, wdw, b_dw.reshape(1, d), ln_g.reshape(1, d), ln_b.reshape(1, d),
      w_pw2, b_pw2.reshape(1, d))


def _mixer_attn(x3, g, w_in, layer, bias_tiles, cache=None):
    b, t, d = x3.shape
    x2 = x3.reshape(b * t, d)
    q, k, v, ki, kb, vb, kib, qi, wi = _attn_proj(x2, g, w_in, layer)
    r3 = lambda a: a.reshape(b, t, a.shape[-1])
    kb, vb, kib = r3(kb), r3(vb), r3(kib)
    past = 0
    if cache is not None:
        ck, cv, cki = cache
        past = ck.shape[1]
        ck2 = jnp.concatenate([ck[:, :, n].astype(BF16) for n in range(N_KV_HEADS) for _ in range(2)], axis=-1)
        kb = jnp.concatenate([ck2, kb], axis=1)
        vb = jnp.concatenate([cv.reshape(b, past, -1).astype(BF16), vb], axis=1)
        kib = jnp.concatenate([cki.astype(BF16), kib], axis=1)
    n_keys = past + t
    k_sel = min(TOPK_MAX, n_keys // 4)
    tpad = -(-t // KEY_BLOCK) * KEY_BLOCK
    lp = max(-(-n_keys // KEY_BLOCK) * KEY_BLOCK, past + tpad)
    padt = lambda a, n: a if a.shape[1] == n else jnp.pad(a, ((0, 0), (0, n - a.shape[1]), (0, 0)))
    o = _sparse_attn(padt(r3(q), tpad), padt(r3(qi), tpad), padt(r3(wi), tpad),
                     padt(kb, lp), padt(vb, lp), padt(kib, lp), bias_tiles,
                     q_off=past, n_keys=n_keys, n_queries=t, k_sel=k_sel)[:, :t]
    return (o.reshape(b * t, d), k.reshape(b, t, N_KV_HEADS, HEAD_DIM),
            v.reshape(b, t, N_KV_HEADS, HEAD_DIM), ki.reshape(b, t, IDX_DIM))


def _mixer_conv(x3, g, w_pw1, b_pw1, w_dw, b_dw, ln_g, ln_b, w_pw2, b_pw2, layer, state=None):
    b, t, d = x3.shape
    pad = CONV_WIDTH - 1
    u3 = _conv_glu(x3.reshape(b * t, d), g, w_pw1, layer, b_pw1).reshape(b, t, d)
    if state is None:
        init = jnp.zeros((b, HALO, d), F32)
        tail = u3[:, -pad:] if t >= pad else jnp.pad(u3, ((0, 0), (pad - t, 0), (0, 0)))
    else:
        init = jnp.pad(state.astype(F32), ((0, 0), (HALO - pad, 0), (0, 0)))
        tail = jnp.concatenate([state.astype(F32), u3], axis=1)[:, -pad:]
    x3 = _conv_rest(x3, u3, init, w_dw, b_dw, ln_g, ln_b, w_pw2, layer, b_pw2)
    return x3, tail


def kernel(x_prompt, x_sample, cache_attn_k, cache_attn_v, cache_attn_kidx, state_conv, cache_mem_k, cache_mem_v, mem_prompt, rel_bias, g_mix, w_in_attn, w_out_attn, w_pw1, b_pw1, w_dw, b_dw, ln_g, ln_b, w_pw2, b_pw2, g_mem_q, g_mem_src, w_mem_q, w_mem_kv, w_mem_o, g_mlp, w_mlp1, w_mlp2, g_final):
    depth = g_mix.shape[0]
    bp, tp, d = x_prompt.shape
    bs, ts, _ = x_sample.shape
    n_mem = mem_prompt.shape[1]
    mem_hd = d // MEM_HEADS
    bias_tiles = _bias_tiles(rel_bias)
    w_in_b, w_out_b = _attn_proj_weights(w_in_attn), w_out_attn.astype(BF16)
    w_pw1_b, w_pw2_b = w_pw1.astype(BF16), w_pw2.astype(BF16)
    w_mq_b, w_mkv_b, w_mo_b = w_mem_q.astype(BF16), w_mem_kv.astype(BF16), w_mem_o.astype(BF16)
    w_mlp1_b, w_mlp2_b = w_mlp1.astype(BF16), w_mlp2.astype(BF16)
    xp, xs = x_prompt, x_sample
    kp_l, vp_l, kip_l, ks_l, vs_l, kis_l = [], [], [], [], [], []
    convp_l, convs_l, memk_l, memv_l = [], [], [], []
    for i in range(depth):
        j = i // 2
        if i % 2 == 0:
            op, kp, vp, kip = _mixer_attn(xp, g_mix[i], w_in_b, j, bias_tiles)
            osm, ks, vs, kis = _mixer_attn(
                xs, g_mix[i], w_in_b, j, bias_tiles,
                cache=(cache_attn_k[j], cache_attn_v[j], cache_attn_kidx[j]))
            pend_p, pend_s = (op, w_out_b, j), (osm, w_out_b, j)
            kp_l.append(kp); vp_l.append(vp); kip_l.append(kip)
            ks_l.append(ks); vs_l.append(vs); kis_l.append(kis)
        else:
            cw = (w_pw1_b, b_pw1[j], w_dw[j], b_dw[j], ln_g[j], ln_b[j], w_pw2_b, b_pw2[j], j)
            xp, cp = _mixer_conv(xp, g_mix[i], *cw)
            xs, cs = _mixer_conv(xs, g_mix[i], *cw, state=state_conv[j])
            pend_p = pend_s = None
            convp_l.append(cp); convs_l.append(cs)
        mk, mv, mkh, mvh = _mem_kv(mem_prompt.reshape(bp * n_mem, d), g_mem_src[i], w_mkv_b, i)
        mk, mv = mk.reshape(bp, n_mem, d), mv.reshape(bp, n_mem, d)
        memk_l.append(mkh.reshape(bp, n_mem, MEM_HEADS, mem_hd))
        memv_l.append(mvh.reshape(bp, n_mem, MEM_HEADS, mem_hd))
        xp2 = _mem_attn(xp.reshape(bp * tp, d), g_mem_q[i], w_mq_b, mk, mv, w_mo_b, i, tp, pend_p)
        xs2 = _mem_attn(xs.reshape(bs * ts, d), g_mem_q[i], w_mq_b,
                        cache_mem_k[i].reshape(bs, n_mem, d), cache_mem_v[i].reshape(bs, n_mem, d),
                        w_mo_b, i, ts, pend_s)
        gf = g_final if i == depth - 1 else None
        xp = _mlp(xp2, g_mlp[i], w_mlp1_b, w_mlp2_b, i, gf).reshape(bp, tp, d)
        xs = _mlp(xs2, g_mlp[i], w_mlp1_b, w_mlp2_b, i, gf).reshape(bs, ts, d)
    return (xp, xs, jnp.stack(kp_l), jnp.stack(vp_l), jnp.stack(kip_l), jnp.stack(convp_l),
            jnp.stack(memk_l), jnp.stack(memv_l), jnp.stack(ks_l), jnp.stack(vs_l),
            jnp.stack(kis_l), jnp.stack(convs_l))
```

```python
import functools
import math

import jax
import jax.numpy as jnp
from jax import lax
from jax.experimental import pallas as pl
from jax.experimental.pallas import tpu as pltpu

CHUNK = 64
HEAD_DIM = 64
N_KV_HEADS = 4
IDX_HEADS = 8
IDX_DIM = 64
TOPK_MAX = 256
N_BUCKETS = 32
MAX_DISTANCE = 128
CONV_WIDTH = 31
MEM_HEADS = 4
EPS = 1e-6

LANES = 128
SUBLANES = 8
KEY_BLOCK = 128
HEADS_PER_TILE = 2
COUNT_UNROLL = 4
HI_BITS = 15
LO_BITS = 32 - HI_BITS
LOW_CHECK_EVERY = 4
V_AUG_ROWS = HEAD_DIM + 16
LOG2E = math.log2(math.e)
ROW_TILE = 1024
CONV_ROW_TILE = 512
HALO = 32
VMEM_LIMIT = 56 * 1024 * 1024

NEG_BIG = -1e30
F32 = jnp.float32
BF16 = jnp.bfloat16

KEY_NEG_INF = -2139095041
KEY_POS_INF = 2139095040


def _const_spec(shape):
    nd = len(shape)
    return pl.BlockSpec(shape, lambda *_: (0,) * nd, pipeline_mode=pl.Buffered(1))


def _layer_spec(stacked, layer):
    nd = stacked.ndim
    return pl.BlockSpec((None,) + stacked.shape[1:], lambda *_: (layer,) + (0,) * (nd - 1),
                        pipeline_mode=pl.Buffered(1))


def _params(*sem):
    return pltpu.CompilerParams(dimension_semantics=sem, vmem_limit_bytes=VMEM_LIMIT)


def _rms(x, g):
    ms = jnp.mean(x * x, axis=-1, keepdims=True)
    return x * lax.rsqrt(ms + EPS) * g


def _dot(a, b):
    return jnp.dot(a, b, preferred_element_type=F32)


def _dot_nt(a, b):
    return lax.dot_general(a, b, (((1,), (1,)), ((), ())), preferred_element_type=F32)


def _row_tile(m, pref):
    t = min(m, pref)
    assert m % t == 0, (m, t)
    return t


def _bias_tiles_kernel(tab_ref, bt_ref, *, n_heads):
    nb = N_BUCKETS // 2
    max_exact = nb // 2
    c = lax.broadcasted_iota(jnp.int32, (KEY_BLOCK, KEY_BLOCK), 0)
    r = lax.broadcasted_iota(jnp.int32, (KEY_BLOCK, KEY_BLOCK), 1)
    for d in range(2):
        rel = c - r - d * KEY_BLOCK
        n = jnp.abs(rel)
        nf = jnp.maximum(n, 1).astype(F32)
        large = max_exact + (jnp.log(nf / max_exact) / math.log(MAX_DISTANCE / max_exact)
                             * (nb - max_exact)).astype(jnp.int32)
        large = jnp.minimum(large, nb - 1)
        bucket = jnp.where(rel > 0, nb, 0) + jnp.where(n < max_exact, n, large)
        for h in range(n_heads):
            acc = jnp.zeros((KEY_BLOCK, KEY_BLOCK), F32)
            for b in range(N_BUCKETS):
                acc = jnp.where(bucket == b, tab_ref[b, h], acc)
            g = h % HEADS_PER_TILE
            rows = slice((1 - d) * KEY_BLOCK, (2 - d) * KEY_BLOCK)
            bt_ref[h // HEADS_PER_TILE, rows, g * KEY_BLOCK:(g + 1) * KEY_BLOCK] = (
                (acc - tab_ref[nb - 1, h]) * LOG2E)


def _bias_tiles(rel_bias):
    n_heads = rel_bias.shape[1]
    assert n_heads % HEADS_PER_TILE == 0
    return pl.pallas_call(
        functools.partial(_bias_tiles_kernel, n_heads=n_heads),
        out_shape=jax.ShapeDtypeStruct(
            (n_heads // HEADS_PER_TILE, 2 * KEY_BLOCK, HEADS_PER_TILE * KEY_BLOCK), F32),
        in_specs=[pl.BlockSpec(memory_space=pltpu.SMEM)],
        out_specs=pl.BlockSpec(memory_space=pltpu.VMEM),
        name="bias_tiles",
    )(rel_bias)


def _attn_proj_kernel(x_ref, g_ref, w_ref, q_ref, k_ref, v_ref, ki_ref, kb_ref, vb_ref,
                      kib_ref, qi_ref, wi_ref, *, dq, dkv, dqi):
    h = _rms(x_ref[...], g_ref[...]).astype(BF16)
    o = 0
    q_ref[...] = (_dot(h, w_ref[:, o:o + dq]) * (HEAD_DIM ** -0.5)).astype(BF16)
    o += dq
    k = _dot(h, w_ref[:, o:o + dkv])
    o += dkv
    v = _dot(h, w_ref[:, o:o + dkv])
    vb_ref[...] = v.astype(BF16)
    k_ref[...] = k.reshape(k_ref.shape)
    v_ref[...] = v.reshape(v_ref.shape)
    o += dkv
    qi_ref[...] = (_dot(h, w_ref[:, o:o + dqi]) * (IDX_DIM ** -0.5)).astype(BF16)
    o += dqi
    ki = _dot(h, w_ref[:, o:o + LANES])[:, :IDX_DIM]
    ki_ref[...] = ki
    kib_ref[...] = ki.astype(BF16)
    o += LANES
    wi_ref[...] = _dot(h, w_ref[:, o:o + LANES]) * (IDX_HEADS ** -0.5)
    o += LANES
    kb_ref[...] = _dot(h, w_ref[:, o:o + 2 * dkv]).astype(BF16)


def _attn_proj_weights(w_in):
    d = w_in.shape[1]
    base = d + 2 * N_KV_HEADS * HEAD_DIM + IDX_HEADS * IDX_DIM
    pad = lambda a: jnp.pad(a, ((0, 0), (0, 0), (0, LANES - a.shape[2])))
    w_k = [w_in[:, :, d + n * HEAD_DIM:d + (n + 1) * HEAD_DIM] for n in range(N_KV_HEADS)]
    return jnp.concatenate([w_in[:, :, :base], pad(w_in[:, :, base:base + IDX_DIM]),
                            pad(w_in[:, :, base + IDX_DIM:])] + [w for w in w_k for _ in range(2)],
                           axis=2).astype(BF16)


def _attn_proj(x2, g, w, layer):
    m, d = x2.shape
    dq = d
    dkv = N_KV_HEADS * HEAD_DIM
    dqi = IDX_HEADS * IDX_DIM
    tm = _row_tile(m, ROW_TILE)
    row = lambda *n: pl.BlockSpec((tm,) + n, lambda i: (i,) + (0,) * len(n))
    heads = (N_KV_HEADS, HEAD_DIM)
    outs = [((dq,), BF16), (heads, F32), (heads, F32), ((IDX_DIM,), F32), ((2 * dkv,), BF16), ((dkv,), BF16),
            ((IDX_DIM,), BF16), ((dqi,), BF16), ((LANES,), F32)]
    return pl.pallas_call(
        functools.partial(_attn_proj_kernel, dq=dq, dkv=dkv, dqi=dqi),
        out_shape=[jax.ShapeDtypeStruct((m,) + n, dt) for n, dt in outs],
        grid=(m // tm,),
        in_specs=[row(d), _const_spec((1, d)), _layer_spec(w, layer)],
        out_specs=[row(*n) for n, _ in outs],
        compiler_params=_params("parallel"),
        name="attn_proj",
    )(x2, g.reshape(1, d), w)


def _sparse_attn_kernel(q_ref, qi_ref, wi_ref, k_ref, vt_ref, ki_ref, bt_ref, o_ref,
                        s_ref, w_ref, dots_ref, qgt_ref, qit_ref, ot_ref, lg_ref, mb_ref, *state_refs,
                        q_off, n_keys, n_queries, k_sel, group):
    tq = KEY_BLOCK
    hpt = HEADS_PER_TILE
    n_tiles = qgt_ref.shape[0]
    m_refs, acc_refs = state_refs[:n_tiles], state_refs[n_tiles:]
    i = pl.program_id(1)
    qs = q_off // KEY_BLOCK + i
    n_kb = qs + 1
    q_start = q_off + i * tq

    qt = q_ref[0].astype(F32).T * LOG2E
    for t in range(n_tiles):
        q2 = jnp.concatenate(
            [qt[(t * hpt + g) * HEAD_DIM:(t * hpt + g + 1) * HEAD_DIM, :] for g in range(hpt)], axis=1)
        q_hi = q2.astype(BF16)
        q_lo = (q2 - q_hi.astype(F32)).astype(BF16)
        qgt_ref[t] = jnp.concatenate([q_hi, q_lo], axis=0)
    qit = qi_ref[0].astype(F32).T
    qit_ref[...] = jnp.concatenate([qit[h * IDX_DIM:(h + 1) * IDX_DIM, :] for h in range(IDX_HEADS)],
                                   axis=1).astype(BF16)
    last_kv = k_ref.shape[1] // KEY_BLOCK - 1
    wit = wi_ref[0].T

    qpos = q_start + lax.broadcasted_iota(jnp.int32, (1, tq), 1)
    lim = jnp.minimum((qpos // CHUNK + 1) * CHUNK, n_keys)
    real_q = qpos < q_off + n_queries
    kidx = lax.broadcasted_iota(jnp.int32, (KEY_BLOCK, tq), 0)

    def key_to_f32(key):
        bits = key ^ ((key >> 31) & 0x7FFFFFFF)
        return lax.bitcast_convert_type(bits, F32)

    def digit(sc):
        bits = lax.bitcast_convert_type(jnp.where(sc == 0.0, 0.0, sc), jnp.int32)
        key = bits ^ ((bits >> 31) & 0x7FFFFFFF)
        return (key >> LO_BITS) + (1 << (HI_BITS - 1))

    def dots(j, slot):
        k0 = pl.multiple_of(jnp.minimum(j, last_kv) * KEY_BLOCK, KEY_BLOCK)
        dots_ref[slot] = _dot(ki_ref[0, pl.ds(k0, KEY_BLOCK), :], qit_ref[...])

    def finish(j, slot):
        acc = jnp.zeros((KEY_BLOCK, tq), F32)
        for h in range(IDX_HEADS):
            acc = acc + wit[h:h + 1, :] * jnp.maximum(dots_ref[slot, :, h * tq:(h + 1) * tq], 0.0)
        sc = jnp.where(kidx + j * KEY_BLOCK < lim, acc, -jnp.inf)
        s_ref[j] = sc
        return digit(sc)

    def finish_pair(j, slot):
        w_ref[j // 2] = (finish(j, slot) << 16) | finish(j + 1, slot + 1)

    def score_quad(quad, carry):
        j = 4 * quad
        dots(j + 2, 2)
        dots(j + 3, 3)
        finish_pair(j, 0)
        dots(j + 4, 0)
        dots(j + 5, 1)
        finish_pair(j + 2, 2)
        return carry

    n_pairs = (n_kb + 1) // 2
    n_quads = (n_kb + 3) // 4
    dots(0, 0)
    dots(1, 1)
    lax.fori_loop(0, n_quads, score_quad, 0)
    for u in range(COUNT_UNROLL):
        s_ref[2 * n_pairs + u] = jnp.full((KEY_BLOCK, tq), -jnp.inf, F32)
    for u in range(COUNT_UNROLL // 2):
        w_ref[2 * n_quads + u] = jnp.zeros((KEY_BLOCK, tq), jnp.int32)
    n_count = (2 * n_pairs + COUNT_UNROLL - 1) // COUNT_UNROLL

    def count_digits(cand):
        comp = (1 << HI_BITS) - cand
        cw = jnp.broadcast_to((comp << 16) | comp, (KEY_BLOCK, tq))

        def body(p, c):
            for u in range(COUNT_UNROLL // 2):
                c = c + (((w_ref[(COUNT_UNROLL // 2) * p + u] + cw) >> HI_BITS) & 0x00010001)
            return c
        c = lax.fori_loop(0, n_count, body, jnp.zeros((KEY_BLOCK, tq), jnp.int32))
        return jnp.sum(((c & 0xFFFF) + (c >> 16)).astype(F32), axis=0, keepdims=True)

    def count_ge(cand):
        cb = jnp.broadcast_to(cand, (KEY_BLOCK, tq))

        def body(p, c):
            for u in range(COUNT_UNROLL):
                c = c + jnp.where(s_ref[COUNT_UNROLL * p + u] >= cb, 1.0, 0.0)
            return c
        c = lax.fori_loop(0, n_count, body, jnp.zeros((KEY_BLOCK, tq), F32))
        return jnp.sum(c, axis=0, keepdims=True)

    def bisect_step(count_at, carry):
        lo, hi, n_lo = carry
        mid = (lo + hi) >> 1
        n_mid = count_at(mid)
        ok = n_mid >= k_sel
        return jnp.where(ok, mid, lo), jnp.where(ok, hi, mid), jnp.where(ok, n_mid, n_lo)

    full = lambda v: jnp.full((1, tq), v, jnp.int32)
    half = 1 << (HI_BITS - 1)
    dig, _, n_sel = lax.fori_loop(
        0, HI_BITS, lambda _, c: bisect_step(count_digits, c),
        (full(((KEY_NEG_INF + 1) >> LO_BITS) + half), full((KEY_POS_INF >> LO_BITS) + half + 1),
         jnp.full((1, tq), jnp.inf, F32)))
    key_hi = (dig - half) << LO_BITS

    count_lo = lambda v: count_ge(key_to_f32(key_hi + v))

    def low_steps(carry):
        step, c = carry[0], carry[1:]
        for _ in range(LOW_CHECK_EVERY):
            c = bisect_step(count_lo, c)
        return (step + LOW_CHECK_EVERY,) + c

    def low_unfinished(carry):
        step, n_lo = carry[0], carry[3]
        return (step < LO_BITS) & (jnp.max(jnp.where(real_q, jnp.abs(n_lo - k_sel), 0.0)) > 0.0)

    _, key_lo, _, n_sel = lax.while_loop(low_unfinished, low_steps, (0, full(0), full(1 << LO_BITS), n_sel))
    thr = key_to_f32(key_hi + key_lo)

    def count(cand, strict):
        cb = jnp.broadcast_to(cand, (KEY_BLOCK, tq))
        hit = (lambda s: s > cb) if strict else (lambda s: s >= cb)

        def body(p, c):
            c = c + jnp.where(hit(s_ref[2 * p]), 1.0, 0.0)
            return c + jnp.where(hit(s_ref[2 * p + 1]), 1.0, 0.0)
        c = lax.fori_loop(0, n_pairs, body, jnp.zeros((KEY_BLOCK, tq), F32))
        return jnp.sum(c, axis=0, keepdims=True)

    surplus = jnp.where((n_sel < jnp.inf) & real_q, n_sel - k_sel, 0.0)

    @pl.when(jnp.max(surplus) > 0.0)
    def _():
        n_ties = k_sel - count(thr, True)
        row_i = lax.broadcasted_iota(jnp.int32, (KEY_BLOCK, KEY_BLOCK), 0)
        col_i = lax.broadcasted_iota(jnp.int32, (KEY_BLOCK, KEY_BLOCK), 1)
        lower = jnp.where(col_i < row_i, 1.0, 0.0).astype(BF16)
        ones = jnp.ones((KEY_BLOCK, KEY_BLOCK), BF16)

        def tie_body(j, seen):
            s = s_ref[j]
            eq = s == thr
            e = jnp.where(eq, 1.0, 0.0).astype(BF16)
            before = _dot(lower, e) + seen
            s_ref[j] = jnp.where(eq & (before >= n_ties), -jnp.inf, s)
            return seen + _dot(ones, e)

        lax.fori_loop(0, n_kb, tie_body, jnp.zeros((KEY_BLOCK, tq), F32))

    thr_sel = jnp.broadcast_to(jnp.maximum(thr, jnp.finfo(F32).min), (KEY_BLOCK, tq))

    for t in range(n_tiles):
        m_refs[t][...] = jnp.full(m_refs[t].shape, NEG_BIG, F32)
        acc_refs[t][...] = jnp.zeros(acc_refs[t].shape, F32)

    def far_blocks(db):
        out = []
        for j in (2 * db, 2 * db + 1):
            js = jnp.where(j < qs - 1, j, n_kb)
            out.append((js, jnp.minimum(js, last_kv)))
        return out

    near = [(jnp.where(qs >= 1, qs - 1, n_kb), jnp.maximum(qs - 1, 0)), (qs, qs)]

    def mask_of(blocks):
        m = jnp.concatenate([jnp.where(s_ref[js] >= thr_sel, 0.0, NEG_BIG) for js, _ in blocks], axis=0)
        return jnp.concatenate([m] * hpt, axis=1)

    def logits(blocks, mask, biased, slot, t):
        n = (t * hpt) // group
        kn = jnp.concatenate(
            [k_ref[0, pl.ds(pl.multiple_of(jk * KEY_BLOCK, KEY_BLOCK), KEY_BLOCK),
                   n * 2 * HEAD_DIM:(n + 1) * 2 * HEAD_DIM] for _, jk in blocks], axis=0)
        lg = _dot(kn, qgt_ref[t]) + mask
        if biased:
            lg = lg + bt_ref[t]
        lg_ref[slot, t] = lg
        mb_ref[slot, t] = jnp.max(lg, axis=0, keepdims=True)

    def softmax_update(blocks, slot, t):
        n = (t * hpt) // group
        m_old = m_refs[t][...]
        m_new = jnp.maximum(m_old, mb_ref[slot, t])
        alpha = jnp.exp2(m_old - m_new)
        p = jnp.exp2(lg_ref[slot, t] - m_new).astype(BF16)
        m_refs[t][...] = m_new
        vtn = jnp.concatenate([vt_ref[0, jk, n] for _, jk in blocks], axis=1)
        acc_refs[t][...] = alpha * acc_refs[t][...] + _dot(jnp.concatenate([vtn, ones_rows], axis=0), p)

    row_id = lax.broadcasted_iota(jnp.int32, (V_AUG_ROWS - HEAD_DIM, 2 * KEY_BLOCK), 0)
    ones_rows = jnp.where(row_id == 0, 1.0, 0.0).astype(BF16)

    n_far = qs // 2
    mask_near = mask_of(near)
    for t in range(n_tiles):
        logits(near, mask_near, True, 0, t)

    def trip_body(trip, carry):
        d0 = 2 * trip
        blk1, blk2 = far_blocks(d0), far_blocks(d0 + 1)
        mask1, mask2 = mask_of(blk1), mask_of(blk2)
        far0 = far_blocks(d0 - 1)
        blk0 = [tuple(jnp.where(d0 == 0, a, b) for a, b in zip(near[h], far0[h])) for h in range(2)]
        for t in range(n_tiles):
            logits(blk1, mask1, False, 1, t)
            softmax_update(blk0, 0, t)
        for t in range(n_tiles):
            logits(blk2, mask2, False, 0, t)
            softmax_update(blk1, 1, t)
        return carry

    lax.fori_loop(0, (n_far + 2) // 2, trip_body, 0)

    for t in range(n_tiles):
        on = acc_refs[t][0:HEAD_DIM, :] / acc_refs[t][HEAD_DIM:HEAD_DIM + 1, :]
        for g in range(hpt):
            hh = t * hpt + g
            ot_ref[hh * HEAD_DIM:(hh + 1) * HEAD_DIM, :] = on[:, g * tq:(g + 1) * tq]
    o_ref[0] = ot_ref[...].T.astype(BF16)


def _sparse_attn(q, qi, wi, k2, vb, kib, bias_tiles, *, q_off, n_keys, n_queries, k_sel):
    b, t, dq = q.shape
    lp = vb.shape[1]
    dkv = vb.shape[2]
    tq = KEY_BLOCK
    n_heads = dq // HEAD_DIM
    group = n_heads // N_KV_HEADS
    hpt = HEADS_PER_TILE
    n_tiles = n_heads // hpt
    assert group % hpt == 0
    nkb = lp // KEY_BLOCK
    assert t % tq == 0 and lp % KEY_BLOCK == 0 and q_off % KEY_BLOCK == 0 and q_off + t <= lp
    vt = jnp.swapaxes(vb.reshape(b, nkb, KEY_BLOCK, dkv), 2, 3).reshape(
        b, nkb, N_KV_HEADS, HEAD_DIM, KEY_BLOCK)
    qblk = lambda n: pl.BlockSpec((1, tq, n), lambda bi, i: (bi, i, 0))
    kblk = lambda n: pl.BlockSpec((1, lp, n), lambda bi, i: (bi, 0, 0))
    return pl.pallas_call(
        functools.partial(_sparse_attn_kernel, q_off=q_off, n_keys=n_keys, n_queries=n_queries, k_sel=k_sel,
                          group=group),
        out_shape=jax.ShapeDtypeStruct((b, t, dq), BF16),
        grid=(b, t // tq),
        in_specs=[qblk(dq), qblk(qi.shape[2]), qblk(LANES), kblk(2 * dkv),
                  pl.BlockSpec((1, nkb, N_KV_HEADS, HEAD_DIM, KEY_BLOCK), lambda bi, i: (bi, 0, 0, 0, 0)),
                  kblk(kib.shape[2]), _const_spec(bias_tiles.shape)],
        out_specs=qblk(dq),
        scratch_shapes=[
            pltpu.VMEM((nkb + 1 + COUNT_UNROLL, KEY_BLOCK, tq), F32),
            pltpu.VMEM((nkb // 2 + COUNT_UNROLL, KEY_BLOCK, tq), jnp.int32),
            pltpu.VMEM((4, KEY_BLOCK, IDX_HEADS * tq), F32),
            pltpu.VMEM((n_tiles, 2 * HEAD_DIM, hpt * tq), BF16),
            pltpu.VMEM((IDX_DIM, IDX_HEADS * tq), BF16),
            pltpu.VMEM((dq, tq), F32),
            pltpu.VMEM((2, n_tiles, 2 * KEY_BLOCK, hpt * tq), F32),
            pltpu.VMEM((2, n_tiles, 1, hpt * tq), F32),
        ] + [pltpu.VMEM((1, hpt * tq), F32)] * n_tiles
          + [pltpu.VMEM((V_AUG_ROWS, hpt * tq), F32)] * n_tiles,
        compiler_params=_params("parallel", "arbitrary"),
        name="sparse_attn",
    )(q, qi, wi, k2, vt, kib, bias_tiles)


def _mem_kv_kernel(x_ref, g_ref, w_ref, k_ref, v_ref, kh_ref, vh_ref, *, d):
    h = _rms(x_ref[...], g_ref[...]).astype(BF16)
    k = _dot(h, w_ref[:, :d])
    v = _dot(h, w_ref[:, d:])
    k_ref[...] = k
    v_ref[...] = v
    kh_ref[...] = k.reshape(kh_ref.shape)
    vh_ref[...] = v.reshape(vh_ref.shape)


def _mem_kv(mem2, g, w_kv, layer):
    m, d = mem2.shape
    tm = _row_tile(m, ROW_TILE)
    row = pl.BlockSpec((tm, d), lambda i: (i, 0))
    hd = d // MEM_HEADS
    split = pl.BlockSpec((tm, MEM_HEADS, hd), lambda i: (i, 0, 0))
    return pl.pallas_call(
        functools.partial(_mem_kv_kernel, d=d),
        out_shape=[jax.ShapeDtypeStruct((m, d), F32)] * 2 + [jax.ShapeDtypeStruct((m, MEM_HEADS, hd), F32)] * 2,
        grid=(m // tm,),
        in_specs=[row, _const_spec((1, d)), _layer_spec(w_kv, layer)],
        out_specs=[row, row, split, split],
        compiler_params=_params("parallel"),
        name="mem_kv",
    )(mem2, g.reshape(1, d), w_kv)


def _mem_attn_kernel(*refs, hd, pending):
    if pending:
        x_ref, a_ref, wa_ref, g_ref, wq_ref, mk_ref, mv_ref, wo_ref, o_ref = refs
        x = x_ref[...] + _dot(a_ref[...], wa_ref[...])
    else:
        x_ref, g_ref, wq_ref, mk_ref, mv_ref, wo_ref, o_ref = refs
        x = x_ref[...]
    h = _rms(x, g_ref[...]).astype(BF16)
    q = (_dot(h, wq_ref[...]) * (hd ** -0.5)).astype(BF16)
    if len(mk_ref.shape) == 4:
        head = lambda ref, a: ref[0, :, a, :]
    else:
        head = lambda ref, a: ref[0, :, a * hd:(a + 1) * hd]
    heads = []
    for a in range(MEM_HEADS):
        lg = _dot_nt(q[:, a * hd:(a + 1) * hd], head(mk_ref, a).astype(BF16))
        p = jnp.exp(lg - jnp.max(lg, axis=1, keepdims=True))
        p = (p / jnp.sum(p, axis=1, keepdims=True)).astype(BF16)
        heads.append(_dot(p, head(mv_ref, a).astype(BF16)).astype(BF16))
    o_ref[...] = x + _dot(jnp.concatenate(heads, axis=1), wo_ref[...])


def _mem_attn(x2, g, w_q, mk, mv, w_o, layer, rows_per_batch, pending=None):
    m, d = x2.shape
    n_mem = mk.shape[1]
    tm = _row_tile(rows_per_batch, ROW_TILE)
    per = rows_per_batch // tm
    row = pl.BlockSpec((tm, d), lambda i: (i, 0))
    mem = pl.BlockSpec((1,) + mk.shape[1:], lambda i: (i // per,) + (0,) * (mk.ndim - 1))
    in_specs = [row, _const_spec((1, d)), _layer_spec(w_q, layer), mem, mem, _layer_spec(w_o, layer)]
    args = [x2, g.reshape(1, d), w_q, mk, mv, w_o]
    if pending is not None:
        a2, w_a, layer_a = pending
        in_specs[1:1] = [pl.BlockSpec((tm, a2.shape[1]), lambda i: (i, 0)), _layer_spec(w_a, layer_a)]
        args[1:1] = [a2, w_a]
    return pl.pallas_call(
        functools.partial(_mem_attn_kernel, hd=d // MEM_HEADS, pending=pending is not None),
        out_shape=jax.ShapeDtypeStruct((m, d), F32),
        grid=(m // tm,),
        in_specs=in_specs,
        out_specs=row,
        compiler_params=_params("parallel"),
        name="mem_attn",
    )(*args)


def _mlp_kernel(*refs, n_chunks, chunk, final):
    if final:
        x_ref, g_ref, w1_ref, w2_ref, gf_ref, o_ref = refs
    else:
        x_ref, g_ref, w1_ref, w2_ref, o_ref = refs
    x = x_ref[...]
    h = _rms(x, g_ref[...]).astype(BF16)
    acc = x
    for c in range(n_chunks):
        a = jnp.maximum(_dot(h, w1_ref[:, c * chunk:(c + 1) * chunk]), 0.0)
        acc = acc + _dot((a * a).astype(BF16), w2_ref[c * chunk:(c + 1) * chunk, :])
    if final:
        acc = _rms(acc, gf_ref[...])
    o_ref[...] = acc


def _mlp(x2, g, w1, w2, layer, g_final=None):
    m, d = x2.shape
    dff = w1.shape[2]
    chunk = min(dff, 1024)
    tm = _row_tile(m, ROW_TILE)
    row = pl.BlockSpec((tm, d), lambda i: (i, 0))
    final = g_final is not None
    in_specs = [row, _const_spec((1, d)), _layer_spec(w1, layer), _layer_spec(w2, layer)]
    args = [x2, g.reshape(1, d), w1, w2]
    if final:
        in_specs.append(_const_spec((1, d)))
        args.append(g_final.reshape(1, d))
    return pl.pallas_call(
        functools.partial(_mlp_kernel, n_chunks=dff // chunk, chunk=chunk, final=final),
        out_shape=jax.ShapeDtypeStruct((m, d), F32),
        grid=(m // tm,),
        in_specs=in_specs,
        out_specs=row,
        compiler_params=_params("parallel"),
        name="mlp",
    )(*args)


def _conv_glu_kernel(x_ref, g_ref, w_ref, b_ref, u_ref, *, d):
    h = _rms(x_ref[...], g_ref[...]).astype(BF16)
    a = _dot(h, w_ref[:, :d]) + b_ref[:, :d]
    gate = _dot(h, w_ref[:, d:]) + b_ref[:, d:]
    u_ref[...] = a * (1.0 / (1.0 + jnp.exp(-gate)))


def _conv_glu(x2, g, w_pw1, layer, b_pw1):
    m, d = x2.shape
    tm = _row_tile(m, ROW_TILE)
    row = pl.BlockSpec((tm, d), lambda i: (i, 0))
    return pl.pallas_call(
        functools.partial(_conv_glu_kernel, d=d),
        out_shape=jax.ShapeDtypeStruct((m, d), F32),
        grid=(m // tm,),
        in_specs=[row, _const_spec((1, d)), _layer_spec(w_pw1, layer), _const_spec((1, 2 * d))],
        out_specs=row,
        compiler_params=_params("parallel"),
        name="conv_glu",
    )(x2, g.reshape(1, d), w_pw1, b_pw1.reshape(1, 2 * d))


def _conv_rest_kernel(x_ref, u_ref, prev_ref, init_ref, wdw_ref, bdw_ref, lng_ref, lnb_ref,
                      w2_ref, b2_ref, o_ref, ext_ref, sh_ref, y_ref, *, tm, rc, lc):
    t = pl.program_id(1)
    d = u_ref.shape[2]
    pad = CONV_WIDTH - 1

    @pl.when(t == 0)
    def _():
        ext_ref[0:HALO, :] = init_ref[0]

    @pl.when(t > 0)
    def _():
        ext_ref[0:HALO, :] = prev_ref[0]

    ext_ref[HALO:HALO + tm, :] = u_ref[0]

    for s in range(SUBLANES):
        rows = tm + SUBLANES * ((CONV_WIDTH - 1 - s) // SUBLANES)
        sh_ref[s, 0:rows, :] = ext_ref[pl.ds(HALO - pad + s, rows), :]
    for r0 in range(0, tm, rc):
        for c0 in range(0, d, lc):
            cols = slice(c0, c0 + lc)
            y = jnp.broadcast_to(bdw_ref[:, cols], (rc, lc))
            for w in range(CONV_WIDTH):
                a, s = divmod(w, SUBLANES)
                y = y + sh_ref[s, r0 + SUBLANES * a:r0 + SUBLANES * a + rc, cols] * wdw_ref[w:w + 1, cols]
            y_ref[r0:r0 + rc, cols] = y

    y = y_ref[...]
    mu = jnp.mean(y, axis=-1, keepdims=True)
    yc = y - mu
    var = jnp.mean(yc * yc, axis=-1, keepdims=True)
    yn = yc * lax.rsqrt(var + EPS) * lng_ref[...] + lnb_ref[...]
    act = (yn * (1.0 / (1.0 + jnp.exp(-yn)))).astype(BF16)
    o_ref[0] = x_ref[0] + _dot(act, w2_ref[...]) + b2_ref[...]


def _conv_rest(x3, u3, init, w_dw, b_dw, ln_g, ln_b, w_pw2, layer, b_pw2):
    b, t, d = x3.shape
    tm = _row_tile(t, CONV_ROW_TILE)
    rc = min(tm, 64)
    lc = min(d, 256)
    assert tm % HALO == 0 and tm % rc == 0 and d % lc == 0
    per = tm // HALO
    tile = pl.BlockSpec((1, tm, d), lambda bi, ti: (bi, ti, 0))
    prev = pl.BlockSpec((1, HALO, d), lambda bi, ti: (bi, jnp.maximum(ti * per - 1, 0), 0))
    first = pl.BlockSpec((1, HALO, d), lambda bi, ti: (bi, 0, 0))
    vec = _const_spec((1, d))
    wdw = jnp.pad(w_dw, ((0, HALO - CONV_WIDTH), (0, 0)))
    return pl.pallas_call(
        functools.partial(_conv_rest_kernel, tm=tm, rc=rc, lc=lc),
        out_shape=jax.ShapeDtypeStruct((b, t, d), F32),
        grid=(b, t // tm),
        in_specs=[tile, tile, prev, first, _const_spec((HALO, d)), vec, vec, vec,
                  _layer_spec(w_pw2, layer), vec],
        out_specs=tile,
        scratch_shapes=[pltpu.VMEM((HALO + tm, d), F32),
                        pltpu.VMEM((SUBLANES, tm + HALO - SUBLANES, d), F32),
                        pltpu.VMEM((tm, d), F32)],
        compiler_params=_params("parallel", "arbitrary"),
        name="conv_rest",
    )(x3, u3, u3, init, wdw, b_dw.reshape(1, d), ln_g.reshape(1, d), ln_b.reshape(1, d),
      w_pw2, b_pw2.reshape(1, d))


def _mixer_attn(x3, g, w_in, layer, bias_tiles, cache=None):
    b, t, d = x3.shape
    x2 = x3.reshape(b * t, d)
    q, k, v, ki, kb, vb, kib, qi, wi = _attn_proj(x2, g, w_in, layer)
    r3 = lambda a: a.reshape(b, t, a.shape[-1])
    kb, vb, kib = r3(kb), r3(vb), r3(kib)
    past = 0
    if cache is not None:
        ck, cv, cki = cache
        past = ck.shape[1]
        ck2 = jnp.concatenate([ck[:, :, n].astype(BF16) for n in range(N_KV_HEADS) for _ in range(2)], axis=-1)
        kb = jnp.concatenate([ck2, kb], axis=1)
        vb = jnp.concatenate([cv.reshape(b, past, -1).astype(BF16), vb], axis=1)
        kib = jnp.concatenate([cki.astype(BF16), kib], axis=1)
    n_keys = past + t
    k_sel = min(TOPK_MAX, n_keys // 4)
    tpad = -(-t // KEY_BLOCK) * KEY_BLOCK
    lp = max(-(-n_keys // KEY_BLOCK) * KEY_BLOCK, past + tpad)
    padt = lambda a, n: a if a.shape[1] == n else jnp.pad(a, ((0, 0), (0, n - a.shape[1]), (0, 0)))
    o = _sparse_attn(padt(r3(q), tpad), padt(r3(qi), tpad), padt(r3(wi), tpad),
                     padt(kb, lp), padt(vb, lp), padt(kib, lp), bias_tiles,
                     q_off=past, n_keys=n_keys, n_queries=t, k_sel=k_sel)[:, :t]
    return (o.reshape(b * t, d), k.reshape(b, t, N_KV_HEADS, HEAD_DIM),
            v.reshape(b, t, N_KV_HEADS, HEAD_DIM), ki.reshape(b, t, IDX_DIM))


def _mixer_conv(x3, g, w_pw1, b_pw1, w_dw, b_dw, ln_g, ln_b, w_pw2, b_pw2, layer, state=None):
    b, t, d = x3.shape
    pad = CONV_WIDTH - 1
    u3 = _conv_glu(x3.reshape(b * t, d), g, w_pw1, layer, b_pw1).reshape(b, t, d)
    if state is None:
        init = jnp.zeros((b, HALO, d), F32)
        tail = u3[:, -pad:] if t >= pad else jnp.pad(u3, ((0, 0), (pad - t, 0), (0, 0)))
    else:
        init = jnp.pad(state.astype(F32), ((0, 0), (HALO - pad, 0), (0, 0)))
        tail = jnp.concatenate([state.astype(F32), u3], axis=1)[:, -pad:]
    x3 = _conv_rest(x3, u3, init, w_dw, b_dw, ln_g, ln_b, w_pw2, layer, b_pw2)
    return x3, tail


def kernel(x_prompt, x_sample, cache_attn_k, cache_attn_v, cache_attn_kidx, state_conv, cache_mem_k, cache_mem_v, mem_prompt, rel_bias, g_mix, w_in_attn, w_out_attn, w_pw1, b_pw1, w_dw, b_dw, ln_g, ln_b, w_pw2, b_pw2, g_mem_q, g_mem_src, w_mem_q, w_mem_kv, w_mem_o, g_mlp, w_mlp1, w_mlp2, g_final):
    depth = g_mix.shape[0]
    bp, tp, d = x_prompt.shape
    bs, ts, _ = x_sample.shape
    n_mem = mem_prompt.shape[1]
    mem_hd = d // MEM_HEADS
    bias_tiles = _bias_tiles(rel_bias)
    w_in_b, w_out_b = _attn_proj_weights(w_in_attn), w_out_attn.astype(BF16)
    w_pw1_b, w_pw2_b = w_pw1.astype(BF16), w_pw2.astype(BF16)
    w_mq_b, w_mkv_b, w_mo_b = w_mem_q.astype(BF16), w_mem_kv.astype(BF16), w_mem_o.astype(BF16)
    w_mlp1_b, w_mlp2_b = w_mlp1.astype(BF16), w_mlp2.astype(BF16)
    xp, xs = x_prompt, x_sample
    kp_l, vp_l, kip_l, ks_l, vs_l, kis_l = [], [], [], [], [], []
    convp_l, convs_l, memk_l, memv_l = [], [], [], []
    for i in range(depth):
        j = i // 2
        if i % 2 == 0:
            op, kp, vp, kip = _mixer_attn(xp, g_mix[i], w_in_b, j, bias_tiles)
            osm, ks, vs, kis = _mixer_attn(
                xs, g_mix[i], w_in_b, j, bias_tiles,
                cache=(cache_attn_k[j], cache_attn_v[j], cache_attn_kidx[j]))
            pend_p, pend_s = (op, w_out_b, j), (osm, w_out_b, j)
            kp_l.append(kp); vp_l.append(vp); kip_l.append(kip)
            ks_l.append(ks); vs_l.append(vs); kis_l.append(kis)
        else:
            cw = (w_pw1_b, b_pw1[j], w_dw[j], b_dw[j], ln_g[j], ln_b[j], w_pw2_b, b_pw2[j], j)
            xp, cp = _mixer_conv(xp, g_mix[i], *cw)
            xs, cs = _mixer_conv(xs, g_mix[i], *cw, state=state_conv[j])
            pend_p = pend_s = None
            convp_l.append(cp); convs_l.append(cs)
        mk, mv, mkh, mvh = _mem_kv(mem_prompt.reshape(bp * n_mem, d), g_mem_src[i], w_mkv_b, i)
        mk, mv = mk.reshape(bp, n_mem, d), mv.reshape(bp, n_mem, d)
        memk_l.append(mkh.reshape(bp, n_mem, MEM_HEADS, mem_hd))
        memv_l.append(mvh.reshape(bp, n_mem, MEM_HEADS, mem_hd))
        xp2 = _mem_attn(xp.reshape(bp * tp, d), g_mem_q[i], w_mq_b, mk, mv, w_mo_b, i, tp, pend_p)
        xs2 = _mem_attn(xs.reshape(bs * ts, d), g_mem_q[i], w_mq_b, cache_mem_k[i], cache_mem_v[i],
                        w_mo_b, i, ts, pend_s)
        gf = g_final if i == depth - 1 else None
        xp = _mlp(xp2, g_mlp[i], w_mlp1_b, w_mlp2_b, i, gf).reshape(bp, tp, d)
        xs = _mlp(xs2, g_mlp[i], w_mlp1_b, w_mlp2_b, i, gf).reshape(bs, ts, d)
    return (xp, xs, jnp.stack(kp_l), jnp.stack(vp_l), jnp.stack(kip_l), jnp.stack(convp_l),
            jnp.stack(memk_l), jnp.stack(memv_l), jnp.stack(ks_l), jnp.stack(vs_l),
            jnp.stack(kis_l), jnp.stack(convs_l))
```
